```python
import jax, jax.numpy as jnp
from jax import lax
import numpy as np

D_MODEL = 1024
BATCH = 8
SEQ = 2048
DEPTH = 1

HEAD_DIM = 64
NSA_HEADS = 8
NSA_KV_GROUPS = 2
NSA_HPG = NSA_HEADS // NSA_KV_GROUPS
FOX_HEADS = 8
NSA_WIDTH = NSA_HEADS * HEAD_DIM
FOX_WIDTH = FOX_HEADS * HEAD_DIM
NSA_KV_WIDTH = NSA_KV_GROUPS * HEAD_DIM
CMP_BLOCK = 32
CMP_STRIDE = 16
CMP_HIDDEN = 128
SEL_BLOCK = 64
N_SELECT = 16
WINDOW = 512
Q_BLOCK = 128
SEL_Q_BLOCK = 32
D_FF = 4 * D_MODEL
NORM_EPS = 1e-6
NEG_INF = -1e30
FORCE_SCORE = 1e9
IN_SIZES = (NSA_WIDTH,) + (NSA_KV_WIDTH,) * 6 + (3 * NSA_HEADS, FOX_WIDTH, FOX_WIDTH, FOX_WIDTH, FOX_HEADS)
IN_WIDTH = sum(IN_SIZES)
IN_OFFSETS = tuple(int(o) for o in np.cumsum(IN_SIZES)[:-1])

kernel_name = "hybrid_nsa_fox_gated_block"


def rms_norm(x, g):
    xf = x.astype(jnp.float32)
    y = xf * lax.rsqrt(jnp.mean(xf * xf, axis=-1, keepdims=True) + NORM_EPS)
    return (y * g.astype(jnp.float32)).astype(x.dtype)


def alibi_slopes(n):
    return jnp.exp2(-8.0 * jnp.arange(1, n + 1, dtype=jnp.float32) / n)


def masked_softmax(s, valid):
    p = jax.nn.softmax(jnp.where(valid, s, NEG_INF), axis=-1)
    return jnp.where(valid, p, 0.0)


def compress_blocks(kv, pos, w1, w2):
    b, g, t, dk = kv.shape
    nc = (t - CMP_BLOCK) // CMP_STRIDE + 1
    idx = np.arange(nc)[:, None] * CMP_STRIDE + np.arange(CMP_BLOCK)[None, :]
    blocks = kv[:, :, idx] + pos.astype(kv.dtype)
    flat = blocks.reshape(b, g, nc, CMP_BLOCK * dk)
    return jax.nn.gelu(flat @ w1) @ w2


def nsa_attention(q, kc, vc, ks, vs, kw, vw, gates, pos_k, w1_k, w2_k, pos_v, w1_v, w2_v):
    b, g, hg, t, dk = q.shape
    scale = dk ** -0.5
    slopes = alibi_slopes(NSA_HEADS).reshape(g, hg)
    tq_all = jnp.arange(t)

    kcmp = compress_blocks(kc, pos_k, w1_k, w2_k)
    vcmp = compress_blocks(vc, pos_v, w1_v, w2_v)
    nc = kcmp.shape[2]
    cmp_start = jnp.arange(nc) * CMP_STRIDE
    cmp_end = cmp_start + CMP_BLOCK - 1
    dist_c = (tq_all[:, None] - cmp_end[None, :]).astype(jnp.float32)
    s_c = jnp.einsum('bghtd,bgnd->bghtn', q, kcmp).astype(jnp.float32) * scale - slopes[:, :, None, None] * dist_c
    p_cmp = masked_softmax(s_c, dist_c >= 0)
    o_cmp = jnp.einsum('bghtn,bgnd->bghtd', p_cmp.astype(vcmp.dtype), vcmp)

    ns = t // SEL_BLOCK
    sel_start = jnp.arange(ns) * SEL_BLOCK
    overlap = ((cmp_start[:, None] <= sel_start[None, :] + SEL_BLOCK - 1) & (cmp_end[:, None] >= sel_start[None, :])).astype(jnp.float32)
    imp = jnp.einsum('bghtn,nj->bgtj', p_cmp, overlap)
    cur = tq_all // SEL_BLOCK
    jj = jnp.arange(ns)
    forced = (jj[None, :] == 0) | (jj[None, :] == cur[:, None]) | (jj[None, :] == cur[:, None] - 1)
    valid_blk = sel_start[None, :] <= tq_all[:, None]
    score = jnp.where(forced, FORCE_SCORE, jnp.where(valid_blk, imp, -1.0))
    n_sel = min(N_SELECT, ns)
    _, sel_idx = lax.top_k(score, n_sel)

    ks_blocks = ks.reshape(b, g, ns, SEL_BLOCK, dk)
    vs_blocks = vs.reshape(b, g, ns, SEL_BLOCK, dk)
    nq = t // SEL_Q_BLOCK
    q_ch = jnp.moveaxis(q.reshape(b, g, hg, nq, SEL_Q_BLOCK, dk), 3, 0)
    idx_ch = jnp.moveaxis(sel_idx.reshape(b, g, nq, SEL_Q_BLOCK, n_sel), 2, 0)
    bi = jnp.arange(b)[:, None, None, None]
    gi = jnp.arange(g)[None, :, None, None]

    def sel_chunk(args):
        c, qc, ic = args
        kg = ks_blocks[bi, gi, ic]
        vg = vs_blocks[bi, gi, ic]
        tq = c * SEL_Q_BLOCK + jnp.arange(SEL_Q_BLOCK)
        pos = ic[..., None] * SEL_BLOCK + jnp.arange(SEL_BLOCK)
        dist = (tq[:, None, None] - pos).astype(jnp.float32)
        s = jnp.einsum('bghcd,bgcnkd->bghcnk', qc, kg).astype(jnp.float32) * scale - slopes[:, :, None, None, None] * dist[:, :, None]
        s = s.reshape(b, g, hg, SEL_Q_BLOCK, n_sel * SEL_BLOCK)
        valid = (dist >= 0).reshape(b, g, 1, SEL_Q_BLOCK, n_sel * SEL_BLOCK)
        p = masked_softmax(s, valid)
        return jnp.einsum('bghcm,bgcmd->bghcd', p.astype(vg.dtype), vg.reshape(b, g, SEL_Q_BLOCK, n_sel * SEL_BLOCK, dk))

    o_sel = lax.map(sel_chunk, (jnp.arange(nq), q_ch, idx_ch))
    o_sel = jnp.moveaxis(o_sel, 0, 3).reshape(b, g, hg, t, dk)

    span = Q_BLOCK + WINDOW
    kw_pad = jnp.pad(kw, ((0, 0), (0, 0), (WINDOW, 0), (0, 0)))
    vw_pad = jnp.pad(vw, ((0, 0), (0, 0), (WINDOW, 0), (0, 0)))
    nb = t // Q_BLOCK
    q_blk = jnp.moveaxis(q.reshape(b, g, hg, nb, Q_BLOCK, dk), 3, 0)

    def win_chunk(args):
        c, qc = args
        start = c * Q_BLOCK
        kb = lax.dynamic_slice_in_dim(kw_pad, start, span, axis=2)
        vb = lax.dynamic_slice_in_dim(vw_pad, start, span, axis=2)
        tq = start + jnp.arange(Q_BLOCK)
        sk = start - WINDOW + jnp.arange(span)
        d = tq[:, None] - sk[None, :]
        valid = (sk[None, :] >= 0) & (d >= 0) & (d < WINDOW)
        s = jnp.einsum('bghqd,bgkd->bghqk', qc, kb).astype(jnp.float32) * scale - slopes[:, :, None, None] * d.astype(jnp.float32)
        p = masked_softmax(s, valid)
        return jnp.einsum('bghqk,bgkd->bghqd', p.astype(vb.dtype), vb)

    o_win = lax.map(win_chunk, (jnp.arange(nb), q_blk))
    o_win = jnp.moveaxis(o_win, 0, 3).reshape(b, g, hg, t, dk)

    return gates[0] * o_cmp + gates[1] * o_sel + gates[2] * o_win


def forgetting_attention(q, k, v, f_logit):
    b, h, t, dk = q.shape
    scale = dk ** -0.5
    csum = jnp.cumsum(jax.nn.log_sigmoid(f_logit.astype(jnp.float32)), axis=-1)
    nb = t // Q_BLOCK
    q_blk = jnp.moveaxis(q.reshape(b, h, nb, Q_BLOCK, dk), 2, 0)
    c_blk = jnp.moveaxis(csum.reshape(b, h, nb, Q_BLOCK), 2, 0)
    tk = jnp.arange(t)

    def blk(args):
        c, qc, cq = args
        tq = c * Q_BLOCK + jnp.arange(Q_BLOCK)
        valid = tk[None, :] <= tq[:, None]
        s = jnp.einsum('bhqd,bhkd->bhqk', qc, k).astype(jnp.float32) * scale + cq[..., None] - csum[:, :, None, :]
        p = masked_softmax(s, valid)
        return jnp.einsum('bhqk,bhkd->bhqd', p.astype(v.dtype), v)

    o = lax.map(blk, (jnp.arange(nb), q_blk, c_blk))
    return jnp.moveaxis(o, 0, 2).reshape(b, h, t, dk)


def setup_inputs(seed: int = 0) -> dict:
    key = jax.random.key(seed)
    ks = jax.random.split(key, 20)
    L = DEPTH
    nrm = lambda k, shape, fan: jax.random.normal(k, shape, jnp.float32) * fan ** -0.5
    gain = lambda k, shape: 1.0 + 0.01 * jax.random.normal(k, shape, jnp.float32)
    return {
        "x": jax.random.normal(ks[0], (BATCH, SEQ, D_MODEL), jnp.float32),
        "norm_mix": gain(ks[1], (L, D_MODEL)),
        "w_in": nrm(ks[2], (L, D_MODEL, IN_WIDTH), D_MODEL),
        "cmp_pos_k": 0.1 * jax.random.normal(ks[3], (L, CMP_BLOCK, HEAD_DIM), jnp.float32),
        "cmp_w1_k": nrm(ks[4], (L, CMP_BLOCK * HEAD_DIM, CMP_HIDDEN), CMP_BLOCK * HEAD_DIM),
        "cmp_w2_k": nrm(ks[5], (L, CMP_HIDDEN, HEAD_DIM), CMP_HIDDEN),
        "cmp_pos_v": 0.1 * jax.random.normal(ks[6], (L, CMP_BLOCK, HEAD_DIM), jnp.float32),
        "cmp_w1_v": nrm(ks[7], (L, CMP_BLOCK * HEAD_DIM, CMP_HIDDEN), CMP_BLOCK * HEAD_DIM),
        "cmp_w2_v": nrm(ks[8], (L, CMP_HIDDEN, HEAD_DIM), CMP_HIDDEN),
        "fox_f_bias": jax.random.uniform(ks[9], (L, FOX_HEADS), jnp.float32, 1.0, 6.0),
        "w_branch_nsa": nrm(ks[10], (L, NSA_WIDTH, D_MODEL), NSA_WIDTH),
        "w_branch_fox": nrm(ks[11], (L, FOX_WIDTH, D_MODEL), FOX_WIDTH),
        "w_merge_gate": nrm(ks[12], (L, D_MODEL, 2 * D_MODEL), D_MODEL),
        "b_merge_gate": 0.01 * jax.random.normal(ks[13], (L, 2 * D_MODEL), jnp.float32),
        "w_out": nrm(ks[14], (L, D_MODEL, D_MODEL), D_MODEL),
        "norm_mlp": gain(ks[15], (L, D_MODEL)),
        "w_up": nrm(ks[16], (L, D_MODEL, D_FF), D_MODEL),
        "w_down": nrm(ks[17], (L, D_FF, D_MODEL), D_FF),
        "norm_final": gain(ks[18], (D_MODEL,)),
    }


def reference(x, norm_mix, w_in, cmp_pos_k, cmp_w1_k, cmp_w2_k, cmp_pos_v, cmp_w1_v, cmp_w2_v, fox_f_bias, w_branch_nsa, w_branch_fox, w_merge_gate, b_merge_gate, w_out, norm_mlp, w_up, w_down, norm_final):
    b, t, _ = x.shape
    g, hg = NSA_KV_GROUPS, NSA_HPG
    h = x
    for i in range(DEPTH):
        u = rms_norm(h, norm_mix[i])
        parts = jnp.split(u @ w_in[i], IN_OFFSETS, axis=-1)
        q_n, kc, vc, ks_, vs_, kw, vw, gate_n, q_f, k_f, v_f, f_l = parts
        q_n = q_n.reshape(b, t, NSA_HEADS, HEAD_DIM).transpose(0, 2, 1, 3).reshape(b, g, hg, t, HEAD_DIM)
        kvs = [a.reshape(b, t, g, HEAD_DIM).transpose(0, 2, 1, 3) for a in (kc, vc, ks_, vs_, kw, vw)]
        gates = jax.nn.sigmoid(gate_n.astype(jnp.float32)).astype(u.dtype)
        gates = gates.reshape(b, t, 3, NSA_HEADS).transpose(2, 0, 3, 1).reshape(3, b, g, hg, t)[..., None]
        o_n = nsa_attention(q_n, *kvs, gates, cmp_pos_k[i], cmp_w1_k[i], cmp_w2_k[i], cmp_pos_v[i], cmp_w1_v[i], cmp_w2_v[i])
        o_n = o_n.reshape(b, NSA_HEADS, t, HEAD_DIM).transpose(0, 2, 1, 3).reshape(b, t, NSA_WIDTH)
        to_heads = lambda a: a.reshape(b, t, FOX_HEADS, HEAD_DIM).transpose(0, 2, 1, 3)
        f_logit = (f_l + fox_f_bias[i]).transpose(0, 2, 1)
        o_f = forgetting_attention(to_heads(q_f), to_heads(k_f), to_heads(v_f), f_logit)
        o_f = o_f.transpose(0, 2, 1, 3).reshape(b, t, FOX_WIDTH)
        mg = jax.nn.sigmoid((u @ w_merge_gate[i] + b_merge_gate[i]).astype(jnp.float32)).astype(u.dtype)
        g_a, g_b = jnp.split(mg, 2, axis=-1)
        y = g_a * (o_n @ w_branch_nsa[i]) + g_b * (o_f @ w_branch_fox[i])
        h = h + y @ w_out[i]
        v = rms_norm(h, norm_mlp[i])
        h = h + jnp.square(jax.nn.relu(v @ w_up[i])) @ w_down[i]
    return rms_norm(h, norm_final)
```

```python
import functools

import numpy as np
import jax
import jax.numpy as jnp
from jax import lax
from jax.experimental import pallas as pl
from jax.experimental.pallas import tpu as pltpu

HEAD_DIM = 64
NSA_HEADS = 8
NSA_GROUPS = 2
NSA_HPG = NSA_HEADS // NSA_GROUPS
FOX_HEADS = 8
CMP_BLOCK = 32
CMP_STRIDE = 16
CMP_HIDDEN = 128
SEL_BLOCK = 64
N_SELECT = 16
WINDOW = 512
NORM_EPS = 1e-6
FORCE_SCORE = 1e9

LANES = 128
NEG_BIG = -(2.0 ** 100)
M_INIT = -3.0e38
VMEM_LIMIT = 56 * 1024 * 1024
ROW_TILE = 512

PROJ_W = 2816
COL_QN = 0
COL_KV = 512
COL_QF = 1280
COL_KF = 1792
COL_VF = 2304
GF_GATE = 0
GF_FOX = 24

F32 = jnp.float32
BF16 = jnp.bfloat16


def _dot(a, b):
    return jnp.dot(a, b, preferred_element_type=F32)


def _iota(shape, dim):
    return lax.broadcasted_iota(jnp.int32, shape, dim)


def _split3(x):
    hi = x.astype(BF16)
    r1 = x - hi.astype(F32)
    mid = r1.astype(BF16)
    lo = (r1 - mid.astype(F32)).astype(BF16)
    return hi, mid, lo


def _inproj_kernel(x_ref, g_ref, wa_ref, wg_ref, wm_ref, bm_ref, proj_ref, gf_ref, mg_ref):
    x = x_ref[...]
    r = lax.rsqrt(jnp.mean(x * x, axis=-1, keepdims=True) + NORM_EPS)
    u = (x * r * g_ref[...]).astype(BF16)
    for c0, c1 in ((0, 1024), (1024, 2048), (2048, PROJ_W)):
        proj_ref[:, c0:c1] = _dot(u, wa_ref[:, c0:c1]).astype(BF16)
    gf_ref[...] = _dot(u, wg_ref[...])
    for c0 in (0, 1024):
        z = _dot(u, wm_ref[:, c0:c0 + 1024]) + bm_ref[:, c0:c0 + 1024]
        mg_ref[:, c0:c0 + 1024] = jax.nn.sigmoid(z).astype(BF16)


def _inproj(x2, g, wa, wg, wm, bm, tm):
    n, d = x2.shape
    const = lambda i: (0, 0)
    return pl.pallas_call(
        _inproj_kernel,
        grid=(n // tm,),
        in_specs=[
            pl.BlockSpec((tm, d), lambda i: (i, 0)),
            pl.BlockSpec((1, d), const),
            pl.BlockSpec(wa.shape, const),
            pl.BlockSpec(wg.shape, const),
            pl.BlockSpec(wm.shape, const),
            pl.BlockSpec(bm.shape, const),
        ],
        out_specs=[
            pl.BlockSpec((tm, PROJ_W), lambda i: (i, 0)),
            pl.BlockSpec((tm, LANES), lambda i: (i, 0)),
            pl.BlockSpec((tm, 2 * d), lambda i: (i, 0)),
        ],
        out_shape=[
            jax.ShapeDtypeStruct((n, PROJ_W), BF16),
            jax.ShapeDtypeStruct((n, LANES), F32),
            jax.ShapeDtypeStruct((n, 2 * d), BF16),
        ],
        compiler_params=pltpu.CompilerParams(
            dimension_semantics=("arbitrary",), vmem_limit_bytes=VMEM_LIMIT),
        name="inproj",
    )(x2, g, wa, wg, wm, bm)


def _compress_kernel(rk_ref, rv_ref, w1k_ref, w2k0_ref, w2k1_ref, pk_ref,
                     w1v_ref, w2v0_ref, w2v1_ref, pv_ref, kc_ref, vc_ref):
    half = CMP_STRIDE * HEAD_DIM

    def one(r_ref, w1_ref, w2_refs, p_ref, o_ref):
        posb = _dot(p_ref[...], w1_ref[...])[0:1]
        acc = jnp.zeros((LANES, LANES), F32)
        for g in range(NSA_GROUPS):
            r = r_ref[0, g]
            a = _dot(r, w1_ref[0:half, :])
            b = _dot(r, w1_ref[half:2 * half, :])
            h = a + pltpu.roll(b, LANES - 1, 0) + posb
            acc = acc + _dot(jax.nn.gelu(h).astype(BF16), w2_refs[g][...])
        o_ref[0] = acc.astype(BF16)

    one(rk_ref, w1k_ref, (w2k0_ref, w2k1_ref), pk_ref, kc_ref)
    one(rv_ref, w1v_ref, (w2v0_ref, w2v1_ref), pv_ref, vc_ref)


def _compress(rk, rv, w1k, w2k0, w2k1, pk, w1v, w2v0, w2v1, pv):
    b = rk.shape[0]
    rspec = pl.BlockSpec((1,) + rk.shape[1:], lambda i: (i, 0, 0, 0))
    c2 = lambda a: pl.BlockSpec(a.shape, lambda i: (0, 0))
    ospec = pl.BlockSpec((1, LANES, LANES), lambda i: (i, 0, 0))
    return pl.pallas_call(
        _compress_kernel,
        grid=(b,),
        in_specs=[rspec, rspec, c2(w1k), c2(w2k0), c2(w2k1), c2(pk), c2(w1v), c2(w2v0), c2(w2v1), c2(pv)],
        out_specs=[ospec, ospec],
        out_shape=[jax.ShapeDtypeStruct((b, LANES, LANES), BF16)] * 2,
        compiler_params=pltpu.CompilerParams(dimension_semantics=("arbitrary",)),
        name="compress",
    )(rk, rv, w1k, w2k0, w2k1, pk, w1v, w2v0, w2v1, pv)


def _flash_init(m_ref, acc_ref):
    m_ref[...] = jnp.full(m_ref.shape, M_INIT, F32)
    acc_ref[...] = jnp.zeros(acc_ref.shape, F32)


def _flash_step(kblk, vt_blk, q, bias, m_ref, acc_ref):
    s = _dot(kblk, q)
    if bias is not None:
        s = s + bias
    m_old = m_ref[...]
    m_new = jnp.maximum(m_old, jnp.max(s, axis=0, keepdims=True))
    p = jnp.exp(s - m_new).astype(BF16)
    alpha = jnp.exp(m_old - m_new)
    acc_ref[...] = acc_ref[...] * alpha + _dot(vt_blk, p)
    m_ref[...] = m_new


def _ones_rows(tk):
    return jnp.where(_iota((16, tk), 0) == 0, 1.0, 0.0).astype(BF16)


NSA_TQ = 128
NSA_N = NSA_HPG * NSA_TQ
SEL_TK = 256
WIN_TK = 128


def _nsa_kernel(q_ref, gf_ref, kc_ref, vc_ref, ks_ref, vs_ref, kw_ref, vw_ref,
                augk_ref, augc_ref, ovt_ref, trisel_ref, triwin_ref, o_ref,
                vst_ref, vwt_ref, vct_ref, gt_ref, m_ref, acc_ref):
    g = pl.program_id(1)
    qb = pl.program_id(2)
    t = ks_ref.shape[1]
    t0 = qb * NSA_TQ
    goff = pl.multiple_of(g * HEAD_DIM, HEAD_DIM)

    @pl.when((g == 0) & (qb == 0))
    def _():
        for c in range(t // LANES):
            sl = slice(c * LANES, (c + 1) * LANES)
            vst_ref[:, sl] = vs_ref[0, sl, :].astype(F32).T.astype(BF16)
            vwt_ref[:, sl] = vw_ref[0, sl, :].astype(F32).T.astype(BF16)
        vct_ref[...] = vc_ref[0].astype(F32).T.astype(BF16)

    qt = q_ref[0].astype(F32).T * (HEAD_DIM ** -0.5)
    half_mask = (_iota((LANES, NSA_TQ), 0) // HEAD_DIM) == g
    tops = []
    for h in range(NSA_HPG):
        qh = qt[h * HEAD_DIM:(h + 1) * HEAD_DIM]
        tops.append(jnp.where(half_mask, jnp.concatenate([qh, qh], axis=0), 0.0))
    q_top = jnp.concatenate(tops, axis=1)

    lane = _iota((1, NSA_N), 1)
    head = lane // NSA_TQ
    tq = t0 + lane % NSA_TQ
    slope = lax.bitcast_convert_type((127 - (4 * g + head + 1)) << 23, F32)
    r0 = 64.0 * slope
    r1 = slope
    r2 = -(64.0 * slope) * (tq // 64).astype(F32)
    r3 = -slope * (tq % 64).astype(F32)
    ridx = _iota((16, NSA_N), 0)
    aug16 = jnp.where(ridx == 0, r0, jnp.where(ridx == 1, r1, jnp.where(ridx == 2, r2, jnp.where(ridx == 3, r3, 0.0))))
    q_plain = jnp.concatenate([q_top, aug16, jnp.zeros((112, NSA_N), F32)], axis=0).astype(BF16)

    kcb = jnp.concatenate([kc_ref[0], augc_ref[...]], axis=1)
    s = _dot(kcb, q_plain)
    n_idx = _iota((LANES, NSA_N), 0)
    valid = (n_idx * CMP_STRIDE + (CMP_BLOCK - 1) <= tq) & (n_idx < LANES - 1)
    sm = jnp.where(valid, s, NEG_BIG)
    mx = jnp.max(sm, axis=0, keepdims=True)
    e = jnp.where(valid, jnp.exp(sm - mx), 0.0)
    den = jnp.sum(e, axis=0, keepdims=True)
    pn = e / jnp.where(den > 0.0, den, 1.0)
    o_cmp = _dot(vct_ref[pl.ds(goff, HEAD_DIM), :], pn.astype(BF16))

    psum = pn[:, 0:NSA_TQ]
    for h in range(1, NSA_HPG):
        psum = psum + pn[:, h * NSA_TQ:(h + 1) * NSA_TQ]
    hi, mid, lo = _split3(psum)
    ovt = ovt_ref[...]
    imp = _dot(ovt, hi) + _dot(ovt, mid) + _dot(ovt, lo)
    ns = ovt.shape[0]
    j_idx = _iota((ns, NSA_TQ), 0)
    tcol = t0 + _iota((ns, NSA_TQ), 1)
    cur = tcol // SEL_BLOCK
    forced = (j_idx == 0) | (j_idx == cur) | (j_idx == cur - 1)
    score = jnp.where(forced, FORCE_SCORE, jnp.where(j_idx * SEL_BLOCK <= tcol, imp, -1.0))
    rank = jnp.zeros((ns, NSA_TQ), jnp.int32)
    for i in range(ns):
        si = score[i:i + 1, :]
        beats = (si > score) | ((si == score) & (j_idx > i))
        rank = rank + beats.astype(jnp.int32)
    maskval = jnp.where(rank < min(N_SELECT, ns), 0.0, NEG_BIG)
    mask4 = jnp.concatenate([maskval] * NSA_HPG, axis=1)
    q_sel = jnp.concatenate([q_top, aug16, jnp.zeros((16, NSA_N), F32), mask4,
                             jnp.zeros((64, NSA_N), F32)], axis=0).astype(BF16)

    ones_s = _ones_rows(SEL_TK)

    def sel_step(kb, bias):
        k0 = pl.multiple_of(kb * SEL_TK, SEL_TK)
        kblk = jnp.concatenate([ks_ref[0, pl.ds(k0, SEL_TK), :], augk_ref[pl.ds(k0, SEL_TK), :]], axis=1)
        vt = jnp.concatenate([vst_ref[pl.ds(goff, HEAD_DIM), pl.ds(k0, SEL_TK)], ones_s], axis=0)
        _flash_step(kblk, vt, q_sel, bias, m_ref, acc_ref)

    _flash_init(m_ref, acc_ref)
    n_full = qb // 2

    def sel_body(kb, carry):
        sel_step(kb, None)
        return carry

    lax.fori_loop(0, n_full, sel_body, 0)
    sel_step(n_full, trisel_ref[qb % 2])
    acc = acc_ref[...]
    o_sel = acc[0:HEAD_DIM] / acc[HEAD_DIM:HEAD_DIM + 1]

    ones_w = _ones_rows(WIN_TK)

    def win_step(kb, bias):
        k0 = pl.multiple_of(kb * WIN_TK, WIN_TK)
        kblk = jnp.concatenate([kw_ref[0, pl.ds(k0, WIN_TK), :], augk_ref[pl.ds(k0, WIN_TK), :]], axis=1)
        vt = jnp.concatenate([vwt_ref[pl.ds(goff, HEAD_DIM), pl.ds(k0, WIN_TK)], ones_w], axis=0)
        _flash_step(kblk, vt, q_plain, bias, m_ref, acc_ref)

    _flash_init(m_ref, acc_ref)
    n_back = WINDOW // WIN_TK

    @pl.when(qb >= n_back)
    def _():
        win_step(qb - n_back, triwin_ref[1])

    def win_body(kb, carry):
        win_step(kb, None)
        return carry

    lax.fori_loop(jnp.maximum(qb - n_back + 1, 0), qb, win_body, 0)
    win_step(qb, triwin_ref[0])
    acc = acc_ref[...]
    o_win = acc[0:HEAD_DIM] / acc[HEAD_DIM:HEAD_DIM + 1]

    gt_ref[...] = gf_ref[0].T
    outs = []
    for h in range(NSA_HPG):
        sl = slice(h * NSA_TQ, (h + 1) * NSA_TQ)
        hrow = GF_GATE + NSA_HPG * g + h
        g0 = jax.nn.sigmoid(gt_ref[pl.ds(hrow, 1), :])
        g1 = jax.nn.sigmoid(gt_ref[pl.ds(hrow + NSA_HEADS, 1), :])
        g2 = jax.nn.sigmoid(gt_ref[pl.ds(hrow + 2 * NSA_HEADS, 1), :])
        outs.append(g0 * o_cmp[:, sl] + g1 * o_sel[:, sl] + g2 * o_win[:, sl])
    o_ref[0] = jnp.concatenate(outs, axis=0).T.astype(BF16)


def _nsa(proj, gf, kcmp, vcmp, augk, augc, ovt, trisel, triwin):
    b, t, _ = proj.shape
    nq = t // NSA_TQ
    kvb = COL_KV // LANES
    whole = lambda a: pl.BlockSpec(a.shape, lambda i, j, k: (0,) * a.ndim)
    kv = lambda idx: pl.BlockSpec((1, t, LANES), lambda i, j, k: (i, 0, idx))
    return pl.pallas_call(
        _nsa_kernel,
        grid=(b, NSA_GROUPS, nq),
        in_specs=[
            pl.BlockSpec((1, NSA_TQ, NSA_HPG * HEAD_DIM), lambda i, j, k: (i, k, j)),
            pl.BlockSpec((1, NSA_TQ, LANES), lambda i, j, k: (i, k, 0)),
            pl.BlockSpec((1, LANES, LANES), lambda i, j, k: (i, 0, 0)),
            pl.BlockSpec((1, LANES, LANES), lambda i, j, k: (i, 0, 0)),
            kv(kvb + 2), kv(kvb + 3), kv(kvb + 4), kv(kvb + 5),
            whole(augk), whole(augc), whole(ovt), whole(trisel), whole(triwin),
        ],
        out_specs=pl.BlockSpec((1, NSA_TQ, NSA_HPG * HEAD_DIM), lambda i, j, k: (i, k, j)),
        out_shape=jax.ShapeDtypeStruct((b, t, NSA_HEADS * HEAD_DIM), BF16),
        scratch_shapes=[
            pltpu.VMEM((LANES, t), BF16),
            pltpu.VMEM((LANES, t), BF16),
            pltpu.VMEM((LANES, LANES), BF16),
            pltpu.VMEM((LANES, LANES), F32),
            pltpu.VMEM((1, NSA_N), F32),
            pltpu.VMEM((HEAD_DIM + 16, NSA_N), F32),
        ],
        compiler_params=pltpu.CompilerParams(
            dimension_semantics=("arbitrary", "arbitrary", "arbitrary"), vmem_limit_bytes=VMEM_LIMIT),
        name="nsa_attention",
    )(proj, gf, kcmp, vcmp, proj, proj, proj, proj, augk, augc, ovt, trisel, triwin)


FOX_CBLK = 256


def _foxprep_kernel(gf_ref, bias_ref, cq_ref, augk_ref):
    t = gf_ref.shape[1]
    cols = []
    for c in range(t // LANES):
        cols.append(gf_ref[0, c * LANES:(c + 1) * LANES, :].T[GF_FOX:GF_FOX + FOX_HEADS])
    f = jnp.concatenate(cols, axis=1) + bias_ref[...]
    ls = jnp.minimum(f, 0.0) - jnp.log(1.0 + jnp.exp(-jnp.abs(f)))
    upper = (_iota((FOX_CBLK, FOX_CBLK), 0) <= _iota((FOX_CBLK, FOX_CBLK), 1)).astype(BF16)
    carry = jnp.zeros((FOX_HEADS, 1), F32)
    parts = []
    for c in range(t // FOX_CBLK):
        hi, mid, lo = _split3(ls[:, c * FOX_CBLK:(c + 1) * FOX_CBLK])
        cb = _dot(hi, upper) + _dot(mid, upper) + _dot(lo, upper) + carry
        carry = cb[:, FOX_CBLK - 1:FOX_CBLK]
        parts.append(cb)
    csum = jnp.concatenate(parts, axis=1)
    c1, c2, c3 = (v.astype(F32) for v in _split3(csum))
    ones = jnp.ones((3 * FOX_HEADS, t), F32)
    zeros = jnp.zeros((LANES - 6 * FOX_HEADS, t), F32)
    cq_ref[0] = jnp.concatenate([c1, c2, c3, ones, zeros], axis=0).astype(BF16)
    slab_k = jnp.concatenate([ones, -c1, -c2, -c3, zeros], axis=0)
    for c in range(t // LANES):
        sl = slice(c * LANES, (c + 1) * LANES)
        augk_ref[0, sl, :] = slab_k[:, sl].T.astype(BF16)


def _foxprep(gf, bias):
    b, t, _ = gf.shape
    return pl.pallas_call(
        _foxprep_kernel,
        grid=(b,),
        in_specs=[pl.BlockSpec((1, t, LANES), lambda i: (i, 0, 0)),
                  pl.BlockSpec(bias.shape, lambda i: (0, 0))],
        out_specs=[pl.BlockSpec((1, LANES, t), lambda i: (i, 0, 0)),
                   pl.BlockSpec((1, t, LANES), lambda i: (i, 0, 0))],
        out_shape=[jax.ShapeDtypeStruct((b, LANES, t), BF16),
                   jax.ShapeDtypeStruct((b, t, LANES), BF16)],
        compiler_params=pltpu.CompilerParams(dimension_semantics=("arbitrary",)),
        name="fox_prep",
    )(gf, bias)


FOX_TQ = 256
FOX_TK = 256
FOX_N = 2 * FOX_TQ
FOX_ROWS = 2 * HEAD_DIM + 16


def _fox_kernel(q_ref, k_ref, v_ref, cq_ref, augk_ref, tri_ref, o_ref, vt_ref, m_ref, acc_ref):
    hp = pl.program_id(1)
    qb = pl.program_id(2)
    t = k_ref.shape[1]

    @pl.when(qb == 0)
    def _():
        for c in range(t // LANES):
            sl = slice(c * LANES, (c + 1) * LANES)
            vt_ref[0:LANES, sl] = v_ref[0, sl, :].astype(F32).T.astype(BF16)
        vt_ref[LANES:FOX_ROWS, :] = _ones_rows(t)

    qt = q_ref[0].astype(F32).T * (HEAD_DIM ** -0.5)
    row = _iota((LANES, FOX_TQ), 0)
    q_top = jnp.concatenate([jnp.where(row < HEAD_DIM, qt, 0.0), jnp.where(row >= HEAD_DIM, qt, 0.0)], axis=1)
    cq = cq_ref[0].astype(F32)
    hrow = row % FOX_HEADS
    q_aug = jnp.concatenate([jnp.where(hrow == 2 * hp, cq, 0.0), jnp.where(hrow == 2 * hp + 1, cq, 0.0)], axis=1)
    q = jnp.concatenate([q_top, q_aug], axis=0).astype(BF16)

    def step(kb, bias):
        k0 = pl.multiple_of(kb * FOX_TK, FOX_TK)
        kblk = jnp.concatenate([k_ref[0, pl.ds(k0, FOX_TK), :], augk_ref[0, pl.ds(k0, FOX_TK), :]], axis=1)
        _flash_step(kblk, vt_ref[:, pl.ds(k0, FOX_TK)], q, bias, m_ref, acc_ref)

    _flash_init(m_ref, acc_ref)

    def body(kb, carry):
        step(kb, None)
        return carry

    lax.fori_loop(0, qb, body, 0)
    step(qb, tri_ref[...])
    acc = acc_ref[...]
    den = acc[2 * HEAD_DIM:2 * HEAD_DIM + 1]
    o_a = acc[0:HEAD_DIM, 0:FOX_TQ] / den[:, 0:FOX_TQ]
    o_b = acc[HEAD_DIM:2 * HEAD_DIM, FOX_TQ:FOX_N] / den[:, FOX_TQ:FOX_N]
    o_ref[0] = jnp.concatenate([o_a, o_b], axis=0).T.astype(BF16)


def _fox(proj, cq, augk, tri):
    b, t, _ = proj.shape
    nq = t // FOX_TQ
    qf, kf, vf = COL_QF // LANES, COL_KF // LANES, COL_VF // LANES
    return pl.pallas_call(
        _fox_kernel,
        grid=(b, FOX_HEADS // 2, nq),
        in_specs=[
            pl.BlockSpec((1, FOX_TQ, LANES), lambda i, j, k: (i, k, qf + j)),
            pl.BlockSpec((1, t, LANES), lambda i, j, k: (i, 0, kf + j)),
            pl.BlockSpec((1, t, LANES), lambda i, j, k: (i, 0, vf + j)),
            pl.BlockSpec((1, LANES, FOX_TQ), lambda i, j, k: (i, 0, k)),
            pl.BlockSpec((1, t, LANES), lambda i, j, k: (i, 0, 0)),
            pl.BlockSpec(tri.shape, lambda i, j, k: (0, 0)),
        ],
        out_specs=pl.BlockSpec((1, FOX_TQ, LANES), lambda i, j, k: (i, k, j)),
        out_shape=jax.ShapeDtypeStruct((b, t, FOX_HEADS * HEAD_DIM), BF16),
        scratch_shapes=[
            pltpu.VMEM((FOX_ROWS, t), BF16),
            pltpu.VMEM((1, FOX_N), F32),
            pltpu.VMEM((FOX_ROWS, FOX_N), F32),
        ],
        compiler_params=pltpu.CompilerParams(
            dimension_semantics=("arbitrary", "arbitrary", "arbitrary"), vmem_limit_bytes=VMEM_LIMIT),
        name="fox_attention",
    )(proj, proj, proj, cq, augk, tri)


def _post_kernel(x_ref, on_ref, of_ref, mg_ref, wn_ref, wf_ref, wo_ref, gm_ref, wu_ref, wd_ref, gfin_ref, o_ref):
    d = x_ref.shape[1]
    ga = mg_ref[:, 0:d].astype(F32)
    gb = mg_ref[:, d:2 * d].astype(F32)
    y = ga * _dot(on_ref[...], wn_ref[...]) + gb * _dot(of_ref[...], wf_ref[...])
    h = x_ref[...] + _dot(y.astype(BF16), wo_ref[...])
    r = lax.rsqrt(jnp.mean(h * h, axis=-1, keepdims=True) + NORM_EPS)
    v = (h * r * gm_ref[...]).astype(BF16)
    acc = h
    ff = wu_ref.shape[1]
    for c0 in range(0, ff, 1024):
        up = jnp.maximum(_dot(v, wu_ref[:, c0:c0 + 1024]), 0.0)
        acc = acc + _dot((up * up).astype(BF16), wd_ref[c0:c0 + 1024, :])
    r2 = lax.rsqrt(jnp.mean(acc * acc, axis=-1, keepdims=True) + NORM_EPS)
    o_ref[...] = acc * r2 * gfin_ref[...]


def _post(x2, on, of, mg, wn, wf, wo, gm, wu, wd, gfin, tm):
    n, d = x2.shape
    row = lambda w: pl.BlockSpec((tm, w), lambda i: (i, 0))
    const = lambda a: pl.BlockSpec(a.shape, lambda i: (0, 0), pipeline_mode=pl.Buffered(1))
    return pl.pallas_call(
        _post_kernel,
        grid=(n // tm,),
        in_specs=[row(d), row(on.shape[1]), row(of.shape[1]), row(2 * d),
                  const(wn), const(wf), const(wo), const(gm), const(wu), const(wd), const(gfin)],
        out_specs=row(d),
        out_shape=jax.ShapeDtypeStruct((n, d), F32),
        compiler_params=pltpu.CompilerParams(
            dimension_semantics=("arbitrary",), vmem_limit_bytes=VMEM_LIMIT),
        name="post_mlp",
    )(x2, on, of, mg, wn, wf, wo, gm, wu, wd, gfin)


def _position_constants(t):
    pos = np.arange(t)
    augk = np.zeros((t, LANES), np.float32)
    augk[:, 0] = pos // 64
    augk[:, 1] = pos % 64
    augk[:, 2] = 1.0
    augk[:, 3] = 1.0
    augk[pos, 32 + pos // SEL_BLOCK] = 1.0

    nc = (t - CMP_BLOCK) // CMP_STRIDE + 1
    cpos = np.arange(LANES) * CMP_STRIDE + CMP_BLOCK - 1
    augc = np.zeros((LANES, LANES), np.float32)
    augc[:, 0] = cpos // 64
    augc[:, 1] = cpos % 64
    augc[:, 2] = 1.0
    augc[:, 3] = 1.0

    ns = t // SEL_BLOCK
    n = np.arange(LANES)[None, :]
    j = np.arange(ns)[:, None]
    ovt = ((n * CMP_STRIDE <= j * SEL_BLOCK + SEL_BLOCK - 1) & (n * CMP_STRIDE + CMP_BLOCK - 1 >= j * SEL_BLOCK) & (n < nc))

    def tri(tk, tq, off, reps, anti=False):
        kk = np.arange(tk)[:, None]
        tt = np.arange(tq)[None, :] + off
        ok = (kk > tt) if anti else (kk <= tt)
        return np.tile(np.where(ok, 0.0, NEG_BIG).astype(np.float32), (1, reps))

    trisel = np.stack([tri(SEL_TK, NSA_TQ, 0, NSA_HPG), tri(SEL_TK, NSA_TQ, NSA_TQ, NSA_HPG)])
    triwin = np.stack([tri(WIN_TK, NSA_TQ, 0, NSA_HPG), tri(WIN_TK, NSA_TQ, 0, NSA_HPG, anti=True)])
    trifox = tri(FOX_TK, FOX_TQ, 0, 2)
    return (jnp.asarray(augk, BF16), jnp.asarray(augc, BF16), jnp.asarray(ovt, BF16),
            jnp.asarray(trisel), jnp.asarray(triwin), jnp.asarray(trifox))


def kernel(x, norm_mix, w_in, cmp_pos_k, cmp_w1_k, cmp_w2_k, cmp_pos_v, cmp_w1_v, cmp_w2_v, fox_f_bias, w_branch_nsa, w_branch_fox, w_merge_gate, b_merge_gate, w_out, norm_mlp, w_up, w_down, norm_final):
    b, t, d = x.shape
    assert w_in.shape[0] == 1, "one layer: the final norm is fused into the MLP kernel"
    assert t % FOX_TQ == 0 and t // SEL_BLOCK == 32 and (t - CMP_BLOCK) // CMP_STRIDE + 1 == LANES - 1
    augk, augc, ovt, trisel, triwin, trifox = _position_constants(t)
    h = x.reshape(b * t, d)
    w = w_in[0]
    gate0 = COL_KV + 6 * LANES
    fox0 = gate0 + 3 * NSA_HEADS
    wa = jnp.concatenate([w[:, :gate0], w[:, fox0:fox0 + 3 * FOX_HEADS * HEAD_DIM]], axis=1).astype(BF16)
    wg = jnp.concatenate([w[:, gate0:fox0], w[:, -FOX_HEADS:],
                          jnp.zeros((d, LANES - 3 * NSA_HEADS - FOX_HEADS), F32)], axis=1).astype(BF16)
    proj, gf, mg = _inproj(h, norm_mix[0][None, :], wa, wg, w_merge_gate[0].astype(BF16),
                           b_merge_gate[0][None, :], ROW_TILE)
    proj = proj.reshape(b, t, PROJ_W)
    gf = gf.reshape(b, t, LANES)

    def blocks16(c0):
        a = proj[:, :, c0:c0 + LANES].reshape(b, t // CMP_STRIDE, CMP_STRIDE, NSA_GROUPS, HEAD_DIM)
        return a.transpose(0, 3, 1, 2, 4).reshape(b, NSA_GROUPS, t // CMP_STRIDE, CMP_STRIDE * HEAD_DIM)

    def w2_pair(w2):
        z = jnp.zeros_like(w2)
        return (jnp.concatenate([w2, z], axis=1).astype(BF16), jnp.concatenate([z, w2], axis=1).astype(BF16))

    def pos8(p):
        return jnp.concatenate([p.reshape(1, -1), jnp.zeros((7, p.size), F32)], axis=0).astype(BF16)

    w2k0, w2k1 = w2_pair(cmp_w2_k[0])
    w2v0, w2v1 = w2_pair(cmp_w2_v[0])
    kcmp, vcmp = _compress(blocks16(COL_KV), blocks16(COL_KV + LANES),
                           cmp_w1_k[0].astype(BF16), w2k0, w2k1, pos8(cmp_pos_k[0]),
                           cmp_w1_v[0].astype(BF16), w2v0, w2v1, pos8(cmp_pos_v[0]))
    o_n = _nsa(proj, gf, kcmp, vcmp, augk, augc, ovt, trisel, triwin)
    cq, faug = _foxprep(gf, fox_f_bias[0][:, None])
    o_f = _fox(proj, cq, faug, trifox)
    out = _post(h, o_n.reshape(b * t, -1), o_f.reshape(b * t, -1), mg,
                w_branch_nsa[0].astype(BF16), w_branch_fox[0].astype(BF16), w_out[0].astype(BF16),
                norm_mlp[0][None, :], w_up[0].astype(BF16), w_down[0].astype(BF16), norm_final[None, :], ROW_TILE)
    return out.reshape(b, t, d)
```

```python
import numpy as np
import jax
import jax.numpy as jnp
from jax import lax
from jax.experimental import pallas as pl
from jax.experimental.pallas import tpu as pltpu

HEAD_DIM = 64
NSA_HEADS = 8
NSA_GROUPS = 2
NSA_HPG = NSA_HEADS // NSA_GROUPS
FOX_HEADS = 8
CMP_BLOCK = 32
CMP_STRIDE = 16
CMP_HIDDEN = 128
SEL_BLOCK = 64
N_SELECT = 16
WINDOW = 512
NORM_EPS = 1e-6
FORCE_SCORE = 1e9

LANES = 128
NEG_BIG = -(2.0 ** 100)
M_INIT = -3.0e38
LOG2E = 1.4426950408889634
VMEM_LIMIT = 56 * 1024 * 1024
ROW_TILE = 512

PROJ_W = 2816
COL_QN = 0
COL_QF = 512
COL_KF = 1024
COL_VF = 1536
COL_KV = 2048
GF_GATE = 0
GF_FOX = 24

F32 = jnp.float32
BF16 = jnp.bfloat16


def _dot(a, b):
    return jnp.dot(a, b, preferred_element_type=F32)


def _iota(shape, dim):
    return lax.broadcasted_iota(jnp.int32, shape, dim)


def _split3(x):
    hi = x.astype(BF16)
    r1 = x - hi.astype(F32)
    mid = r1.astype(BF16)
    lo = (r1 - mid.astype(F32)).astype(BF16)
    return hi, mid, lo


def _inproj_kernel(x_ref, g_ref, wa_ref, wg_ref, wm_ref, bm_ref, proj_ref, gf_ref, mg_ref):
    x = x_ref[...]
    r = lax.rsqrt(jnp.mean(x * x, axis=-1, keepdims=True) + NORM_EPS)
    u = (x * r * g_ref[...]).astype(BF16)
    for c0, c1 in ((0, 1024), (1024, 2048), (2048, PROJ_W)):
        proj_ref[:, c0:c1] = _dot(u, wa_ref[:, c0:c1]).astype(BF16)
    gf_ref[...] = _dot(u, wg_ref[...])
    for c0 in (0, 1024):
        z = _dot(u, wm_ref[:, c0:c0 + 1024]) + bm_ref[:, c0:c0 + 1024]
        mg_ref[:, c0:c0 + 1024] = jax.nn.sigmoid(z).astype(BF16)


def _inproj(x2, g, wa, wg, wm, bm, tm):
    n, d = x2.shape
    const = lambda i: (0, 0)
    return pl.pallas_call(
        _inproj_kernel,
        grid=(n // tm,),
        in_specs=[
            pl.BlockSpec((tm, d), lambda i: (i, 0)),
            pl.BlockSpec((1, d), const),
            pl.BlockSpec(wa.shape, const),
            pl.BlockSpec(wg.shape, const),
            pl.BlockSpec(wm.shape, const),
            pl.BlockSpec(bm.shape, const),
        ],
        out_specs=[
            pl.BlockSpec((tm, PROJ_W), lambda i: (i, 0)),
            pl.BlockSpec((tm, LANES), lambda i: (i, 0)),
            pl.BlockSpec((tm, 2 * d), lambda i: (i, 0)),
        ],
        out_shape=[
            jax.ShapeDtypeStruct((n, PROJ_W), BF16),
            jax.ShapeDtypeStruct((n, LANES), F32),
            jax.ShapeDtypeStruct((n, 2 * d), BF16),
        ],
        compiler_params=pltpu.CompilerParams(
            dimension_semantics=("arbitrary",), vmem_limit_bytes=VMEM_LIMIT),
        name="inproj",
    )(x2, g, wa, wg, wm, bm)


def _compress_kernel(rk_ref, rv_ref, w1k_ref, w2k0_ref, w2k1_ref, pk_ref,
                     w1v_ref, w2v0_ref, w2v1_ref, pv_ref, kc_ref, vc_ref):
    half = CMP_STRIDE * HEAD_DIM

    def one(r_ref, w1_ref, w2_refs, p_ref, o_ref):
        posb = _dot(p_ref[...], w1_ref[...])[0:1]
        acc = jnp.zeros((LANES, LANES), F32)
        for g in range(NSA_GROUPS):
            r = r_ref[0, g]
            a = _dot(r, w1_ref[0:half, :])
            b = _dot(r, w1_ref[half:2 * half, :])
            h = a + pltpu.roll(b, LANES - 1, 0) + posb
            acc = acc + _dot(jax.nn.gelu(h).astype(BF16), w2_refs[g][...])
        o_ref[0] = acc.astype(BF16)

    one(rk_ref, w1k_ref, (w2k0_ref, w2k1_ref), pk_ref, kc_ref)
    one(rv_ref, w1v_ref, (w2v0_ref, w2v1_ref), pv_ref, vc_ref)


def _compress(rk, rv, w1k, w2k0, w2k1, pk, w1v, w2v0, w2v1, pv):
    b = rk.shape[0]
    rspec = pl.BlockSpec((1,) + rk.shape[1:], lambda i: (i, 0, 0, 0))
    c2 = lambda a: pl.BlockSpec(a.shape, lambda i: (0, 0))
    ospec = pl.BlockSpec((1, LANES, LANES), lambda i: (i, 0, 0))
    return pl.pallas_call(
        _compress_kernel,
        grid=(b,),
        in_specs=[rspec, rspec, c2(w1k), c2(w2k0), c2(w2k1), c2(pk), c2(w1v), c2(w2v0), c2(w2v1), c2(pv)],
        out_specs=[ospec, ospec],
        out_shape=[jax.ShapeDtypeStruct((b, LANES, LANES), BF16)] * 2,
        compiler_params=pltpu.CompilerParams(dimension_semantics=("arbitrary",)),
        name="compress",
    )(rk, rv, w1k, w2k0, w2k1, pk, w1v, w2v0, w2v1, pv)


def _flash_init(m_ref, acc_ref):
    m_ref[...] = jnp.full(m_ref.shape, M_INIT, F32)
    acc_ref[...] = jnp.zeros(acc_ref.shape, F32)


def _flash_steps(kblks, vt_blks, qs, bias, m_refs, acc_refs):
    scores = [_dot(kblk, q) for kblk, q in zip(kblks, qs)]
    ps, alphas = [], []
    for s, m_ref in zip(scores, m_refs):
        if bias is not None:
            s = s + bias
        m_old = m_ref[...]
        m_new = jnp.maximum(m_old, jnp.max(s, axis=0, keepdims=True))
        ps.append(jnp.exp2(s - m_new).astype(BF16))
        alphas.append(jnp.exp2(m_old - m_new))
        m_ref[...] = m_new
    for vt_blk, p, alpha, acc_ref in zip(vt_blks, ps, alphas, acc_refs):
        acc_ref[...] = acc_ref[...] * alpha + _dot(vt_blk, p)


def _ones_rows(tk):
    return jnp.where(_iota((16, tk), 0) == 0, 1.0, 0.0).astype(BF16)


NSA_TQ = 128
NSA_N = NSA_HPG * NSA_TQ
SEL_TK = 512
WIN_KEYS = WINDOW + NSA_TQ

AUG_POS = 0
AUG_TQ = 6
AUG_PAD = 9
AUG_MASK = 32


def _bf16_terms(x, n=3):
    out, r = [], np.float32(x)
    for _ in range(n):
        term = np.asarray(r, np.float32).astype(BF16).astype(np.float32)
        out.append(float(term))
        r = np.float32(r - term)
    return out


def _nsa_kernel(q_ref, gf_ref, kc_ref, vc_ref, ks_ref, vs_ref, kw_ref, vw_ref,
                augk_ref, augw_ref, augc_ref, ovt_ref, trisel_ref, triwin_ref, o_ref,
                vst_ref, vwt_ref, kwp_ref, vct_ref, gt_ref, m_ref, acc_ref):
    qb = pl.program_id(1)
    t = ks_ref.shape[1]
    t0 = qb * NSA_TQ
    groups = range(NSA_GROUPS)

    @pl.when(qb == 0)
    def _():
        vwt_ref[:, 0:WINDOW] = jnp.zeros((LANES, WINDOW), BF16)
        kwp_ref[0:WINDOW, :] = jnp.zeros((WINDOW, LANES), BF16)
        kwp_ref[WINDOW:WINDOW + t, :] = kw_ref[0]
        for c in range(t // LANES):
            sl = slice(c * LANES, (c + 1) * LANES)
            vst_ref[:, sl] = vs_ref[0, sl, :].astype(F32).T.astype(BF16)
            vwt_ref[:, WINDOW + c * LANES:WINDOW + (c + 1) * LANES] = vw_ref[0, sl, :].astype(F32).T.astype(BF16)
        vct_ref[...] = vc_ref[0].astype(F32).T.astype(BF16)

    qt = q_ref[0].astype(F32).T * (HEAD_DIM ** -0.5 * LOG2E)
    lane = _iota((1, NSA_N), 1)
    hl = lane // NSA_TQ
    tq = t0 + lane % NSA_TQ
    tqf = tq.astype(F32)
    ridx = _iota((16, NSA_N), 0)
    zeros_h = jnp.zeros((HEAD_DIM, NSA_TQ), F32)
    l2e = _bf16_terms(LOG2E)
    q_top, aug16 = [], []
    for g in groups:
        tops = []
        for h in range(NSA_HPG):
            head = NSA_HPG * g + h
            qh = qt[head * HEAD_DIM:(head + 1) * HEAD_DIM]
            tops.append(jnp.concatenate([qh, zeros_h] if g == 0 else [zeros_h, qh], axis=0))
        q_top.append(jnp.concatenate(tops, axis=1))
        slope = lax.bitcast_convert_type((127 - (NSA_HPG * g + hl + 1)) << 23, F32)
        rows = {}
        for i, term in enumerate(l2e):
            rows[AUG_POS + 2 * i] = slope * (64.0 * term)
            rows[AUG_POS + 2 * i + 1] = slope * term
        for i, term in enumerate(_split3(-(slope * tqf) * LOG2E)):
            rows[AUG_TQ + i] = term.astype(F32)
        a = jnp.zeros((16, NSA_N), F32)
        for r, val in rows.items():
            a = jnp.where(ridx == r, val, a)
        aug16.append(a)
    zeros_tail = jnp.zeros((LANES - 16, NSA_N), F32)
    q_plain = [jnp.concatenate([q_top[g], aug16[g], zeros_tail], axis=0).astype(BF16) for g in groups]
    q_win = [jnp.concatenate([q_top[g], jnp.where(ridx == AUG_PAD, NEG_BIG, aug16[g]), zeros_tail], axis=0).astype(BF16)
             for g in groups]

    kcb = jnp.concatenate([kc_ref[0], augc_ref[...]], axis=1)
    n_idx = _iota((LANES, NSA_N), 0)
    valid = (n_idx * CMP_STRIDE + (CMP_BLOCK - 1) <= tq) & (n_idx < LANES - 1)
    ovt = ovt_ref[...]
    ns = ovt.shape[0]
    j_idx = _iota((ns, NSA_TQ), 0)
    tcol = t0 + _iota((ns, NSA_TQ), 1)
    cur = tcol // SEL_BLOCK
    forced = (j_idx == 0) | (j_idx == cur) | (j_idx == cur - 1)
    in_past = j_idx * SEL_BLOCK <= tcol
    o_cmp, q_sel = [], []
    for g in groups:
        sm = jnp.where(valid, _dot(kcb, q_plain[g]), NEG_BIG)
        mx = jnp.max(sm, axis=0, keepdims=True)
        e = jnp.where(valid, jnp.exp2(sm - mx), 0.0)
        den = jnp.sum(e, axis=0, keepdims=True)
        pn = e / jnp.where(den > 0.0, den, 1.0)
        o_cmp.append(_dot(vct_ref[g * HEAD_DIM:(g + 1) * HEAD_DIM, :], pn.astype(BF16)))

        psum = pn[:, 0:NSA_TQ]
        for h in range(1, NSA_HPG):
            psum = psum + pn[:, h * NSA_TQ:(h + 1) * NSA_TQ]
        hi, mid, lo = _split3(psum)
        imp = _dot(ovt, hi) + _dot(ovt, mid) + _dot(ovt, lo)
        score = jnp.where(forced, FORCE_SCORE, jnp.where(in_past, imp, -1.0))
        rank = jnp.zeros((ns, NSA_TQ), jnp.int32)
        for i in range(ns):
            si = score[i:i + 1, :]
            beats = (si > score) | ((si == score) & (j_idx > i))
            rank = rank + beats.astype(jnp.int32)
        maskval = jnp.where((rank < min(N_SELECT, ns)) & in_past, 0.0, NEG_BIG)
        mask4 = jnp.concatenate([maskval] * NSA_HPG, axis=1)
        q_sel.append(jnp.concatenate([q_top[g], aug16[g], jnp.zeros((AUG_MASK - 16, NSA_N), F32), mask4,
                                      jnp.zeros((LANES - AUG_MASK - ns, NSA_N), F32)], axis=0).astype(BF16))

    ones_s = _ones_rows(SEL_TK)

    def sel_step(cb, bias):
        k0 = pl.multiple_of(cb * SEL_TK, SEL_TK)
        kblk = jnp.concatenate([ks_ref[0, pl.ds(k0, SEL_TK), :], augk_ref[pl.ds(k0, SEL_TK), :]], axis=1)
        vts = [jnp.concatenate([vst_ref[g * HEAD_DIM:(g + 1) * HEAD_DIM, pl.ds(k0, SEL_TK)], ones_s], axis=0)
               for g in groups]
        _flash_steps([kblk] * NSA_GROUPS, vts, q_sel, bias,
                     [m_ref.at[g] for g in groups], [acc_ref.at[g] for g in groups])

    for g in groups:
        _flash_init(m_ref.at[g], acc_ref.at[g])
    per_chunk = SEL_TK // NSA_TQ
    n_full = qb // per_chunk

    def sel_body(cb, carry):
        sel_step(cb, None)
        return carry

    lax.fori_loop(0, n_full, sel_body, 0)
    diag = trisel_ref[qb % per_chunk]
    sel_step(n_full, jnp.concatenate([diag] * NSA_HPG, axis=1))
    o_sel = []
    for g in groups:
        acc = acc_ref[g]
        o_sel.append(acc[0:HEAD_DIM] / acc[HEAD_DIM:HEAD_DIM + 1])

    k0 = pl.multiple_of(t0, NSA_TQ)
    kwin = jnp.concatenate([kwp_ref[pl.ds(k0, WIN_KEYS), :], augw_ref[pl.ds(k0, WIN_KEYS), :]], axis=1)
    ones_w = _ones_rows(WIN_KEYS)
    lo_edge = jnp.concatenate([triwin_ref[1]] * NSA_HPG, axis=1)
    hi_edge = jnp.concatenate([triwin_ref[0]] * NSA_HPG, axis=1)
    o_win = []
    for g in groups:
        s = _dot(kwin, q_win[g])
        s = jnp.concatenate([s[0:NSA_TQ] + lo_edge, s[NSA_TQ:WINDOW], s[WINDOW:WIN_KEYS] + hi_edge], axis=0)
        mx = jnp.max(s, axis=0, keepdims=True)
        p = jnp.exp2(s - mx).astype(BF16)
        vt = jnp.concatenate([vwt_ref[g * HEAD_DIM:(g + 1) * HEAD_DIM, pl.ds(k0, WIN_KEYS)], ones_w], axis=0)
        acc = _dot(vt, p)
        o_win.append(acc[0:HEAD_DIM] / acc[HEAD_DIM:HEAD_DIM + 1])

    gt_ref[...] = gf_ref[0].T
    sig = jax.nn.sigmoid(gt_ref[GF_GATE:GF_GATE + 3 * NSA_HEADS, :])
    outs = []
    for g in groups:
        for h in range(NSA_HPG):
            sl = slice(h * NSA_TQ, (h + 1) * NSA_TQ)
            head = NSA_HPG * g + h
            g0 = sig[head:head + 1]
            g1 = sig[NSA_HEADS + head:NSA_HEADS + head + 1]
            g2 = sig[2 * NSA_HEADS + head:2 * NSA_HEADS + head + 1]
            outs.append(g0 * o_cmp[g][:, sl] + g1 * o_sel[g][:, sl] + g2 * o_win[g][:, sl])
    o_ref[0] = jnp.concatenate(outs, axis=0).T.astype(BF16)


def _nsa(proj, gf, kcmp, vcmp, augk, augw, augc, ovt, trisel, triwin):
    b, t, _ = proj.shape
    nq = t // NSA_TQ
    kvb = COL_KV // LANES
    width = NSA_HEADS * HEAD_DIM
    whole = lambda a: pl.BlockSpec(a.shape, lambda i, k: (0,) * a.ndim)
    kv = lambda idx: pl.BlockSpec((1, t, LANES), lambda i, k: (i, 0, idx))
    return pl.pallas_call(
        _nsa_kernel,
        grid=(b, nq),
        in_specs=[
            pl.BlockSpec((1, NSA_TQ, width), lambda i, k: (i, k, COL_QN // width)),
            pl.BlockSpec((1, NSA_TQ, LANES), lambda i, k: (i, k, 0)),
            pl.BlockSpec((1, LANES, LANES), lambda i, k: (i, 0, 0)),
            pl.BlockSpec((1, LANES, LANES), lambda i, k: (i, 0, 0)),
            kv(kvb + 2), kv(kvb + 3), kv(kvb + 4), kv(kvb + 5),
            whole(augk), whole(augw), whole(augc), whole(ovt), whole(trisel), whole(triwin),
        ],
        out_specs=pl.BlockSpec((1, NSA_TQ, width), lambda i, k: (i, k, 0)),
        out_shape=jax.ShapeDtypeStruct((b, t, width), BF16),
        scratch_shapes=[
            pltpu.VMEM((LANES, t), BF16),
            pltpu.VMEM((LANES, t + WINDOW), BF16),
            pltpu.VMEM((t + WINDOW, LANES), BF16),
            pltpu.VMEM((LANES, LANES), BF16),
            pltpu.VMEM((LANES, LANES), F32),
            pltpu.VMEM((NSA_GROUPS, 1, NSA_N), F32),
            pltpu.VMEM((NSA_GROUPS, HEAD_DIM + 16, NSA_N), F32),
        ],
        compiler_params=pltpu.CompilerParams(
            dimension_semantics=("arbitrary", "arbitrary"), vmem_limit_bytes=VMEM_LIMIT),
        name="nsa_attention",
    )(proj, gf, kcmp, vcmp, proj, proj, proj, proj, augk, augw, augc, ovt, trisel, triwin)


FOX_CBLK = 256


def _foxprep_kernel(gf_ref, bias_ref, cq_ref, augk_ref):
    t = gf_ref.shape[1]
    cols = []
    for c in range(t // LANES):
        cols.append(gf_ref[0, c * LANES:(c + 1) * LANES, :].T[GF_FOX:GF_FOX + FOX_HEADS])
    f = jnp.concatenate(cols, axis=1) + bias_ref[...]
    ls = jnp.minimum(f, 0.0) - jnp.log(1.0 + jnp.exp(-jnp.abs(f)))
    upper = (_iota((FOX_CBLK, FOX_CBLK), 0) <= _iota((FOX_CBLK, FOX_CBLK), 1)).astype(BF16)
    carry = jnp.zeros((FOX_HEADS, 1), F32)
    parts = []
    for c in range(t // FOX_CBLK):
        hi, mid, lo = _split3(ls[:, c * FOX_CBLK:(c + 1) * FOX_CBLK])
        cb = _dot(hi, upper) + _dot(mid, upper) + _dot(lo, upper) + carry
        carry = cb[:, FOX_CBLK - 1:FOX_CBLK]
        parts.append(cb)
    csum = jnp.concatenate(parts, axis=1) * LOG2E
    c1, c2, c3 = (v.astype(F32) for v in _split3(csum))
    ones = jnp.ones((3 * FOX_HEADS, t), F32)
    zeros = jnp.zeros((LANES - 6 * FOX_HEADS, t), F32)
    cq_ref[0] = jnp.concatenate([c1, c2, c3, ones, zeros], axis=0).astype(BF16)
    slab_k = jnp.concatenate([ones, -c1, -c2, -c3, zeros], axis=0)
    for c in range(t // LANES):
        sl = slice(c * LANES, (c + 1) * LANES)
        augk_ref[0, sl, :] = slab_k[:, sl].T.astype(BF16)


def _foxprep(gf, bias):
    b, t, _ = gf.shape
    return pl.pallas_call(
        _foxprep_kernel,
        grid=(b,),
        in_specs=[pl.BlockSpec((1, t, LANES), lambda i: (i, 0, 0)),
                  pl.BlockSpec(bias.shape, lambda i: (0, 0))],
        out_specs=[pl.BlockSpec((1, LANES, t), lambda i: (i, 0, 0)),
                   pl.BlockSpec((1, t, LANES), lambda i: (i, 0, 0))],
        out_shape=[jax.ShapeDtypeStruct((b, LANES, t), BF16),
                   jax.ShapeDtypeStruct((b, t, LANES), BF16)],
        compiler_params=pltpu.CompilerParams(dimension_semantics=("arbitrary",)),
        name="fox_prep",
    )(gf, bias)


FOX_TQ = 256
FOX_TK = 256
FOX_NP = FOX_HEADS // 2
FOX_N = 2 * FOX_TQ
FOX_ROWS = 2 * HEAD_DIM + 16


def _fox_kernel(q_ref, k_ref, v_ref, cq_ref, augk_ref, tri_ref, o_ref, vt_ref, m_ref, acc_ref):
    qb = pl.program_id(1)
    t = k_ref.shape[1]

    @pl.when(qb == 0)
    def _():
        for p in range(FOX_NP):
            for c in range(t // LANES):
                sl = slice(c * LANES, (c + 1) * LANES)
                vt_ref[p, 0:LANES, sl] = v_ref[0, sl, p * LANES:(p + 1) * LANES].astype(F32).T.astype(BF16)
            vt_ref[p, LANES:FOX_ROWS, :] = _ones_rows(t)

    row = _iota((LANES, FOX_TQ), 0)
    hrow = row % FOX_HEADS
    cq = cq_ref[0].astype(F32)
    qs = []
    for p in range(FOX_NP):
        qt = q_ref[0, :, p * LANES:(p + 1) * LANES].astype(F32).T * (HEAD_DIM ** -0.5 * LOG2E)
        q_top = jnp.concatenate([jnp.where(row < HEAD_DIM, qt, 0.0), jnp.where(row >= HEAD_DIM, qt, 0.0)], axis=1)
        q_aug = jnp.concatenate([jnp.where(hrow == 2 * p, cq, 0.0), jnp.where(hrow == 2 * p + 1, cq, 0.0)], axis=1)
        qs.append(jnp.concatenate([q_top, q_aug], axis=0).astype(BF16))

    def step(kb, bias):
        k0 = pl.multiple_of(kb * FOX_TK, FOX_TK)
        ak = augk_ref[0, pl.ds(k0, FOX_TK), :]
        pairs = range(FOX_NP)
        kblks = [jnp.concatenate([k_ref[0, pl.ds(k0, FOX_TK), p * LANES:(p + 1) * LANES], ak], axis=1) for p in pairs]
        _flash_steps(kblks, [vt_ref[p, :, pl.ds(k0, FOX_TK)] for p in pairs], qs, bias,
                     [m_ref.at[p] for p in pairs], [acc_ref.at[p] for p in pairs])

    for p in range(FOX_NP):
        _flash_init(m_ref.at[p], acc_ref.at[p])

    def body(kb, carry):
        step(kb, None)
        return carry

    lax.fori_loop(0, qb, body, 0)
    step(qb, tri_ref[...])
    for p in range(FOX_NP):
        acc = acc_ref[p]
        den = acc[2 * HEAD_DIM:2 * HEAD_DIM + 1]
        o_a = acc[0:HEAD_DIM, 0:FOX_TQ] / den[:, 0:FOX_TQ]
        o_b = acc[HEAD_DIM:2 * HEAD_DIM, FOX_TQ:FOX_N] / den[:, FOX_TQ:FOX_N]
        o_ref[0, :, p * LANES:(p + 1) * LANES] = jnp.concatenate([o_a, o_b], axis=0).T.astype(BF16)


def _fox(proj, cq, augk, tri):
    b, t, _ = proj.shape
    nq = t // FOX_TQ
    width = FOX_HEADS * HEAD_DIM
    qf, kf, vf = COL_QF // width, COL_KF // width, COL_VF // width
    return pl.pallas_call(
        _fox_kernel,
        grid=(b, nq),
        in_specs=[
            pl.BlockSpec((1, FOX_TQ, width), lambda i, k: (i, k, qf)),
            pl.BlockSpec((1, t, width), lambda i, k: (i, 0, kf)),
            pl.BlockSpec((1, t, width), lambda i, k: (i, 0, vf)),
            pl.BlockSpec((1, LANES, FOX_TQ), lambda i, k: (i, 0, k)),
            pl.BlockSpec((1, t, LANES), lambda i, k: (i, 0, 0)),
            pl.BlockSpec(tri.shape, lambda i, k: (0, 0)),
        ],
        out_specs=pl.BlockSpec((1, FOX_TQ, width), lambda i, k: (i, k, 0)),
        out_shape=jax.ShapeDtypeStruct((b, t, width), BF16),
        scratch_shapes=[
            pltpu.VMEM((FOX_NP, FOX_ROWS, t), BF16),
            pltpu.VMEM((FOX_NP, 1, FOX_N), F32),
            pltpu.VMEM((FOX_NP, FOX_ROWS, FOX_N), F32),
        ],
        compiler_params=pltpu.CompilerParams(
            dimension_semantics=("arbitrary", "arbitrary"), vmem_limit_bytes=VMEM_LIMIT),
        name="fox_attention",
    )(proj, proj, proj, cq, augk, tri)


def _post_kernel(x_ref, on_ref, of_ref, mg_ref, wn_ref, wf_ref, wo_ref, gm_ref, wu_ref, wd_ref, gfin_ref, o_ref):
    d = x_ref.shape[1]
    ga = mg_ref[:, 0:d].astype(F32)
    gb = mg_ref[:, d:2 * d].astype(F32)
    y = ga * _dot(on_ref[...], wn_ref[...]) + gb * _dot(of_ref[...], wf_ref[...])
    h = x_ref[...] + _dot(y.astype(BF16), wo_ref[...])
    r = lax.rsqrt(jnp.mean(h * h, axis=-1, keepdims=True) + NORM_EPS)
    v = (h * r * gm_ref[...]).astype(BF16)
    acc = h
    ff = wu_ref.shape[1]
    for c0 in range(0, ff, 1024):
        up = jnp.maximum(_dot(v, wu_ref[:, c0:c0 + 1024]), 0.0)
        acc = acc + _dot((up * up).astype(BF16), wd_ref[c0:c0 + 1024, :])
    r2 = lax.rsqrt(jnp.mean(acc * acc, axis=-1, keepdims=True) + NORM_EPS)
    o_ref[...] = acc * r2 * gfin_ref[...]


def _post(x2, on, of, mg, wn, wf, wo, gm, wu, wd, gfin, tm):
    n, d = x2.shape
    row = lambda w: pl.BlockSpec((tm, w), lambda i: (i, 0))
    const = lambda a: pl.BlockSpec(a.shape, lambda i: (0, 0), pipeline_mode=pl.Buffered(1))
    return pl.pallas_call(
        _post_kernel,
        grid=(n // tm,),
        in_specs=[row(d), row(on.shape[1]), row(of.shape[1]), row(2 * d),
                  const(wn), const(wf), const(wo), const(gm), const(wu), const(wd), const(gfin)],
        out_specs=row(d),
        out_shape=jax.ShapeDtypeStruct((n, d), F32),
        compiler_params=pltpu.CompilerParams(
            dimension_semantics=("arbitrary",), vmem_limit_bytes=VMEM_LIMIT),
        name="post_mlp",
    )(x2, on, of, mg, wn, wf, wo, gm, wu, wd, gfin)


def _position_constants(t):
    def key_side(pos, pad):
        a = np.zeros((pos.size, LANES), np.float32)
        for i in range(3):
            a[:, AUG_POS + 2 * i] = pos // 64
            a[:, AUG_POS + 2 * i + 1] = pos % 64
        a[:, AUG_TQ:AUG_TQ + 3] = 1.0
        a[:, AUG_PAD] = pad
        return a

    pos = np.arange(t)
    augk = key_side(pos, 0.0)
    augk[pos, AUG_MASK + pos // SEL_BLOCK] = 1.0
    augw = np.concatenate([key_side(np.zeros(WINDOW, np.int64), 1.0), key_side(pos, 0.0)], axis=0)
    augc = key_side(np.arange(LANES) * CMP_STRIDE + CMP_BLOCK - 1, 0.0)

    nc = (t - CMP_BLOCK) // CMP_STRIDE + 1
    ns = t // SEL_BLOCK
    n = np.arange(LANES)[None, :]
    j = np.arange(ns)[:, None]
    ovt = ((n * CMP_STRIDE <= j * SEL_BLOCK + SEL_BLOCK - 1) & (n * CMP_STRIDE + CMP_BLOCK - 1 >= j * SEL_BLOCK) & (n < nc))

    def tri(tk, tq, anti=False):
        kk = np.arange(tk)[:, None]
        tt = np.arange(tq)[None, :]
        ok = (kk > tt) if anti else (kk <= tt)
        return np.where(ok, 0.0, NEG_BIG).astype(np.float32)

    edge = tri(NSA_TQ, NSA_TQ)
    trisel = np.zeros((SEL_TK // NSA_TQ, SEL_TK, NSA_TQ), np.float32)
    for v in range(SEL_TK // NSA_TQ):
        trisel[v, v * NSA_TQ:(v + 1) * NSA_TQ] = edge
    triwin = np.stack([edge, tri(NSA_TQ, NSA_TQ, anti=True)])
    trifox = np.tile(tri(FOX_TK, FOX_TQ), (1, 2))
    return (jnp.asarray(augk, BF16), jnp.asarray(augw, BF16), jnp.asarray(augc, BF16), jnp.asarray(ovt, BF16),
            jnp.asarray(trisel), jnp.asarray(triwin), jnp.asarray(trifox))


def kernel(x, norm_mix, w_in, cmp_pos_k, cmp_w1_k, cmp_w2_k, cmp_pos_v, cmp_w1_v, cmp_w2_v, fox_f_bias, w_branch_nsa, w_branch_fox, w_merge_gate, b_merge_gate, w_out, norm_mlp, w_up, w_down, norm_final):
    b, t, d = x.shape
    assert w_in.shape[0] == 1, "one layer: the final norm is fused into the MLP kernel"
    assert t % FOX_TQ == 0 and t // SEL_BLOCK == 32 and (t - CMP_BLOCK) // CMP_STRIDE + 1 == LANES - 1
    augk, augw, augc, ovt, trisel, triwin, trifox = _position_constants(t)
    h = x.reshape(b * t, d)
    w = w_in[0]
    nq_w = NSA_HEADS * HEAD_DIM
    gate0 = nq_w + 6 * LANES
    fox0 = gate0 + 3 * NSA_HEADS
    wa = jnp.concatenate([w[:, :nq_w], w[:, fox0:fox0 + 3 * FOX_HEADS * HEAD_DIM], w[:, nq_w:gate0]], axis=1).astype(BF16)
    wg = jnp.concatenate([w[:, gate0:fox0], w[:, -FOX_HEADS:],
                          jnp.zeros((d, LANES - 3 * NSA_HEADS - FOX_HEADS), F32)], axis=1).astype(BF16)
    proj, gf, mg = _inproj(h, norm_mix[0][None, :], wa, wg, w_merge_gate[0].astype(BF16),
                           b_merge_gate[0][None, :], ROW_TILE)
    proj = proj.reshape(b, t, PROJ_W)
    gf = gf.reshape(b, t, LANES)

    def blocks16(c0):
        a = proj[:, :, c0:c0 + LANES].reshape(b, t // CMP_STRIDE, CMP_STRIDE, NSA_GROUPS, HEAD_DIM)
        return a.transpose(0, 3, 1, 2, 4).reshape(b, NSA_GROUPS, t // CMP_STRIDE, CMP_STRIDE * HEAD_DIM)

    def w2_pair(w2):
        z = jnp.zeros_like(w2)
        return (jnp.concatenate([w2, z], axis=1).astype(BF16), jnp.concatenate([z, w2], axis=1).astype(BF16))

    def pos8(p):
        return jnp.concatenate([p.reshape(1, -1), jnp.zeros((7, p.size), F32)], axis=0).astype(BF16)

    w2k0, w2k1 = w2_pair(cmp_w2_k[0])
    w2v0, w2v1 = w2_pair(cmp_w2_v[0])
    kcmp, vcmp = _compress(blocks16(COL_KV), blocks16(COL_KV + LANES),
                           cmp_w1_k[0].astype(BF16), w2k0, w2k1, pos8(cmp_pos_k[0]),
                           cmp_w1_v[0].astype(BF16), w2v0, w2v1, pos8(cmp_pos_v[0]))
    o_n = _nsa(proj, gf, kcmp, vcmp, augk, augw, augc, ovt, trisel, triwin)
    cq, faug = _foxprep(gf, fox_f_bias[0][:, None])
    o_f = _fox(proj, cq, faug, trifox)
    out = _post(h, o_n.reshape(b * t, -1), o_f.reshape(b * t, -1), mg,
                w_branch_nsa[0].astype(BF16), w_branch_fox[0].astype(BF16), w_out[0].astype(BF16),
                norm_mlp[0][None, :], w_up[0].astype(BF16), w_down[0].astype(BF16), norm_final[None, :], ROW_TILE)
    return out.reshape(b, t, d)
```

```python
import numpy as np
import jax
import jax.numpy as jnp
from jax import lax
from jax.experimental import pallas as pl
from jax.experimental.pallas import tpu as pltpu

HEAD_DIM = 64
NSA_HEADS = 8
NSA_GROUPS = 2
NSA_HPG = NSA_HEADS // NSA_GROUPS
FOX_HEADS = 8
CMP_BLOCK = 32
CMP_STRIDE = 16
CMP_HIDDEN = 128
SEL_BLOCK = 64
N_SELECT = 16
WINDOW = 512
NORM_EPS = 1e-6
FORCE_SCORE = 1e9

LANES = 128
NEG_BIG = -(2.0 ** 100)
M_INIT = -3.0e38
LOG2E = 1.4426950408889634
VMEM_LIMIT = 56 * 1024 * 1024
ROW_TILE = 512

PROJ_W = 2816
COL_QN = 0
COL_QF = 512
COL_KF = 1024
COL_VF = 1536
COL_KV = 2048
GF_GATE = 0
GF_FOX = 24

F32 = jnp.float32
BF16 = jnp.bfloat16


def _dot(a, b):
    return jnp.dot(a, b, preferred_element_type=F32)


def _iota(shape, dim):
    return lax.broadcasted_iota(jnp.int32, shape, dim)


def _split3(x):
    hi = x.astype(BF16)
    r1 = x - hi.astype(F32)
    mid = r1.astype(BF16)
    lo = (r1 - mid.astype(F32)).astype(BF16)
    return hi, mid, lo


def _inproj_kernel(x_ref, g_ref, wa_ref, wg_ref, wm_ref, bm_ref, proj_ref, gf_ref, mg_ref):
    x = x_ref[...]
    r = lax.rsqrt(jnp.mean(x * x, axis=-1, keepdims=True) + NORM_EPS)
    u = (x * r * g_ref[...]).astype(BF16)
    for c0, c1 in ((0, 1024), (1024, 2048), (2048, PROJ_W)):
        proj_ref[:, c0:c1] = _dot(u, wa_ref[:, c0:c1]).astype(BF16)
    gf_ref[...] = _dot(u, wg_ref[...])
    for c0 in (0, 1024):
        z = _dot(u, wm_ref[:, c0:c0 + 1024]) + bm_ref[:, c0:c0 + 1024]
        mg_ref[:, c0:c0 + 1024] = jax.nn.sigmoid(z).astype(BF16)


def _inproj(x2, g, wa, wg, wm, bm, tm):
    n, d = x2.shape
    const = lambda i: (0, 0)
    return pl.pallas_call(
        _inproj_kernel,
        grid=(n // tm,),
        in_specs=[
            pl.BlockSpec((tm, d), lambda i: (i, 0)),
            pl.BlockSpec((1, d), const),
            pl.BlockSpec(wa.shape, const),
            pl.BlockSpec(wg.shape, const),
            pl.BlockSpec(wm.shape, const),
            pl.BlockSpec(bm.shape, const),
        ],
        out_specs=[
            pl.BlockSpec((tm, PROJ_W), lambda i: (i, 0)),
            pl.BlockSpec((tm, LANES), lambda i: (i, 0)),
            pl.BlockSpec((tm, 2 * d), lambda i: (i, 0)),
        ],
        out_shape=[
            jax.ShapeDtypeStruct((n, PROJ_W), BF16),
            jax.ShapeDtypeStruct((n, LANES), F32),
            jax.ShapeDtypeStruct((n, 2 * d), BF16),
        ],
        compiler_params=pltpu.CompilerParams(
            dimension_semantics=("arbitrary",), vmem_limit_bytes=VMEM_LIMIT),
        name="inproj",
    )(x2, g, wa, wg, wm, bm)


def _compress_kernel(rk_ref, rv_ref, w1k_ref, w2k0_ref, w2k1_ref, pk_ref,
                     w1v_ref, w2v0_ref, w2v1_ref, pv_ref, kc_ref, vc_ref):
    half = CMP_STRIDE * HEAD_DIM

    def one(r_ref, w1_ref, w2_refs, p_ref, o_ref):
        posb = _dot(p_ref[...], w1_ref[...])[0:1]
        acc = jnp.zeros((LANES, LANES), F32)
        for g in range(NSA_GROUPS):
            r = r_ref[0, g]
            a = _dot(r, w1_ref[0:half, :])
            b = _dot(r, w1_ref[half:2 * half, :])
            h = a + pltpu.roll(b, LANES - 1, 0) + posb
            acc = acc + _dot(jax.nn.gelu(h).astype(BF16), w2_refs[g][...])
        o_ref[0] = acc.astype(BF16)

    one(rk_ref, w1k_ref, (w2k0_ref, w2k1_ref), pk_ref, kc_ref)
    one(rv_ref, w1v_ref, (w2v0_ref, w2v1_ref), pv_ref, vc_ref)


def _compress(rk, rv, w1k, w2k0, w2k1, pk, w1v, w2v0, w2v1, pv):
    b = rk.shape[0]
    rspec = pl.BlockSpec((1,) + rk.shape[1:], lambda i: (i, 0, 0, 0))
    c2 = lambda a: pl.BlockSpec(a.shape, lambda i: (0, 0))
    ospec = pl.BlockSpec((1, LANES, LANES), lambda i: (i, 0, 0))
    return pl.pallas_call(
        _compress_kernel,
        grid=(b,),
        in_specs=[rspec, rspec, c2(w1k), c2(w2k0), c2(w2k1), c2(pk), c2(w1v), c2(w2v0), c2(w2v1), c2(pv)],
        out_specs=[ospec, ospec],
        out_shape=[jax.ShapeDtypeStruct((b, LANES, LANES), BF16)] * 2,
        compiler_params=pltpu.CompilerParams(dimension_semantics=("arbitrary",)),
        name="compress",
    )(rk, rv, w1k, w2k0, w2k1, pk, w1v, w2v0, w2v1, pv)


def _flash_init(m_ref, acc_ref):
    m_ref[...] = jnp.full(m_ref.shape, M_INIT, F32)
    acc_ref[...] = jnp.zeros(acc_ref.shape, F32)


def _softmax_pv(scores, vt_blks, m_refs=None, acc_refs=None):
    ps, alphas = [], []
    for i, s in enumerate(scores):
        mx = jnp.max(s, axis=0, keepdims=True)
        if m_refs is None:
            ps.append(jnp.exp2(s - mx).astype(BF16))
            continue
        m_old = m_refs[i][...]
        m_new = jnp.maximum(m_old, mx)
        ps.append(jnp.exp2(s - m_new).astype(BF16))
        alphas.append(jnp.exp2(m_old - m_new))
        m_refs[i][...] = m_new
    if m_refs is None:
        return [_dot(vt, p) for vt, p in zip(vt_blks, ps)]
    for vt, p, alpha, acc_ref in zip(vt_blks, ps, alphas, acc_refs):
        acc_ref[...] = acc_ref[...] * alpha + _dot(vt, p)
    return None


def _pipelined_sweep(n_rest, qk_to, spv_from):
    def body(jj, carry):
        j = 2 * jj
        qk_to(1, j + 1)
        spv_from(0, j)
        qk_to(0, j + 2)
        spv_from(1, j + 1)
        return carry

    lax.fori_loop(0, n_rest // 2, body, 0)

    @pl.when(n_rest % 2 == 1)
    def _():
        qk_to(1, n_rest)
        spv_from(0, n_rest - 1)
        spv_from(1, n_rest)

    @pl.when(n_rest % 2 == 0)
    def _():
        spv_from(0, n_rest)


def _ones_rows(tk):
    return jnp.where(_iota((16, tk), 0) == 0, 1.0, 0.0).astype(BF16)


NSA_TQ = 128
NSA_N = NSA_HPG * NSA_TQ
SEL_TK = 512
WIN_KEYS = WINDOW + NSA_TQ

AUG_POS = 0
AUG_TQ = 6
AUG_PAD = 9
AUG_MASK = 32


def _bf16_terms(x, n=3):
    out, r = [], np.float32(x)
    for _ in range(n):
        term = np.asarray(r, np.float32).astype(BF16).astype(np.float32)
        out.append(float(term))
        r = np.float32(r - term)
    return out


def _nsa_kernel(q_ref, gf_ref, kc_ref, vc_ref, ks_ref, vs_ref, kw_ref, vw_ref,
                augk_ref, augw_ref, augc_ref, ovt_ref, trisel_ref, triwin_ref, o_ref,
                vst_ref, vwt_ref, kwp_ref, vct_ref, gt_ref, m_ref, acc_ref, s0_ref, s1_ref):
    qb = pl.program_id(1)
    t = ks_ref.shape[1]
    t0 = qb * NSA_TQ
    groups = range(NSA_GROUPS)

    @pl.when(qb == 0)
    def _():
        vwt_ref[:, 0:WINDOW] = jnp.zeros((LANES, WINDOW), BF16)
        kwp_ref[0:WINDOW, :] = jnp.zeros((WINDOW, LANES), BF16)
        kwp_ref[WINDOW:WINDOW + t, :] = kw_ref[0]
        for c in range(t // LANES):
            sl = slice(c * LANES, (c + 1) * LANES)
            vst_ref[:, sl] = vs_ref[0, sl, :].astype(F32).T.astype(BF16)
            vwt_ref[:, WINDOW + c * LANES:WINDOW + (c + 1) * LANES] = vw_ref[0, sl, :].astype(F32).T.astype(BF16)
        vct_ref[...] = vc_ref[0].astype(F32).T.astype(BF16)

    qt = q_ref[0].astype(F32).T * (HEAD_DIM ** -0.5 * LOG2E)
    lane = _iota((1, NSA_N), 1)
    hl = lane // NSA_TQ
    tq = t0 + lane % NSA_TQ
    tqf = tq.astype(F32)
    ridx = _iota((16, NSA_N), 0)
    zeros_h = jnp.zeros((HEAD_DIM, NSA_TQ), F32)
    l2e = _bf16_terms(LOG2E)
    q_top, aug16 = [], []
    for g in groups:
        tops = []
        for h in range(NSA_HPG):
            head = NSA_HPG * g + h
            qh = qt[head * HEAD_DIM:(head + 1) * HEAD_DIM]
            tops.append(jnp.concatenate([qh, zeros_h] if g == 0 else [zeros_h, qh], axis=0))
        q_top.append(jnp.concatenate(tops, axis=1))
        slope = lax.bitcast_convert_type((127 - (NSA_HPG * g + hl + 1)) << 23, F32)
        rows = {}
        for i, term in enumerate(l2e):
            rows[AUG_POS + 2 * i] = slope * (64.0 * term)
            rows[AUG_POS + 2 * i + 1] = slope * term
        for i, term in enumerate(_split3(-(slope * tqf) * LOG2E)):
            rows[AUG_TQ + i] = term.astype(F32)
        a = jnp.zeros((16, NSA_N), F32)
        for r, val in rows.items():
            a = jnp.where(ridx == r, val, a)
        aug16.append(a)
    zeros_tail = jnp.zeros((LANES - 16, NSA_N), F32)
    q_plain = [jnp.concatenate([q_top[g], aug16[g], zeros_tail], axis=0).astype(BF16) for g in groups]
    q_win = [jnp.concatenate([q_top[g], jnp.where(ridx == AUG_PAD, NEG_BIG, aug16[g]), zeros_tail], axis=0).astype(BF16)
             for g in groups]

    kcb = jnp.concatenate([kc_ref[0], augc_ref[...]], axis=1)
    kw0 = pl.multiple_of(t0, NSA_TQ)
    kwin = jnp.concatenate([kwp_ref[pl.ds(kw0, WIN_KEYS), :], augw_ref[pl.ds(kw0, WIN_KEYS), :]], axis=1)
    s_cmp = [_dot(kcb, q_plain[g]) for g in groups]
    s_win = [_dot(kwin, q_win[g]) for g in groups]

    n_idx = _iota((LANES, NSA_N), 0)
    valid = (n_idx * CMP_STRIDE + (CMP_BLOCK - 1) <= tq) & (n_idx < LANES - 1)
    ovt = ovt_ref[...]
    ns = ovt.shape[0]
    tcol = t0 + _iota((8, NSA_TQ), 1)
    o_cmp, q_sel = [], []
    for g in groups:
        sm = jnp.where(valid, s_cmp[g], NEG_BIG)
        mx = jnp.max(sm, axis=0, keepdims=True)
        e = jnp.where(valid, jnp.exp2(sm - mx), 0.0)
        den = jnp.sum(e, axis=0, keepdims=True)
        pn = e / jnp.where(den > 0.0, den, 1.0)
        o_cmp.append(_dot(vct_ref[g * HEAD_DIM:(g + 1) * HEAD_DIM, :], pn.astype(BF16)))

        psum = pn[:, 0:NSA_TQ]
        for h in range(1, NSA_HPG):
            psum = psum + pn[:, h * NSA_TQ:(h + 1) * NSA_TQ]
        hi, mid, lo = _split3(psum)
        imp = _dot(ovt, hi) + _dot(ovt, mid) + _dot(ovt, lo)
        tiles = range(ns // 8)
        j_idx = [8 * r + _iota((8, NSA_TQ), 0) for r in tiles]
        in_past = [j_idx[r] * SEL_BLOCK <= tcol for r in tiles]
        cur = tcol // SEL_BLOCK
        score = []
        for r in tiles:
            forced = (j_idx[r] == 0) | (j_idx[r] == cur) | (j_idx[r] == cur - 1)
            score.append(jnp.where(forced, FORCE_SCORE, jnp.where(in_past[r], imp[8 * r:8 * r + 8], -1.0)))
        rank = [jnp.zeros((8, NSA_TQ), F32) for _ in tiles]
        for i in range(ns):
            si = score[i // 8][i % 8:i % 8 + 1, :]
            for r in tiles:
                if 8 * r > i:
                    beats = si >= score[r]
                elif 8 * r + 7 <= i:
                    beats = si > score[r]
                else:
                    beats = (si > score[r]) | ((si == score[r]) & (j_idx[r] > i))
                rank[r] = rank[r] + jnp.where(beats, 1.0, 0.0)
        maskval = jnp.concatenate([jnp.where((rank[r] < min(N_SELECT, ns)) & in_past[r], 0.0, NEG_BIG) for r in tiles], axis=0)
        mask4 = jnp.concatenate([maskval] * NSA_HPG, axis=1)
        q_sel.append(jnp.concatenate([q_top[g], aug16[g], jnp.zeros((AUG_MASK - 16, NSA_N), F32), mask4,
                                      jnp.zeros((LANES - AUG_MASK - ns, NSA_N), F32)], axis=0).astype(BF16))

    ones_w = _ones_rows(WIN_KEYS)
    lo_edge = jnp.concatenate([triwin_ref[1]] * NSA_HPG, axis=1)
    hi_edge = jnp.concatenate([triwin_ref[0]] * NSA_HPG, axis=1)
    s_win = [jnp.concatenate([s[0:NSA_TQ] + lo_edge, s[NSA_TQ:WINDOW], s[WINDOW:WIN_KEYS] + hi_edge], axis=0)
             for s in s_win]
    vt_win = [jnp.concatenate([vwt_ref[g * HEAD_DIM:(g + 1) * HEAD_DIM, pl.ds(kw0, WIN_KEYS)], ones_w], axis=0)
              for g in groups]
    o_win = [acc[0:HEAD_DIM] / acc[HEAD_DIM:HEAD_DIM + 1] for acc in _softmax_pv(s_win, vt_win)]

    ones_s = _ones_rows(SEL_TK)
    per_chunk = SEL_TK // NSA_TQ
    n_full = qb // per_chunk
    s_bufs = (s0_ref, s1_ref)

    def sel_scores(cb):
        k0 = pl.multiple_of(cb * SEL_TK, SEL_TK)
        kblk = jnp.concatenate([ks_ref[0, pl.ds(k0, SEL_TK), :], augk_ref[pl.ds(k0, SEL_TK), :]], axis=1)
        return [_dot(kblk, q_sel[g]) for g in groups]

    def qk_to(buf, j):
        for g, s in enumerate(sel_scores(j - 1)):
            s_bufs[buf][g] = s

    def spv_from(buf, j):
        cb = jnp.where(j == 0, n_full, j - 1)
        k0 = pl.multiple_of(cb * SEL_TK, SEL_TK)
        vts = [jnp.concatenate([vst_ref[g * HEAD_DIM:(g + 1) * HEAD_DIM, pl.ds(k0, SEL_TK)], ones_s], axis=0)
               for g in groups]
        _softmax_pv([s_bufs[buf][g] for g in groups], vts,
                    [m_ref.at[g] for g in groups], [acc_ref.at[g] for g in groups])

    diag = trisel_ref[qb % per_chunk]
    diag4 = jnp.concatenate([diag] * NSA_HPG, axis=1)
    for g, s in enumerate(sel_scores(n_full)):
        s0_ref[g] = s + diag4
        _flash_init(m_ref.at[g], acc_ref.at[g])
    _pipelined_sweep(n_full, qk_to, spv_from)
    o_sel = []
    for g in groups:
        acc = acc_ref[g]
        o_sel.append(acc[0:HEAD_DIM] / acc[HEAD_DIM:HEAD_DIM + 1])

    gt_ref[...] = gf_ref[0].T
    sig = jax.nn.sigmoid(gt_ref[GF_GATE:GF_GATE + 3 * NSA_HEADS, :])
    outs = []
    for g in groups:
        for h in range(NSA_HPG):
            sl = slice(h * NSA_TQ, (h + 1) * NSA_TQ)
            head = NSA_HPG * g + h
            g0 = sig[head:head + 1]
            g1 = sig[NSA_HEADS + head:NSA_HEADS + head + 1]
            g2 = sig[2 * NSA_HEADS + head:2 * NSA_HEADS + head + 1]
            outs.append(g0 * o_cmp[g][:, sl] + g1 * o_sel[g][:, sl] + g2 * o_win[g][:, sl])
    o_ref[0] = jnp.concatenate(outs, axis=0).T.astype(BF16)


def _nsa(proj, gf, kcmp, vcmp, augk, augw, augc, ovt, trisel, triwin):
    b, t, _ = proj.shape
    nq = t // NSA_TQ
    kvb = COL_KV // LANES
    width = NSA_HEADS * HEAD_DIM
    whole = lambda a: pl.BlockSpec(a.shape, lambda i, k: (0,) * a.ndim)
    kv = lambda idx: pl.BlockSpec((1, t, LANES), lambda i, k: (i, 0, idx))
    return pl.pallas_call(
        _nsa_kernel,
        grid=(b, nq),
        in_specs=[
            pl.BlockSpec((1, NSA_TQ, width), lambda i, k: (i, k, COL_QN // width)),
            pl.BlockSpec((1, NSA_TQ, LANES), lambda i, k: (i, k, 0)),
            pl.BlockSpec((1, LANES, LANES), lambda i, k: (i, 0, 0)),
            pl.BlockSpec((1, LANES, LANES), lambda i, k: (i, 0, 0)),
            kv(kvb + 2), kv(kvb + 3), kv(kvb + 4), kv(kvb + 5),
            whole(augk), whole(augw), whole(augc), whole(ovt), whole(trisel), whole(triwin),
        ],
        out_specs=pl.BlockSpec((1, NSA_TQ, width), lambda i, k: (i, k, 0)),
        out_shape=jax.ShapeDtypeStruct((b, t, width), BF16),
        scratch_shapes=[
            pltpu.VMEM((LANES, t), BF16),
            pltpu.VMEM((LANES, t + WINDOW), BF16),
            pltpu.VMEM((t + WINDOW, LANES), BF16),
            pltpu.VMEM((LANES, LANES), BF16),
            pltpu.VMEM((LANES, LANES), F32),
            pltpu.VMEM((NSA_GROUPS, 1, NSA_N), F32),
            pltpu.VMEM((NSA_GROUPS, HEAD_DIM + 16, NSA_N), F32),
            pltpu.VMEM((NSA_GROUPS, SEL_TK, NSA_N), F32),
            pltpu.VMEM((NSA_GROUPS, SEL_TK, NSA_N), F32),
        ],
        compiler_params=pltpu.CompilerParams(
            dimension_semantics=("arbitrary", "arbitrary"), vmem_limit_bytes=VMEM_LIMIT),
        name="nsa_attention",
    )(proj, gf, kcmp, vcmp, proj, proj, proj, proj, augk, augw, augc, ovt, trisel, triwin)


FOX_CBLK = 256


def _foxprep_kernel(gf_ref, bias_ref, cq_ref, augk_ref):
    t = gf_ref.shape[1]
    cols = []
    for c in range(t // LANES):
        cols.append(gf_ref[0, c * LANES:(c + 1) * LANES, :].T[GF_FOX:GF_FOX + FOX_HEADS])
    f = jnp.concatenate(cols, axis=1) + bias_ref[...]
    ls = jnp.minimum(f, 0.0) - jnp.log(1.0 + jnp.exp(-jnp.abs(f)))
    upper = (_iota((FOX_CBLK, FOX_CBLK), 0) <= _iota((FOX_CBLK, FOX_CBLK), 1)).astype(BF16)
    carry = jnp.zeros((FOX_HEADS, 1), F32)
    parts = []
    for c in range(t // FOX_CBLK):
        hi, mid, lo = _split3(ls[:, c * FOX_CBLK:(c + 1) * FOX_CBLK])
        cb = _dot(hi, upper) + _dot(mid, upper) + _dot(lo, upper) + carry
        carry = cb[:, FOX_CBLK - 1:FOX_CBLK]
        parts.append(cb)
    csum = jnp.concatenate(parts, axis=1) * LOG2E
    c1, c2, c3 = (v.astype(F32) for v in _split3(csum))
    ones = jnp.ones((3 * FOX_HEADS, t), F32)
    zeros = jnp.zeros((LANES - 6 * FOX_HEADS, t), F32)
    cq_ref[0] = jnp.concatenate([c1, c2, c3, ones, zeros], axis=0).astype(BF16)
    slab_k = jnp.concatenate([ones, -c1, -c2, -c3, zeros], axis=0)
    for c in range(t // LANES):
        sl = slice(c * LANES, (c + 1) * LANES)
        augk_ref[0, sl, :] = slab_k[:, sl].T.astype(BF16)


def _foxprep(gf, bias):
    b, t, _ = gf.shape
    return pl.pallas_call(
        _foxprep_kernel,
        grid=(b,),
        in_specs=[pl.BlockSpec((1, t, LANES), lambda i: (i, 0, 0)),
                  pl.BlockSpec(bias.shape, lambda i: (0, 0))],
        out_specs=[pl.BlockSpec((1, LANES, t), lambda i: (i, 0, 0)),
                   pl.BlockSpec((1, t, LANES), lambda i: (i, 0, 0))],
        out_shape=[jax.ShapeDtypeStruct((b, LANES, t), BF16),
                   jax.ShapeDtypeStruct((b, t, LANES), BF16)],
        compiler_params=pltpu.CompilerParams(dimension_semantics=("arbitrary",)),
        name="fox_prep",
    )(gf, bias)


FOX_TQ = 256
FOX_TK = 256
FOX_NP = FOX_HEADS // 2
FOX_N = 2 * FOX_TQ
FOX_ROWS = 2 * HEAD_DIM + 16


def _fox_kernel(q_ref, k_ref, v_ref, cq_ref, augk_ref, tri_ref, o_ref, vt_ref, m_ref, acc_ref, s0_ref, s1_ref):
    qb = pl.program_id(1)
    t = k_ref.shape[1]

    @pl.when(qb == 0)
    def _():
        for p in range(FOX_NP):
            for c in range(t // LANES):
                sl = slice(c * LANES, (c + 1) * LANES)
                vt_ref[p, 0:LANES, sl] = v_ref[0, sl, p * LANES:(p + 1) * LANES].astype(F32).T.astype(BF16)
            vt_ref[p, LANES:FOX_ROWS, :] = _ones_rows(t)

    row = _iota((LANES, FOX_TQ), 0)
    hrow = row % FOX_HEADS
    cq = cq_ref[0].astype(F32)
    qs = []
    for p in range(FOX_NP):
        qt = q_ref[0, :, p * LANES:(p + 1) * LANES].astype(F32).T * (HEAD_DIM ** -0.5 * LOG2E)
        q_top = jnp.concatenate([jnp.where(row < HEAD_DIM, qt, 0.0), jnp.where(row >= HEAD_DIM, qt, 0.0)], axis=1)
        q_aug = jnp.concatenate([jnp.where(hrow == 2 * p, cq, 0.0), jnp.where(hrow == 2 * p + 1, cq, 0.0)], axis=1)
        qs.append(jnp.concatenate([q_top, q_aug], axis=0).astype(BF16))

    pairs = range(FOX_NP)
    s_bufs = (s0_ref, s1_ref)

    def scores(kb):
        k0 = pl.multiple_of(kb * FOX_TK, FOX_TK)
        ak = augk_ref[0, pl.ds(k0, FOX_TK), :]
        return [_dot(jnp.concatenate([k_ref[0, pl.ds(k0, FOX_TK), p * LANES:(p + 1) * LANES], ak], axis=1), qs[p])
                for p in pairs]

    def qk_to(buf, j):
        for p, s in enumerate(scores(j - 1)):
            s_bufs[buf][p] = s

    def spv_from(buf, j):
        kb = jnp.where(j == 0, qb, j - 1)
        k0 = pl.multiple_of(kb * FOX_TK, FOX_TK)
        _softmax_pv([s_bufs[buf][p] for p in pairs], [vt_ref[p, :, pl.ds(k0, FOX_TK)] for p in pairs],
                    [m_ref.at[p] for p in pairs], [acc_ref.at[p] for p in pairs])

    tri = tri_ref[...]
    for p, s in enumerate(scores(qb)):
        s0_ref[p] = s + tri
        _flash_init(m_ref.at[p], acc_ref.at[p])
    _pipelined_sweep(qb, qk_to, spv_from)
    for p in range(FOX_NP):
        acc = acc_ref[p]
        den = acc[2 * HEAD_DIM:2 * HEAD_DIM + 1]
        o_a = acc[0:HEAD_DIM, 0:FOX_TQ] / den[:, 0:FOX_TQ]
        o_b = acc[HEAD_DIM:2 * HEAD_DIM, FOX_TQ:FOX_N] / den[:, FOX_TQ:FOX_N]
        o_ref[0, :, p * LANES:(p + 1) * LANES] = jnp.concatenate([o_a, o_b], axis=0).T.astype(BF16)


def _fox(proj, cq, augk, tri):
    b, t, _ = proj.shape
    nq = t // FOX_TQ
    width = FOX_HEADS * HEAD_DIM
    qf, kf, vf = COL_QF // width, COL_KF // width, COL_VF // width
    return pl.pallas_call(
        _fox_kernel,
        grid=(b, nq),
        in_specs=[
            pl.BlockSpec((1, FOX_TQ, width), lambda i, k: (i, k, qf)),
            pl.BlockSpec((1, t, width), lambda i, k: (i, 0, kf)),
            pl.BlockSpec((1, t, width), lambda i, k: (i, 0, vf)),
            pl.BlockSpec((1, LANES, FOX_TQ), lambda i, k: (i, 0, k)),
            pl.BlockSpec((1, t, LANES), lambda i, k: (i, 0, 0)),
            pl.BlockSpec(tri.shape, lambda i, k: (0, 0)),
        ],
        out_specs=pl.BlockSpec((1, FOX_TQ, width), lambda i, k: (i, k, 0)),
        out_shape=jax.ShapeDtypeStruct((b, t, width), BF16),
        scratch_shapes=[
            pltpu.VMEM((FOX_NP, FOX_ROWS, t), BF16),
            pltpu.VMEM((FOX_NP, 1, FOX_N), F32),
            pltpu.VMEM((FOX_NP, FOX_ROWS, FOX_N), F32),
            pltpu.VMEM((FOX_NP, FOX_TK, FOX_N), F32),
            pltpu.VMEM((FOX_NP, FOX_TK, FOX_N), F32),
        ],
        compiler_params=pltpu.CompilerParams(
            dimension_semantics=("arbitrary", "arbitrary"), vmem_limit_bytes=VMEM_LIMIT),
        name="fox_attention",
    )(proj, proj, proj, cq, augk, tri)


def _post_kernel(x_ref, on_ref, of_ref, mg_ref, wn_ref, wf_ref, wo_ref, gm_ref, wu_ref, wd_ref, gfin_ref, o_ref):
    d = x_ref.shape[1]
    ga = mg_ref[:, 0:d].astype(F32)
    gb = mg_ref[:, d:2 * d].astype(F32)
    y = ga * _dot(on_ref[...], wn_ref[...]) + gb * _dot(of_ref[...], wf_ref[...])
    h = x_ref[...] + _dot(y.astype(BF16), wo_ref[...])
    r = lax.rsqrt(jnp.mean(h * h, axis=-1, keepdims=True) + NORM_EPS)
    v = (h * r * gm_ref[...]).astype(BF16)
    acc = h
    ff = wu_ref.shape[1]
    for c0 in range(0, ff, 1024):
        up = jnp.maximum(_dot(v, wu_ref[:, c0:c0 + 1024]), 0.0)
        acc = acc + _dot((up * up).astype(BF16), wd_ref[c0:c0 + 1024, :])
    r2 = lax.rsqrt(jnp.mean(acc * acc, axis=-1, keepdims=True) + NORM_EPS)
    o_ref[...] = acc * r2 * gfin_ref[...]


def _post(x2, on, of, mg, wn, wf, wo, gm, wu, wd, gfin, tm):
    n, d = x2.shape
    row = lambda w: pl.BlockSpec((tm, w), lambda i: (i, 0))
    const = lambda a: pl.BlockSpec(a.shape, lambda i: (0, 0), pipeline_mode=pl.Buffered(1))
    return pl.pallas_call(
        _post_kernel,
        grid=(n // tm,),
        in_specs=[row(d), row(on.shape[1]), row(of.shape[1]), row(2 * d),
                  const(wn), const(wf), const(wo), const(gm), const(wu), const(wd), const(gfin)],
        out_specs=row(d),
        out_shape=jax.ShapeDtypeStruct((n, d), F32),
        compiler_params=pltpu.CompilerParams(
            dimension_semantics=("arbitrary",), vmem_limit_bytes=VMEM_LIMIT),
        name="post_mlp",
    )(x2, on, of, mg, wn, wf, wo, gm, wu, wd, gfin)


def _position_constants(t):
    def key_side(pos, pad):
        a = np.zeros((pos.size, LANES), np.float32)
        for i in range(3):
            a[:, AUG_POS + 2 * i] = pos // 64
            a[:, AUG_POS + 2 * i + 1] = pos % 64
        a[:, AUG_TQ:AUG_TQ + 3] = 1.0
        a[:, AUG_PAD] = pad
        return a

    pos = np.arange(t)
    augk = key_side(pos, 0.0)
    augk[pos, AUG_MASK + pos // SEL_BLOCK] = 1.0
    augw = np.concatenate([key_side(np.zeros(WINDOW, np.int64), 1.0), key_side(pos, 0.0)], axis=0)
    augc = key_side(np.arange(LANES) * CMP_STRIDE + CMP_BLOCK - 1, 0.0)

    nc = (t - CMP_BLOCK) // CMP_STRIDE + 1
    ns = t // SEL_BLOCK
    n = np.arange(LANES)[None, :]
    j = np.arange(ns)[:, None]
    ovt = ((n * CMP_STRIDE <= j * SEL_BLOCK + SEL_BLOCK - 1) & (n * CMP_STRIDE + CMP_BLOCK - 1 >= j * SEL_BLOCK) & (n < nc))

    def tri(tk, tq, anti=False):
        kk = np.arange(tk)[:, None]
        tt = np.arange(tq)[None, :]
        ok = (kk > tt) if anti else (kk <= tt)
        return np.where(ok, 0.0, NEG_BIG).astype(np.float32)

    edge = tri(NSA_TQ, NSA_TQ)
    trisel = np.zeros((SEL_TK // NSA_TQ, SEL_TK, NSA_TQ), np.float32)
    for v in range(SEL_TK // NSA_TQ):
        trisel[v, v * NSA_TQ:(v + 1) * NSA_TQ] = edge
    triwin = np.stack([edge, tri(NSA_TQ, NSA_TQ, anti=True)])
    trifox = np.tile(tri(FOX_TK, FOX_TQ), (1, 2))
    return (jnp.asarray(augk, BF16), jnp.asarray(augw, BF16), jnp.asarray(augc, BF16), jnp.asarray(ovt, BF16),
            jnp.asarray(trisel), jnp.asarray(triwin), jnp.asarray(trifox))


def kernel(x, norm_mix, w_in, cmp_pos_k, cmp_w1_k, cmp_w2_k, cmp_pos_v, cmp_w1_v, cmp_w2_v, fox_f_bias, w_branch_nsa, w_branch_fox, w_merge_gate, b_merge_gate, w_out, norm_mlp, w_up, w_down, norm_final):
    b, t, d = x.shape
    assert w_in.shape[0] == 1, "one layer: the final norm is fused into the MLP kernel"
    assert t % FOX_TQ == 0 and t // SEL_BLOCK == 32 and (t - CMP_BLOCK) // CMP_STRIDE + 1 == LANES - 1
    augk, augw, augc, ovt, trisel, triwin, trifox = _position_constants(t)
    h = x.reshape(b * t, d)
    w = w_in[0]
    nq_w = NSA_HEADS * HEAD_DIM
    gate0 = nq_w + 6 * LANES
    fox0 = gate0 + 3 * NSA_HEADS
    wa = jnp.concatenate([w[:, :nq_w], w[:, fox0:fox0 + 3 * FOX_HEADS * HEAD_DIM], w[:, nq_w:gate0]], axis=1).astype(BF16)
    wg = jnp.concatenate([w[:, gate0:fox0], w[:, -FOX_HEADS:],
                          jnp.zeros((d, LANES - 3 * NSA_HEADS - FOX_HEADS), F32)], axis=1).astype(BF16)
    proj, gf, mg = _inproj(h, norm_mix[0][None, :], wa, wg, w_merge_gate[0].astype(BF16),
                           b_merge_gate[0][None, :], ROW_TILE)
    proj = proj.reshape(b, t, PROJ_W)
    gf = gf.reshape(b, t, LANES)

    def blocks16(c0):
        a = proj[:, :, c0:c0 + LANES].reshape(b, t // CMP_STRIDE, CMP_STRIDE, NSA_GROUPS, HEAD_DIM)
        return a.transpose(0, 3, 1, 2, 4).reshape(b, NSA_GROUPS, t // CMP_STRIDE, CMP_STRIDE * HEAD_DIM)

    def w2_pair(w2):
        z = jnp.zeros_like(w2)
        return (jnp.concatenate([w2, z], axis=1).astype(BF16), jnp.concatenate([z, w2], axis=1).astype(BF16))

    def pos8(p):
        return jnp.concatenate([p.reshape(1, -1), jnp.zeros((7, p.size), F32)], axis=0).astype(BF16)

    w2k0, w2k1 = w2_pair(cmp_w2_k[0])
    w2v0, w2v1 = w2_pair(cmp_w2_v[0])
    kcmp, vcmp = _compress(blocks16(COL_KV), blocks16(COL_KV + LANES),
                           cmp_w1_k[0].astype(BF16), w2k0, w2k1, pos8(cmp_pos_k[0]),
                           cmp_w1_v[0].astype(BF16), w2v0, w2v1, pos8(cmp_pos_v[0]))
    o_n = _nsa(proj, gf, kcmp, vcmp, augk, augw, augc, ovt, trisel, triwin)
    cq, faug = _foxprep(gf, fox_f_bias[0][:, None])
    o_f = _fox(proj, cq, faug, trifox)
    out = _post(h, o_n.reshape(b * t, -1), o_f.reshape(b * t, -1), mg,
                w_branch_nsa[0].astype(BF16), w_branch_fox[0].astype(BF16), w_out[0].astype(BF16),
                norm_mlp[0][None, :], w_up[0].astype(BF16), w_down[0].astype(BF16), norm_final[None, :], ROW_TILE)
    return out.reshape(b, t, d)
```

```python
import numpy as np
import jax
import jax.numpy as jnp
from jax import lax
from jax.experimental import pallas as pl
from jax.experimental.pallas import tpu as pltpu

HEAD_DIM = 64
NSA_HEADS = 8
NSA_GROUPS = 2
NSA_HPG = NSA_HEADS // NSA_GROUPS
FOX_HEADS = 8
CMP_BLOCK = 32
CMP_STRIDE = 16
CMP_HIDDEN = 128
SEL_BLOCK = 64
N_SELECT = 16
WINDOW = 512
NORM_EPS = 1e-6
FORCE_SCORE = 1e9

LANES = 128
NEG_BIG = -(2.0 ** 100)
M_INIT = -3.0e38
LOG2E = 1.4426950408889634
VMEM_LIMIT = 56 * 1024 * 1024
ROW_TILE = 512

PROJ_W = 2816
COL_QN = 0
COL_QF = 512
COL_KF = 1024
COL_VF = 1536
COL_KV = 2048
GF_GATE = 0
GF_FOX = 24

F32 = jnp.float32
BF16 = jnp.bfloat16


def _dot(a, b):
    return jnp.dot(a, b, preferred_element_type=F32)


def _iota(shape, dim):
    return lax.broadcasted_iota(jnp.int32, shape, dim)


def _split3(x):
    hi = x.astype(BF16)
    r1 = x - hi.astype(F32)
    mid = r1.astype(BF16)
    lo = (r1 - mid.astype(F32)).astype(BF16)
    return hi, mid, lo


def _inproj_kernel(x_ref, g_ref, wa_ref, wg_ref, wm_ref, bm_ref, proj_ref, gf_ref, mg_ref):
    x = x_ref[...]
    r = lax.rsqrt(jnp.mean(x * x, axis=-1, keepdims=True) + NORM_EPS)
    u = (x * r * g_ref[...]).astype(BF16)
    for c0, c1 in ((0, 1024), (1024, 2048), (2048, PROJ_W)):
        proj_ref[:, c0:c1] = _dot(u, wa_ref[:, c0:c1]).astype(BF16)
    gf_ref[...] = _dot(u, wg_ref[...])
    for c0 in (0, 1024):
        z = _dot(u, wm_ref[:, c0:c0 + 1024]) + bm_ref[:, c0:c0 + 1024]
        mg_ref[:, c0:c0 + 1024] = jax.nn.sigmoid(z).astype(BF16)


def _inproj(x2, g, wa, wg, wm, bm, tm):
    n, d = x2.shape
    const = lambda i: (0, 0)
    return pl.pallas_call(
        _inproj_kernel,
        grid=(n // tm,),
        in_specs=[
            pl.BlockSpec((tm, d), lambda i: (i, 0)),
            pl.BlockSpec((1, d), const),
            pl.BlockSpec(wa.shape, const),
            pl.BlockSpec(wg.shape, const),
            pl.BlockSpec(wm.shape, const),
            pl.BlockSpec(bm.shape, const),
        ],
        out_specs=[
            pl.BlockSpec((tm, PROJ_W), lambda i: (i, 0)),
            pl.BlockSpec((tm, LANES), lambda i: (i, 0)),
            pl.BlockSpec((tm, 2 * d), lambda i: (i, 0)),
        ],
        out_shape=[
            jax.ShapeDtypeStruct((n, PROJ_W), BF16),
            jax.ShapeDtypeStruct((n, LANES), F32),
            jax.ShapeDtypeStruct((n, 2 * d), BF16),
        ],
        compiler_params=pltpu.CompilerParams(
            dimension_semantics=("arbitrary",), vmem_limit_bytes=VMEM_LIMIT),
        name="inproj",
    )(x2, g, wa, wg, wm, bm)


def _compress_kernel(kc_ref, vc_ref, wk_ref, w2k0_ref, w2k1_ref, pk_ref,
                     wv_ref, w2v0_ref, w2v1_ref, pv_ref, kco_ref, vco_ref, xs_ref):
    t = kc_ref.shape[1]
    nblk = t // CMP_STRIDE

    def one(x_ref, w_ref, w2_refs, p_ref, o_ref):
        xs_ref[...] = x_ref[0].astype(F32)
        acc = jnp.zeros((nblk, 4 * LANES), F32)
        pos = jnp.zeros((8, 4 * LANES), F32)
        for j in range(CMP_STRIDE):
            xj = xs_ref[pl.ds(j, nblk, stride=CMP_STRIDE), :].astype(BF16)
            acc = acc + _dot(xj, w_ref[j])
            pos = pos + _dot(p_ref[j], w_ref[j])
        posb = pos[0:1, 0:LANES] + pos[0:1, 3 * LANES:4 * LANES]
        out = jnp.zeros((nblk, LANES), F32)
        for g in range(NSA_GROUPS):
            a = acc[:, g * LANES:(g + 1) * LANES]
            b = acc[:, (2 + g) * LANES:(3 + g) * LANES]
            h = a + pltpu.roll(b, nblk - 1, 0) + posb
            out = out + _dot(jax.nn.gelu(h).astype(BF16), w2_refs[g][...])
        o_ref[0] = out.astype(BF16)

    one(kc_ref, wk_ref, (w2k0_ref, w2k1_ref), pk_ref, kco_ref)
    one(vc_ref, wv_ref, (w2v0_ref, w2v1_ref), pv_ref, vco_ref)


def _compress(proj, wk, w2k0, w2k1, pk, wv, w2v0, w2v1, pv):
    b, t, _ = proj.shape
    kvb = COL_KV // LANES
    whole = lambda a: pl.BlockSpec(a.shape, lambda i: (0,) * a.ndim)
    ospec = pl.BlockSpec((1, LANES, LANES), lambda i: (i, 0, 0))
    return pl.pallas_call(
        _compress_kernel,
        grid=(b,),
        in_specs=[pl.BlockSpec((1, t, LANES), lambda i: (i, 0, kvb)),
                  pl.BlockSpec((1, t, LANES), lambda i: (i, 0, kvb + 1)),
                  whole(wk), whole(w2k0), whole(w2k1), whole(pk), whole(wv), whole(w2v0), whole(w2v1), whole(pv)],
        out_specs=[ospec, ospec],
        out_shape=[jax.ShapeDtypeStruct((b, LANES, LANES), BF16)] * 2,
        scratch_shapes=[pltpu.VMEM((t, LANES), F32)],
        compiler_params=pltpu.CompilerParams(dimension_semantics=("arbitrary",)),
        name="compress",
    )(proj, proj, wk, w2k0, w2k1, pk, wv, w2v0, w2v1, pv)


def _flash_init(m_ref, acc_ref):
    m_ref[...] = jnp.full(m_ref.shape, M_INIT, F32)
    acc_ref[...] = jnp.zeros(acc_ref.shape, F32)


def _softmax_pv(scores, vt_blks, m_refs=None, acc_refs=None):
    ps, alphas = [], []
    for i, s in enumerate(scores):
        mx = jnp.max(s, axis=0, keepdims=True)
        if m_refs is None:
            ps.append(jnp.exp2(s - mx).astype(BF16))
            continue
        m_old = m_refs[i][...]
        m_new = jnp.maximum(m_old, mx)
        ps.append(jnp.exp2(s - m_new).astype(BF16))
        alphas.append(jnp.exp2(m_old - m_new))
        m_refs[i][...] = m_new
    if m_refs is None:
        return [_dot(vt, p) for vt, p in zip(vt_blks, ps)]
    for vt, p, alpha, acc_ref in zip(vt_blks, ps, alphas, acc_refs):
        acc_ref[...] = acc_ref[...] * alpha + _dot(vt, p)
    return None


def _pipelined_sweep(n_rest, qk_to, spv_from):
    def body(jj, carry):
        j = 2 * jj
        qk_to(1, j + 1)
        spv_from(0, j)
        qk_to(0, j + 2)
        spv_from(1, j + 1)
        return carry

    lax.fori_loop(0, n_rest // 2, body, 0)

    @pl.when(n_rest % 2 == 1)
    def _():
        qk_to(1, n_rest)
        spv_from(0, n_rest - 1)
        spv_from(1, n_rest)

    @pl.when(n_rest % 2 == 0)
    def _():
        spv_from(0, n_rest)


def _ones_rows(tk):
    return jnp.where(_iota((16, tk), 0) == 0, 1.0, 0.0).astype(BF16)


NSA_TQ = 128
NSA_N = NSA_HPG * NSA_TQ
SEL_TK = 512
WIN_KEYS = WINDOW + NSA_TQ

AUG_POS = 0
AUG_TQ = 6
AUG_PAD = 9
AUG_MASK = 32


def _bf16_terms(x, n=3):
    out, r = [], np.float32(x)
    for _ in range(n):
        term = np.asarray(r, np.float32).astype(BF16).astype(np.float32)
        out.append(float(term))
        r = np.float32(r - term)
    return out


def _nsa_kernel(q_ref, gf_ref, kc_ref, vc_ref, ks_ref, vs_ref, kw_ref, vw_ref,
                augk_ref, augw_ref, augc_ref, ovt_ref, trisel_ref, triwin_ref, o_ref,
                vst_ref, vwt_ref, kwp_ref, vct_ref, gt_ref, m_ref, acc_ref, s0_ref, s1_ref):
    qb = pl.program_id(1)
    t = ks_ref.shape[1]
    t0 = qb * NSA_TQ
    groups = range(NSA_GROUPS)

    @pl.when(qb == 0)
    def _():
        vwt_ref[:, 0:WINDOW] = jnp.zeros((LANES, WINDOW), BF16)
        kwp_ref[0:WINDOW, :] = jnp.zeros((WINDOW, LANES), BF16)
        kwp_ref[WINDOW:WINDOW + t, :] = kw_ref[0]
        for c in range(t // LANES):
            sl = slice(c * LANES, (c + 1) * LANES)
            vst_ref[:, sl] = vs_ref[0, sl, :].astype(F32).T.astype(BF16)
            vwt_ref[:, WINDOW + c * LANES:WINDOW + (c + 1) * LANES] = vw_ref[0, sl, :].astype(F32).T.astype(BF16)
        vct_ref[...] = vc_ref[0].astype(F32).T.astype(BF16)

    qt = q_ref[0].astype(F32).T * (HEAD_DIM ** -0.5 * LOG2E)
    lane = _iota((1, NSA_N), 1)
    hl = lane // NSA_TQ
    tq = t0 + lane % NSA_TQ
    tqf = tq.astype(F32)
    ridx = _iota((16, NSA_N), 0)
    zeros_h = jnp.zeros((HEAD_DIM, NSA_TQ), F32)
    l2e = _bf16_terms(LOG2E)
    q_top, aug16 = [], []
    for g in groups:
        tops = []
        for h in range(NSA_HPG):
            head = NSA_HPG * g + h
            qh = qt[head * HEAD_DIM:(head + 1) * HEAD_DIM]
            tops.append(jnp.concatenate([qh, zeros_h] if g == 0 else [zeros_h, qh], axis=0))
        q_top.append(jnp.concatenate(tops, axis=1))
        slope = lax.bitcast_convert_type((127 - (NSA_HPG * g + hl + 1)) << 23, F32)
        rows = {}
        for i, term in enumerate(l2e):
            rows[AUG_POS + 2 * i] = slope * (64.0 * term)
            rows[AUG_POS + 2 * i + 1] = slope * term
        for i, term in enumerate(_split3(-(slope * tqf) * LOG2E)):
            rows[AUG_TQ + i] = term.astype(F32)
        a = jnp.zeros((16, NSA_N), F32)
        for r, val in rows.items():
            a = jnp.where(ridx == r, val, a)
        aug16.append(a)
    zeros_tail = jnp.zeros((LANES - 16, NSA_N), F32)
    q_plain = [jnp.concatenate([q_top[g], aug16[g], zeros_tail], axis=0).astype(BF16) for g in groups]
    q_win = [jnp.concatenate([q_top[g], jnp.where(ridx == AUG_PAD, NEG_BIG, aug16[g]), zeros_tail], axis=0).astype(BF16)
             for g in groups]

    kcb = jnp.concatenate([kc_ref[0], augc_ref[...]], axis=1)
    kw0 = pl.multiple_of(t0, NSA_TQ)
    kwin = jnp.concatenate([kwp_ref[pl.ds(kw0, WIN_KEYS), :], augw_ref[pl.ds(kw0, WIN_KEYS), :]], axis=1)
    s_cmp = [_dot(kcb, q_plain[g]) for g in groups]
    s_win = [_dot(kwin, q_win[g]) for g in groups]

    n_idx = _iota((LANES, NSA_N), 0)
    valid = (n_idx * CMP_STRIDE + (CMP_BLOCK - 1) <= tq) & (n_idx < LANES - 1)
    ovt = ovt_ref[...]
    ns = ovt.shape[0]
    tcol = t0 + _iota((8, NSA_TQ), 1)
    o_cmp, q_sel = [], []
    for g in groups:
        sm = jnp.where(valid, s_cmp[g], NEG_BIG)
        mx = jnp.max(sm, axis=0, keepdims=True)
        e = jnp.where(valid, jnp.exp2(sm - mx), 0.0)
        den = jnp.sum(e, axis=0, keepdims=True)
        pn = e / jnp.where(den > 0.0, den, 1.0)
        o_cmp.append(_dot(vct_ref[g * HEAD_DIM:(g + 1) * HEAD_DIM, :], pn.astype(BF16)))

        psum = pn[:, 0:NSA_TQ]
        for h in range(1, NSA_HPG):
            psum = psum + pn[:, h * NSA_TQ:(h + 1) * NSA_TQ]
        hi, mid, lo = _split3(psum)
        imp = _dot(ovt, hi) + _dot(ovt, mid) + _dot(ovt, lo)
        tiles = range(ns // 8)
        j_idx = [8 * r + _iota((8, NSA_TQ), 0) for r in tiles]
        in_past = [j_idx[r] * SEL_BLOCK <= tcol for r in tiles]
        cur = tcol // SEL_BLOCK
        score = []
        for r in tiles:
            forced = (j_idx[r] == 0) | (j_idx[r] == cur) | (j_idx[r] == cur - 1)
            score.append(jnp.where(forced, FORCE_SCORE, jnp.where(in_past[r], imp[8 * r:8 * r + 8], -1.0)))
        rank = [jnp.zeros((8, NSA_TQ), F32) for _ in tiles]
        for i in range(ns):
            si = score[i // 8][i % 8:i % 8 + 1, :]
            for r in tiles:
                if 8 * r > i:
                    beats = si >= score[r]
                elif 8 * r + 7 <= i:
                    beats = si > score[r]
                else:
                    beats = (si > score[r]) | ((si == score[r]) & (j_idx[r] > i))
                rank[r] = rank[r] + jnp.where(beats, 1.0, 0.0)
        maskval = jnp.concatenate([jnp.where((rank[r] < min(N_SELECT, ns)) & in_past[r], 0.0, NEG_BIG) for r in tiles], axis=0)
        mask4 = jnp.concatenate([maskval] * NSA_HPG, axis=1)
        q_sel.append(jnp.concatenate([q_top[g], aug16[g], jnp.zeros((AUG_MASK - 16, NSA_N), F32), mask4,
                                      jnp.zeros((LANES - AUG_MASK - ns, NSA_N), F32)], axis=0).astype(BF16))

    ones_w = _ones_rows(WIN_KEYS)
    lo_edge = jnp.concatenate([triwin_ref[1]] * NSA_HPG, axis=1)
    hi_edge = jnp.concatenate([triwin_ref[0]] * NSA_HPG, axis=1)
    s_win = [jnp.concatenate([s[0:NSA_TQ] + lo_edge, s[NSA_TQ:WINDOW], s[WINDOW:WIN_KEYS] + hi_edge], axis=0)
             for s in s_win]
    vt_win = [jnp.concatenate([vwt_ref[g * HEAD_DIM:(g + 1) * HEAD_DIM, pl.ds(kw0, WIN_KEYS)], ones_w], axis=0)
              for g in groups]
    o_win = [acc[0:HEAD_DIM] / acc[HEAD_DIM:HEAD_DIM + 1] for acc in _softmax_pv(s_win, vt_win)]

    ones_s = _ones_rows(SEL_TK)
    per_chunk = SEL_TK // NSA_TQ
    n_full = qb // per_chunk
    s_bufs = (s0_ref, s1_ref)

    def sel_scores(cb):
        k0 = pl.multiple_of(cb * SEL_TK, SEL_TK)
        kblk = jnp.concatenate([ks_ref[0, pl.ds(k0, SEL_TK), :], augk_ref[pl.ds(k0, SEL_TK), :]], axis=1)
        return [_dot(kblk, q_sel[g]) for g in groups]

    def qk_to(buf, j):
        for g, s in enumerate(sel_scores(j - 1)):
            s_bufs[buf][g] = s

    def spv_from(buf, j):
        cb = jnp.where(j == 0, n_full, j - 1)
        k0 = pl.multiple_of(cb * SEL_TK, SEL_TK)
        vts = [jnp.concatenate([vst_ref[g * HEAD_DIM:(g + 1) * HEAD_DIM, pl.ds(k0, SEL_TK)], ones_s], axis=0)
               for g in groups]
        _softmax_pv([s_bufs[buf][g] for g in groups], vts,
                    [m_ref.at[g] for g in groups], [acc_ref.at[g] for g in groups])

    diag = trisel_ref[qb % per_chunk]
    diag4 = jnp.concatenate([diag] * NSA_HPG, axis=1)
    for g, s in enumerate(sel_scores(n_full)):
        s0_ref[g] = s + diag4
        _flash_init(m_ref.at[g], acc_ref.at[g])
    _pipelined_sweep(n_full, qk_to, spv_from)
    o_sel = []
    for g in groups:
        acc = acc_ref[g]
        o_sel.append(acc[0:HEAD_DIM] / acc[HEAD_DIM:HEAD_DIM + 1])

    gt_ref[...] = gf_ref[0].T
    sig = jax.nn.sigmoid(gt_ref[GF_GATE:GF_GATE + 3 * NSA_HEADS, :])
    outs = []
    for g in groups:
        for h in range(NSA_HPG):
            sl = slice(h * NSA_TQ, (h + 1) * NSA_TQ)
            head = NSA_HPG * g + h
            g0 = sig[head:head + 1]
            g1 = sig[NSA_HEADS + head:NSA_HEADS + head + 1]
            g2 = sig[2 * NSA_HEADS + head:2 * NSA_HEADS + head + 1]
            outs.append(g0 * o_cmp[g][:, sl] + g1 * o_sel[g][:, sl] + g2 * o_win[g][:, sl])
    o_ref[0] = jnp.concatenate(outs, axis=0).T.astype(BF16)


def _nsa(proj, gf, kcmp, vcmp, augk, augw, augc, ovt, trisel, triwin):
    b, t, _ = proj.shape
    nq = t // NSA_TQ
    kvb = COL_KV // LANES
    width = NSA_HEADS * HEAD_DIM
    whole = lambda a: pl.BlockSpec(a.shape, lambda i, k: (0,) * a.ndim)
    kv = lambda idx: pl.BlockSpec((1, t, LANES), lambda i, k: (i, 0, idx))
    return pl.pallas_call(
        _nsa_kernel,
        grid=(b, nq),
        in_specs=[
            pl.BlockSpec((1, NSA_TQ, width), lambda i, k: (i, k, COL_QN // width)),
            pl.BlockSpec((1, NSA_TQ, LANES), lambda i, k: (i, k, 0)),
            pl.BlockSpec((1, LANES, LANES), lambda i, k: (i, 0, 0)),
            pl.BlockSpec((1, LANES, LANES), lambda i, k: (i, 0, 0)),
            kv(kvb + 2), kv(kvb + 3), kv(kvb + 4), kv(kvb + 5),
            whole(augk), whole(augw), whole(augc), whole(ovt), whole(trisel), whole(triwin),
        ],
        out_specs=pl.BlockSpec((1, NSA_TQ, width), lambda i, k: (i, k, 0)),
        out_shape=jax.ShapeDtypeStruct((b, t, width), BF16),
        scratch_shapes=[
            pltpu.VMEM((LANES, t), BF16),
            pltpu.VMEM((LANES, t + WINDOW), BF16),
            pltpu.VMEM((t + WINDOW, LANES), BF16),
            pltpu.VMEM((LANES, LANES), BF16),
            pltpu.VMEM((LANES, LANES), F32),
            pltpu.VMEM((NSA_GROUPS, 1, NSA_N), F32),
            pltpu.VMEM((NSA_GROUPS, HEAD_DIM + 16, NSA_N), F32),
            pltpu.VMEM((NSA_GROUPS, SEL_TK, NSA_N), F32),
            pltpu.VMEM((NSA_GROUPS, SEL_TK, NSA_N), F32),
        ],
        compiler_params=pltpu.CompilerParams(
            dimension_semantics=("arbitrary", "arbitrary"), vmem_limit_bytes=VMEM_LIMIT),
        name="nsa_attention",
    )(proj, gf, kcmp, vcmp, proj, proj, proj, proj, augk, augw, augc, ovt, trisel, triwin)


FOX_CBLK = 256


def _foxprep_kernel(gf_ref, bias_ref, cq_ref, augk_ref):
    t = gf_ref.shape[1]
    cols = []
    for c in range(t // LANES):
        cols.append(gf_ref[0, c * LANES:(c + 1) * LANES, :].T[GF_FOX:GF_FOX + FOX_HEADS])
    f = jnp.concatenate(cols, axis=1) + bias_ref[...]
    ls = jnp.minimum(f, 0.0) - jnp.log(1.0 + jnp.exp(-jnp.abs(f)))
    upper = (_iota((FOX_CBLK, FOX_CBLK), 0) <= _iota((FOX_CBLK, FOX_CBLK), 1)).astype(BF16)
    carry = jnp.zeros((FOX_HEADS, 1), F32)
    parts = []
    for c in range(t // FOX_CBLK):
        hi, mid, lo = _split3(ls[:, c * FOX_CBLK:(c + 1) * FOX_CBLK])
        cb = _dot(hi, upper) + _dot(mid, upper) + _dot(lo, upper) + carry
        carry = cb[:, FOX_CBLK - 1:FOX_CBLK]
        parts.append(cb)
    csum = jnp.concatenate(parts, axis=1) * LOG2E
    c1, c2, c3 = (v.astype(F32) for v in _split3(csum))
    ones = jnp.ones((3 * FOX_HEADS, t), F32)
    zeros = jnp.zeros((LANES - 6 * FOX_HEADS, t), F32)
    cq_ref[0] = jnp.concatenate([c1, c2, c3, ones, zeros], axis=0).astype(BF16)
    slab_k = jnp.concatenate([ones, -c1, -c2, -c3, zeros], axis=0)
    for c in range(t // LANES):
        sl = slice(c * LANES, (c + 1) * LANES)
        augk_ref[0, sl, :] = slab_k[:, sl].T.astype(BF16)


def _foxprep(gf, bias):
    b, t, _ = gf.shape
    return pl.pallas_call(
        _foxprep_kernel,
        grid=(b,),
        in_specs=[pl.BlockSpec((1, t, LANES), lambda i: (i, 0, 0)),
                  pl.BlockSpec(bias.shape, lambda i: (0, 0))],
        out_specs=[pl.BlockSpec((1, LANES, t), lambda i: (i, 0, 0)),
                   pl.BlockSpec((1, t, LANES), lambda i: (i, 0, 0))],
        out_shape=[jax.ShapeDtypeStruct((b, LANES, t), BF16),
                   jax.ShapeDtypeStruct((b, t, LANES), BF16)],
        compiler_params=pltpu.CompilerParams(dimension_semantics=("arbitrary",)),
        name="fox_prep",
    )(gf, bias)


FOX_TQ = 256
FOX_TK = 256
FOX_NP = FOX_HEADS // 2
FOX_N = 2 * FOX_TQ
FOX_ROWS = 2 * HEAD_DIM + 16


def _fox_kernel(q_ref, k_ref, v_ref, cq_ref, augk_ref, tri_ref, o_ref, vt_ref, m_ref, acc_ref, s0_ref, s1_ref):
    qb = pl.program_id(1)
    t = k_ref.shape[1]

    @pl.when(qb == 0)
    def _():
        for p in range(FOX_NP):
            for c in range(t // LANES):
                sl = slice(c * LANES, (c + 1) * LANES)
                vt_ref[p, 0:LANES, sl] = v_ref[0, sl, p * LANES:(p + 1) * LANES].astype(F32).T.astype(BF16)
            vt_ref[p, LANES:FOX_ROWS, :] = _ones_rows(t)

    row = _iota((LANES, FOX_TQ), 0)
    hrow = row % FOX_HEADS
    cq = cq_ref[0].astype(F32)
    qs = []
    for p in range(FOX_NP):
        qt = q_ref[0, :, p * LANES:(p + 1) * LANES].astype(F32).T * (HEAD_DIM ** -0.5 * LOG2E)
        q_top = jnp.concatenate([jnp.where(row < HEAD_DIM, qt, 0.0), jnp.where(row >= HEAD_DIM, qt, 0.0)], axis=1)
        q_aug = jnp.concatenate([jnp.where(hrow == 2 * p, cq, 0.0), jnp.where(hrow == 2 * p + 1, cq, 0.0)], axis=1)
        qs.append(jnp.concatenate([q_top, q_aug], axis=0).astype(BF16))

    pairs = range(FOX_NP)
    s_bufs = (s0_ref, s1_ref)

    def scores(kb):
        k0 = pl.multiple_of(kb * FOX_TK, FOX_TK)
        ak = augk_ref[0, pl.ds(k0, FOX_TK), :]
        return [_dot(jnp.concatenate([k_ref[0, pl.ds(k0, FOX_TK), p * LANES:(p + 1) * LANES], ak], axis=1), qs[p])
                for p in pairs]

    def qk_to(buf, j):
        for p, s in enumerate(scores(j - 1)):
            s_bufs[buf][p] = s

    def spv_from(buf, j):
        kb = jnp.where(j == 0, qb, j - 1)
        k0 = pl.multiple_of(kb * FOX_TK, FOX_TK)
        _softmax_pv([s_bufs[buf][p] for p in pairs], [vt_ref[p, :, pl.ds(k0, FOX_TK)] for p in pairs],
                    [m_ref.at[p] for p in pairs], [acc_ref.at[p] for p in pairs])

    tri = tri_ref[...]
    for p, s in enumerate(scores(qb)):
        s0_ref[p] = s + tri
        _flash_init(m_ref.at[p], acc_ref.at[p])
    _pipelined_sweep(qb, qk_to, spv_from)
    for p in range(FOX_NP):
        acc = acc_ref[p]
        den = acc[2 * HEAD_DIM:2 * HEAD_DIM + 1]
        o_a = acc[0:HEAD_DIM, 0:FOX_TQ] / den[:, 0:FOX_TQ]
        o_b = acc[HEAD_DIM:2 * HEAD_DIM, FOX_TQ:FOX_N] / den[:, FOX_TQ:FOX_N]
        o_ref[0, :, p * LANES:(p + 1) * LANES] = jnp.concatenate([o_a, o_b], axis=0).T.astype(BF16)


def _fox(proj, cq, augk, tri):
    b, t, _ = proj.shape
    nq = t // FOX_TQ
    width = FOX_HEADS * HEAD_DIM
    qf, kf, vf = COL_QF // width, COL_KF // width, COL_VF // width
    return pl.pallas_call(
        _fox_kernel,
        grid=(b, nq),
        in_specs=[
            pl.BlockSpec((1, FOX_TQ, width), lambda i, k: (i, k, qf)),
            pl.BlockSpec((1, t, width), lambda i, k: (i, 0, kf)),
            pl.BlockSpec((1, t, width), lambda i, k: (i, 0, vf)),
            pl.BlockSpec((1, LANES, FOX_TQ), lambda i, k: (i, 0, k)),
            pl.BlockSpec((1, t, LANES), lambda i, k: (i, 0, 0)),
            pl.BlockSpec(tri.shape, lambda i, k: (0, 0)),
        ],
        out_specs=pl.BlockSpec((1, FOX_TQ, width), lambda i, k: (i, k, 0)),
        out_shape=jax.ShapeDtypeStruct((b, t, width), BF16),
        scratch_shapes=[
            pltpu.VMEM((FOX_NP, FOX_ROWS, t), BF16),
            pltpu.VMEM((FOX_NP, 1, FOX_N), F32),
            pltpu.VMEM((FOX_NP, FOX_ROWS, FOX_N), F32),
            pltpu.VMEM((FOX_NP, FOX_TK, FOX_N), F32),
            pltpu.VMEM((FOX_NP, FOX_TK, FOX_N), F32),
        ],
        compiler_params=pltpu.CompilerParams(
            dimension_semantics=("arbitrary", "arbitrary"), vmem_limit_bytes=VMEM_LIMIT),
        name="fox_attention",
    )(proj, proj, proj, cq, augk, tri)


def _post_kernel(x_ref, on_ref, of_ref, mg_ref, wn_ref, wf_ref, wo_ref, gm_ref, wu_ref, wd_ref, gfin_ref, o_ref):
    d = x_ref.shape[1]
    ga = mg_ref[:, 0:d].astype(F32)
    gb = mg_ref[:, d:2 * d].astype(F32)
    y = ga * _dot(on_ref[...], wn_ref[...]) + gb * _dot(of_ref[...], wf_ref[...])
    h = x_ref[...] + _dot(y.astype(BF16), wo_ref[...])
    r = lax.rsqrt(jnp.mean(h * h, axis=-1, keepdims=True) + NORM_EPS)
    v = (h * r * gm_ref[...]).astype(BF16)
    acc = h
    ff = wu_ref.shape[1]
    for c0 in range(0, ff, 1024):
        up = jnp.maximum(_dot(v, wu_ref[:, c0:c0 + 1024]), 0.0)
        acc = acc + _dot((up * up).astype(BF16), wd_ref[c0:c0 + 1024, :])
    r2 = lax.rsqrt(jnp.mean(acc * acc, axis=-1, keepdims=True) + NORM_EPS)
    o_ref[...] = acc * r2 * gfin_ref[...]


def _post(x2, on, of, mg, wn, wf, wo, gm, wu, wd, gfin, tm):
    n, d = x2.shape
    row = lambda w: pl.BlockSpec((tm, w), lambda i: (i, 0))
    const = lambda a: pl.BlockSpec(a.shape, lambda i: (0, 0), pipeline_mode=pl.Buffered(1))
    return pl.pallas_call(
        _post_kernel,
        grid=(n // tm,),
        in_specs=[row(d), row(on.shape[1]), row(of.shape[1]), row(2 * d),
                  const(wn), const(wf), const(wo), const(gm), const(wu), const(wd), const(gfin)],
        out_specs=row(d),
        out_shape=jax.ShapeDtypeStruct((n, d), F32),
        compiler_params=pltpu.CompilerParams(
            dimension_semantics=("arbitrary",), vmem_limit_bytes=VMEM_LIMIT),
        name="post_mlp",
    )(x2, on, of, mg, wn, wf, wo, gm, wu, wd, gfin)


def _position_constants(t):
    def key_side(pos, pad):
        a = np.zeros((pos.size, LANES), np.float32)
        for i in range(3):
            a[:, AUG_POS + 2 * i] = pos // 64
            a[:, AUG_POS + 2 * i + 1] = pos % 64
        a[:, AUG_TQ:AUG_TQ + 3] = 1.0
        a[:, AUG_PAD] = pad
        return a

    pos = np.arange(t)
    augk = key_side(pos, 0.0)
    augk[pos, AUG_MASK + pos // SEL_BLOCK] = 1.0
    augw = np.concatenate([key_side(np.zeros(WINDOW, np.int64), 1.0), key_side(pos, 0.0)], axis=0)
    augc = key_side(np.arange(LANES) * CMP_STRIDE + CMP_BLOCK - 1, 0.0)

    nc = (t - CMP_BLOCK) // CMP_STRIDE + 1
    ns = t // SEL_BLOCK
    n = np.arange(LANES)[None, :]
    j = np.arange(ns)[:, None]
    ovt = ((n * CMP_STRIDE <= j * SEL_BLOCK + SEL_BLOCK - 1) & (n * CMP_STRIDE + CMP_BLOCK - 1 >= j * SEL_BLOCK) & (n < nc))

    def tri(tk, tq, anti=False):
        kk = np.arange(tk)[:, None]
        tt = np.arange(tq)[None, :]
        ok = (kk > tt) if anti else (kk <= tt)
        return np.where(ok, 0.0, NEG_BIG).astype(np.float32)

    edge = tri(NSA_TQ, NSA_TQ)
    trisel = np.zeros((SEL_TK // NSA_TQ, SEL_TK, NSA_TQ), np.float32)
    for v in range(SEL_TK // NSA_TQ):
        trisel[v, v * NSA_TQ:(v + 1) * NSA_TQ] = edge
    triwin = np.stack([edge, tri(NSA_TQ, NSA_TQ, anti=True)])
    trifox = np.tile(tri(FOX_TK, FOX_TQ), (1, 2))
    return (jnp.asarray(augk, BF16), jnp.asarray(augw, BF16), jnp.asarray(augc, BF16), jnp.asarray(ovt, BF16),
            jnp.asarray(trisel), jnp.asarray(triwin), jnp.asarray(trifox))


def kernel(x, norm_mix, w_in, cmp_pos_k, cmp_w1_k, cmp_w2_k, cmp_pos_v, cmp_w1_v, cmp_w2_v, fox_f_bias, w_branch_nsa, w_branch_fox, w_merge_gate, b_merge_gate, w_out, norm_mlp, w_up, w_down, norm_final):
    b, t, d = x.shape
    assert w_in.shape[0] == 1, "one layer: the final norm is fused into the MLP kernel"
    assert t % FOX_TQ == 0 and t // SEL_BLOCK == 32 and (t - CMP_BLOCK) // CMP_STRIDE + 1 == LANES - 1
    augk, augw, augc, ovt, trisel, triwin, trifox = _position_constants(t)
    h = x.reshape(b * t, d)
    w = w_in[0]
    nq_w = NSA_HEADS * HEAD_DIM
    gate0 = nq_w + 6 * LANES
    fox0 = gate0 + 3 * NSA_HEADS
    wa = jnp.concatenate([w[:, :nq_w], w[:, fox0:fox0 + 3 * FOX_HEADS * HEAD_DIM], w[:, nq_w:gate0]], axis=1).astype(BF16)
    wg = jnp.concatenate([w[:, gate0:fox0], w[:, -FOX_HEADS:],
                          jnp.zeros((d, LANES - 3 * NSA_HEADS - FOX_HEADS), F32)], axis=1).astype(BF16)
    proj, gf, mg = _inproj(h, norm_mix[0][None, :], wa, wg, w_merge_gate[0].astype(BF16),
                           b_merge_gate[0][None, :], ROW_TILE)
    proj = proj.reshape(b, t, PROJ_W)
    gf = gf.reshape(b, t, LANES)

    def w1_blocks(w1):
        wa_ = w1[:CMP_STRIDE * HEAD_DIM].reshape(CMP_STRIDE, HEAD_DIM, CMP_HIDDEN)
        wb_ = w1[CMP_STRIDE * HEAD_DIM:].reshape(CMP_STRIDE, HEAD_DIM, CMP_HIDDEN)
        z = jnp.zeros_like(wa_)
        top = jnp.concatenate([wa_, z, wb_, z], axis=2)
        bot = jnp.concatenate([z, wa_, z, wb_], axis=2)
        return jnp.concatenate([top, bot], axis=1).astype(BF16)

    def w2_pair(w2):
        z = jnp.zeros_like(w2)
        return (jnp.concatenate([w2, z], axis=1).astype(BF16), jnp.concatenate([z, w2], axis=1).astype(BF16))

    def pos_rows(p):
        row = jnp.concatenate([p[:CMP_STRIDE], p[CMP_STRIDE:]], axis=1)[:, None, :]
        return jnp.concatenate([row, jnp.zeros((CMP_STRIDE, 7, 2 * HEAD_DIM), F32)], axis=1).astype(BF16)

    w2k0, w2k1 = w2_pair(cmp_w2_k[0])
    w2v0, w2v1 = w2_pair(cmp_w2_v[0])
    kcmp, vcmp = _compress(proj, w1_blocks(cmp_w1_k[0]), w2k0, w2k1, pos_rows(cmp_pos_k[0]),
                           w1_blocks(cmp_w1_v[0]), w2v0, w2v1, pos_rows(cmp_pos_v[0]))
    o_n = _nsa(proj, gf, kcmp, vcmp, augk, augw, augc, ovt, trisel, triwin)
    cq, faug = _foxprep(gf, fox_f_bias[0][:, None])
    o_f = _fox(proj, cq, faug, trifox)
    out = _post(h, o_n.reshape(b * t, -1), o_f.reshape(b * t, -1), mg,
                w_branch_nsa[0].astype(BF16), w_branch_fox[0].astype(BF16), w_out[0].astype(BF16),
                norm_mlp[0][None, :], w_up[0].astype(BF16), w_down[0].astype(BF16), norm_final[None, :], ROW_TILE)
    return out.reshape(b, t, d)
```

```python
import numpy as np
import jax
import jax.numpy as jnp
from jax import lax
from jax.experimental import pallas as pl
from jax.experimental.pallas import tpu as pltpu

HEAD_DIM = 64
NSA_HEADS = 8
NSA_GROUPS = 2
NSA_HPG = NSA_HEADS // NSA_GROUPS
FOX_HEADS = 8
CMP_BLOCK = 32
CMP_STRIDE = 16
CMP_HIDDEN = 128
SEL_BLOCK = 64
N_SELECT = 16
WINDOW = 512
NORM_EPS = 1e-6
FORCE_SCORE = 1e9

LANES = 128
NEG_BIG = -(2.0 ** 100)
M_INIT = -3.0e38
LOG2E = 1.4426950408889634
VMEM_LIMIT = 56 * 1024 * 1024
ROW_TILE = 512

PROJ_W = 2816
COL_QN = 0
COL_QF = 512
COL_KF = 1024
COL_VF = 1536
COL_KV = 2048
GF_GATE = 0
GF_FOX = 24

F32 = jnp.float32
BF16 = jnp.bfloat16


def _dot(a, b):
    return jnp.dot(a, b, preferred_element_type=F32)


def _iota(shape, dim):
    return lax.broadcasted_iota(jnp.int32, shape, dim)


def _split3(x):
    hi = x.astype(BF16)
    r1 = x - hi.astype(F32)
    mid = r1.astype(BF16)
    lo = (r1 - mid.astype(F32)).astype(BF16)
    return hi, mid, lo


def _inproj_kernel(x_ref, g_ref, wa_ref, wg_ref, wm_ref, bm_ref, proj_ref, gf_ref, mg_ref):
    x = x_ref[...]
    r = lax.rsqrt(jnp.mean(x * x, axis=-1, keepdims=True) + NORM_EPS)
    u = (x * r * g_ref[...]).astype(BF16)
    for c0, c1 in ((0, 1024), (1024, 2048), (2048, PROJ_W)):
        proj_ref[:, c0:c1] = _dot(u, wa_ref[:, c0:c1]).astype(BF16)
    gf_ref[...] = _dot(u, wg_ref[...])
    for c0 in (0, 1024):
        z = _dot(u, wm_ref[:, c0:c0 + 1024]) + bm_ref[:, c0:c0 + 1024]
        mg_ref[:, c0:c0 + 1024] = jax.nn.sigmoid(z).astype(BF16)


def _inproj(x2, g, wa, wg, wm, bm, tm):
    n, d = x2.shape
    const = lambda i: (0, 0)
    return pl.pallas_call(
        _inproj_kernel,
        grid=(n // tm,),
        in_specs=[
            pl.BlockSpec((tm, d), lambda i: (i, 0)),
            pl.BlockSpec((1, d), const),
            pl.BlockSpec(wa.shape, const),
            pl.BlockSpec(wg.shape, const),
            pl.BlockSpec(wm.shape, const),
            pl.BlockSpec(bm.shape, const),
        ],
        out_specs=[
            pl.BlockSpec((tm, PROJ_W), lambda i: (i, 0)),
            pl.BlockSpec((tm, LANES), lambda i: (i, 0)),
            pl.BlockSpec((tm, 2 * d), lambda i: (i, 0)),
        ],
        out_shape=[
            jax.ShapeDtypeStruct((n, PROJ_W), BF16),
            jax.ShapeDtypeStruct((n, LANES), F32),
            jax.ShapeDtypeStruct((n, 2 * d), BF16),
        ],
        compiler_params=pltpu.CompilerParams(
            dimension_semantics=("arbitrary",), vmem_limit_bytes=VMEM_LIMIT),
        name="inproj",
    )(x2, g, wa, wg, wm, bm)


def _compress_kernel(kc_ref, vc_ref, wk_ref, w2k0_ref, w2k1_ref, pk_ref,
                     wv_ref, w2v0_ref, w2v1_ref, pv_ref, kco_ref, vco_ref, xs_ref):
    t = kc_ref.shape[1]
    nblk = t // CMP_STRIDE

    def one(x_ref, w_ref, w2_refs, p_ref, o_ref):
        xs_ref[...] = x_ref[0].astype(F32)
        acc = jnp.zeros((nblk, 4 * LANES), F32)
        pos = jnp.zeros((8, 4 * LANES), F32)
        for j in range(CMP_STRIDE):
            xj = xs_ref[pl.ds(j, nblk, stride=CMP_STRIDE), :].astype(BF16)
            acc = acc + _dot(xj, w_ref[j])
            pos = pos + _dot(p_ref[j], w_ref[j])
        posb = pos[0:1, 0:LANES] + pos[0:1, 3 * LANES:4 * LANES]
        out = jnp.zeros((nblk, LANES), F32)
        for g in range(NSA_GROUPS):
            a = acc[:, g * LANES:(g + 1) * LANES]
            b = acc[:, (2 + g) * LANES:(3 + g) * LANES]
            h = a + pltpu.roll(b, nblk - 1, 0) + posb
            out = out + _dot(jax.nn.gelu(h).astype(BF16), w2_refs[g][...])
        o_ref[0] = out.astype(BF16)

    one(kc_ref, wk_ref, (w2k0_ref, w2k1_ref), pk_ref, kco_ref)
    one(vc_ref, wv_ref, (w2v0_ref, w2v1_ref), pv_ref, vco_ref)


def _compress(proj, wk, w2k0, w2k1, pk, wv, w2v0, w2v1, pv):
    b, t, _ = proj.shape
    kvb = COL_KV // LANES
    whole = lambda a: pl.BlockSpec(a.shape, lambda i: (0,) * a.ndim)
    ospec = pl.BlockSpec((1, LANES, LANES), lambda i: (i, 0, 0))
    return pl.pallas_call(
        _compress_kernel,
        grid=(b,),
        in_specs=[pl.BlockSpec((1, t, LANES), lambda i: (i, 0, kvb)),
                  pl.BlockSpec((1, t, LANES), lambda i: (i, 0, kvb + 1)),
                  whole(wk), whole(w2k0), whole(w2k1), whole(pk), whole(wv), whole(w2v0), whole(w2v1), whole(pv)],
        out_specs=[ospec, ospec],
        out_shape=[jax.ShapeDtypeStruct((b, LANES, LANES), BF16)] * 2,
        scratch_shapes=[pltpu.VMEM((t, LANES), F32)],
        compiler_params=pltpu.CompilerParams(dimension_semantics=("arbitrary",)),
        name="compress",
    )(proj, proj, wk, w2k0, w2k1, pk, wv, w2v0, w2v1, pv)


def _flash_init(m_ref, acc_ref):
    m_ref[...] = jnp.full(m_ref.shape, M_INIT, F32)
    acc_ref[...] = jnp.zeros(acc_ref.shape, F32)


def _softmax_pv(scores, vt_blks, m_refs=None, acc_refs=None):
    def pv(vt, p):
        if not isinstance(vt, tuple):
            return _dot(vt, p)
        w = p.shape[1] // len(vt)
        return jnp.concatenate([_dot(v, p[:, j * w:(j + 1) * w]) for j, v in enumerate(vt)], axis=1)

    ps, alphas = [], []
    for i, s in enumerate(scores):
        mx = jnp.max(s, axis=0, keepdims=True)
        if m_refs is None:
            ps.append(jnp.exp2(s - mx).astype(BF16))
            continue
        m_old = m_refs[i][...]
        m_new = jnp.maximum(m_old, mx)
        ps.append(jnp.exp2(s - m_new).astype(BF16))
        alphas.append(jnp.exp2(m_old - m_new))
        m_refs[i][...] = m_new
    if m_refs is None:
        return [pv(vt, p) for vt, p in zip(vt_blks, ps)]
    for vt, p, alpha, acc_ref in zip(vt_blks, ps, alphas, acc_refs):
        acc_ref[...] = acc_ref[...] * alpha + pv(vt, p)
    return None


def _pipelined_sweep(n_rest, qk_to, spv_from):
    def body(jj, carry):
        j = 2 * jj
        qk_to(1, j + 1)
        spv_from(0, j)
        qk_to(0, j + 2)
        spv_from(1, j + 1)
        return carry

    lax.fori_loop(0, n_rest // 2, body, 0)

    @pl.when(n_rest % 2 == 1)
    def _():
        qk_to(1, n_rest)
        spv_from(0, n_rest - 1)
        spv_from(1, n_rest)

    @pl.when(n_rest % 2 == 0)
    def _():
        spv_from(0, n_rest)


def _ones_rows(tk):
    return jnp.where(_iota((16, tk), 0) == 0, 1.0, 0.0).astype(BF16)


NSA_TQ = 128
NSA_N = NSA_HPG * NSA_TQ
SEL_TK = 512
WIN_KEYS = WINDOW + NSA_TQ

AUG_POS = 0
AUG_TQ = 6
AUG_PAD = 9
AUG_MASK = 32


def _bf16_terms(x, n=3):
    out, r = [], np.float32(x)
    for _ in range(n):
        term = np.asarray(r, np.float32).astype(BF16).astype(np.float32)
        out.append(float(term))
        r = np.float32(r - term)
    return out


def _nsa_queries(q_ref, blk):
    t0 = blk * NSA_TQ
    qt = q_ref[0].astype(F32).T * (HEAD_DIM ** -0.5 * LOG2E)
    lane = _iota((1, NSA_N), 1)
    hl = lane // NSA_TQ
    tqf = (t0 + lane % NSA_TQ).astype(F32)
    ridx = _iota((16, NSA_N), 0)
    zeros_h = jnp.zeros((HEAD_DIM, NSA_TQ), F32)
    l2e = _bf16_terms(LOG2E)
    q_top, aug16 = [], []
    for g in range(NSA_GROUPS):
        tops = []
        for h in range(NSA_HPG):
            head = NSA_HPG * g + h
            qh = qt[head * HEAD_DIM:(head + 1) * HEAD_DIM]
            tops.append(jnp.concatenate([qh, zeros_h] if g == 0 else [zeros_h, qh], axis=0))
        q_top.append(jnp.concatenate(tops, axis=1))
        slope = lax.bitcast_convert_type((127 - (NSA_HPG * g + hl + 1)) << 23, F32)
        rows = {}
        for i, term in enumerate(l2e):
            rows[AUG_POS + 2 * i] = slope * (64.0 * term)
            rows[AUG_POS + 2 * i + 1] = slope * term
        for i, term in enumerate(_split3(-(slope * tqf) * LOG2E)):
            rows[AUG_TQ + i] = term.astype(F32)
        a = jnp.zeros((16, NSA_N), F32)
        for r, val in rows.items():
            a = jnp.where(ridx == r, val, a)
        aug16.append(a)
    return q_top, aug16


def _nsa_compressed(s_cmp, blk, vct_ref, ovt, ocmp_ref):
    tq = blk * NSA_TQ + _iota((1, NSA_N), 1) % NSA_TQ
    n_idx = _iota((LANES, NSA_N), 0)
    valid = (n_idx * CMP_STRIDE + (CMP_BLOCK - 1) <= tq) & (n_idx < LANES - 1)
    imps = []
    for g in range(NSA_GROUPS):
        sm = jnp.where(valid, s_cmp[g], NEG_BIG)
        mx = jnp.max(sm, axis=0, keepdims=True)
        e = jnp.where(valid, jnp.exp2(sm - mx), 0.0)
        den = jnp.sum(e, axis=0, keepdims=True)
        pn = e / jnp.where(den > 0.0, den, 1.0)
        ocmp_ref[g] = _dot(vct_ref[g * HEAD_DIM:(g + 1) * HEAD_DIM, :], pn.astype(BF16))

        psum = pn[:, 0:NSA_TQ]
        for h in range(1, NSA_HPG):
            psum = psum + pn[:, h * NSA_TQ:(h + 1) * NSA_TQ]
        hi, mid, lo = _split3(psum)
        imps.append(_dot(ovt, hi) + _dot(ovt, mid) + _dot(ovt, lo))
    return imps


def _nsa_select(imps, blk, q_top, aug16, qsel_ref):
    t0 = blk * NSA_TQ
    ns = imps[0].shape[0]
    tcol = t0 + _iota((8, NSA_TQ), 1)
    for g, imp in enumerate(imps):
        tiles = range(ns // 8)
        j_idx = [8 * r + _iota((8, NSA_TQ), 0) for r in tiles]
        in_past = [j_idx[r] * SEL_BLOCK <= tcol for r in tiles]
        cur = tcol // SEL_BLOCK
        score = []
        for r in tiles:
            forced = (j_idx[r] == 0) | (j_idx[r] == cur) | (j_idx[r] == cur - 1)
            score.append(jnp.where(forced, FORCE_SCORE, jnp.where(in_past[r], imp[8 * r:8 * r + 8], -1.0)))
        rank = [jnp.zeros((8, NSA_TQ), F32) for _ in tiles]
        for i in range(ns):
            si = score[i // 8][i % 8:i % 8 + 1, :]
            for r in tiles:
                if 8 * r > i:
                    beats = si >= score[r]
                elif 8 * r + 7 <= i:
                    beats = si > score[r]
                else:
                    beats = (si > score[r]) | ((si == score[r]) & (j_idx[r] > i))
                rank[r] = rank[r] + jnp.where(beats, 1.0, 0.0)
        maskval = jnp.concatenate([jnp.where((rank[r] < min(N_SELECT, ns)) & in_past[r], 0.0, NEG_BIG) for r in tiles], axis=0)
        mask4 = jnp.concatenate([maskval] * NSA_HPG, axis=1)
        qsel_ref[g] = jnp.concatenate([q_top[g], aug16[g], jnp.zeros((AUG_MASK - 16, NSA_N), F32), mask4,
                                       jnp.zeros((LANES - AUG_MASK - ns, NSA_N), F32)], axis=0).astype(BF16)


def _nsa_kernel(q_ref, gf_ref, kc_ref, vc_ref, ks_ref, vs_ref, kw_ref, vw_ref,
                augk_ref, augw_ref, augc_ref, ovt_ref, triwin_ref, o_ref,
                vst_ref, vwt_ref, kwp_ref, vct_ref, gt_ref, m_ref, acc_ref, s0_ref, s1_ref,
                qsel_ref, ocmp_ref):
    qb = pl.program_id(1)
    t = ks_ref.shape[1]
    t0 = qb * NSA_TQ
    groups = range(NSA_GROUPS)
    zeros_tail = jnp.zeros((LANES - 16, NSA_N), F32)

    def cmp_scores(q_top, aug16):
        kcb = jnp.concatenate([kc_ref[0], augc_ref[...]], axis=1)
        return [_dot(kcb, jnp.concatenate([q_top[g], aug16[g], zeros_tail], axis=0).astype(BF16)) for g in groups]

    @pl.when(qb == 0)
    def _():
        vwt_ref[:, 0:WINDOW] = jnp.zeros((LANES, WINDOW), BF16)
        kwp_ref[0:WINDOW, :] = jnp.zeros((WINDOW, LANES), BF16)
        kwp_ref[WINDOW:WINDOW + t, :] = kw_ref[0]
        for c in range(t // LANES):
            sl = slice(c * LANES, (c + 1) * LANES)
            vst_ref[:, sl] = vs_ref[0, sl, :].astype(F32).T.astype(BF16)
            vwt_ref[:, WINDOW + c * LANES:WINDOW + (c + 1) * LANES] = vw_ref[0, sl, :].astype(F32).T.astype(BF16)
        vct_ref[...] = vc_ref[0].astype(F32).T.astype(BF16)

    q_top, aug16 = _nsa_queries(q_ref, qb)
    s_cmp = cmp_scores(q_top, aug16)
    kw0 = pl.multiple_of(t0, NSA_TQ)
    kwin = jnp.concatenate([kwp_ref[pl.ds(kw0, WIN_KEYS), :], augw_ref[pl.ds(kw0, WIN_KEYS), :]], axis=1)
    pad_row = _iota((16, NSA_N), 0) == AUG_PAD
    s_win = [_dot(kwin, jnp.concatenate([q_top[g], jnp.where(pad_row, NEG_BIG, aug16[g]), zeros_tail], axis=0).astype(BF16))
             for g in groups]
    _nsa_select(_nsa_compressed(s_cmp, qb, vct_ref, ovt_ref[...], ocmp_ref), qb, q_top, aug16, qsel_ref)

    ones_s = _ones_rows(SEL_TK)
    per_chunk = SEL_TK // NSA_TQ
    n_full = qb // per_chunk
    s_bufs = (s0_ref, s1_ref)
    hi_edge = jnp.concatenate([triwin_ref[0]] * NSA_HPG, axis=1)
    lo_edge = jnp.concatenate([triwin_ref[1]] * NSA_HPG, axis=1)

    def sel_scores(cb):
        k0 = pl.multiple_of(cb * SEL_TK, SEL_TK)
        kblk = jnp.concatenate([ks_ref[0, pl.ds(k0, SEL_TK), :], augk_ref[pl.ds(k0, SEL_TK), :]], axis=1)
        return [_dot(kblk, qsel_ref[g]) for g in groups]

    def qk_to(buf, j):
        for g, s in enumerate(sel_scores(j - 1)):
            s_bufs[buf][g] = s

    def spv_from(buf, j):
        cb = jnp.where(j == 0, n_full, j - 1)
        k0 = pl.multiple_of(cb * SEL_TK, SEL_TK)
        vts = [jnp.concatenate([vst_ref[g * HEAD_DIM:(g + 1) * HEAD_DIM, pl.ds(k0, SEL_TK)], ones_s], axis=0)
               for g in groups]
        _softmax_pv([s_bufs[buf][g] for g in groups], vts,
                    [m_ref.at[g] for g in groups], [acc_ref.at[g] for g in groups])

    doff = pl.multiple_of((qb % per_chunk) * NSA_TQ, NSA_TQ)
    for g, s in enumerate(sel_scores(n_full)):
        s0_ref[g] = s
        s0_ref[g, pl.ds(doff, NSA_TQ), :] = s0_ref[g, pl.ds(doff, NSA_TQ), :] + hi_edge
        _flash_init(m_ref.at[g], acc_ref.at[g])

    ones_w = _ones_rows(WIN_KEYS)
    s_win = [jnp.concatenate([s[0:NSA_TQ] + lo_edge, s[NSA_TQ:WINDOW], s[WINDOW:WIN_KEYS] + hi_edge], axis=0)
             for s in s_win]
    vt_win = [jnp.concatenate([vwt_ref[g * HEAD_DIM:(g + 1) * HEAD_DIM, pl.ds(kw0, WIN_KEYS)], ones_w], axis=0)
              for g in groups]
    o_win = [acc[0:HEAD_DIM] / acc[HEAD_DIM:HEAD_DIM + 1] for acc in _softmax_pv(s_win, vt_win)]

    _pipelined_sweep(n_full, qk_to, spv_from)

    gt_ref[...] = gf_ref[0].T
    sig = jax.nn.sigmoid(gt_ref[GF_GATE:GF_GATE + 3 * NSA_HEADS, :])
    outs = []
    for g in groups:
        acc = acc_ref[g]
        o_sel = acc[0:HEAD_DIM] / acc[HEAD_DIM:HEAD_DIM + 1]
        o_cmp = ocmp_ref[g]
        for h in range(NSA_HPG):
            sl = slice(h * NSA_TQ, (h + 1) * NSA_TQ)
            head = NSA_HPG * g + h
            g0 = sig[head:head + 1]
            g1 = sig[NSA_HEADS + head:NSA_HEADS + head + 1]
            g2 = sig[2 * NSA_HEADS + head:2 * NSA_HEADS + head + 1]
            outs.append(g0 * o_cmp[:, sl] + g1 * o_sel[:, sl] + g2 * o_win[g][:, sl])
    o_ref[0] = jnp.concatenate(outs, axis=0).T.astype(BF16)


def _nsa(proj, gf, kcmp, vcmp, augk, augw, augc, ovt, triwin):
    b, t, _ = proj.shape
    nq = t // NSA_TQ
    kvb = COL_KV // LANES
    width = NSA_HEADS * HEAD_DIM
    whole = lambda a: pl.BlockSpec(a.shape, lambda i, k: (0,) * a.ndim)
    kv = lambda idx: pl.BlockSpec((1, t, LANES), lambda i, k: (i, 0, idx))
    return pl.pallas_call(
        _nsa_kernel,
        grid=(b, nq),
        in_specs=[
            pl.BlockSpec((1, NSA_TQ, width), lambda i, k: (i, k, COL_QN // width)),
            pl.BlockSpec((1, NSA_TQ, LANES), lambda i, k: (i, k, 0)),
            pl.BlockSpec((1, LANES, LANES), lambda i, k: (i, 0, 0)),
            pl.BlockSpec((1, LANES, LANES), lambda i, k: (i, 0, 0)),
            kv(kvb + 2), kv(kvb + 3), kv(kvb + 4), kv(kvb + 5),
            whole(augk), whole(augw), whole(augc), whole(ovt), whole(triwin),
        ],
        out_specs=pl.BlockSpec((1, NSA_TQ, width), lambda i, k: (i, k, 0)),
        out_shape=jax.ShapeDtypeStruct((b, t, width), BF16),
        scratch_shapes=[
            pltpu.VMEM((LANES, t), BF16),
            pltpu.VMEM((LANES, t + WINDOW), BF16),
            pltpu.VMEM((t + WINDOW, LANES), BF16),
            pltpu.VMEM((LANES, LANES), BF16),
            pltpu.VMEM((LANES, NSA_TQ), F32),
            pltpu.VMEM((NSA_GROUPS, 1, NSA_N), F32),
            pltpu.VMEM((NSA_GROUPS, HEAD_DIM + 16, NSA_N), F32),
            pltpu.VMEM((NSA_GROUPS, SEL_TK, NSA_N), F32),
            pltpu.VMEM((NSA_GROUPS, SEL_TK, NSA_N), F32),
            pltpu.VMEM((NSA_GROUPS, 2 * LANES, NSA_N), BF16),
            pltpu.VMEM((NSA_GROUPS, HEAD_DIM, NSA_N), F32),
        ],
        compiler_params=pltpu.CompilerParams(
            dimension_semantics=("arbitrary", "arbitrary"), vmem_limit_bytes=VMEM_LIMIT),
        name="nsa_attention",
    )(proj, gf, kcmp, vcmp, proj, proj, proj, proj, augk, augw, augc, ovt, triwin)


FOX_CBLK = 256


def _foxprep_kernel(gf_ref, bias_ref, cq_ref, augk_ref):
    t = gf_ref.shape[1]
    cols = []
    for c in range(t // LANES):
        cols.append(gf_ref[0, c * LANES:(c + 1) * LANES, :].T[GF_FOX:GF_FOX + FOX_HEADS])
    f = jnp.concatenate(cols, axis=1) + bias_ref[...]
    ls = jnp.minimum(f, 0.0) - jnp.log(1.0 + jnp.exp(-jnp.abs(f)))
    upper = (_iota((FOX_CBLK, FOX_CBLK), 0) <= _iota((FOX_CBLK, FOX_CBLK), 1)).astype(BF16)
    carry = jnp.zeros((FOX_HEADS, 1), F32)
    parts = []
    for c in range(t // FOX_CBLK):
        hi, mid, lo = _split3(ls[:, c * FOX_CBLK:(c + 1) * FOX_CBLK])
        cb = _dot(hi, upper) + _dot(mid, upper) + _dot(lo, upper) + carry
        carry = cb[:, FOX_CBLK - 1:FOX_CBLK]
        parts.append(cb)
    csum = jnp.concatenate(parts, axis=1) * LOG2E
    c1, c2, c3 = (v.astype(F32) for v in _split3(csum))
    ones = jnp.ones((3 * FOX_HEADS, t), F32)
    zeros = jnp.zeros((LANES - 6 * FOX_HEADS, t), F32)
    cq_ref[0] = jnp.concatenate([c1, c2, c3, ones, zeros], axis=0).astype(BF16)
    slab_k = jnp.concatenate([ones, -c1, -c2, -c3, zeros], axis=0)
    for c in range(t // LANES):
        sl = slice(c * LANES, (c + 1) * LANES)
        augk_ref[0, sl, :] = slab_k[:, sl].T.astype(BF16)


def _foxprep(gf, bias):
    b, t, _ = gf.shape
    return pl.pallas_call(
        _foxprep_kernel,
        grid=(b,),
        in_specs=[pl.BlockSpec((1, t, LANES), lambda i: (i, 0, 0)),
                  pl.BlockSpec(bias.shape, lambda i: (0, 0))],
        out_specs=[pl.BlockSpec((1, LANES, t), lambda i: (i, 0, 0)),
                   pl.BlockSpec((1, t, LANES), lambda i: (i, 0, 0))],
        out_shape=[jax.ShapeDtypeStruct((b, LANES, t), BF16),
                   jax.ShapeDtypeStruct((b, t, LANES), BF16)],
        compiler_params=pltpu.CompilerParams(dimension_semantics=("arbitrary",)),
        name="fox_prep",
    )(gf, bias)


FOX_TQ = 256
FOX_TK = 256
FOX_NP = FOX_HEADS // 2
FOX_N = 2 * FOX_TQ
FOX_ROWS = HEAD_DIM + 16


def _fox_kernel(q_ref, k_ref, v_ref, cq_ref, augk_ref, tri_ref, o_ref, vt_ref, m_ref, acc_ref, s0_ref, s1_ref):
    qb = pl.program_id(1)
    t = k_ref.shape[1]

    @pl.when(qb == 0)
    def _():
        for p in range(FOX_NP):
            for c in range(t // LANES):
                sl = slice(c * LANES, (c + 1) * LANES)
                vt = v_ref[0, sl, p * LANES:(p + 1) * LANES].astype(F32).T.astype(BF16)
                for hh in range(2):
                    vt_ref[p, hh, 0:HEAD_DIM, sl] = vt[hh * HEAD_DIM:(hh + 1) * HEAD_DIM]
            for hh in range(2):
                vt_ref[p, hh, HEAD_DIM:FOX_ROWS, :] = _ones_rows(t)

    row = _iota((LANES, FOX_TQ), 0)
    hrow = row % FOX_HEADS
    cq = cq_ref[0].astype(F32)
    qs = []
    for p in range(FOX_NP):
        qt = q_ref[0, :, p * LANES:(p + 1) * LANES].astype(F32).T * (HEAD_DIM ** -0.5 * LOG2E)
        q_top = jnp.concatenate([jnp.where(row < HEAD_DIM, qt, 0.0), jnp.where(row >= HEAD_DIM, qt, 0.0)], axis=1)
        q_aug = jnp.concatenate([jnp.where(hrow == 2 * p, cq, 0.0), jnp.where(hrow == 2 * p + 1, cq, 0.0)], axis=1)
        qs.append(jnp.concatenate([q_top, q_aug], axis=0).astype(BF16))

    pairs = range(FOX_NP)
    s_bufs = (s0_ref, s1_ref)

    def scores(kb):
        k0 = pl.multiple_of(kb * FOX_TK, FOX_TK)
        ak = augk_ref[0, pl.ds(k0, FOX_TK), :]
        return [_dot(jnp.concatenate([k_ref[0, pl.ds(k0, FOX_TK), p * LANES:(p + 1) * LANES], ak], axis=1), qs[p])
                for p in pairs]

    def qk_to(buf, j):
        for p, s in enumerate(scores(j - 1)):
            s_bufs[buf][p] = s

    def spv_from(buf, j):
        kb = jnp.where(j == 0, qb, j - 1)
        k0 = pl.multiple_of(kb * FOX_TK, FOX_TK)
        vts = [(vt_ref[p, 0, :, pl.ds(k0, FOX_TK)], vt_ref[p, 1, :, pl.ds(k0, FOX_TK)]) for p in pairs]
        _softmax_pv([s_bufs[buf][p] for p in pairs], vts,
                    [m_ref.at[p] for p in pairs], [acc_ref.at[p] for p in pairs])

    tri = tri_ref[...]
    for p, s in enumerate(scores(qb)):
        s0_ref[p] = s + tri
        _flash_init(m_ref.at[p], acc_ref.at[p])
    _pipelined_sweep(qb, qk_to, spv_from)
    for p in range(FOX_NP):
        acc = acc_ref[p]
        den = acc[HEAD_DIM:HEAD_DIM + 1]
        o_a = acc[0:HEAD_DIM, 0:FOX_TQ] / den[:, 0:FOX_TQ]
        o_b = acc[0:HEAD_DIM, FOX_TQ:FOX_N] / den[:, FOX_TQ:FOX_N]
        o_ref[0, :, p * LANES:(p + 1) * LANES] = jnp.concatenate([o_a, o_b], axis=0).T.astype(BF16)


def _fox(proj, cq, augk, tri):
    b, t, _ = proj.shape
    nq = t // FOX_TQ
    width = FOX_HEADS * HEAD_DIM
    qf, kf, vf = COL_QF // width, COL_KF // width, COL_VF // width
    return pl.pallas_call(
        _fox_kernel,
        grid=(b, nq),
        in_specs=[
            pl.BlockSpec((1, FOX_TQ, width), lambda i, k: (i, k, qf)),
            pl.BlockSpec((1, t, width), lambda i, k: (i, 0, kf)),
            pl.BlockSpec((1, t, width), lambda i, k: (i, 0, vf)),
            pl.BlockSpec((1, LANES, FOX_TQ), lambda i, k: (i, 0, k)),
            pl.BlockSpec((1, t, LANES), lambda i, k: (i, 0, 0)),
            pl.BlockSpec(tri.shape, lambda i, k: (0, 0)),
        ],
        out_specs=pl.BlockSpec((1, FOX_TQ, width), lambda i, k: (i, k, 0)),
        out_shape=jax.ShapeDtypeStruct((b, t, width), BF16),
        scratch_shapes=[
            pltpu.VMEM((FOX_NP, 2, FOX_ROWS, t), BF16),
            pltpu.VMEM((FOX_NP, 1, FOX_N), F32),
            pltpu.VMEM((FOX_NP, FOX_ROWS, FOX_N), F32),
            pltpu.VMEM((FOX_NP, FOX_TK, FOX_N), F32),
            pltpu.VMEM((FOX_NP, FOX_TK, FOX_N), F32),
        ],
        compiler_params=pltpu.CompilerParams(
            dimension_semantics=("arbitrary", "arbitrary"), vmem_limit_bytes=VMEM_LIMIT),
        name="fox_attention",
    )(proj, proj, proj, cq, augk, tri)


def _post_kernel(x_ref, on_ref, of_ref, mg_ref, wn_ref, wf_ref, wo_ref, gm_ref, wu_ref, wd_ref, gfin_ref, o_ref):
    d = x_ref.shape[1]
    ga = mg_ref[:, 0:d].astype(F32)
    gb = mg_ref[:, d:2 * d].astype(F32)
    y = ga * _dot(on_ref[...], wn_ref[...]) + gb * _dot(of_ref[...], wf_ref[...])
    h = x_ref[...] + _dot(y.astype(BF16), wo_ref[...])
    r = lax.rsqrt(jnp.mean(h * h, axis=-1, keepdims=True) + NORM_EPS)
    v = (h * r * gm_ref[...]).astype(BF16)
    acc = h
    ff = wu_ref.shape[1]
    for c0 in range(0, ff, 1024):
        up = jnp.maximum(_dot(v, wu_ref[:, c0:c0 + 1024]), 0.0)
        acc = acc + _dot((up * up).astype(BF16), wd_ref[c0:c0 + 1024, :])
    r2 = lax.rsqrt(jnp.mean(acc * acc, axis=-1, keepdims=True) + NORM_EPS)
    o_ref[...] = acc * r2 * gfin_ref[...]


def _post(x2, on, of, mg, wn, wf, wo, gm, wu, wd, gfin, tm):
    n, d = x2.shape
    row = lambda w: pl.BlockSpec((tm, w), lambda i: (i, 0))
    const = lambda a: pl.BlockSpec(a.shape, lambda i: (0, 0), pipeline_mode=pl.Buffered(1))
    return pl.pallas_call(
        _post_kernel,
        grid=(n // tm,),
        in_specs=[row(d), row(on.shape[1]), row(of.shape[1]), row(2 * d),
                  const(wn), const(wf), const(wo), const(gm), const(wu), const(wd), const(gfin)],
        out_specs=row(d),
        out_shape=jax.ShapeDtypeStruct((n, d), F32),
        compiler_params=pltpu.CompilerParams(
            dimension_semantics=("arbitrary",), vmem_limit_bytes=VMEM_LIMIT),
        name="post_mlp",
    )(x2, on, of, mg, wn, wf, wo, gm, wu, wd, gfin)


def _position_constants(t):
    def key_side(pos, pad):
        a = np.zeros((pos.size, LANES), np.float32)
        for i in range(3):
            a[:, AUG_POS + 2 * i] = pos // 64
            a[:, AUG_POS + 2 * i + 1] = pos % 64
        a[:, AUG_TQ:AUG_TQ + 3] = 1.0
        a[:, AUG_PAD] = pad
        return a

    pos = np.arange(t)
    augk = key_side(pos, 0.0)
    augk[pos, AUG_MASK + pos // SEL_BLOCK] = 1.0
    augw = np.concatenate([key_side(np.zeros(WINDOW, np.int64), 1.0), key_side(pos, 0.0)], axis=0)
    augc = key_side(np.arange(LANES) * CMP_STRIDE + CMP_BLOCK - 1, 0.0)

    nc = (t - CMP_BLOCK) // CMP_STRIDE + 1
    ns = t // SEL_BLOCK
    n = np.arange(LANES)[None, :]
    j = np.arange(ns)[:, None]
    ovt = ((n * CMP_STRIDE <= j * SEL_BLOCK + SEL_BLOCK - 1) & (n * CMP_STRIDE + CMP_BLOCK - 1 >= j * SEL_BLOCK) & (n < nc))

    def tri(tk, tq, anti=False):
        kk = np.arange(tk)[:, None]
        tt = np.arange(tq)[None, :]
        ok = (kk > tt) if anti else (kk <= tt)
        return np.where(ok, 0.0, NEG_BIG).astype(np.float32)

    triwin = np.stack([tri(NSA_TQ, NSA_TQ), tri(NSA_TQ, NSA_TQ, anti=True)])
    trifox = np.tile(tri(FOX_TK, FOX_TQ), (1, 2))
    return (jnp.asarray(augk, BF16), jnp.asarray(augw, BF16), jnp.asarray(augc, BF16), jnp.asarray(ovt, BF16),
            jnp.asarray(triwin), jnp.asarray(trifox))


def kernel(x, norm_mix, w_in, cmp_pos_k, cmp_w1_k, cmp_w2_k, cmp_pos_v, cmp_w1_v, cmp_w2_v, fox_f_bias, w_branch_nsa, w_branch_fox, w_merge_gate, b_merge_gate, w_out, norm_mlp, w_up, w_down, norm_final):
    b, t, d = x.shape
    assert w_in.shape[0] == 1, "one layer: the final norm is fused into the MLP kernel"
    assert t % FOX_TQ == 0 and t // SEL_BLOCK == 32 and (t - CMP_BLOCK) // CMP_STRIDE + 1 == LANES - 1
    augk, augw, augc, ovt, triwin, trifox = _position_constants(t)
    h = x.reshape(b * t, d)
    w = w_in[0]
    nq_w = NSA_HEADS * HEAD_DIM
    gate0 = nq_w + 6 * LANES
    fox0 = gate0 + 3 * NSA_HEADS
    wa = jnp.concatenate([w[:, :nq_w], w[:, fox0:fox0 + 3 * FOX_HEADS * HEAD_DIM], w[:, nq_w:gate0]], axis=1).astype(BF16)
    wg = jnp.concatenate([w[:, gate0:fox0], w[:, -FOX_HEADS:],
                          jnp.zeros((d, LANES - 3 * NSA_HEADS - FOX_HEADS), F32)], axis=1).astype(BF16)
    proj, gf, mg = _inproj(h, norm_mix[0][None, :], wa, wg, w_merge_gate[0].astype(BF16),
                           b_merge_gate[0][None, :], ROW_TILE)
    proj = proj.reshape(b, t, PROJ_W)
    gf = gf.reshape(b, t, LANES)

    def w1_blocks(w1):
        wa_ = w1[:CMP_STRIDE * HEAD_DIM].reshape(CMP_STRIDE, HEAD_DIM, CMP_HIDDEN)
        wb_ = w1[CMP_STRIDE * HEAD_DIM:].reshape(CMP_STRIDE, HEAD_DIM, CMP_HIDDEN)
        z = jnp.zeros_like(wa_)
        top = jnp.concatenate([wa_, z, wb_, z], axis=2)
        bot = jnp.concatenate([z, wa_, z, wb_], axis=2)
        return jnp.concatenate([top, bot], axis=1).astype(BF16)

    def w2_pair(w2):
        z = jnp.zeros_like(w2)
        return (jnp.concatenate([w2, z], axis=1).astype(BF16), jnp.concatenate([z, w2], axis=1).astype(BF16))

    def pos_rows(p):
        row = jnp.concatenate([p[:CMP_STRIDE], p[CMP_STRIDE:]], axis=1)[:, None, :]
        return jnp.concatenate([row, jnp.zeros((CMP_STRIDE, 7, 2 * HEAD_DIM), F32)], axis=1).astype(BF16)

    w2k0, w2k1 = w2_pair(cmp_w2_k[0])
    w2v0, w2v1 = w2_pair(cmp_w2_v[0])
    kcmp, vcmp = _compress(proj, w1_blocks(cmp_w1_k[0]), w2k0, w2k1, pos_rows(cmp_pos_k[0]),
                           w1_blocks(cmp_w1_v[0]), w2v0, w2v1, pos_rows(cmp_pos_v[0]))
    o_n = _nsa(proj, gf, kcmp, vcmp, augk, augw, augc, ovt, triwin)
    cq, faug = _foxprep(gf, fox_f_bias[0][:, None])
    o_f = _fox(proj, cq, faug, trifox)
    out = _post(h, o_n.reshape(b * t, -1), o_f.reshape(b * t, -1), mg,
                w_branch_nsa[0].astype(BF16), w_branch_fox[0].astype(BF16), w_out[0].astype(BF16),
                norm_mlp[0][None, :], w_up[0].astype(BF16), w_down[0].astype(BF16), norm_final[None, :], ROW_TILE)
    return out.reshape(b, t, d)
```

```python
import numpy as np
import jax
import jax.numpy as jnp
from jax import lax
from jax.experimental import pallas as pl
from jax.experimental.pallas import tpu as pltpu

HEAD_DIM = 64
NSA_HEADS = 8
NSA_GROUPS = 2
NSA_HPG = NSA_HEADS // NSA_GROUPS
FOX_HEADS = 8
CMP_BLOCK = 32
CMP_STRIDE = 16
CMP_HIDDEN = 128
SEL_BLOCK = 64
N_SELECT = 16
WINDOW = 512
NORM_EPS = 1e-6
FORCE_SCORE = 1e9

LANES = 128
NEG_BIG = -(2.0 ** 100)
M_INIT = -3.0e38
LOG2E = 1.4426950408889634
VMEM_LIMIT = 56 * 1024 * 1024
ROW_TILE = 512
ROW_SPLIT = 2

PROJ_W = 2816
COL_QN = 0
COL_QF = 512
COL_KF = 1024
COL_VF = 1536
COL_KV = 2048
GF_GATE = 0
GF_FOX = 24

F32 = jnp.float32
BF16 = jnp.bfloat16


def _dot(a, b):
    return jnp.dot(a, b, preferred_element_type=F32)


def _iota(shape, dim):
    return lax.broadcasted_iota(jnp.int32, shape, dim)


def _split3(x):
    hi = x.astype(BF16)
    r1 = x - hi.astype(F32)
    mid = r1.astype(BF16)
    lo = (r1 - mid.astype(F32)).astype(BF16)
    return hi, mid, lo


def _inproj_kernel(x_ref, g_ref, wa_ref, wg_ref, wm_ref, bm_ref, proj_ref, gf_ref, mg_ref):
    sub = x_ref.shape[0] // ROW_SPLIT
    for h in range(ROW_SPLIT):
        rows = slice(h * sub, (h + 1) * sub)
        x = x_ref[rows, :]
        r = lax.rsqrt(jnp.mean(x * x, axis=-1, keepdims=True) + NORM_EPS)
        u = (x * r * g_ref[...]).astype(BF16)
        for c0, c1 in ((0, 1024), (1024, 2048), (2048, PROJ_W)):
            proj_ref[rows, c0:c1] = _dot(u, wa_ref[:, c0:c1]).astype(BF16)
        gf_ref[rows, :] = _dot(u, wg_ref[...])
        for c0 in (0, 1024):
            z = _dot(u, wm_ref[:, c0:c0 + 1024]) + bm_ref[:, c0:c0 + 1024]
            mg_ref[rows, c0:c0 + 1024] = jax.nn.sigmoid(z).astype(BF16)


def _inproj(x2, g, wa, wg, wm, bm, tm):
    n, d = x2.shape
    const = lambda i: (0, 0)
    return pl.pallas_call(
        _inproj_kernel,
        grid=(n // tm,),
        in_specs=[
            pl.BlockSpec((tm, d), lambda i: (i, 0)),
            pl.BlockSpec((1, d), const),
            pl.BlockSpec(wa.shape, const),
            pl.BlockSpec(wg.shape, const),
            pl.BlockSpec(wm.shape, const),
            pl.BlockSpec(bm.shape, const),
        ],
        out_specs=[
            pl.BlockSpec((tm, PROJ_W), lambda i: (i, 0)),
            pl.BlockSpec((tm, LANES), lambda i: (i, 0)),
            pl.BlockSpec((tm, 2 * d), lambda i: (i, 0)),
        ],
        out_shape=[
            jax.ShapeDtypeStruct((n, PROJ_W), BF16),
            jax.ShapeDtypeStruct((n, LANES), F32),
            jax.ShapeDtypeStruct((n, 2 * d), BF16),
        ],
        compiler_params=pltpu.CompilerParams(
            dimension_semantics=("arbitrary",), vmem_limit_bytes=VMEM_LIMIT),
        name="inproj",
    )(x2, g, wa, wg, wm, bm)


def _compress_kernel(kc_ref, vc_ref, wk_ref, w2k0_ref, w2k1_ref, pk_ref,
                     wv_ref, w2v0_ref, w2v1_ref, pv_ref, kco_ref, vco_ref, xs_ref):
    t = kc_ref.shape[1]
    nblk = t // CMP_STRIDE

    def one(x_ref, w_ref, w2_refs, p_ref, o_ref):
        xs_ref[...] = x_ref[0].astype(F32)
        acc = jnp.zeros((nblk, 4 * LANES), F32)
        pos = jnp.zeros((8, 4 * LANES), F32)
        for j in range(CMP_STRIDE):
            xj = xs_ref[pl.ds(j, nblk, stride=CMP_STRIDE), :].astype(BF16)
            acc = acc + _dot(xj, w_ref[j])
            pos = pos + _dot(p_ref[j], w_ref[j])
        posb = pos[0:1, 0:LANES] + pos[0:1, 3 * LANES:4 * LANES]
        out = jnp.zeros((nblk, LANES), F32)
        for g in range(NSA_GROUPS):
            a = acc[:, g * LANES:(g + 1) * LANES]
            b = acc[:, (2 + g) * LANES:(3 + g) * LANES]
            h = a + pltpu.roll(b, nblk - 1, 0) + posb
            out = out + _dot(jax.nn.gelu(h).astype(BF16), w2_refs[g][...])
        o_ref[0] = out.astype(BF16)

    one(kc_ref, wk_ref, (w2k0_ref, w2k1_ref), pk_ref, kco_ref)
    one(vc_ref, wv_ref, (w2v0_ref, w2v1_ref), pv_ref, vco_ref)


def _compress(proj, wk, w2k0, w2k1, pk, wv, w2v0, w2v1, pv):
    b, t, _ = proj.shape
    kvb = COL_KV // LANES
    whole = lambda a: pl.BlockSpec(a.shape, lambda i: (0,) * a.ndim)
    ospec = pl.BlockSpec((1, LANES, LANES), lambda i: (i, 0, 0))
    return pl.pallas_call(
        _compress_kernel,
        grid=(b,),
        in_specs=[pl.BlockSpec((1, t, LANES), lambda i: (i, 0, kvb)),
                  pl.BlockSpec((1, t, LANES), lambda i: (i, 0, kvb + 1)),
                  whole(wk), whole(w2k0), whole(w2k1), whole(pk), whole(wv), whole(w2v0), whole(w2v1), whole(pv)],
        out_specs=[ospec, ospec],
        out_shape=[jax.ShapeDtypeStruct((b, LANES, LANES), BF16)] * 2,
        scratch_shapes=[pltpu.VMEM((t, LANES), F32)],
        compiler_params=pltpu.CompilerParams(dimension_semantics=("arbitrary",)),
        name="compress",
    )(proj, proj, wk, w2k0, w2k1, pk, wv, w2v0, w2v1, pv)


def _flash_init(m_ref, acc_ref):
    m_ref[...] = jnp.full(m_ref.shape, M_INIT, F32)
    acc_ref[...] = jnp.zeros(acc_ref.shape, F32)


def _softmax_pv(scores, vt_blks, m_refs=None, acc_refs=None):
    def pv(vt, p):
        if not isinstance(vt, tuple):
            return _dot(vt, p)
        w = p.shape[1] // len(vt)
        return jnp.concatenate([_dot(v, p[:, j * w:(j + 1) * w]) for j, v in enumerate(vt)], axis=1)

    ps, alphas = [], []
    for i, s in enumerate(scores):
        mx = jnp.max(s, axis=0, keepdims=True)
        if m_refs is None:
            ps.append(jnp.exp2(s - mx).astype(BF16))
            continue
        m_old = m_refs[i][...]
        m_new = jnp.maximum(m_old, mx)
        ps.append(jnp.exp2(s - m_new).astype(BF16))
        alphas.append(jnp.exp2(m_old - m_new))
        m_refs[i][...] = m_new
    if m_refs is None:
        return [pv(vt, p) for vt, p in zip(vt_blks, ps)]
    for vt, p, alpha, acc_ref in zip(vt_blks, ps, alphas, acc_refs):
        acc_ref[...] = acc_ref[...] * alpha + pv(vt, p)
    return None


def _pipelined_sweep(n_rest, qk_to, spv_from):
    def body(jj, carry):
        j = 2 * jj
        qk_to(1, j + 1)
        spv_from(0, j)
        qk_to(0, j + 2)
        spv_from(1, j + 1)
        return carry

    lax.fori_loop(0, n_rest // 2, body, 0)

    @pl.when(n_rest % 2 == 1)
    def _():
        qk_to(1, n_rest)
        spv_from(0, n_rest - 1)
        spv_from(1, n_rest)

    @pl.when(n_rest % 2 == 0)
    def _():
        spv_from(0, n_rest)


def _ones_rows(tk):
    return jnp.where(_iota((16, tk), 0) == 0, 1.0, 0.0).astype(BF16)


NSA_TQ = 128
NSA_N = NSA_HPG * NSA_TQ
SEL_TK = 512
WIN_KEYS = WINDOW + NSA_TQ

AUG_POS = 0
AUG_TQ = 6
AUG_PAD = 9
AUG_MASK = 32


def _bf16_terms(x, n=3):
    out, r = [], np.float32(x)
    for _ in range(n):
        term = np.asarray(r, np.float32).astype(BF16).astype(np.float32)
        out.append(float(term))
        r = np.float32(r - term)
    return out


def _nsa_queries(q_ref, blk):
    t0 = blk * NSA_TQ
    qt = q_ref[0].astype(F32).T * (HEAD_DIM ** -0.5 * LOG2E)
    lane = _iota((1, NSA_N), 1)
    hl = lane // NSA_TQ
    tqf = (t0 + lane % NSA_TQ).astype(F32)
    ridx = _iota((16, NSA_N), 0)
    zeros_h = jnp.zeros((HEAD_DIM, NSA_TQ), F32)
    l2e = _bf16_terms(LOG2E)
    q_top, aug16 = [], []
    for g in range(NSA_GROUPS):
        tops = []
        for h in range(NSA_HPG):
            head = NSA_HPG * g + h
            qh = qt[head * HEAD_DIM:(head + 1) * HEAD_DIM]
            tops.append(jnp.concatenate([qh, zeros_h] if g == 0 else [zeros_h, qh], axis=0))
        q_top.append(jnp.concatenate(tops, axis=1))
        slope = lax.bitcast_convert_type((127 - (NSA_HPG * g + hl + 1)) << 23, F32)
        rows = {}
        for i, term in enumerate(l2e):
            rows[AUG_POS + 2 * i] = slope * (64.0 * term)
            rows[AUG_POS + 2 * i + 1] = slope * term
        for i, term in enumerate(_split3(-(slope * tqf) * LOG2E)):
            rows[AUG_TQ + i] = term.astype(F32)
        a = jnp.zeros((16, NSA_N), F32)
        for r, val in rows.items():
            a = jnp.where(ridx == r, val, a)
        aug16.append(a)
    return q_top, aug16


def _nsa_compressed(s_cmp, blk, vct_ref, ovt):
    tq = blk * NSA_TQ + _iota((1, NSA_N), 1) % NSA_TQ
    n_idx = _iota((LANES, NSA_N), 0)
    valid = (n_idx * CMP_STRIDE + (CMP_BLOCK - 1) <= tq) & (n_idx < LANES - 1)
    o_cmp, imps = [], []
    for g in range(NSA_GROUPS):
        sm = jnp.where(valid, s_cmp[g], NEG_BIG)
        mx = jnp.max(sm, axis=0, keepdims=True)
        e = jnp.where(valid, jnp.exp2(sm - mx), 0.0)
        den = jnp.sum(e, axis=0, keepdims=True)
        pn = e / jnp.where(den > 0.0, den, 1.0)
        o_cmp.append(_dot(vct_ref[g * HEAD_DIM:(g + 1) * HEAD_DIM, :], pn.astype(BF16)))

        psum = pn[:, 0:NSA_TQ]
        for h in range(1, NSA_HPG):
            psum = psum + pn[:, h * NSA_TQ:(h + 1) * NSA_TQ]
        hi, mid, lo = _split3(psum)
        imps.append(_dot(ovt, hi) + _dot(ovt, mid) + _dot(ovt, lo))
    return o_cmp, imps


def _nsa_select(imps, blk, q_top, aug16):
    t0 = blk * NSA_TQ
    ns = imps[0].shape[0]
    tcol = t0 + _iota((8, NSA_TQ), 1)
    q_sel = []
    for g, imp in enumerate(imps):
        tiles = range(ns // 8)
        j_idx = [8 * r + _iota((8, NSA_TQ), 0) for r in tiles]
        in_past = [j_idx[r] * SEL_BLOCK <= tcol for r in tiles]
        cur = tcol // SEL_BLOCK
        score = []
        for r in tiles:
            forced = (j_idx[r] == 0) | (j_idx[r] == cur) | (j_idx[r] == cur - 1)
            score.append(jnp.where(forced, FORCE_SCORE, jnp.where(in_past[r], imp[8 * r:8 * r + 8], -1.0)))
        rank = [jnp.zeros((8, NSA_TQ), F32) for _ in tiles]
        for i in range(ns):
            si = score[i // 8][i % 8:i % 8 + 1, :]
            for r in tiles:
                if 8 * r > i:
                    beats = si >= score[r]
                elif 8 * r + 7 <= i:
                    beats = si > score[r]
                else:
                    beats = (si > score[r]) | ((si == score[r]) & (j_idx[r] > i))
                rank[r] = rank[r] + jnp.where(beats, 1.0, 0.0)
        maskval = jnp.concatenate([jnp.where((rank[r] < min(N_SELECT, ns)) & in_past[r], 0.0, NEG_BIG) for r in tiles], axis=0)
        mask4 = jnp.concatenate([maskval] * NSA_HPG, axis=1)
        q_sel.append(jnp.concatenate([q_top[g], aug16[g], jnp.zeros((AUG_MASK - 16, NSA_N), F32), mask4,
                                      jnp.zeros((LANES - AUG_MASK - ns, NSA_N), F32)], axis=0).astype(BF16))
    return q_sel


def _nsa_kernel(q_ref, gf_ref, kc_ref, vc_ref, ks_ref, vs_ref, kw_ref, vw_ref,
                augk_ref, augw_ref, augc_ref, ovt_ref, trisel_ref, triwin_ref, o_ref,
                vst_ref, vwt_ref, kwp_ref, vct_ref, gt_ref, m_ref, acc_ref, s0_ref, s1_ref):
    qb = pl.program_id(1)
    t = ks_ref.shape[1]
    t0 = qb * NSA_TQ
    groups = range(NSA_GROUPS)
    zeros_tail = jnp.zeros((LANES - 16, NSA_N), F32)

    def cmp_scores(q_top, aug16):
        kcb = jnp.concatenate([kc_ref[0], augc_ref[...]], axis=1)
        return [_dot(kcb, jnp.concatenate([q_top[g], aug16[g], zeros_tail], axis=0).astype(BF16)) for g in groups]

    @pl.when(qb == 0)
    def _():
        vwt_ref[:, 0:WINDOW] = jnp.zeros((LANES, WINDOW), BF16)
        kwp_ref[0:WINDOW, :] = jnp.zeros((WINDOW, LANES), BF16)
        kwp_ref[WINDOW:WINDOW + t, :] = kw_ref[0]
        for c in range(t // LANES):
            sl = slice(c * LANES, (c + 1) * LANES)
            vst_ref[:, sl] = vs_ref[0, sl, :].astype(F32).T.astype(BF16)
            vwt_ref[:, WINDOW + c * LANES:WINDOW + (c + 1) * LANES] = vw_ref[0, sl, :].astype(F32).T.astype(BF16)
        vct_ref[...] = vc_ref[0].astype(F32).T.astype(BF16)

    q_top, aug16 = _nsa_queries(q_ref, qb)
    s_cmp = cmp_scores(q_top, aug16)
    kw0 = pl.multiple_of(t0, NSA_TQ)
    kwin = jnp.concatenate([kwp_ref[pl.ds(kw0, WIN_KEYS), :], augw_ref[pl.ds(kw0, WIN_KEYS), :]], axis=1)
    pad_row = _iota((16, NSA_N), 0) == AUG_PAD
    s_win = [_dot(kwin, jnp.concatenate([q_top[g], jnp.where(pad_row, NEG_BIG, aug16[g]), zeros_tail], axis=0).astype(BF16))
             for g in groups]
    o_cmp, imps = _nsa_compressed(s_cmp, qb, vct_ref, ovt_ref[...])
    q_sel = _nsa_select(imps, qb, q_top, aug16)

    hi_edge = jnp.concatenate([triwin_ref[0]] * NSA_HPG, axis=1)
    lo_edge = jnp.concatenate([triwin_ref[1]] * NSA_HPG, axis=1)
    ones_w = _ones_rows(WIN_KEYS)
    s_win = [jnp.concatenate([s[0:NSA_TQ] + lo_edge, s[NSA_TQ:WINDOW], s[WINDOW:WIN_KEYS] + hi_edge], axis=0)
             for s in s_win]
    vt_win = [jnp.concatenate([vwt_ref[g * HEAD_DIM:(g + 1) * HEAD_DIM, pl.ds(kw0, WIN_KEYS)], ones_w], axis=0)
              for g in groups]
    o_win = [acc[0:HEAD_DIM] / acc[HEAD_DIM:HEAD_DIM + 1] for acc in _softmax_pv(s_win, vt_win)]

    ones_s = _ones_rows(SEL_TK)
    per_chunk = SEL_TK // NSA_TQ
    n_full = qb // per_chunk
    s_bufs = (s0_ref, s1_ref)

    def sel_scores(cb):
        k0 = pl.multiple_of(cb * SEL_TK, SEL_TK)
        kblk = jnp.concatenate([ks_ref[0, pl.ds(k0, SEL_TK), :], augk_ref[pl.ds(k0, SEL_TK), :]], axis=1)
        return [_dot(kblk, q_sel[g]) for g in groups]

    def qk_to(buf, j):
        for g, s in enumerate(sel_scores(j - 1)):
            s_bufs[buf][g] = s

    def spv_from(buf, j):
        cb = jnp.where(j == 0, n_full, j - 1)
        k0 = pl.multiple_of(cb * SEL_TK, SEL_TK)
        vts = [jnp.concatenate([vst_ref[g * HEAD_DIM:(g + 1) * HEAD_DIM, pl.ds(k0, SEL_TK)], ones_s], axis=0)
               for g in groups]
        _softmax_pv([s_bufs[buf][g] for g in groups], vts,
                    [m_ref.at[g] for g in groups], [acc_ref.at[g] for g in groups])

    diag = trisel_ref[qb % per_chunk]
    diag4 = jnp.concatenate([diag] * NSA_HPG, axis=1)
    for g, s in enumerate(sel_scores(n_full)):
        s0_ref[g] = s + diag4
        _flash_init(m_ref.at[g], acc_ref.at[g])
    _pipelined_sweep(n_full, qk_to, spv_from)

    gt_ref[...] = gf_ref[0].T
    sig = jax.nn.sigmoid(gt_ref[GF_GATE:GF_GATE + 3 * NSA_HEADS, :])
    outs = []
    for g in groups:
        acc = acc_ref[g]
        o_sel = acc[0:HEAD_DIM] / acc[HEAD_DIM:HEAD_DIM + 1]
        for h in range(NSA_HPG):
            sl = slice(h * NSA_TQ, (h + 1) * NSA_TQ)
            head = NSA_HPG * g + h
            g0 = sig[head:head + 1]
            g1 = sig[NSA_HEADS + head:NSA_HEADS + head + 1]
            g2 = sig[2 * NSA_HEADS + head:2 * NSA_HEADS + head + 1]
            outs.append(g0 * o_cmp[g][:, sl] + g1 * o_sel[:, sl] + g2 * o_win[g][:, sl])
    o_ref[0] = jnp.concatenate(outs, axis=0).T.astype(BF16)


def _nsa(proj, gf, kcmp, vcmp, augk, augw, augc, ovt, trisel, triwin):
    b, t, _ = proj.shape
    nq = t // NSA_TQ
    kvb = COL_KV // LANES
    width = NSA_HEADS * HEAD_DIM
    whole = lambda a: pl.BlockSpec(a.shape, lambda i, k: (0,) * a.ndim)
    kv = lambda idx: pl.BlockSpec((1, t, LANES), lambda i, k: (i, 0, idx))
    return pl.pallas_call(
        _nsa_kernel,
        grid=(b, nq),
        in_specs=[
            pl.BlockSpec((1, NSA_TQ, width), lambda i, k: (i, k, COL_QN // width)),
            pl.BlockSpec((1, NSA_TQ, LANES), lambda i, k: (i, k, 0)),
            pl.BlockSpec((1, LANES, LANES), lambda i, k: (i, 0, 0)),
            pl.BlockSpec((1, LANES, LANES), lambda i, k: (i, 0, 0)),
            kv(kvb + 2), kv(kvb + 3), kv(kvb + 4), kv(kvb + 5),
            whole(augk), whole(augw), whole(augc), whole(ovt), whole(trisel), whole(triwin),
        ],
        out_specs=pl.BlockSpec((1, NSA_TQ, width), lambda i, k: (i, k, 0)),
        out_shape=jax.ShapeDtypeStruct((b, t, width), BF16),
        scratch_shapes=[
            pltpu.VMEM((LANES, t), BF16),
            pltpu.VMEM((LANES, t + WINDOW), BF16),
            pltpu.VMEM((t + WINDOW, LANES), BF16),
            pltpu.VMEM((LANES, LANES), BF16),
            pltpu.VMEM((LANES, NSA_TQ), F32),
            pltpu.VMEM((NSA_GROUPS, 1, NSA_N), F32),
            pltpu.VMEM((NSA_GROUPS, HEAD_DIM + 16, NSA_N), F32),
            pltpu.VMEM((NSA_GROUPS, SEL_TK, NSA_N), F32),
            pltpu.VMEM((NSA_GROUPS, SEL_TK, NSA_N), F32),
        ],
        compiler_params=pltpu.CompilerParams(
            dimension_semantics=("arbitrary", "arbitrary"), vmem_limit_bytes=VMEM_LIMIT),
        name="nsa_attention",
    )(proj, gf, kcmp, vcmp, proj, proj, proj, proj, augk, augw, augc, ovt, trisel, triwin)


FOX_CBLK = 256


def _foxprep_kernel(gf_ref, bias_ref, cq_ref, augk_ref):
    t = gf_ref.shape[1]
    cols = []
    for c in range(t // LANES):
        cols.append(gf_ref[0, c * LANES:(c + 1) * LANES, :].T[GF_FOX:GF_FOX + FOX_HEADS])
    f = jnp.concatenate(cols, axis=1) + bias_ref[...]
    ls = jnp.minimum(f, 0.0) - jnp.log(1.0 + jnp.exp(-jnp.abs(f)))
    upper = (_iota((FOX_CBLK, FOX_CBLK), 0) <= _iota((FOX_CBLK, FOX_CBLK), 1)).astype(BF16)
    carry = jnp.zeros((FOX_HEADS, 1), F32)
    parts = []
    for c in range(t // FOX_CBLK):
        hi, mid, lo = _split3(ls[:, c * FOX_CBLK:(c + 1) * FOX_CBLK])
        cb = _dot(hi, upper) + _dot(mid, upper) + _dot(lo, upper) + carry
        carry = cb[:, FOX_CBLK - 1:FOX_CBLK]
        parts.append(cb)
    csum = jnp.concatenate(parts, axis=1) * LOG2E
    c1, c2, c3 = (v.astype(F32) for v in _split3(csum))
    ones = jnp.ones((3 * FOX_HEADS, t), F32)
    zeros = jnp.zeros((LANES - 6 * FOX_HEADS, t), F32)
    cq_ref[0] = jnp.concatenate([c1, c2, c3, ones, zeros], axis=0).astype(BF16)
    slab_k = jnp.concatenate([ones, -c1, -c2, -c3, zeros], axis=0)
    for c in range(t // LANES):
        sl = slice(c * LANES, (c + 1) * LANES)
        augk_ref[0, sl, :] = slab_k[:, sl].T.astype(BF16)


def _foxprep(gf, bias):
    b, t, _ = gf.shape
    return pl.pallas_call(
        _foxprep_kernel,
        grid=(b,),
        in_specs=[pl.BlockSpec((1, t, LANES), lambda i: (i, 0, 0)),
                  pl.BlockSpec(bias.shape, lambda i: (0, 0))],
        out_specs=[pl.BlockSpec((1, LANES, t), lambda i: (i, 0, 0)),
                   pl.BlockSpec((1, t, LANES), lambda i: (i, 0, 0))],
        out_shape=[jax.ShapeDtypeStruct((b, LANES, t), BF16),
                   jax.ShapeDtypeStruct((b, t, LANES), BF16)],
        compiler_params=pltpu.CompilerParams(dimension_semantics=("arbitrary",)),
        name="fox_prep",
    )(gf, bias)


FOX_TQ = 256
FOX_TK = 256
FOX_NP = FOX_HEADS // 2
FOX_N = 2 * FOX_TQ
FOX_ROWS = HEAD_DIM + 16


def _fox_kernel(q_ref, k_ref, v_ref, cq_ref, augk_ref, tri_ref, o_ref, vt_ref, m_ref, acc_ref, s0_ref, s1_ref):
    qb = pl.program_id(1)
    t = k_ref.shape[1]

    @pl.when(qb == 0)
    def _():
        for p in range(FOX_NP):
            for c in range(t // LANES):
                sl = slice(c * LANES, (c + 1) * LANES)
                vt = v_ref[0, sl, p * LANES:(p + 1) * LANES].astype(F32).T.astype(BF16)
                for hh in range(2):
                    vt_ref[p, hh, 0:HEAD_DIM, sl] = vt[hh * HEAD_DIM:(hh + 1) * HEAD_DIM]
            for hh in range(2):
                vt_ref[p, hh, HEAD_DIM:FOX_ROWS, :] = _ones_rows(t)

    row = _iota((LANES, FOX_TQ), 0)
    hrow = row % FOX_HEADS
    cq = cq_ref[0].astype(F32)
    qs = []
    for p in range(FOX_NP):
        qt = q_ref[0, :, p * LANES:(p + 1) * LANES].astype(F32).T * (HEAD_DIM ** -0.5 * LOG2E)
        q_top = jnp.concatenate([jnp.where(row < HEAD_DIM, qt, 0.0), jnp.where(row >= HEAD_DIM, qt, 0.0)], axis=1)
        q_aug = jnp.concatenate([jnp.where(hrow == 2 * p, cq, 0.0), jnp.where(hrow == 2 * p + 1, cq, 0.0)], axis=1)
        qs.append(jnp.concatenate([q_top, q_aug], axis=0).astype(BF16))

    pairs = range(FOX_NP)
    s_bufs = (s0_ref, s1_ref)

    def scores(kb):
        k0 = pl.multiple_of(kb * FOX_TK, FOX_TK)
        ak = augk_ref[0, pl.ds(k0, FOX_TK), :]
        return [_dot(jnp.concatenate([k_ref[0, pl.ds(k0, FOX_TK), p * LANES:(p + 1) * LANES], ak], axis=1), qs[p])
                for p in pairs]

    def qk_to(buf, j):
        for p, s in enumerate(scores(j - 1)):
            s_bufs[buf][p] = s

    def spv_from(buf, j):
        kb = jnp.where(j == 0, qb, j - 1)
        k0 = pl.multiple_of(kb * FOX_TK, FOX_TK)
        vts = [(vt_ref[p, 0, :, pl.ds(k0, FOX_TK)], vt_ref[p, 1, :, pl.ds(k0, FOX_TK)]) for p in pairs]
        _softmax_pv([s_bufs[buf][p] for p in pairs], vts,
                    [m_ref.at[p] for p in pairs], [acc_ref.at[p] for p in pairs])

    tri = tri_ref[...]
    for p, s in enumerate(scores(qb)):
        s0_ref[p] = s + tri
        _flash_init(m_ref.at[p], acc_ref.at[p])
    _pipelined_sweep(qb, qk_to, spv_from)
    for p in range(FOX_NP):
        acc = acc_ref[p]
        den = acc[HEAD_DIM:HEAD_DIM + 1]
        o_a = acc[0:HEAD_DIM, 0:FOX_TQ] / den[:, 0:FOX_TQ]
        o_b = acc[0:HEAD_DIM, FOX_TQ:FOX_N] / den[:, FOX_TQ:FOX_N]
        o_ref[0, :, p * LANES:(p + 1) * LANES] = jnp.concatenate([o_a, o_b], axis=0).T.astype(BF16)


def _fox(proj, cq, augk, tri):
    b, t, _ = proj.shape
    nq = t // FOX_TQ
    width = FOX_HEADS * HEAD_DIM
    qf, kf, vf = COL_QF // width, COL_KF // width, COL_VF // width
    return pl.pallas_call(
        _fox_kernel,
        grid=(b, nq),
        in_specs=[
            pl.BlockSpec((1, FOX_TQ, width), lambda i, k: (i, k, qf)),
            pl.BlockSpec((1, t, width), lambda i, k: (i, 0, kf)),
            pl.BlockSpec((1, t, width), lambda i, k: (i, 0, vf)),
            pl.BlockSpec((1, LANES, FOX_TQ), lambda i, k: (i, 0, k)),
            pl.BlockSpec((1, t, LANES), lambda i, k: (i, 0, 0)),
            pl.BlockSpec(tri.shape, lambda i, k: (0, 0)),
        ],
        out_specs=pl.BlockSpec((1, FOX_TQ, width), lambda i, k: (i, k, 0)),
        out_shape=jax.ShapeDtypeStruct((b, t, width), BF16),
        scratch_shapes=[
            pltpu.VMEM((FOX_NP, 2, FOX_ROWS, t), BF16),
            pltpu.VMEM((FOX_NP, 1, FOX_N), F32),
            pltpu.VMEM((FOX_NP, FOX_ROWS, FOX_N), F32),
            pltpu.VMEM((FOX_NP, FOX_TK, FOX_N), F32),
            pltpu.VMEM((FOX_NP, FOX_TK, FOX_N), F32),
        ],
        compiler_params=pltpu.CompilerParams(
            dimension_semantics=("arbitrary", "arbitrary"), vmem_limit_bytes=VMEM_LIMIT),
        name="fox_attention",
    )(proj, proj, proj, cq, augk, tri)


def _post_kernel(x_ref, on_ref, of_ref, mg_ref, wn_ref, wf_ref, wo_ref, gm_ref, wu_ref, wd_ref, gfin_ref, o_ref):
    d = x_ref.shape[1]
    ff = wu_ref.shape[1]
    ga = mg_ref[:, 0:d].astype(F32)
    gb = mg_ref[:, d:2 * d].astype(F32)
    y = ga * _dot(on_ref[...], wn_ref[...]) + gb * _dot(of_ref[...], wf_ref[...])
    h = x_ref[...] + _dot(y.astype(BF16), wo_ref[...])
    r = lax.rsqrt(jnp.mean(h * h, axis=-1, keepdims=True) + NORM_EPS)
    v = (h * r * gm_ref[...]).astype(BF16)
    acc = h
    for c0 in range(0, ff, 1024):
        up = jnp.maximum(_dot(v, wu_ref[:, c0:c0 + 1024]), 0.0)
        acc = acc + _dot((up * up).astype(BF16), wd_ref[c0:c0 + 1024, :])
    r2 = lax.rsqrt(jnp.mean(acc * acc, axis=-1, keepdims=True) + NORM_EPS)
    o_ref[...] = acc * r2 * gfin_ref[...]


def _post(x2, on, of, mg, wn, wf, wo, gm, wu, wd, gfin, tm):
    n, d = x2.shape
    row = lambda w: pl.BlockSpec((tm, w), lambda i: (i, 0))
    const = lambda a: pl.BlockSpec(a.shape, lambda i: (0, 0), pipeline_mode=pl.Buffered(1))
    return pl.pallas_call(
        _post_kernel,
        grid=(n // tm,),
        in_specs=[row(d), row(on.shape[1]), row(of.shape[1]), row(2 * d),
                  const(wn), const(wf), const(wo), const(gm), const(wu), const(wd), const(gfin)],
        out_specs=row(d),
        out_shape=jax.ShapeDtypeStruct((n, d), F32),
        compiler_params=pltpu.CompilerParams(
            dimension_semantics=("arbitrary",), vmem_limit_bytes=VMEM_LIMIT),
        name="post_mlp",
    )(x2, on, of, mg, wn, wf, wo, gm, wu, wd, gfin)


def _position_constants(t):
    def key_side(pos, pad):
        a = np.zeros((pos.size, LANES), np.float32)
        for i in range(3):
            a[:, AUG_POS + 2 * i] = pos // 64
            a[:, AUG_POS + 2 * i + 1] = pos % 64
        a[:, AUG_TQ:AUG_TQ + 3] = 1.0
        a[:, AUG_PAD] = pad
        return a

    pos = np.arange(t)
    augk = key_side(pos, 0.0)
    augk[pos, AUG_MASK + pos // SEL_BLOCK] = 1.0
    augw = np.concatenate([key_side(np.zeros(WINDOW, np.int64), 1.0), key_side(pos, 0.0)], axis=0)
    augc = key_side(np.arange(LANES) * CMP_STRIDE + CMP_BLOCK - 1, 0.0)

    nc = (t - CMP_BLOCK) // CMP_STRIDE + 1
    ns = t // SEL_BLOCK
    n = np.arange(LANES)[None, :]
    j = np.arange(ns)[:, None]
    ovt = ((n * CMP_STRIDE <= j * SEL_BLOCK + SEL_BLOCK - 1) & (n * CMP_STRIDE + CMP_BLOCK - 1 >= j * SEL_BLOCK) & (n < nc))

    def tri(tk, tq, anti=False):
        kk = np.arange(tk)[:, None]
        tt = np.arange(tq)[None, :]
        ok = (kk > tt) if anti else (kk <= tt)
        return np.where(ok, 0.0, NEG_BIG).astype(np.float32)

    edge = tri(NSA_TQ, NSA_TQ)
    trisel = np.zeros((SEL_TK // NSA_TQ, SEL_TK, NSA_TQ), np.float32)
    for v in range(SEL_TK // NSA_TQ):
        trisel[v, v * NSA_TQ:(v + 1) * NSA_TQ] = edge
    triwin = np.stack([edge, tri(NSA_TQ, NSA_TQ, anti=True)])
    trifox = np.tile(tri(FOX_TK, FOX_TQ), (1, 2))
    return (jnp.asarray(augk, BF16), jnp.asarray(augw, BF16), jnp.asarray(augc, BF16), jnp.asarray(ovt, BF16),
            jnp.asarray(trisel), jnp.asarray(triwin), jnp.asarray(trifox))


def kernel(x, norm_mix, w_in, cmp_pos_k, cmp_w1_k, cmp_w2_k, cmp_pos_v, cmp_w1_v, cmp_w2_v, fox_f_bias, w_branch_nsa, w_branch_fox, w_merge_gate, b_merge_gate, w_out, norm_mlp, w_up, w_down, norm_final):
    b, t, d = x.shape
    assert w_in.shape[0] == 1, "one layer: the final norm is fused into the MLP kernel"
    assert t % FOX_TQ == 0 and t // SEL_BLOCK == 32 and (t - CMP_BLOCK) // CMP_STRIDE + 1 == LANES - 1
    augk, augw, augc, ovt, trisel, triwin, trifox = _position_constants(t)
    h = x.reshape(b * t, d)
    w = w_in[0]
    nq_w = NSA_HEADS * HEAD_DIM
    gate0 = nq_w + 6 * LANES
    fox0 = gate0 + 3 * NSA_HEADS
    wa = jnp.concatenate([w[:, :nq_w], w[:, fox0:fox0 + 3 * FOX_HEADS * HEAD_DIM], w[:, nq_w:gate0]], axis=1).astype(BF16)
    wg = jnp.concatenate([w[:, gate0:fox0], w[:, -FOX_HEADS:],
                          jnp.zeros((d, LANES - 3 * NSA_HEADS - FOX_HEADS), F32)], axis=1).astype(BF16)
    proj, gf, mg = _inproj(h, norm_mix[0][None, :], wa, wg, w_merge_gate[0].astype(BF16),
                           b_merge_gate[0][None, :], ROW_TILE)
    proj = proj.reshape(b, t, PROJ_W)
    gf = gf.reshape(b, t, LANES)

    def w1_blocks(w1):
        wa_ = w1[:CMP_STRIDE * HEAD_DIM].reshape(CMP_STRIDE, HEAD_DIM, CMP_HIDDEN)
        wb_ = w1[CMP_STRIDE * HEAD_DIM:].reshape(CMP_STRIDE, HEAD_DIM, CMP_HIDDEN)
        z = jnp.zeros_like(wa_)
        top = jnp.concatenate([wa_, z, wb_, z], axis=2)
        bot = jnp.concatenate([z, wa_, z, wb_], axis=2)
        return jnp.concatenate([top, bot], axis=1).astype(BF16)

    def w2_pair(w2):
        z = jnp.zeros_like(w2)
        return (jnp.concatenate([w2, z], axis=1).astype(BF16), jnp.concatenate([z, w2], axis=1).astype(BF16))

    def pos_rows(p):
        row = jnp.concatenate([p[:CMP_STRIDE], p[CMP_STRIDE:]], axis=1)[:, None, :]
        return jnp.concatenate([row, jnp.zeros((CMP_STRIDE, 7, 2 * HEAD_DIM), F32)], axis=1).astype(BF16)

    w2k0, w2k1 = w2_pair(cmp_w2_k[0])
    w2v0, w2v1 = w2_pair(cmp_w2_v[0])
    kcmp, vcmp = _compress(proj, w1_blocks(cmp_w1_k[0]), w2k0, w2k1, pos_rows(cmp_pos_k[0]),
                           w1_blocks(cmp_w1_v[0]), w2v0, w2v1, pos_rows(cmp_pos_v[0]))
    o_n = _nsa(proj, gf, kcmp, vcmp, augk, augw, augc, ovt, trisel, triwin)
    cq, faug = _foxprep(gf, fox_f_bias[0][:, None])
    o_f = _fox(proj, cq, faug, trifox)
    out = _post(h, o_n.reshape(b * t, -1), o_f.reshape(b * t, -1), mg,
                w_branch_nsa[0].astype(BF16), w_branch_fox[0].astype(BF16), w_out[0].astype(BF16),
                norm_mlp[0][None, :], w_up[0].astype(BF16), w_down[0].astype(BF16), norm_final[None, :], ROW_TILE)
    return out.reshape(b, t, d)
```

```python
import numpy as np
import jax
import jax.numpy as jnp
from jax import lax
from jax.experimental import pallas as pl
from jax.experimental.pallas import tpu as pltpu

HEAD_DIM = 64
NSA_HEADS = 8
NSA_GROUPS = 2
NSA_HPG = NSA_HEADS // NSA_GROUPS
FOX_HEADS = 8
CMP_BLOCK = 32
CMP_STRIDE = 16
CMP_HIDDEN = 128
SEL_BLOCK = 64
N_SELECT = 16
WINDOW = 512
NORM_EPS = 1e-6
FORCE_SCORE = 1e9

LANES = 128
NEG_BIG = -(2.0 ** 100)
M_INIT = -3.0e38
LOG2E = 1.4426950408889634
VMEM_LIMIT = 56 * 1024 * 1024
ROW_TILE = 512
ROW_SPLIT = 2

PROJ_W = 2816
COL_QN = 0
COL_QF = 512
COL_KF = 1024
COL_VF = 1536
COL_KV = 2048
GF_GATE = 0
GF_FOX = 24

F32 = jnp.float32
BF16 = jnp.bfloat16


def _dot(a, b):
    return jnp.dot(a, b, preferred_element_type=F32)


def _iota(shape, dim):
    return lax.broadcasted_iota(jnp.int32, shape, dim)


def _split3(x):
    hi = x.astype(BF16)
    r1 = x - hi.astype(F32)
    mid = r1.astype(BF16)
    lo = (r1 - mid.astype(F32)).astype(BF16)
    return hi, mid, lo


def _inproj_kernel(x_ref, g_ref, wa_ref, wg_ref, wm_ref, bm_ref, proj_ref, gf_ref, mg_ref):
    sub = x_ref.shape[0] // ROW_SPLIT
    for h in range(ROW_SPLIT):
        rows = slice(h * sub, (h + 1) * sub)
        x = x_ref[rows, :]
        r = lax.rsqrt(jnp.mean(x * x, axis=-1, keepdims=True) + NORM_EPS)
        u = (x * r * g_ref[...]).astype(BF16)
        for c0, c1 in ((0, 1024), (1024, 2048), (2048, PROJ_W)):
            proj_ref[rows, c0:c1] = _dot(u, wa_ref[:, c0:c1]).astype(BF16)
        gf_ref[rows, :] = _dot(u, wg_ref[...])
        for c0 in (0, 1024):
            z = _dot(u, wm_ref[:, c0:c0 + 1024]) + bm_ref[:, c0:c0 + 1024]
            mg_ref[rows, c0:c0 + 1024] = jax.nn.sigmoid(z).astype(BF16)


def _inproj(x2, g, wa, wg, wm, bm, tm):
    n, d = x2.shape
    const = lambda i: (0, 0)
    return pl.pallas_call(
        _inproj_kernel,
        grid=(n // tm,),
        in_specs=[
            pl.BlockSpec((tm, d), lambda i: (i, 0)),
            pl.BlockSpec((1, d), const),
            pl.BlockSpec(wa.shape, const),
            pl.BlockSpec(wg.shape, const),
            pl.BlockSpec(wm.shape, const),
            pl.BlockSpec(bm.shape, const),
        ],
        out_specs=[
            pl.BlockSpec((tm, PROJ_W), lambda i: (i, 0)),
            pl.BlockSpec((tm, LANES), lambda i: (i, 0)),
            pl.BlockSpec((tm, 2 * d), lambda i: (i, 0)),
        ],
        out_shape=[
            jax.ShapeDtypeStruct((n, PROJ_W), BF16),
            jax.ShapeDtypeStruct((n, LANES), F32),
            jax.ShapeDtypeStruct((n, 2 * d), BF16),
        ],
        compiler_params=pltpu.CompilerParams(
            dimension_semantics=("arbitrary",), vmem_limit_bytes=VMEM_LIMIT),
        name="inproj",
    )(x2, g, wa, wg, wm, bm)


def _compress_kernel(kc_ref, vc_ref, wk_ref, w2k0_ref, w2k1_ref, pk_ref,
                     wv_ref, w2v0_ref, w2v1_ref, pv_ref, kco_ref, vco_ref, xs_ref):
    t = kc_ref.shape[1]
    nblk = t // CMP_STRIDE

    def one(x_ref, w_ref, w2_refs, p_ref, o_ref):
        xs_ref[...] = x_ref[0].astype(F32)
        acc = jnp.zeros((nblk, 4 * LANES), F32)
        pos = jnp.zeros((8, 4 * LANES), F32)
        for j in range(CMP_STRIDE):
            xj = xs_ref[pl.ds(j, nblk, stride=CMP_STRIDE), :].astype(BF16)
            acc = acc + _dot(xj, w_ref[j])
            pos = pos + _dot(p_ref[j], w_ref[j])
        posb = pos[0:1, 0:LANES] + pos[0:1, 3 * LANES:4 * LANES]
        out = jnp.zeros((nblk, LANES), F32)
        for g in range(NSA_GROUPS):
            a = acc[:, g * LANES:(g + 1) * LANES]
            b = acc[:, (2 + g) * LANES:(3 + g) * LANES]
            h = a + pltpu.roll(b, nblk - 1, 0) + posb
            out = out + _dot(jax.nn.gelu(h).astype(BF16), w2_refs[g][...])
        o_ref[0] = out.astype(BF16)

    one(kc_ref, wk_ref, (w2k0_ref, w2k1_ref), pk_ref, kco_ref)
    one(vc_ref, wv_ref, (w2v0_ref, w2v1_ref), pv_ref, vco_ref)


def _compress(proj, wk, w2k0, w2k1, pk, wv, w2v0, w2v1, pv):
    b, t, _ = proj.shape
    kvb = COL_KV // LANES
    whole = lambda a: pl.BlockSpec(a.shape, lambda i: (0,) * a.ndim)
    ospec = pl.BlockSpec((1, LANES, LANES), lambda i: (i, 0, 0))
    return pl.pallas_call(
        _compress_kernel,
        grid=(b,),
        in_specs=[pl.BlockSpec((1, t, LANES), lambda i: (i, 0, kvb)),
                  pl.BlockSpec((1, t, LANES), lambda i: (i, 0, kvb + 1)),
                  whole(wk), whole(w2k0), whole(w2k1), whole(pk), whole(wv), whole(w2v0), whole(w2v1), whole(pv)],
        out_specs=[ospec, ospec],
        out_shape=[jax.ShapeDtypeStruct((b, LANES, LANES), BF16)] * 2,
        scratch_shapes=[pltpu.VMEM((t, LANES), F32)],
        compiler_params=pltpu.CompilerParams(dimension_semantics=("arbitrary",)),
        name="compress",
    )(proj, proj, wk, w2k0, w2k1, pk, wv, w2v0, w2v1, pv)


def _flash_init(m_ref, acc_ref):
    m_ref[...] = jnp.full(m_ref.shape, M_INIT, F32)
    acc_ref[...] = jnp.zeros(acc_ref.shape, F32)


def _softmax_pv(scores, vt_blks, m_refs=None, acc_refs=None, maxes=None):
    def pv(vt, p):
        if not isinstance(vt, tuple):
            return _dot(vt, p)
        w = p.shape[1] // len(vt)
        return jnp.concatenate([_dot(v, p[:, j * w:(j + 1) * w]) for j, v in enumerate(vt)], axis=1)

    ps, alphas = [], []
    for i, s in enumerate(scores):
        mx = jnp.max(s, axis=0, keepdims=True) if maxes is None else maxes[i]
        if m_refs is None:
            ps.append(jnp.exp2(s - mx).astype(BF16))
            continue
        m_old = m_refs[i][...]
        m_new = jnp.maximum(m_old, mx)
        ps.append(jnp.exp2(s - m_new).astype(BF16))
        alphas.append(jnp.exp2(m_old - m_new))
        m_refs[i][...] = m_new
    if m_refs is None:
        return [pv(vt, p) for vt, p in zip(vt_blks, ps)]
    for vt, p, alpha, acc_ref in zip(vt_blks, ps, alphas, acc_refs):
        acc_ref[...] = acc_ref[...] * alpha + pv(vt, p)
    return None


def _pipelined_sweep(n_rest, qk_to, spv_from):
    def body(jj, carry):
        j = 2 * jj
        qk_to(1, j + 1)
        spv_from(0, j)
        qk_to(0, j + 2)
        spv_from(1, j + 1)
        return carry

    lax.fori_loop(0, n_rest // 2, body, 0)

    @pl.when(n_rest % 2 == 1)
    def _():
        qk_to(1, n_rest)
        spv_from(0, n_rest - 1)
        spv_from(1, n_rest)

    @pl.when(n_rest % 2 == 0)
    def _():
        spv_from(0, n_rest)


def _ones_rows(tk):
    return jnp.where(_iota((16, tk), 0) == 0, 1.0, 0.0).astype(BF16)


NSA_TQ = 128
NSA_N = NSA_HPG * NSA_TQ
SEL_TK = 512
WIN_KEYS = WINDOW + NSA_TQ

AUG_POS = 0
AUG_TQ = 6
AUG_PAD = 9
AUG_MASK = 32


def _bf16_terms(x, n=3):
    out, r = [], np.float32(x)
    for _ in range(n):
        term = np.asarray(r, np.float32).astype(BF16).astype(np.float32)
        out.append(float(term))
        r = np.float32(r - term)
    return out


def _nsa_queries(q_ref, blk):
    t0 = blk * NSA_TQ
    qt = q_ref[0].astype(F32).T * (HEAD_DIM ** -0.5 * LOG2E)
    lane = _iota((1, NSA_N), 1)
    hl = lane // NSA_TQ
    tqf = (t0 + lane % NSA_TQ).astype(F32)
    ridx = _iota((16, NSA_N), 0)
    zeros_h = jnp.zeros((HEAD_DIM, NSA_TQ), F32)
    l2e = _bf16_terms(LOG2E)
    q_top, aug16 = [], []
    for g in range(NSA_GROUPS):
        tops = []
        for h in range(NSA_HPG):
            head = NSA_HPG * g + h
            qh = qt[head * HEAD_DIM:(head + 1) * HEAD_DIM]
            tops.append(jnp.concatenate([qh, zeros_h] if g == 0 else [zeros_h, qh], axis=0))
        q_top.append(jnp.concatenate(tops, axis=1))
        slope = lax.bitcast_convert_type((127 - (NSA_HPG * g + hl + 1)) << 23, F32)
        rows = {}
        for i, term in enumerate(l2e):
            rows[AUG_POS + 2 * i] = slope * (64.0 * term)
            rows[AUG_POS + 2 * i + 1] = slope * term
        for i, term in enumerate(_split3(-(slope * tqf) * LOG2E)):
            rows[AUG_TQ + i] = term.astype(F32)
        a = jnp.zeros((16, NSA_N), F32)
        for r, val in rows.items():
            a = jnp.where(ridx == r, val, a)
        aug16.append(a)
    return q_top, aug16


def _nsa_compressed(s_cmp, blk, vct_ref, ovt):
    tq = blk * NSA_TQ + _iota((1, NSA_N), 1) % NSA_TQ
    n_idx = _iota((LANES, NSA_N), 0)
    valid = (n_idx * CMP_STRIDE + (CMP_BLOCK - 1) <= tq) & (n_idx < LANES - 1)
    o_cmp, imps = [], []
    for g in range(NSA_GROUPS):
        sm = jnp.where(valid, s_cmp[g], NEG_BIG)
        mx = jnp.max(sm, axis=0, keepdims=True)
        e = jnp.where(valid, jnp.exp2(sm - mx), 0.0)
        den = jnp.sum(e, axis=0, keepdims=True)
        pn = e / jnp.where(den > 0.0, den, 1.0)
        o_cmp.append(_dot(vct_ref[g * HEAD_DIM:(g + 1) * HEAD_DIM, :], pn.astype(BF16)))

        psum = pn[:, 0:NSA_TQ]
        for h in range(1, NSA_HPG):
            psum = psum + pn[:, h * NSA_TQ:(h + 1) * NSA_TQ]
        hi, mid, lo = _split3(psum)
        imps.append(_dot(ovt, hi) + _dot(ovt, mid) + _dot(ovt, lo))
    return o_cmp, imps


def _nsa_select(imps, blk, q_top, aug16):
    t0 = blk * NSA_TQ
    ns = imps[0].shape[0]
    tcol = t0 + _iota((8, NSA_TQ), 1)
    q_sel = []
    for g, imp in enumerate(imps):
        tiles = range(ns // 8)
        j_idx = [8 * r + _iota((8, NSA_TQ), 0) for r in tiles]
        in_past = [j_idx[r] * SEL_BLOCK <= tcol for r in tiles]
        cur = tcol // SEL_BLOCK
        score = []
        for r in tiles:
            forced = (j_idx[r] == 0) | (j_idx[r] == cur) | (j_idx[r] == cur - 1)
            score.append(jnp.where(forced, FORCE_SCORE, jnp.where(in_past[r], imp[8 * r:8 * r + 8], -1.0)))
        rank = [jnp.zeros((8, NSA_TQ), F32) for _ in tiles]
        for i in range(ns):
            si = score[i // 8][i % 8:i % 8 + 1, :]
            for r in tiles:
                if 8 * r > i:
                    beats = si >= score[r]
                elif 8 * r + 7 <= i:
                    beats = si > score[r]
                else:
                    beats = (si > score[r]) | ((si == score[r]) & (j_idx[r] > i))
                rank[r] = rank[r] + jnp.where(beats, 1.0, 0.0)
        maskval = jnp.concatenate([jnp.where((rank[r] < min(N_SELECT, ns)) & in_past[r], 0.0, NEG_BIG) for r in tiles], axis=0)
        mask4 = jnp.concatenate([maskval] * NSA_HPG, axis=1)
        q_sel.append(jnp.concatenate([q_top[g], aug16[g], jnp.zeros((AUG_MASK - 16, NSA_N), F32), mask4,
                                      jnp.zeros((LANES - AUG_MASK - ns, NSA_N), F32)], axis=0).astype(BF16))
    return q_sel


def _nsa_kernel(q_ref, gf_ref, kc_ref, vc_ref, ks_ref, vs_ref, kw_ref, vw_ref,
                augk_ref, augw_ref, augc_ref, ovt_ref, trisel_ref, triwin_ref, o_ref,
                vst_ref, vwt_ref, kwp_ref, vct_ref, gt_ref, m_ref, acc_ref, s0_ref, s1_ref, mx0_ref, mx1_ref):
    qb = pl.program_id(1)
    t = ks_ref.shape[1]
    t0 = qb * NSA_TQ
    groups = range(NSA_GROUPS)
    zeros_tail = jnp.zeros((LANES - 16, NSA_N), F32)

    def cmp_scores(q_top, aug16):
        kcb = jnp.concatenate([kc_ref[0], augc_ref[...]], axis=1)
        return [_dot(kcb, jnp.concatenate([q_top[g], aug16[g], zeros_tail], axis=0).astype(BF16)) for g in groups]

    @pl.when(qb == 0)
    def _():
        vwt_ref[:, 0:WINDOW] = jnp.zeros((LANES, WINDOW), BF16)
        kwp_ref[0:WINDOW, :] = jnp.zeros((WINDOW, LANES), BF16)
        kwp_ref[WINDOW:WINDOW + t, :] = kw_ref[0]
        for c in range(t // LANES):
            sl = slice(c * LANES, (c + 1) * LANES)
            vst_ref[:, sl] = vs_ref[0, sl, :].astype(F32).T.astype(BF16)
            vwt_ref[:, WINDOW + c * LANES:WINDOW + (c + 1) * LANES] = vw_ref[0, sl, :].astype(F32).T.astype(BF16)
        vct_ref[...] = vc_ref[0].astype(F32).T.astype(BF16)

    q_top, aug16 = _nsa_queries(q_ref, qb)
    s_cmp = cmp_scores(q_top, aug16)
    kw0 = pl.multiple_of(t0, NSA_TQ)
    kwin = jnp.concatenate([kwp_ref[pl.ds(kw0, WIN_KEYS), :], augw_ref[pl.ds(kw0, WIN_KEYS), :]], axis=1)
    pad_row = _iota((16, NSA_N), 0) == AUG_PAD
    s_win = [_dot(kwin, jnp.concatenate([q_top[g], jnp.where(pad_row, NEG_BIG, aug16[g]), zeros_tail], axis=0).astype(BF16))
             for g in groups]
    o_cmp, imps = _nsa_compressed(s_cmp, qb, vct_ref, ovt_ref[...])
    q_sel = _nsa_select(imps, qb, q_top, aug16)

    hi_edge = jnp.concatenate([triwin_ref[0]] * NSA_HPG, axis=1)
    lo_edge = jnp.concatenate([triwin_ref[1]] * NSA_HPG, axis=1)
    ones_w = _ones_rows(WIN_KEYS)
    s_win = [jnp.concatenate([s[0:NSA_TQ] + lo_edge, s[NSA_TQ:WINDOW], s[WINDOW:WIN_KEYS] + hi_edge], axis=0)
             for s in s_win]
    vt_win = [jnp.concatenate([vwt_ref[g * HEAD_DIM:(g + 1) * HEAD_DIM, pl.ds(kw0, WIN_KEYS)], ones_w], axis=0)
              for g in groups]
    o_win = [acc[0:HEAD_DIM] / acc[HEAD_DIM:HEAD_DIM + 1] for acc in _softmax_pv(s_win, vt_win)]

    ones_s = _ones_rows(SEL_TK)
    per_chunk = SEL_TK // NSA_TQ
    n_full = qb // per_chunk
    s_bufs = (s0_ref, s1_ref)
    mx_bufs = (mx0_ref, mx1_ref)

    def sel_scores(cb):
        k0 = pl.multiple_of(cb * SEL_TK, SEL_TK)
        kblk = jnp.concatenate([ks_ref[0, pl.ds(k0, SEL_TK), :], augk_ref[pl.ds(k0, SEL_TK), :]], axis=1)
        return [_dot(kblk, q_sel[g]) for g in groups]

    def qk_to(buf, j):
        for g, s in enumerate(sel_scores(j - 1)):
            s_bufs[buf][g] = s
            mx_bufs[buf][g] = jnp.max(s, axis=0, keepdims=True)

    def spv_from(buf, j):
        cb = jnp.where(j == 0, n_full, j - 1)
        k0 = pl.multiple_of(cb * SEL_TK, SEL_TK)
        vts = [jnp.concatenate([vst_ref[g * HEAD_DIM:(g + 1) * HEAD_DIM, pl.ds(k0, SEL_TK)], ones_s], axis=0)
               for g in groups]
        _softmax_pv([s_bufs[buf][g] for g in groups], vts,
                    [m_ref.at[g] for g in groups], [acc_ref.at[g] for g in groups],
                    maxes=[mx_bufs[buf][g] for g in groups])

    diag = trisel_ref[qb % per_chunk]
    diag4 = jnp.concatenate([diag] * NSA_HPG, axis=1)
    for g, s in enumerate(sel_scores(n_full)):
        s = s + diag4
        s0_ref[g] = s
        mx0_ref[g] = jnp.max(s, axis=0, keepdims=True)
        _flash_init(m_ref.at[g], acc_ref.at[g])
    _pipelined_sweep(n_full, qk_to, spv_from)

    gt_ref[...] = gf_ref[0].T
    sig = jax.nn.sigmoid(gt_ref[GF_GATE:GF_GATE + 3 * NSA_HEADS, :])
    outs = []
    for g in groups:
        acc = acc_ref[g]
        o_sel = acc[0:HEAD_DIM] / acc[HEAD_DIM:HEAD_DIM + 1]
        for h in range(NSA_HPG):
            sl = slice(h * NSA_TQ, (h + 1) * NSA_TQ)
            head = NSA_HPG * g + h
            g0 = sig[head:head + 1]
            g1 = sig[NSA_HEADS + head:NSA_HEADS + head + 1]
            g2 = sig[2 * NSA_HEADS + head:2 * NSA_HEADS + head + 1]
            outs.append(g0 * o_cmp[g][:, sl] + g1 * o_sel[:, sl] + g2 * o_win[g][:, sl])
    o_ref[0] = jnp.concatenate(outs, axis=0).T.astype(BF16)


def _nsa(proj, gf, kcmp, vcmp, augk, augw, augc, ovt, trisel, triwin):
    b, t, _ = proj.shape
    nq = t // NSA_TQ
    kvb = COL_KV // LANES
    width = NSA_HEADS * HEAD_DIM
    whole = lambda a: pl.BlockSpec(a.shape, lambda i, k: (0,) * a.ndim)
    kv = lambda idx: pl.BlockSpec((1, t, LANES), lambda i, k: (i, 0, idx))
    return pl.pallas_call(
        _nsa_kernel,
        grid=(b, nq),
        in_specs=[
            pl.BlockSpec((1, NSA_TQ, width), lambda i, k: (i, k, COL_QN // width)),
            pl.BlockSpec((1, NSA_TQ, LANES), lambda i, k: (i, k, 0)),
            pl.BlockSpec((1, LANES, LANES), lambda i, k: (i, 0, 0)),
            pl.BlockSpec((1, LANES, LANES), lambda i, k: (i, 0, 0)),
            kv(kvb + 2), kv(kvb + 3), kv(kvb + 4), kv(kvb + 5),
            whole(augk), whole(augw), whole(augc), whole(ovt), whole(trisel), whole(triwin),
        ],
        out_specs=pl.BlockSpec((1, NSA_TQ, width), lambda i, k: (i, k, 0)),
        out_shape=jax.ShapeDtypeStruct((b, t, width), BF16),
        scratch_shapes=[
            pltpu.VMEM((LANES, t), BF16),
            pltpu.VMEM((LANES, t + WINDOW), BF16),
            pltpu.VMEM((t + WINDOW, LANES), BF16),
            pltpu.VMEM((LANES, LANES), BF16),
            pltpu.VMEM((LANES, NSA_TQ), F32),
            pltpu.VMEM((NSA_GROUPS, 1, NSA_N), F32),
            pltpu.VMEM((NSA_GROUPS, HEAD_DIM + 16, NSA_N), F32),
            pltpu.VMEM((NSA_GROUPS, SEL_TK, NSA_N), F32),
            pltpu.VMEM((NSA_GROUPS, SEL_TK, NSA_N), F32),
            pltpu.VMEM((NSA_GROUPS, 1, NSA_N), F32),
            pltpu.VMEM((NSA_GROUPS, 1, NSA_N), F32),
        ],
        compiler_params=pltpu.CompilerParams(
            dimension_semantics=("arbitrary", "arbitrary"), vmem_limit_bytes=VMEM_LIMIT),
        name="nsa_attention",
    )(proj, gf, kcmp, vcmp, proj, proj, proj, proj, augk, augw, augc, ovt, trisel, triwin)


FOX_CBLK = 256


def _foxprep_kernel(gf_ref, bias_ref, cq_ref, augk_ref):
    t = gf_ref.shape[1]
    cols = []
    for c in range(t // LANES):
        cols.append(gf_ref[0, c * LANES:(c + 1) * LANES, :].T[GF_FOX:GF_FOX + FOX_HEADS])
    f = jnp.concatenate(cols, axis=1) + bias_ref[...]
    ls = jnp.minimum(f, 0.0) - jnp.log(1.0 + jnp.exp(-jnp.abs(f)))
    upper = (_iota((FOX_CBLK, FOX_CBLK), 0) <= _iota((FOX_CBLK, FOX_CBLK), 1)).astype(BF16)
    carry = jnp.zeros((FOX_HEADS, 1), F32)
    parts = []
    for c in range(t // FOX_CBLK):
        hi, mid, lo = _split3(ls[:, c * FOX_CBLK:(c + 1) * FOX_CBLK])
        cb = _dot(hi, upper) + _dot(mid, upper) + _dot(lo, upper) + carry
        carry = cb[:, FOX_CBLK - 1:FOX_CBLK]
        parts.append(cb)
    csum = jnp.concatenate(parts, axis=1) * LOG2E
    c1, c2, c3 = (v.astype(F32) for v in _split3(csum))
    ones = jnp.ones((3 * FOX_HEADS, t), F32)
    zeros = jnp.zeros((LANES - 6 * FOX_HEADS, t), F32)
    cq_ref[0] = jnp.concatenate([c1, c2, c3, ones, zeros], axis=0).astype(BF16)
    slab_k = jnp.concatenate([ones, -c1, -c2, -c3, zeros], axis=0)
    for c in range(t // LANES):
        sl = slice(c * LANES, (c + 1) * LANES)
        augk_ref[0, sl, :] = slab_k[:, sl].T.astype(BF16)


def _foxprep(gf, bias):
    b, t, _ = gf.shape
    return pl.pallas_call(
        _foxprep_kernel,
        grid=(b,),
        in_specs=[pl.BlockSpec((1, t, LANES), lambda i: (i, 0, 0)),
                  pl.BlockSpec(bias.shape, lambda i: (0, 0))],
        out_specs=[pl.BlockSpec((1, LANES, t), lambda i: (i, 0, 0)),
                   pl.BlockSpec((1, t, LANES), lambda i: (i, 0, 0))],
        out_shape=[jax.ShapeDtypeStruct((b, LANES, t), BF16),
                   jax.ShapeDtypeStruct((b, t, LANES), BF16)],
        compiler_params=pltpu.CompilerParams(dimension_semantics=("arbitrary",)),
        name="fox_prep",
    )(gf, bias)


FOX_TQ = 256
FOX_TK = 256
FOX_NP = FOX_HEADS // 2
FOX_N = 2 * FOX_TQ
FOX_ROWS = HEAD_DIM + 16


def _fox_kernel(q_ref, k_ref, v_ref, cq_ref, augk_ref, tri_ref, o_ref, vt_ref, m_ref, acc_ref, s0_ref, s1_ref,
                mx0_ref, mx1_ref):
    qb = pl.program_id(1)
    t = k_ref.shape[1]

    @pl.when(qb == 0)
    def _():
        for p in range(FOX_NP):
            for c in range(t // LANES):
                sl = slice(c * LANES, (c + 1) * LANES)
                vt = v_ref[0, sl, p * LANES:(p + 1) * LANES].astype(F32).T.astype(BF16)
                for hh in range(2):
                    vt_ref[p, hh, 0:HEAD_DIM, sl] = vt[hh * HEAD_DIM:(hh + 1) * HEAD_DIM]
            for hh in range(2):
                vt_ref[p, hh, HEAD_DIM:FOX_ROWS, :] = _ones_rows(t)

    row = _iota((LANES, FOX_TQ), 0)
    hrow = row % FOX_HEADS
    cq = cq_ref[0].astype(F32)
    qs = []
    for p in range(FOX_NP):
        qt = q_ref[0, :, p * LANES:(p + 1) * LANES].astype(F32).T * (HEAD_DIM ** -0.5 * LOG2E)
        q_top = jnp.concatenate([jnp.where(row < HEAD_DIM, qt, 0.0), jnp.where(row >= HEAD_DIM, qt, 0.0)], axis=1)
        q_aug = jnp.concatenate([jnp.where(hrow == 2 * p, cq, 0.0), jnp.where(hrow == 2 * p + 1, cq, 0.0)], axis=1)
        qs.append(jnp.concatenate([q_top, q_aug], axis=0).astype(BF16))

    pairs = range(FOX_NP)
    s_bufs = (s0_ref, s1_ref)
    mx_bufs = (mx0_ref, mx1_ref)

    def scores(kb):
        k0 = pl.multiple_of(kb * FOX_TK, FOX_TK)
        ak = augk_ref[0, pl.ds(k0, FOX_TK), :]
        return [_dot(jnp.concatenate([k_ref[0, pl.ds(k0, FOX_TK), p * LANES:(p + 1) * LANES], ak], axis=1), qs[p])
                for p in pairs]

    def qk_to(buf, j):
        for p, s in enumerate(scores(j - 1)):
            s_bufs[buf][p] = s
            mx_bufs[buf][p] = jnp.max(s, axis=0, keepdims=True)

    def spv_from(buf, j):
        kb = jnp.where(j == 0, qb, j - 1)
        k0 = pl.multiple_of(kb * FOX_TK, FOX_TK)
        vts = [(vt_ref[p, 0, :, pl.ds(k0, FOX_TK)], vt_ref[p, 1, :, pl.ds(k0, FOX_TK)]) for p in pairs]
        _softmax_pv([s_bufs[buf][p] for p in pairs], vts,
                    [m_ref.at[p] for p in pairs], [acc_ref.at[p] for p in pairs],
                    maxes=[mx_bufs[buf][p] for p in pairs])

    tri = tri_ref[...]
    for p, s in enumerate(scores(qb)):
        s = s + tri
        s0_ref[p] = s
        mx0_ref[p] = jnp.max(s, axis=0, keepdims=True)
        _flash_init(m_ref.at[p], acc_ref.at[p])
    _pipelined_sweep(qb, qk_to, spv_from)
    for p in range(FOX_NP):
        acc = acc_ref[p]
        den = acc[HEAD_DIM:HEAD_DIM + 1]
        o_a = acc[0:HEAD_DIM, 0:FOX_TQ] / den[:, 0:FOX_TQ]
        o_b = acc[0:HEAD_DIM, FOX_TQ:FOX_N] / den[:, FOX_TQ:FOX_N]
        o_ref[0, :, p * LANES:(p + 1) * LANES] = jnp.concatenate([o_a, o_b], axis=0).T.astype(BF16)


def _fox(proj, cq, augk, tri):
    b, t, _ = proj.shape
    nq = t // FOX_TQ
    width = FOX_HEADS * HEAD_DIM
    qf, kf, vf = COL_QF // width, COL_KF // width, COL_VF // width
    return pl.pallas_call(
        _fox_kernel,
        grid=(b, nq),
        in_specs=[
            pl.BlockSpec((1, FOX_TQ, width), lambda i, k: (i, k, qf)),
            pl.BlockSpec((1, t, width), lambda i, k: (i, 0, kf)),
            pl.BlockSpec((1, t, width), lambda i, k: (i, 0, vf)),
            pl.BlockSpec((1, LANES, FOX_TQ), lambda i, k: (i, 0, k)),
            pl.BlockSpec((1, t, LANES), lambda i, k: (i, 0, 0)),
            pl.BlockSpec(tri.shape, lambda i, k: (0, 0)),
        ],
        out_specs=pl.BlockSpec((1, FOX_TQ, width), lambda i, k: (i, k, 0)),
        out_shape=jax.ShapeDtypeStruct((b, t, width), BF16),
        scratch_shapes=[
            pltpu.VMEM((FOX_NP, 2, FOX_ROWS, t), BF16),
            pltpu.VMEM((FOX_NP, 1, FOX_N), F32),
            pltpu.VMEM((FOX_NP, FOX_ROWS, FOX_N), F32),
            pltpu.VMEM((FOX_NP, FOX_TK, FOX_N), F32),
            pltpu.VMEM((FOX_NP, FOX_TK, FOX_N), F32),
            pltpu.VMEM((FOX_NP, 1, FOX_N), F32),
            pltpu.VMEM((FOX_NP, 1, FOX_N), F32),
        ],
        compiler_params=pltpu.CompilerParams(
            dimension_semantics=("arbitrary", "arbitrary"), vmem_limit_bytes=VMEM_LIMIT),
        name="fox_attention",
    )(proj, proj, proj, cq, augk, tri)


def _post_kernel(x_ref, on_ref, of_ref, mg_ref, wn_ref, wf_ref, wo_ref, gm_ref, wu_ref, wd_ref, gfin_ref, o_ref):
    d = x_ref.shape[1]
    ff = wu_ref.shape[1]
    ga = mg_ref[:, 0:d].astype(F32)
    gb = mg_ref[:, d:2 * d].astype(F32)
    y = ga * _dot(on_ref[...], wn_ref[...]) + gb * _dot(of_ref[...], wf_ref[...])
    h = x_ref[...] + _dot(y.astype(BF16), wo_ref[...])
    r = lax.rsqrt(jnp.mean(h * h, axis=-1, keepdims=True) + NORM_EPS)
    v = (h * r * gm_ref[...]).astype(BF16)
    acc = h
    for c0 in range(0, ff, 1024):
        up = jnp.maximum(_dot(v, wu_ref[:, c0:c0 + 1024]), 0.0)
        acc = acc + _dot((up * up).astype(BF16), wd_ref[c0:c0 + 1024, :])
    r2 = lax.rsqrt(jnp.mean(acc * acc, axis=-1, keepdims=True) + NORM_EPS)
    o_ref[...] = acc * r2 * gfin_ref[...]


def _post(x2, on, of, mg, wn, wf, wo, gm, wu, wd, gfin, tm):
    n, d = x2.shape
    row = lambda w: pl.BlockSpec((tm, w), lambda i: (i, 0))
    const = lambda a: pl.BlockSpec(a.shape, lambda i: (0, 0), pipeline_mode=pl.Buffered(1))
    return pl.pallas_call(
        _post_kernel,
        grid=(n // tm,),
        in_specs=[row(d), row(on.shape[1]), row(of.shape[1]), row(2 * d),
                  const(wn), const(wf), const(wo), const(gm), const(wu), const(wd), const(gfin)],
        out_specs=row(d),
        out_shape=jax.ShapeDtypeStruct((n, d), F32),
        compiler_params=pltpu.CompilerParams(
            dimension_semantics=("arbitrary",), vmem_limit_bytes=VMEM_LIMIT),
        name="post_mlp",
    )(x2, on, of, mg, wn, wf, wo, gm, wu, wd, gfin)


def _position_constants(t):
    def key_side(pos, pad):
        a = np.zeros((pos.size, LANES), np.float32)
        for i in range(3):
            a[:, AUG_POS + 2 * i] = pos // 64
            a[:, AUG_POS + 2 * i + 1] = pos % 64
        a[:, AUG_TQ:AUG_TQ + 3] = 1.0
        a[:, AUG_PAD] = pad
        return a

    pos = np.arange(t)
    augk = key_side(pos, 0.0)
    augk[pos, AUG_MASK + pos // SEL_BLOCK] = 1.0
    augw = np.concatenate([key_side(np.zeros(WINDOW, np.int64), 1.0), key_side(pos, 0.0)], axis=0)
    augc = key_side(np.arange(LANES) * CMP_STRIDE + CMP_BLOCK - 1, 0.0)

    nc = (t - CMP_BLOCK) // CMP_STRIDE + 1
    ns = t // SEL_BLOCK
    n = np.arange(LANES)[None, :]
    j = np.arange(ns)[:, None]
    ovt = ((n * CMP_STRIDE <= j * SEL_BLOCK + SEL_BLOCK - 1) & (n * CMP_STRIDE + CMP_BLOCK - 1 >= j * SEL_BLOCK) & (n < nc))

    def tri(tk, tq, anti=False):
        kk = np.arange(tk)[:, None]
        tt = np.arange(tq)[None, :]
        ok = (kk > tt) if anti else (kk <= tt)
        return np.where(ok, 0.0, NEG_BIG).astype(np.float32)

    edge = tri(NSA_TQ, NSA_TQ)
    trisel = np.zeros((SEL_TK // NSA_TQ, SEL_TK, NSA_TQ), np.float32)
    for v in range(SEL_TK // NSA_TQ):
        trisel[v, v * NSA_TQ:(v + 1) * NSA_TQ] = edge
    triwin = np.stack([edge, tri(NSA_TQ, NSA_TQ, anti=True)])
    trifox = np.tile(tri(FOX_TK, FOX_TQ), (1, 2))
    return (jnp.asarray(augk, BF16), jnp.asarray(augw, BF16), jnp.asarray(augc, BF16), jnp.asarray(ovt, BF16),
            jnp.asarray(trisel), jnp.asarray(triwin), jnp.asarray(trifox))


def kernel(x, norm_mix, w_in, cmp_pos_k, cmp_w1_k, cmp_w2_k, cmp_pos_v, cmp_w1_v, cmp_w2_v, fox_f_bias, w_branch_nsa, w_branch_fox, w_merge_gate, b_merge_gate, w_out, norm_mlp, w_up, w_down, norm_final):
    b, t, d = x.shape
    assert w_in.shape[0] == 1, "one layer: the final norm is fused into the MLP kernel"
    assert t % FOX_TQ == 0 and t // SEL_BLOCK == 32 and (t - CMP_BLOCK) // CMP_STRIDE + 1 == LANES - 1
    augk, augw, augc, ovt, trisel, triwin, trifox = _position_constants(t)
    h = x.reshape(b * t, d)
    w = w_in[0]
    nq_w = NSA_HEADS * HEAD_DIM
    gate0 = nq_w + 6 * LANES
    fox0 = gate0 + 3 * NSA_HEADS
    wa = jnp.concatenate([w[:, :nq_w], w[:, fox0:fox0 + 3 * FOX_HEADS * HEAD_DIM], w[:, nq_w:gate0]], axis=1).astype(BF16)
    wg = jnp.concatenate([w[:, gate0:fox0], w[:, -FOX_HEADS:],
                          jnp.zeros((d, LANES - 3 * NSA_HEADS - FOX_HEADS), F32)], axis=1).astype(BF16)
    proj, gf, mg = _inproj(h, norm_mix[0][None, :], wa, wg, w_merge_gate[0].astype(BF16),
                           b_merge_gate[0][None, :], ROW_TILE)
    proj = proj.reshape(b, t, PROJ_W)
    gf = gf.reshape(b, t, LANES)

    def w1_blocks(w1):
        wa_ = w1[:CMP_STRIDE * HEAD_DIM].reshape(CMP_STRIDE, HEAD_DIM, CMP_HIDDEN)
        wb_ = w1[CMP_STRIDE * HEAD_DIM:].reshape(CMP_STRIDE, HEAD_DIM, CMP_HIDDEN)
        z = jnp.zeros_like(wa_)
        top = jnp.concatenate([wa_, z, wb_, z], axis=2)
        bot = jnp.concatenate([z, wa_, z, wb_], axis=2)
        return jnp.concatenate([top, bot], axis=1).astype(BF16)

    def w2_pair(w2):
        z = jnp.zeros_like(w2)
        return (jnp.concatenate([w2, z], axis=1).astype(BF16), jnp.concatenate([z, w2], axis=1).astype(BF16))

    def pos_rows(p):
        row = jnp.concatenate([p[:CMP_STRIDE], p[CMP_STRIDE:]], axis=1)[:, None, :]
        return jnp.concatenate([row, jnp.zeros((CMP_STRIDE, 7, 2 * HEAD_DIM), F32)], axis=1).astype(BF16)

    w2k0, w2k1 = w2_pair(cmp_w2_k[0])
    w2v0, w2v1 = w2_pair(cmp_w2_v[0])
    kcmp, vcmp = _compress(proj, w1_blocks(cmp_w1_k[0]), w2k0, w2k1, pos_rows(cmp_pos_k[0]),
                           w1_blocks(cmp_w1_v[0]), w2v0, w2v1, pos_rows(cmp_pos_v[0]))
    o_n = _nsa(proj, gf, kcmp, vcmp, augk, augw, augc, ovt, trisel, triwin)
    cq, faug = _foxprep(gf, fox_f_bias[0][:, None])
    o_f = _fox(proj, cq, faug, trifox)
    out = _post(h, o_n.reshape(b * t, -1), o_f.reshape(b * t, -1), mg,
                w_branch_nsa[0].astype(BF16), w_branch_fox[0].astype(BF16), w_out[0].astype(BF16),
                norm_mlp[0][None, :], w_up[0].astype(BF16), w_down[0].astype(BF16), norm_final[None, :], ROW_TILE)
    return out.reshape(b, t, d)
```

```python
import numpy as np
import jax
import jax.numpy as jnp
from jax import lax
from jax.experimental import pallas as pl
from jax.experimental.pallas import tpu as pltpu

HEAD_DIM = 64
NSA_HEADS = 8
NSA_GROUPS = 2
NSA_HPG = NSA_HEADS // NSA_GROUPS
FOX_HEADS = 8
CMP_BLOCK = 32
CMP_STRIDE = 16
CMP_HIDDEN = 128
SEL_BLOCK = 64
N_SELECT = 16
WINDOW = 512
NORM_EPS = 1e-6
FORCE_SCORE = 1e9

LANES = 128
NEG_BIG = -(2.0 ** 100)
M_INIT = -3.0e38
LOG2E = 1.4426950408889634
VMEM_LIMIT = 56 * 1024 * 1024
ROW_TILE = 512
ROW_SPLIT = 2
FF_CHUNK = 1024

PROJ_W = 2816
COL_QN = 0
COL_QF = 512
COL_KF = 1024
COL_VF = 1536
COL_KV = 2048
GF_GATE = 0
GF_FOX = 24

F32 = jnp.float32
BF16 = jnp.bfloat16


def _dot(a, b):
    return jnp.dot(a, b, preferred_element_type=F32)


def _iota(shape, dim):
    return lax.broadcasted_iota(jnp.int32, shape, dim)


def _split3(x):
    hi = x.astype(BF16)
    r1 = x - hi.astype(F32)
    mid = r1.astype(BF16)
    lo = (r1 - mid.astype(F32)).astype(BF16)
    return hi, mid, lo


W_QN = NSA_HEADS * HEAD_DIM
W_KV = 6 * NSA_GROUPS * HEAD_DIM
W_GATE = 3 * NSA_HEADS
W_FOX = 3 * FOX_HEADS * HEAD_DIM
W_PIECES = ((0, W_QN), (W_QN + W_KV + W_GATE, W_FOX), (W_QN, W_KV))


def _inproj_kernel(x_ref, g_ref, wt_ref, wm_ref, bm_ref, proj_ref, gf_ref, mg_ref, wq_ref, wf_ref, wkv_ref, wg_ref):
    w_refs = (wq_ref, wf_ref, wkv_ref)

    @pl.when(pl.program_id(0) == 0)
    def _():
        d = wt_ref.shape[2]
        for (c0, width), dst in zip(W_PIECES, w_refs):
            for k0 in range(0, width, 256):
                dst[:, k0:k0 + 256] = wt_ref[0, c0 + k0:c0 + k0 + 256, :].T.astype(BF16)
        gate0 = W_QN + W_KV
        fox0 = gate0 + W_GATE + W_FOX
        gates = jnp.concatenate([wt_ref[0, gate0:gate0 + W_GATE, :], wt_ref[0, fox0:fox0 + FOX_HEADS, :],
                                 jnp.zeros((LANES - W_GATE - FOX_HEADS, d), F32)], axis=0)
        wg_ref[...] = gates.T.astype(BF16)

    sub = x_ref.shape[0] // ROW_SPLIT
    for h in range(ROW_SPLIT):
        rows = slice(h * sub, (h + 1) * sub)
        x = x_ref[rows, :]
        r = lax.rsqrt(jnp.mean(x * x, axis=-1, keepdims=True) + NORM_EPS)
        u = (x * r * g_ref[...]).astype(BF16)
        c0 = 0
        for piece in w_refs:
            for k0 in range(0, piece.shape[1], 768):
                k1 = min(k0 + 768, piece.shape[1])
                proj_ref[rows, c0 + k0:c0 + k1] = _dot(u, piece[:, k0:k1]).astype(BF16)
            c0 += piece.shape[1]
        gf_ref[rows, :] = _dot(u, wg_ref[...])
        for c0 in (0, 1024):
            z = _dot(u, wm_ref[:, c0:c0 + 1024]) + bm_ref[:, c0:c0 + 1024]
            mg_ref[rows, c0:c0 + 1024] = jax.nn.sigmoid(z).astype(BF16)


def _inproj(x2, g, w_in_t, wm, bm, tm):
    n, d = x2.shape
    const = lambda i: (0, 0)
    once = dict(pipeline_mode=pl.Buffered(1))
    return pl.pallas_call(
        _inproj_kernel,
        grid=(n // tm,),
        in_specs=[
            pl.BlockSpec((tm, d), lambda i: (i, 0)),
            pl.BlockSpec((1, d), const),
            pl.BlockSpec(w_in_t.shape, lambda i: (0, 0, 0), **once),
            pl.BlockSpec(wm.shape, const, **once),
            pl.BlockSpec(bm.shape, const),
        ],
        out_specs=[
            pl.BlockSpec((tm, PROJ_W), lambda i: (i, 0)),
            pl.BlockSpec((tm, LANES), lambda i: (i, 0)),
            pl.BlockSpec((tm, 2 * d), lambda i: (i, 0)),
        ],
        out_shape=[
            jax.ShapeDtypeStruct((n, PROJ_W), BF16),
            jax.ShapeDtypeStruct((n, LANES), F32),
            jax.ShapeDtypeStruct((n, 2 * d), BF16),
        ],
        scratch_shapes=[pltpu.VMEM((d, width), BF16) for _, width in W_PIECES] + [pltpu.VMEM((d, LANES), BF16)],
        compiler_params=pltpu.CompilerParams(
            dimension_semantics=("arbitrary",), vmem_limit_bytes=VMEM_LIMIT),
        name="inproj",
    )(x2, g, w_in_t, wm, bm)


def _compress_kernel(kc_ref, vc_ref, wk_ref, w2k0_ref, w2k1_ref, pk_ref,
                     wv_ref, w2v0_ref, w2v1_ref, pv_ref, kco_ref, vco_ref, xs_ref):
    t = kc_ref.shape[1]
    nblk = t // CMP_STRIDE

    def one(x_ref, w_ref, w2_refs, p_ref, o_ref):
        xs_ref[...] = x_ref[0].astype(F32)
        acc = jnp.zeros((nblk + 16, 4 * LANES), F32)
        for j in range(CMP_STRIDE):
            xj = xs_ref[pl.ds(j, nblk, stride=CMP_STRIDE), :].astype(BF16)
            acc = acc + _dot(jnp.concatenate([xj, p_ref[j]], axis=0), w_ref[j])
        posb = acc[nblk:nblk + 1, 0:LANES] + acc[nblk:nblk + 1, 3 * LANES:4 * LANES]
        out = jnp.zeros((nblk, LANES), F32)
        for g in range(NSA_GROUPS):
            a = acc[0:nblk, g * LANES:(g + 1) * LANES]
            b = acc[0:nblk, (2 + g) * LANES:(3 + g) * LANES]
            h = a + pltpu.roll(b, nblk - 1, 0) + posb
            out = out + _dot(jax.nn.gelu(h).astype(BF16), w2_refs[g][...])
        o_ref[0] = out.astype(BF16)

    one(kc_ref, wk_ref, (w2k0_ref, w2k1_ref), pk_ref, kco_ref)
    one(vc_ref, wv_ref, (w2v0_ref, w2v1_ref), pv_ref, vco_ref)


def _compress(proj, wk, w2k0, w2k1, pk, wv, w2v0, w2v1, pv):
    b, t, _ = proj.shape
    kvb = COL_KV // LANES
    whole = lambda a: pl.BlockSpec(a.shape, lambda i: (0,) * a.ndim)
    ospec = pl.BlockSpec((1, LANES, LANES), lambda i: (i, 0, 0))
    return pl.pallas_call(
        _compress_kernel,
        grid=(b,),
        in_specs=[pl.BlockSpec((1, t, LANES), lambda i: (i, 0, kvb)),
                  pl.BlockSpec((1, t, LANES), lambda i: (i, 0, kvb + 1)),
                  whole(wk), whole(w2k0), whole(w2k1), whole(pk), whole(wv), whole(w2v0), whole(w2v1), whole(pv)],
        out_specs=[ospec, ospec],
        out_shape=[jax.ShapeDtypeStruct((b, LANES, LANES), BF16)] * 2,
        scratch_shapes=[pltpu.VMEM((t, LANES), F32)],
        compiler_params=pltpu.CompilerParams(dimension_semantics=("arbitrary",)),
        name="compress",
    )(proj, proj, wk, w2k0, w2k1, pk, wv, w2v0, w2v1, pv)


def _flash_init(m_ref, acc_ref):
    m_ref[...] = jnp.full(m_ref.shape, M_INIT, F32)
    acc_ref[...] = jnp.zeros(acc_ref.shape, F32)


def _softmax_pv(scores, vt_blks, m_refs=None, acc_refs=None, maxes=None):
    def pv(vt, p):
        if not isinstance(vt, tuple):
            return _dot(vt, p)
        w = p.shape[1] // len(vt)
        return jnp.concatenate([_dot(v, p[:, j * w:(j + 1) * w]) for j, v in enumerate(vt)], axis=1)

    ps, alphas = [], []
    for i, s in enumerate(scores):
        mx = jnp.max(s, axis=0, keepdims=True) if maxes is None else maxes[i]
        if m_refs is None:
            ps.append(jnp.exp2(s - mx).astype(BF16))
            continue
        m_old = m_refs[i][...]
        m_new = jnp.maximum(m_old, mx)
        ps.append(jnp.exp2(s - m_new).astype(BF16))
        alphas.append(jnp.exp2(m_old - m_new))
        m_refs[i][...] = m_new
    if m_refs is None:
        return [pv(vt, p) for vt, p in zip(vt_blks, ps)]
    for vt, p, alpha, acc_ref in zip(vt_blks, ps, alphas, acc_refs):
        acc_ref[...] = acc_ref[...] * alpha + pv(vt, p)
    return None


def _pipelined_sweep(n_rest, qk_to, spv_from):
    def body(jj, carry):
        j = 2 * jj
        qk_to(1, j + 1)
        spv_from(0, j)
        qk_to(0, j + 2)
        spv_from(1, j + 1)
        return carry

    lax.fori_loop(0, n_rest // 2, body, 0)

    @pl.when(n_rest % 2 == 1)
    def _():
        qk_to(1, n_rest)
        spv_from(0, n_rest - 1)
        spv_from(1, n_rest)

    @pl.when(n_rest % 2 == 0)
    def _():
        spv_from(0, n_rest)


def _ones_rows(tk):
    return jnp.where(_iota((16, tk), 0) == 0, 1.0, 0.0).astype(BF16)


NSA_TQ = 128
NSA_N = NSA_HPG * NSA_TQ
SEL_TK = 512
WIN_KEYS = WINDOW + NSA_TQ

AUG_POS = 0
AUG_TQ = 6
AUG_PAD = 9
AUG_MASK = 32


def _bf16_terms(x, n=3):
    out, r = [], np.float32(x)
    for _ in range(n):
        term = np.asarray(r, np.float32).astype(BF16).astype(np.float32)
        out.append(float(term))
        r = np.float32(r - term)
    return out


def _nsa_queries(q_ref, blk):
    t0 = blk * NSA_TQ
    qt = q_ref[0].astype(F32).T * (HEAD_DIM ** -0.5 * LOG2E)
    lane = _iota((1, NSA_N), 1)
    hl = lane // NSA_TQ
    tqf = (t0 + lane % NSA_TQ).astype(F32)
    ridx = _iota((16, NSA_N), 0)
    zeros_h = jnp.zeros((HEAD_DIM, NSA_TQ), F32)
    l2e = _bf16_terms(LOG2E)
    q_top, aug16 = [], []
    for g in range(NSA_GROUPS):
        tops = []
        for h in range(NSA_HPG):
            head = NSA_HPG * g + h
            qh = qt[head * HEAD_DIM:(head + 1) * HEAD_DIM]
            tops.append(jnp.concatenate([qh, zeros_h] if g == 0 else [zeros_h, qh], axis=0))
        q_top.append(jnp.concatenate(tops, axis=1))
        slope = lax.bitcast_convert_type((127 - (NSA_HPG * g + hl + 1)) << 23, F32)
        rows = {}
        for i, term in enumerate(l2e):
            rows[AUG_POS + 2 * i] = slope * (64.0 * term)
            rows[AUG_POS + 2 * i + 1] = slope * term
        for i, term in enumerate(_split3(-(slope * tqf) * LOG2E)):
            rows[AUG_TQ + i] = term.astype(F32)
        a = jnp.zeros((16, NSA_N), F32)
        for r, val in rows.items():
            a = jnp.where(ridx == r, val, a)
        aug16.append(a)
    return q_top, aug16


def _nsa_compressed(s_cmp, blk, vct_ref, ovt):
    tq = blk * NSA_TQ + _iota((1, NSA_N), 1) % NSA_TQ
    n_idx = _iota((LANES, NSA_N), 0)
    valid = (n_idx * CMP_STRIDE + (CMP_BLOCK - 1) <= tq) & (n_idx < LANES - 1)
    o_cmp, imps = [], []
    for g in range(NSA_GROUPS):
        sm = jnp.where(valid, s_cmp[g], NEG_BIG)
        mx = jnp.max(sm, axis=0, keepdims=True)
        e = jnp.where(valid, jnp.exp2(sm - mx), 0.0)
        den = jnp.sum(e, axis=0, keepdims=True)
        pn = e / jnp.where(den > 0.0, den, 1.0)
        o_cmp.append(_dot(vct_ref[g * HEAD_DIM:(g + 1) * HEAD_DIM, :], pn.astype(BF16)))

        psum = pn[:, 0:NSA_TQ]
        for h in range(1, NSA_HPG):
            psum = psum + pn[:, h * NSA_TQ:(h + 1) * NSA_TQ]
        hi, mid, lo = _split3(psum)
        imps.append(_dot(ovt, hi) + _dot(ovt, mid) + _dot(ovt, lo))
    return o_cmp, imps


def _nsa_select(imps, blk, q_top, aug16):
    t0 = blk * NSA_TQ
    ns = imps[0].shape[0]
    tcol = t0 + _iota((8, NSA_TQ), 1)
    q_sel = []
    for g, imp in enumerate(imps):
        tiles = range(ns // 8)
        j_idx = [8 * r + _iota((8, NSA_TQ), 0) for r in tiles]
        in_past = [j_idx[r] * SEL_BLOCK <= tcol for r in tiles]
        cur = tcol // SEL_BLOCK
        score = []
        for r in tiles:
            forced = (j_idx[r] == 0) | (j_idx[r] == cur) | (j_idx[r] == cur - 1)
            score.append(jnp.where(forced, FORCE_SCORE, jnp.where(in_past[r], imp[8 * r:8 * r + 8], -1.0)))
        rank = [jnp.zeros((8, NSA_TQ), F32) for _ in tiles]
        for i in range(ns):
            si = score[i // 8][i % 8:i % 8 + 1, :]
            for r in tiles:
                if 8 * r > i:
                    beats = si >= score[r]
                elif 8 * r + 7 <= i:
                    beats = si > score[r]
                else:
                    beats = (si > score[r]) | ((si == score[r]) & (j_idx[r] > i))
                rank[r] = rank[r] + jnp.where(beats, 1.0, 0.0)
        maskval = jnp.concatenate([jnp.where((rank[r] < min(N_SELECT, ns)) & in_past[r], 0.0, NEG_BIG) for r in tiles], axis=0)
        mask4 = jnp.concatenate([maskval] * NSA_HPG, axis=1)
        q_sel.append(jnp.concatenate([q_top[g], aug16[g], jnp.zeros((AUG_MASK - 16, NSA_N), F32), mask4,
                                      jnp.zeros((LANES - AUG_MASK - ns, NSA_N), F32)], axis=0).astype(BF16))
    return q_sel


def _nsa_kernel(q_ref, gf_ref, kc_ref, vc_ref, ks_ref, vs_ref, kw_ref, vw_ref,
                augk_ref, augw_ref, augc_ref, ovt_ref, trisel_ref, triwin_ref, o_ref,
                vst_ref, vwt_ref, kwp_ref, vct_ref, gt_ref, m_ref, acc_ref, s0_ref, s1_ref, mx0_ref, mx1_ref):
    qb = pl.program_id(1)
    t = ks_ref.shape[1]
    t0 = qb * NSA_TQ
    groups = range(NSA_GROUPS)
    zeros_tail = jnp.zeros((LANES - 16, NSA_N), F32)

    def cmp_scores(q_top, aug16):
        kcb = jnp.concatenate([kc_ref[0], augc_ref[...]], axis=1)
        return [_dot(kcb, jnp.concatenate([q_top[g], aug16[g], zeros_tail], axis=0).astype(BF16)) for g in groups]

    @pl.when(qb == 0)
    def _():
        vwt_ref[:, 0:WINDOW] = jnp.zeros((LANES, WINDOW), BF16)
        kwp_ref[0:WINDOW, :] = jnp.zeros((WINDOW, LANES), BF16)
        kwp_ref[WINDOW:WINDOW + t, :] = kw_ref[0]
        for c in range(t // LANES):
            sl = slice(c * LANES, (c + 1) * LANES)
            vst_ref[:, sl] = vs_ref[0, sl, :].astype(F32).T.astype(BF16)
            vwt_ref[:, WINDOW + c * LANES:WINDOW + (c + 1) * LANES] = vw_ref[0, sl, :].astype(F32).T.astype(BF16)
        vct_ref[...] = vc_ref[0].astype(F32).T.astype(BF16)

    q_top, aug16 = _nsa_queries(q_ref, qb)
    s_cmp = cmp_scores(q_top, aug16)
    kw0 = pl.multiple_of(t0, NSA_TQ)
    kwin = jnp.concatenate([kwp_ref[pl.ds(kw0, WIN_KEYS), :], augw_ref[pl.ds(kw0, WIN_KEYS), :]], axis=1)
    pad_row = _iota((16, NSA_N), 0) == AUG_PAD
    s_win = [_dot(kwin, jnp.concatenate([q_top[g], jnp.where(pad_row, NEG_BIG, aug16[g]), zeros_tail], axis=0).astype(BF16))
             for g in groups]
    o_cmp, imps = _nsa_compressed(s_cmp, qb, vct_ref, ovt_ref[...])
    q_sel = _nsa_select(imps, qb, q_top, aug16)

    hi_edge = jnp.concatenate([triwin_ref[0]] * NSA_HPG, axis=1)
    lo_edge = jnp.concatenate([triwin_ref[1]] * NSA_HPG, axis=1)
    ones_w = _ones_rows(WIN_KEYS)
    s_win = [jnp.concatenate([s[0:NSA_TQ] + lo_edge, s[NSA_TQ:WINDOW], s[WINDOW:WIN_KEYS] + hi_edge], axis=0)
             for s in s_win]
    vt_win = [jnp.concatenate([vwt_ref[g * HEAD_DIM:(g + 1) * HEAD_DIM, pl.ds(kw0, WIN_KEYS)], ones_w], axis=0)
              for g in groups]
    o_win = [acc[0:HEAD_DIM] / acc[HEAD_DIM:HEAD_DIM + 1] for acc in _softmax_pv(s_win, vt_win)]

    ones_s = _ones_rows(SEL_TK)
    per_chunk = SEL_TK // NSA_TQ
    n_full = qb // per_chunk
    s_bufs = (s0_ref, s1_ref)
    mx_bufs = (mx0_ref, mx1_ref)

    def sel_scores(cb):
        k0 = pl.multiple_of(cb * SEL_TK, SEL_TK)
        kblk = jnp.concatenate([ks_ref[0, pl.ds(k0, SEL_TK), :], augk_ref[pl.ds(k0, SEL_TK), :]], axis=1)
        return [_dot(kblk, q_sel[g]) for g in groups]

    def qk_to(buf, j):
        for g, s in enumerate(sel_scores(j - 1)):
            s_bufs[buf][g] = s
            mx_bufs[buf][g] = jnp.max(s, axis=0, keepdims=True)

    def spv_from(buf, j):
        cb = jnp.where(j == 0, n_full, j - 1)
        k0 = pl.multiple_of(cb * SEL_TK, SEL_TK)
        vts = [jnp.concatenate([vst_ref[g * HEAD_DIM:(g + 1) * HEAD_DIM, pl.ds(k0, SEL_TK)], ones_s], axis=0)
               for g in groups]
        _softmax_pv([s_bufs[buf][g] for g in groups], vts,
                    [m_ref.at[g] for g in groups], [acc_ref.at[g] for g in groups],
                    maxes=[mx_bufs[buf][g] for g in groups])

    diag = trisel_ref[qb % per_chunk]
    diag4 = jnp.concatenate([diag] * NSA_HPG, axis=1)
    for g, s in enumerate(sel_scores(n_full)):
        s = s + diag4
        s0_ref[g] = s
        mx0_ref[g] = jnp.max(s, axis=0, keepdims=True)
        _flash_init(m_ref.at[g], acc_ref.at[g])
    _pipelined_sweep(n_full, qk_to, spv_from)

    gt_ref[...] = gf_ref[0].T
    sig = jax.nn.sigmoid(gt_ref[GF_GATE:GF_GATE + 3 * NSA_HEADS, :])
    outs = []
    for g in groups:
        acc = acc_ref[g]
        o_sel = acc[0:HEAD_DIM] / acc[HEAD_DIM:HEAD_DIM + 1]
        for h in range(NSA_HPG):
            sl = slice(h * NSA_TQ, (h + 1) * NSA_TQ)
            head = NSA_HPG * g + h
            g0 = sig[head:head + 1]
            g1 = sig[NSA_HEADS + head:NSA_HEADS + head + 1]
            g2 = sig[2 * NSA_HEADS + head:2 * NSA_HEADS + head + 1]
            outs.append(g0 * o_cmp[g][:, sl] + g1 * o_sel[:, sl] + g2 * o_win[g][:, sl])
    o_ref[0] = jnp.concatenate(outs, axis=0).T.astype(BF16)


def _nsa(proj, gf, kcmp, vcmp, augk, augw, augc, ovt, trisel, triwin):
    b, t, _ = proj.shape
    nq = t // NSA_TQ
    kvb = COL_KV // LANES
    width = NSA_HEADS * HEAD_DIM
    whole = lambda a: pl.BlockSpec(a.shape, lambda i, k: (0,) * a.ndim)
    kv = lambda idx: pl.BlockSpec((1, t, LANES), lambda i, k: (i, 0, idx))
    return pl.pallas_call(
        _nsa_kernel,
        grid=(b, nq),
        in_specs=[
            pl.BlockSpec((1, NSA_TQ, width), lambda i, k: (i, k, COL_QN // width)),
            pl.BlockSpec((1, NSA_TQ, LANES), lambda i, k: (i, k, 0)),
            pl.BlockSpec((1, LANES, LANES), lambda i, k: (i, 0, 0)),
            pl.BlockSpec((1, LANES, LANES), lambda i, k: (i, 0, 0)),
            kv(kvb + 2), kv(kvb + 3), kv(kvb + 4), kv(kvb + 5),
            whole(augk), whole(augw), whole(augc), whole(ovt), whole(trisel), whole(triwin),
        ],
        out_specs=pl.BlockSpec((1, NSA_TQ, width), lambda i, k: (i, k, 0)),
        out_shape=jax.ShapeDtypeStruct((b, t, width), BF16),
        scratch_shapes=[
            pltpu.VMEM((LANES, t), BF16),
            pltpu.VMEM((LANES, t + WINDOW), BF16),
            pltpu.VMEM((t + WINDOW, LANES), BF16),
            pltpu.VMEM((LANES, LANES), BF16),
            pltpu.VMEM((LANES, NSA_TQ), F32),
            pltpu.VMEM((NSA_GROUPS, 1, NSA_N), F32),
            pltpu.VMEM((NSA_GROUPS, HEAD_DIM + 16, NSA_N), F32),
            pltpu.VMEM((NSA_GROUPS, SEL_TK, NSA_N), F32),
            pltpu.VMEM((NSA_GROUPS, SEL_TK, NSA_N), F32),
            pltpu.VMEM((NSA_GROUPS, 1, NSA_N), F32),
            pltpu.VMEM((NSA_GROUPS, 1, NSA_N), F32),
        ],
        compiler_params=pltpu.CompilerParams(
            dimension_semantics=("arbitrary", "arbitrary"), vmem_limit_bytes=VMEM_LIMIT),
        name="nsa_attention",
    )(proj, gf, kcmp, vcmp, proj, proj, proj, proj, augk, augw, augc, ovt, trisel, triwin)


FOX_CBLK = 256


def _foxprep_kernel(gf_ref, bias_ref, cq_ref, augk_ref):
    t = gf_ref.shape[1]
    cols = []
    for c in range(t // LANES):
        cols.append(gf_ref[0, c * LANES:(c + 1) * LANES, :].T[GF_FOX:GF_FOX + FOX_HEADS])
    f = jnp.concatenate(cols, axis=1) + bias_ref[...]
    ls = jnp.minimum(f, 0.0) - jnp.log(1.0 + jnp.exp(-jnp.abs(f)))
    upper = (_iota((FOX_CBLK, FOX_CBLK), 0) <= _iota((FOX_CBLK, FOX_CBLK), 1)).astype(BF16)
    carry = jnp.zeros((FOX_HEADS, 1), F32)
    parts = []
    for c in range(t // FOX_CBLK):
        hi, mid, lo = _split3(ls[:, c * FOX_CBLK:(c + 1) * FOX_CBLK])
        cb = _dot(hi, upper) + _dot(mid, upper) + _dot(lo, upper) + carry
        carry = cb[:, FOX_CBLK - 1:FOX_CBLK]
        parts.append(cb)
    csum = jnp.concatenate(parts, axis=1) * LOG2E
    c1, c2, c3 = (v.astype(F32) for v in _split3(csum))
    ones = jnp.ones((3 * FOX_HEADS, t), F32)
    zeros = jnp.zeros((LANES - 6 * FOX_HEADS, t), F32)
    cq_ref[0] = jnp.concatenate([c1, c2, c3, ones, zeros], axis=0).astype(BF16)
    slab_k = jnp.concatenate([ones, -c1, -c2, -c3, zeros], axis=0)
    for c in range(t // LANES):
        sl = slice(c * LANES, (c + 1) * LANES)
        augk_ref[0, sl, :] = slab_k[:, sl].T.astype(BF16)


def _foxprep(gf, bias):
    b, t, _ = gf.shape
    return pl.pallas_call(
        _foxprep_kernel,
        grid=(b,),
        in_specs=[pl.BlockSpec((1, t, LANES), lambda i: (i, 0, 0)),
                  pl.BlockSpec(bias.shape, lambda i: (0, 0))],
        out_specs=[pl.BlockSpec((1, LANES, t), lambda i: (i, 0, 0)),
                   pl.BlockSpec((1, t, LANES), lambda i: (i, 0, 0))],
        out_shape=[jax.ShapeDtypeStruct((b, LANES, t), BF16),
                   jax.ShapeDtypeStruct((b, t, LANES), BF16)],
        compiler_params=pltpu.CompilerParams(dimension_semantics=("arbitrary",)),
        name="fox_prep",
    )(gf, bias)


FOX_TQ = 256
FOX_TK = 256
FOX_NP = FOX_HEADS // 2
FOX_N = 2 * FOX_TQ
FOX_ROWS = HEAD_DIM + 16


def _fox_kernel(q_ref, k_ref, v_ref, cq_ref, augk_ref, tri_ref, o_ref, vt_ref, m_ref, acc_ref, s0_ref, s1_ref,
                mx0_ref, mx1_ref):
    qb = pl.program_id(1)
    t = k_ref.shape[1]

    @pl.when(qb == 0)
    def _():
        for p in range(FOX_NP):
            for c in range(t // LANES):
                sl = slice(c * LANES, (c + 1) * LANES)
                vt = v_ref[0, sl, p * LANES:(p + 1) * LANES].astype(F32).T.astype(BF16)
                for hh in range(2):
                    vt_ref[p, hh, 0:HEAD_DIM, sl] = vt[hh * HEAD_DIM:(hh + 1) * HEAD_DIM]
            for hh in range(2):
                vt_ref[p, hh, HEAD_DIM:FOX_ROWS, :] = _ones_rows(t)

    row = _iota((LANES, FOX_TQ), 0)
    hrow = row % FOX_HEADS
    cq = cq_ref[0].astype(F32)
    qs = []
    for p in range(FOX_NP):
        qt = q_ref[0, :, p * LANES:(p + 1) * LANES].astype(F32).T * (HEAD_DIM ** -0.5 * LOG2E)
        q_top = jnp.concatenate([jnp.where(row < HEAD_DIM, qt, 0.0), jnp.where(row >= HEAD_DIM, qt, 0.0)], axis=1)
        q_aug = jnp.concatenate([jnp.where(hrow == 2 * p, cq, 0.0), jnp.where(hrow == 2 * p + 1, cq, 0.0)], axis=1)
        qs.append(jnp.concatenate([q_top, q_aug], axis=0).astype(BF16))

    pairs = range(FOX_NP)
    s_bufs = (s0_ref, s1_ref)
    mx_bufs = (mx0_ref, mx1_ref)

    def scores(kb):
        k0 = pl.multiple_of(kb * FOX_TK, FOX_TK)
        ak = augk_ref[0, pl.ds(k0, FOX_TK), :]
        return [_dot(jnp.concatenate([k_ref[0, pl.ds(k0, FOX_TK), p * LANES:(p + 1) * LANES], ak], axis=1), qs[p])
                for p in pairs]

    def qk_to(buf, j):
        for p, s in enumerate(scores(j - 1)):
            s_bufs[buf][p] = s
            mx_bufs[buf][p] = jnp.max(s, axis=0, keepdims=True)

    def spv_from(buf, j):
        kb = jnp.where(j == 0, qb, j - 1)
        k0 = pl.multiple_of(kb * FOX_TK, FOX_TK)
        vts = [(vt_ref[p, 0, :, pl.ds(k0, FOX_TK)], vt_ref[p, 1, :, pl.ds(k0, FOX_TK)]) for p in pairs]
        _softmax_pv([s_bufs[buf][p] for p in pairs], vts,
                    [m_ref.at[p] for p in pairs], [acc_ref.at[p] for p in pairs],
                    maxes=[mx_bufs[buf][p] for p in pairs])

    tri = tri_ref[...]
    for p, s in enumerate(scores(qb)):
        s = s + tri
        s0_ref[p] = s
        mx0_ref[p] = jnp.max(s, axis=0, keepdims=True)
        _flash_init(m_ref.at[p], acc_ref.at[p])
    _pipelined_sweep(qb, qk_to, spv_from)
    for p in range(FOX_NP):
        acc = acc_ref[p]
        den = acc[HEAD_DIM:HEAD_DIM + 1]
        o_a = acc[0:HEAD_DIM, 0:FOX_TQ] / den[:, 0:FOX_TQ]
        o_b = acc[0:HEAD_DIM, FOX_TQ:FOX_N] / den[:, FOX_TQ:FOX_N]
        o_ref[0, :, p * LANES:(p + 1) * LANES] = jnp.concatenate([o_a, o_b], axis=0).T.astype(BF16)


def _fox(proj, cq, augk, tri):
    b, t, _ = proj.shape
    nq = t // FOX_TQ
    width = FOX_HEADS * HEAD_DIM
    qf, kf, vf = COL_QF // width, COL_KF // width, COL_VF // width
    return pl.pallas_call(
        _fox_kernel,
        grid=(b, nq),
        in_specs=[
            pl.BlockSpec((1, FOX_TQ, width), lambda i, k: (i, k, qf)),
            pl.BlockSpec((1, t, width), lambda i, k: (i, 0, kf)),
            pl.BlockSpec((1, t, width), lambda i, k: (i, 0, vf)),
            pl.BlockSpec((1, LANES, FOX_TQ), lambda i, k: (i, 0, k)),
            pl.BlockSpec((1, t, LANES), lambda i, k: (i, 0, 0)),
            pl.BlockSpec(tri.shape, lambda i, k: (0, 0)),
        ],
        out_specs=pl.BlockSpec((1, FOX_TQ, width), lambda i, k: (i, k, 0)),
        out_shape=jax.ShapeDtypeStruct((b, t, width), BF16),
        scratch_shapes=[
            pltpu.VMEM((FOX_NP, 2, FOX_ROWS, t), BF16),
            pltpu.VMEM((FOX_NP, 1, FOX_N), F32),
            pltpu.VMEM((FOX_NP, FOX_ROWS, FOX_N), F32),
            pltpu.VMEM((FOX_NP, FOX_TK, FOX_N), F32),
            pltpu.VMEM((FOX_NP, FOX_TK, FOX_N), F32),
            pltpu.VMEM((FOX_NP, 1, FOX_N), F32),
            pltpu.VMEM((FOX_NP, 1, FOX_N), F32),
        ],
        compiler_params=pltpu.CompilerParams(
            dimension_semantics=("arbitrary", "arbitrary"), vmem_limit_bytes=VMEM_LIMIT),
        name="fox_attention",
    )(proj, proj, proj, cq, augk, tri)


def _post_kernel(x_ref, on_ref, of_ref, mg_ref, wn_ref, wf_ref, wo_ref, gm_ref, wu_ref, wd_ref, gfin_ref, o_ref):
    d = x_ref.shape[1]
    ff = wu_ref.shape[1]
    ga = mg_ref[:, 0:d].astype(F32)
    gb = mg_ref[:, d:2 * d].astype(F32)
    y = ga * _dot(on_ref[...], wn_ref[...]) + gb * _dot(of_ref[...], wf_ref[...])
    h = x_ref[...] + _dot(y.astype(BF16), wo_ref[...])
    r = lax.rsqrt(jnp.mean(h * h, axis=-1, keepdims=True) + NORM_EPS)
    v = (h * r * gm_ref[...]).astype(BF16)
    acc = h
    for c0 in range(0, ff, FF_CHUNK):
        up = jnp.maximum(_dot(v, wu_ref[:, c0:c0 + FF_CHUNK]), 0.0)
        acc = acc + _dot((up * up).astype(BF16), wd_ref[c0:c0 + FF_CHUNK, :])
    r2 = lax.rsqrt(jnp.mean(acc * acc, axis=-1, keepdims=True) + NORM_EPS)
    o_ref[...] = acc * r2 * gfin_ref[...]


def _post(x2, on, of, mg, wn, wf, wo, gm, wu, wd, gfin, tm):
    n, d = x2.shape
    row = lambda w: pl.BlockSpec((tm, w), lambda i: (i, 0))
    const = lambda a: pl.BlockSpec(a.shape, lambda i: (0, 0), pipeline_mode=pl.Buffered(1))
    return pl.pallas_call(
        _post_kernel,
        grid=(n // tm,),
        in_specs=[row(d), row(on.shape[1]), row(of.shape[1]), row(2 * d),
                  const(wn), const(wf), const(wo), const(gm), const(wu), const(wd), const(gfin)],
        out_specs=row(d),
        out_shape=jax.ShapeDtypeStruct((n, d), F32),
        compiler_params=pltpu.CompilerParams(
            dimension_semantics=("arbitrary",), vmem_limit_bytes=VMEM_LIMIT),
        name="post_mlp",
    )(x2, on, of, mg, wn, wf, wo, gm, wu, wd, gfin)


def _position_constants(t):
    def key_side(pos, pad):
        a = np.zeros((pos.size, LANES), np.float32)
        for i in range(3):
            a[:, AUG_POS + 2 * i] = pos // 64
            a[:, AUG_POS + 2 * i + 1] = pos % 64
        a[:, AUG_TQ:AUG_TQ + 3] = 1.0
        a[:, AUG_PAD] = pad
        return a

    pos = np.arange(t)
    augk = key_side(pos, 0.0)
    augk[pos, AUG_MASK + pos // SEL_BLOCK] = 1.0
    augw = np.concatenate([key_side(np.zeros(WINDOW, np.int64), 1.0), key_side(pos, 0.0)], axis=0)
    augc = key_side(np.arange(LANES) * CMP_STRIDE + CMP_BLOCK - 1, 0.0)

    nc = (t - CMP_BLOCK) // CMP_STRIDE + 1
    ns = t // SEL_BLOCK
    n = np.arange(LANES)[None, :]
    j = np.arange(ns)[:, None]
    ovt = ((n * CMP_STRIDE <= j * SEL_BLOCK + SEL_BLOCK - 1) & (n * CMP_STRIDE + CMP_BLOCK - 1 >= j * SEL_BLOCK) & (n < nc))

    def tri(tk, tq, anti=False):
        kk = np.arange(tk)[:, None]
        tt = np.arange(tq)[None, :]
        ok = (kk > tt) if anti else (kk <= tt)
        return np.where(ok, 0.0, NEG_BIG).astype(np.float32)

    edge = tri(NSA_TQ, NSA_TQ)
    trisel = np.zeros((SEL_TK // NSA_TQ, SEL_TK, NSA_TQ), np.float32)
    for v in range(SEL_TK // NSA_TQ):
        trisel[v, v * NSA_TQ:(v + 1) * NSA_TQ] = edge
    triwin = np.stack([edge, tri(NSA_TQ, NSA_TQ, anti=True)])
    trifox = np.tile(tri(FOX_TK, FOX_TQ), (1, 2))
    return (jnp.asarray(augk, BF16), jnp.asarray(augw, BF16), jnp.asarray(augc, BF16), jnp.asarray(ovt, BF16),
            jnp.asarray(trisel), jnp.asarray(triwin), jnp.asarray(trifox))


def kernel(x, norm_mix, w_in, cmp_pos_k, cmp_w1_k, cmp_w2_k, cmp_pos_v, cmp_w1_v, cmp_w2_v, fox_f_bias, w_branch_nsa, w_branch_fox, w_merge_gate, b_merge_gate, w_out, norm_mlp, w_up, w_down, norm_final):
    b, t, d = x.shape
    assert w_in.shape[0] == 1, "one layer: the final norm is fused into the MLP kernel"
    assert t % FOX_TQ == 0 and t // SEL_BLOCK == 32 and (t - CMP_BLOCK) // CMP_STRIDE + 1 == LANES - 1
    augk, augw, augc, ovt, trisel, triwin, trifox = _position_constants(t)
    h = x.reshape(b * t, d)
    proj, gf, mg = _inproj(h, norm_mix[0][None, :], jnp.swapaxes(w_in, 1, 2), w_merge_gate[0].astype(BF16),
                           b_merge_gate[0][None, :], ROW_TILE)
    proj = proj.reshape(b, t, PROJ_W)
    gf = gf.reshape(b, t, LANES)

    def w1_blocks(w1):
        wa_ = w1[:CMP_STRIDE * HEAD_DIM].reshape(CMP_STRIDE, HEAD_DIM, CMP_HIDDEN)
        wb_ = w1[CMP_STRIDE * HEAD_DIM:].reshape(CMP_STRIDE, HEAD_DIM, CMP_HIDDEN)
        z = jnp.zeros_like(wa_)
        top = jnp.concatenate([wa_, z, wb_, z], axis=2)
        bot = jnp.concatenate([z, wa_, z, wb_], axis=2)
        return jnp.concatenate([top, bot], axis=1).astype(BF16)

    def w2_pair(w2):
        z = jnp.zeros_like(w2)
        return (jnp.concatenate([w2, z], axis=1).astype(BF16), jnp.concatenate([z, w2], axis=1).astype(BF16))

    def pos_rows(p):
        row = jnp.concatenate([p[:CMP_STRIDE], p[CMP_STRIDE:]], axis=1)[:, None, :]
        return jnp.concatenate([row, jnp.zeros((CMP_STRIDE, 15, 2 * HEAD_DIM), F32)], axis=1).astype(BF16)

    w2k0, w2k1 = w2_pair(cmp_w2_k[0])
    w2v0, w2v1 = w2_pair(cmp_w2_v[0])
    kcmp, vcmp = _compress(proj, w1_blocks(cmp_w1_k[0]), w2k0, w2k1, pos_rows(cmp_pos_k[0]),
                           w1_blocks(cmp_w1_v[0]), w2v0, w2v1, pos_rows(cmp_pos_v[0]))
    o_n = _nsa(proj, gf, kcmp, vcmp, augk, augw, augc, ovt, trisel, triwin)
    cq, faug = _foxprep(gf, fox_f_bias[0][:, None])
    o_f = _fox(proj, cq, faug, trifox)
    out = _post(h, o_n.reshape(b * t, -1), o_f.reshape(b * t, -1), mg,
                w_branch_nsa[0].astype(BF16), w_branch_fox[0].astype(BF16), w_out[0].astype(BF16),
                norm_mlp[0][None, :], w_up[0].astype(BF16), w_down[0].astype(BF16), norm_final[None, :], ROW_TILE)
    return out.reshape(b, t, d)
```

```python
import numpy as np
import jax
import jax.numpy as jnp
from jax import lax
from jax.experimental import pallas as pl
from jax.experimental.pallas import tpu as pltpu

HEAD_DIM = 64
NSA_HEADS = 8
NSA_GROUPS = 2
NSA_HPG = NSA_HEADS // NSA_GROUPS
FOX_HEADS = 8
CMP_BLOCK = 32
CMP_STRIDE = 16
CMP_HIDDEN = 128
SEL_BLOCK = 64
N_SELECT = 16
WINDOW = 512
NORM_EPS = 1e-6
FORCE_SCORE = 1e9

LANES = 128
NEG_BIG = -(2.0 ** 100)
M_INIT = -3.0e38
LOG2E = 1.4426950408889634
VMEM_LIMIT = 56 * 1024 * 1024
ROW_TILE = 512
ROW_SPLIT = 2
FF_CHUNK = 1024

PROJ_W = 2816
COL_QN = 0
COL_QF = 512
COL_KF = 1024
COL_VF = 1536
COL_KV = 2048
GF_GATE = 0
GF_FOX = 24

F32 = jnp.float32
BF16 = jnp.bfloat16


def _dot(a, b):
    return jnp.dot(a, b, preferred_element_type=F32)


def _iota(shape, dim):
    return lax.broadcasted_iota(jnp.int32, shape, dim)


def _split3(x):
    hi = x.astype(BF16)
    r1 = x - hi.astype(F32)
    mid = r1.astype(BF16)
    lo = (r1 - mid.astype(F32)).astype(BF16)
    return hi, mid, lo


W_QN = NSA_HEADS * HEAD_DIM
W_KV = 6 * NSA_GROUPS * HEAD_DIM
W_GATE = 3 * NSA_HEADS
W_FOX = 3 * FOX_HEADS * HEAD_DIM
W_PIECES = ((0, W_QN), (W_QN + W_KV + W_GATE, W_FOX), (W_QN, W_KV))


def _inproj_kernel(x_ref, g_ref, wt_ref, wm32_ref, bm_ref, proj_ref, gf_ref, mg_ref, wq_ref, wf_ref, wkv_ref, wg_ref, wm_ref):
    w_refs = (wq_ref, wf_ref, wkv_ref)

    @pl.when(pl.program_id(0) == 0)
    def _():
        d = wt_ref.shape[2]
        for (c0, width), dst in zip(W_PIECES, w_refs):
            for k0 in range(0, width, 256):
                dst[:, k0:k0 + 256] = wt_ref[0, c0 + k0:c0 + k0 + 256, :].T.astype(BF16)
        gate0 = W_QN + W_KV
        fox0 = gate0 + W_GATE + W_FOX
        gates = jnp.concatenate([wt_ref[0, gate0:gate0 + W_GATE, :], wt_ref[0, fox0:fox0 + FOX_HEADS, :],
                                 jnp.zeros((LANES - W_GATE - FOX_HEADS, d), F32)], axis=0)
        wg_ref[...] = gates.T.astype(BF16)
        for r0 in range(0, d, 256):
            wm_ref[r0:r0 + 256, :] = wm32_ref[0, r0:r0 + 256, :].astype(BF16)

    sub = x_ref.shape[0] // ROW_SPLIT
    for h in range(ROW_SPLIT):
        rows = slice(h * sub, (h + 1) * sub)
        x = x_ref[rows, :]
        r = lax.rsqrt(jnp.mean(x * x, axis=-1, keepdims=True) + NORM_EPS)
        u = (x * r * g_ref[...]).astype(BF16)
        c0 = 0
        for piece in w_refs:
            for k0 in range(0, piece.shape[1], 768):
                k1 = min(k0 + 768, piece.shape[1])
                proj_ref[rows, c0 + k0:c0 + k1] = _dot(u, piece[:, k0:k1]).astype(BF16)
            c0 += piece.shape[1]
        gf_ref[rows, :] = _dot(u, wg_ref[...])
        for c0 in (0, 1024):
            z = _dot(u, wm_ref[:, c0:c0 + 1024]) + bm_ref[:, c0:c0 + 1024]
            mg_ref[rows, c0:c0 + 1024] = jax.nn.sigmoid(z).astype(BF16)


def _inproj(x2, g, w_in_t, wm, bm, tm):
    n, d = x2.shape
    const = lambda i: (0, 0)
    once = dict(pipeline_mode=pl.Buffered(1))
    return pl.pallas_call(
        _inproj_kernel,
        grid=(n // tm,),
        in_specs=[
            pl.BlockSpec((tm, d), lambda i: (i, 0)),
            pl.BlockSpec((1, d), const),
            pl.BlockSpec(w_in_t.shape, lambda i: (0, 0, 0), **once),
            pl.BlockSpec(wm.shape, lambda i: (0, 0, 0), **once),
            pl.BlockSpec(bm.shape, const),
        ],
        out_specs=[
            pl.BlockSpec((tm, PROJ_W), lambda i: (i, 0)),
            pl.BlockSpec((tm, LANES), lambda i: (i, 0)),
            pl.BlockSpec((tm, 2 * d), lambda i: (i, 0)),
        ],
        out_shape=[
            jax.ShapeDtypeStruct((n, PROJ_W), BF16),
            jax.ShapeDtypeStruct((n, LANES), F32),
            jax.ShapeDtypeStruct((n, 2 * d), BF16),
        ],
        scratch_shapes=[pltpu.VMEM((d, width), BF16) for _, width in W_PIECES]
        + [pltpu.VMEM((d, LANES), BF16), pltpu.VMEM((d, 2 * d), BF16)],
        compiler_params=pltpu.CompilerParams(
            dimension_semantics=("arbitrary",), vmem_limit_bytes=VMEM_LIMIT),
        name="inproj",
    )(x2, g, w_in_t, wm, bm)


def _compress_kernel(kc_ref, vc_ref, wk_ref, w2k0_ref, w2k1_ref, pk_ref,
                     wv_ref, w2v0_ref, w2v1_ref, pv_ref, kco_ref, vco_ref, xs_ref):
    t = kc_ref.shape[1]
    nblk = t // CMP_STRIDE

    def one(x_ref, w_ref, w2_refs, p_ref, o_ref):
        xs_ref[...] = x_ref[0].astype(F32)
        acc = jnp.zeros((nblk + 16, 4 * LANES), F32)
        for j in range(CMP_STRIDE):
            xj = xs_ref[pl.ds(j, nblk, stride=CMP_STRIDE), :].astype(BF16)
            acc = acc + _dot(jnp.concatenate([xj, p_ref[j]], axis=0), w_ref[j])
        posb = acc[nblk:nblk + 1, 0:LANES] + acc[nblk:nblk + 1, 3 * LANES:4 * LANES]
        out = jnp.zeros((nblk, LANES), F32)
        for g in range(NSA_GROUPS):
            a = acc[0:nblk, g * LANES:(g + 1) * LANES]
            b = acc[0:nblk, (2 + g) * LANES:(3 + g) * LANES]
            h = a + pltpu.roll(b, nblk - 1, 0) + posb
            out = out + _dot(jax.nn.gelu(h).astype(BF16), w2_refs[g][...])
        o_ref[0] = out.astype(BF16)

    one(kc_ref, wk_ref, (w2k0_ref, w2k1_ref), pk_ref, kco_ref)
    one(vc_ref, wv_ref, (w2v0_ref, w2v1_ref), pv_ref, vco_ref)


def _compress(proj, wk, w2k0, w2k1, pk, wv, w2v0, w2v1, pv):
    b, t, _ = proj.shape
    kvb = COL_KV // LANES
    whole = lambda a: pl.BlockSpec(a.shape, lambda i: (0,) * a.ndim)
    ospec = pl.BlockSpec((1, LANES, LANES), lambda i: (i, 0, 0))
    return pl.pallas_call(
        _compress_kernel,
        grid=(b,),
        in_specs=[pl.BlockSpec((1, t, LANES), lambda i: (i, 0, kvb)),
                  pl.BlockSpec((1, t, LANES), lambda i: (i, 0, kvb + 1)),
                  whole(wk), whole(w2k0), whole(w2k1), whole(pk), whole(wv), whole(w2v0), whole(w2v1), whole(pv)],
        out_specs=[ospec, ospec],
        out_shape=[jax.ShapeDtypeStruct((b, LANES, LANES), BF16)] * 2,
        scratch_shapes=[pltpu.VMEM((t, LANES), F32)],
        compiler_params=pltpu.CompilerParams(dimension_semantics=("arbitrary",)),
        name="compress",
    )(proj, proj, wk, w2k0, w2k1, pk, wv, w2v0, w2v1, pv)


def _flash_init(m_ref, acc_ref):
    m_ref[...] = jnp.full(m_ref.shape, M_INIT, F32)
    acc_ref[...] = jnp.zeros(acc_ref.shape, F32)


def _softmax_pv(scores, vt_blks, m_refs=None, acc_refs=None, maxes=None):
    def pv(vt, p):
        if not isinstance(vt, tuple):
            return _dot(vt, p)
        w = p.shape[1] // len(vt)
        return jnp.concatenate([_dot(v, p[:, j * w:(j + 1) * w]) for j, v in enumerate(vt)], axis=1)

    ps, alphas = [], []
    for i, s in enumerate(scores):
        mx = jnp.max(s, axis=0, keepdims=True) if maxes is None else maxes[i]
        if m_refs is None:
            ps.append(jnp.exp2(s - mx).astype(BF16))
            continue
        m_old = m_refs[i][...]
        m_new = jnp.maximum(m_old, mx)
        ps.append(jnp.exp2(s - m_new).astype(BF16))
        alphas.append(jnp.exp2(m_old - m_new))
        m_refs[i][...] = m_new
    if m_refs is None:
        return [pv(vt, p) for vt, p in zip(vt_blks, ps)]
    for vt, p, alpha, acc_ref in zip(vt_blks, ps, alphas, acc_refs):
        acc_ref[...] = acc_ref[...] * alpha + pv(vt, p)
    return None


def _pipelined_sweep(n_rest, qk_to, spv_from):
    def body(jj, carry):
        j = 2 * jj
        qk_to(1, j + 1)
        spv_from(0, j)
        qk_to(0, j + 2)
        spv_from(1, j + 1)
        return carry

    lax.fori_loop(0, n_rest // 2, body, 0)

    @pl.when(n_rest % 2 == 1)
    def _():
        qk_to(1, n_rest)
        spv_from(0, n_rest - 1)
        spv_from(1, n_rest)

    @pl.when(n_rest % 2 == 0)
    def _():
        spv_from(0, n_rest)


def _ones_rows(tk):
    return jnp.where(_iota((16, tk), 0) == 0, 1.0, 0.0).astype(BF16)


NSA_TQ = 128
NSA_N = NSA_HPG * NSA_TQ
SEL_TK = 512
WIN_KEYS = WINDOW + NSA_TQ

AUG_POS = 0
AUG_TQ = 6
AUG_PAD = 9
AUG_MASK = 32


def _bf16_terms(x, n=3):
    out, r = [], np.float32(x)
    for _ in range(n):
        term = np.asarray(r, np.float32).astype(BF16).astype(np.float32)
        out.append(float(term))
        r = np.float32(r - term)
    return out


def _nsa_queries(q_ref, blk):
    t0 = blk * NSA_TQ
    qt = q_ref[0].astype(F32).T * (HEAD_DIM ** -0.5 * LOG2E)
    lane = _iota((1, NSA_N), 1)
    hl = lane // NSA_TQ
    tqf = (t0 + lane % NSA_TQ).astype(F32)
    ridx = _iota((16, NSA_N), 0)
    zeros_h = jnp.zeros((HEAD_DIM, NSA_TQ), F32)
    l2e = _bf16_terms(LOG2E)
    q_top, aug16 = [], []
    for g in range(NSA_GROUPS):
        tops = []
        for h in range(NSA_HPG):
            head = NSA_HPG * g + h
            qh = qt[head * HEAD_DIM:(head + 1) * HEAD_DIM]
            tops.append(jnp.concatenate([qh, zeros_h] if g == 0 else [zeros_h, qh], axis=0))
        q_top.append(jnp.concatenate(tops, axis=1))
        slope = lax.bitcast_convert_type((127 - (NSA_HPG * g + hl + 1)) << 23, F32)
        rows = {}
        for i, term in enumerate(l2e):
            rows[AUG_POS + 2 * i] = slope * (64.0 * term)
            rows[AUG_POS + 2 * i + 1] = slope * term
        for i, term in enumerate(_split3(-(slope * tqf) * LOG2E)):
            rows[AUG_TQ + i] = term.astype(F32)
        a = jnp.zeros((16, NSA_N), F32)
        for r, val in rows.items():
            a = jnp.where(ridx == r, val, a)
        aug16.append(a)
    return q_top, aug16


def _nsa_compressed(s_cmp, blk, vct_ref, ovt, gs):
    tq = blk * NSA_TQ + _iota((1, NSA_N), 1) % NSA_TQ
    n_idx = _iota((LANES, NSA_N), 0)
    valid = (n_idx * CMP_STRIDE + (CMP_BLOCK - 1) <= tq) & (n_idx < LANES - 1)
    o_cmp, imps = [], []
    for g in gs:
        sm = jnp.where(valid, s_cmp[g], NEG_BIG)
        mx = jnp.max(sm, axis=0, keepdims=True)
        e = jnp.where(valid, jnp.exp2(sm - mx), 0.0)
        den = jnp.sum(e, axis=0, keepdims=True)
        pn = e / jnp.where(den > 0.0, den, 1.0)
        o_cmp.append(_dot(vct_ref[g * HEAD_DIM:(g + 1) * HEAD_DIM, :], pn.astype(BF16)))

        psum = pn[:, 0:NSA_TQ]
        for h in range(1, NSA_HPG):
            psum = psum + pn[:, h * NSA_TQ:(h + 1) * NSA_TQ]
        hi, mid, lo = _split3(psum)
        imps.append(_dot(ovt, hi) + _dot(ovt, mid) + _dot(ovt, lo))
    return o_cmp, imps


def _nsa_select(imps, blk, q_top, aug16, gs):
    t0 = blk * NSA_TQ
    ns = imps[0].shape[0]
    tcol = t0 + _iota((8, NSA_TQ), 1)
    q_sel = []
    for g, imp in zip(gs, imps):
        tiles = range(ns // 8)
        j_idx = [8 * r + _iota((8, NSA_TQ), 0) for r in tiles]
        in_past = [j_idx[r] * SEL_BLOCK <= tcol for r in tiles]
        cur = tcol // SEL_BLOCK
        score = []
        for r in tiles:
            forced = (j_idx[r] == 0) | (j_idx[r] == cur) | (j_idx[r] == cur - 1)
            score.append(jnp.where(forced, FORCE_SCORE, jnp.where(in_past[r], imp[8 * r:8 * r + 8], -1.0)))
        rank = [jnp.zeros((8, NSA_TQ), F32) for _ in tiles]
        for i in range(ns):
            si = score[i // 8][i % 8:i % 8 + 1, :]
            for r in tiles:
                if 8 * r > i:
                    beats = si >= score[r]
                elif 8 * r + 7 <= i:
                    beats = si > score[r]
                else:
                    beats = (si > score[r]) | ((si == score[r]) & (j_idx[r] > i))
                rank[r] = rank[r] + jnp.where(beats, 1.0, 0.0)
        maskval = jnp.concatenate([jnp.where((rank[r] < min(N_SELECT, ns)) & in_past[r], 0.0, NEG_BIG) for r in tiles], axis=0)
        mask4 = jnp.concatenate([maskval] * NSA_HPG, axis=1)
        q_sel.append(jnp.concatenate([q_top[g], aug16[g], jnp.zeros((AUG_MASK - 16, NSA_N), F32), mask4,
                                      jnp.zeros((LANES - AUG_MASK - ns, NSA_N), F32)], axis=0).astype(BF16))
    return q_sel


def _nsa_kernel(q_ref, gf_ref, kc_ref, vc_ref, ks_ref, vs_ref, kw_ref, vw_ref,
                augk_ref, augw_ref, augc_ref, ovt_ref, trisel_ref, triwin_ref, o_ref,
                vst_ref, vwt_ref, kwp_ref, vct_ref, gt_ref, m_ref, acc_ref, s0_ref, s1_ref, mx0_ref, mx1_ref):
    qb = pl.program_id(1)
    t = ks_ref.shape[1]
    t0 = qb * NSA_TQ
    groups = range(NSA_GROUPS)
    zeros_tail = jnp.zeros((LANES - 16, NSA_N), F32)

    def cmp_scores(q_top, aug16):
        kcb = jnp.concatenate([kc_ref[0], augc_ref[...]], axis=1)
        return [_dot(kcb, jnp.concatenate([q_top[g], aug16[g], zeros_tail], axis=0).astype(BF16)) for g in groups]

    @pl.when(qb == 0)
    def _():
        vwt_ref[:, 0:WINDOW] = jnp.zeros((LANES, WINDOW), BF16)
        kwp_ref[0:WINDOW, :] = jnp.zeros((WINDOW, LANES), BF16)
        kwp_ref[WINDOW:WINDOW + t, :] = kw_ref[0]
        for c in range(t // LANES):
            sl = slice(c * LANES, (c + 1) * LANES)
            vst_ref[:, sl] = vs_ref[0, sl, :].astype(F32).T.astype(BF16)
            vwt_ref[:, WINDOW + c * LANES:WINDOW + (c + 1) * LANES] = vw_ref[0, sl, :].astype(F32).T.astype(BF16)
        vct_ref[...] = vc_ref[0].astype(F32).T.astype(BF16)

    q_top, aug16 = _nsa_queries(q_ref, qb)
    s_cmp = cmp_scores(q_top, aug16)
    kw0 = pl.multiple_of(t0, NSA_TQ)
    kwin = jnp.concatenate([kwp_ref[pl.ds(kw0, WIN_KEYS), :], augw_ref[pl.ds(kw0, WIN_KEYS), :]], axis=1)
    pad_row = _iota((16, NSA_N), 0) == AUG_PAD
    s_win = [_dot(kwin, jnp.concatenate([q_top[g], jnp.where(pad_row, NEG_BIG, aug16[g]), zeros_tail], axis=0).astype(BF16))
             for g in groups]
    o_cmp, imps = _nsa_compressed(s_cmp, qb, vct_ref, ovt_ref[...], groups)
    q_sel = _nsa_select(imps, qb, q_top, aug16, groups)

    hi_edge = jnp.concatenate([triwin_ref[0]] * NSA_HPG, axis=1)
    lo_edge = jnp.concatenate([triwin_ref[1]] * NSA_HPG, axis=1)
    ones_w = _ones_rows(WIN_KEYS)
    s_win = [jnp.concatenate([s[0:NSA_TQ] + lo_edge, s[NSA_TQ:WINDOW], s[WINDOW:WIN_KEYS] + hi_edge], axis=0)
             for s in s_win]
    vt_win = [jnp.concatenate([vwt_ref[g * HEAD_DIM:(g + 1) * HEAD_DIM, pl.ds(kw0, WIN_KEYS)], ones_w], axis=0)
              for g in groups]
    o_win = [acc[0:HEAD_DIM] / acc[HEAD_DIM:HEAD_DIM + 1] for acc in _softmax_pv(s_win, vt_win)]

    ones_s = _ones_rows(SEL_TK)
    per_chunk = SEL_TK // NSA_TQ
    n_full = qb // per_chunk
    s_bufs = (s0_ref, s1_ref)
    mx_bufs = (mx0_ref, mx1_ref)

    def sel_scores(cb):
        k0 = pl.multiple_of(cb * SEL_TK, SEL_TK)
        kblk = jnp.concatenate([ks_ref[0, pl.ds(k0, SEL_TK), :], augk_ref[pl.ds(k0, SEL_TK), :]], axis=1)
        return [_dot(kblk, q_sel[g]) for g in groups]

    def qk_to(buf, j):
        for g, s in enumerate(sel_scores(j - 1)):
            s_bufs[buf][g] = s
            mx_bufs[buf][g] = jnp.max(s, axis=0, keepdims=True)

    def spv_from(buf, j):
        cb = jnp.where(j == 0, n_full, j - 1)
        k0 = pl.multiple_of(cb * SEL_TK, SEL_TK)
        vts = [jnp.concatenate([vst_ref[g * HEAD_DIM:(g + 1) * HEAD_DIM, pl.ds(k0, SEL_TK)], ones_s], axis=0)
               for g in groups]
        _softmax_pv([s_bufs[buf][g] for g in groups], vts,
                    [m_ref.at[g] for g in groups], [acc_ref.at[g] for g in groups],
                    maxes=[mx_bufs[buf][g] for g in groups])

    diag = trisel_ref[qb % per_chunk]
    diag4 = jnp.concatenate([diag] * NSA_HPG, axis=1)
    for g, s in enumerate(sel_scores(n_full)):
        s = s + diag4
        s0_ref[g] = s
        mx0_ref[g] = jnp.max(s, axis=0, keepdims=True)
        _flash_init(m_ref.at[g], acc_ref.at[g])
    _pipelined_sweep(n_full, qk_to, spv_from)

    gt_ref[...] = gf_ref[0].T
    sig = jax.nn.sigmoid(gt_ref[GF_GATE:GF_GATE + 3 * NSA_HEADS, :])
    outs = []
    for g in groups:
        acc = acc_ref[g]
        o_sel = acc[0:HEAD_DIM] / acc[HEAD_DIM:HEAD_DIM + 1]
        for h in range(NSA_HPG):
            sl = slice(h * NSA_TQ, (h + 1) * NSA_TQ)
            head = NSA_HPG * g + h
            g0 = sig[head:head + 1]
            g1 = sig[NSA_HEADS + head:NSA_HEADS + head + 1]
            g2 = sig[2 * NSA_HEADS + head:2 * NSA_HEADS + head + 1]
            outs.append(g0 * o_cmp[g][:, sl] + g1 * o_sel[:, sl] + g2 * o_win[g][:, sl])
    o_ref[0] = jnp.concatenate(outs, axis=0).T.astype(BF16)


def _nsa(proj, gf, kcmp, vcmp, augk, augw, augc, ovt, trisel, triwin):
    b, t, _ = proj.shape
    nq = t // NSA_TQ
    kvb = COL_KV // LANES
    width = NSA_HEADS * HEAD_DIM
    whole = lambda a: pl.BlockSpec(a.shape, lambda i, k: (0,) * a.ndim)
    kv = lambda idx: pl.BlockSpec((1, t, LANES), lambda i, k: (i, 0, idx))
    return pl.pallas_call(
        _nsa_kernel,
        grid=(b, nq),
        in_specs=[
            pl.BlockSpec((1, NSA_TQ, width), lambda i, k: (i, k, COL_QN // width)),
            pl.BlockSpec((1, NSA_TQ, LANES), lambda i, k: (i, k, 0)),
            pl.BlockSpec((1, LANES, LANES), lambda i, k: (i, 0, 0)),
            pl.BlockSpec((1, LANES, LANES), lambda i, k: (i, 0, 0)),
            kv(kvb + 2), kv(kvb + 3), kv(kvb + 4), kv(kvb + 5),
            whole(augk), whole(augw), whole(augc), whole(ovt), whole(trisel), whole(triwin),
        ],
        out_specs=pl.BlockSpec((1, NSA_TQ, width), lambda i, k: (i, k, 0)),
        out_shape=jax.ShapeDtypeStruct((b, t, width), BF16),
        scratch_shapes=[
            pltpu.VMEM((LANES, t), BF16),
            pltpu.VMEM((LANES, t + WINDOW), BF16),
            pltpu.VMEM((t + WINDOW, LANES), BF16),
            pltpu.VMEM((LANES, LANES), BF16),
            pltpu.VMEM((LANES, NSA_TQ), F32),
            pltpu.VMEM((NSA_GROUPS, 1, NSA_N), F32),
            pltpu.VMEM((NSA_GROUPS, HEAD_DIM + 16, NSA_N), F32),
            pltpu.VMEM((NSA_GROUPS, SEL_TK, NSA_N), F32),
            pltpu.VMEM((NSA_GROUPS, SEL_TK, NSA_N), F32),
            pltpu.VMEM((NSA_GROUPS, 1, NSA_N), F32),
            pltpu.VMEM((NSA_GROUPS, 1, NSA_N), F32),
        ],
        compiler_params=pltpu.CompilerParams(
            dimension_semantics=("arbitrary", "arbitrary"), vmem_limit_bytes=VMEM_LIMIT),
        name="nsa_attention",
    )(proj, gf, kcmp, vcmp, proj, proj, proj, proj, augk, augw, augc, ovt, trisel, triwin)


FOX_CBLK = 256


def _foxprep_kernel(gf_ref, bias_ref, cq_ref, augk_ref):
    t = gf_ref.shape[1]
    cols = []
    for c in range(t // LANES):
        cols.append(gf_ref[0, c * LANES:(c + 1) * LANES, :].T[GF_FOX:GF_FOX + FOX_HEADS])
    f = jnp.concatenate(cols, axis=1) + bias_ref[...]
    ls = jnp.minimum(f, 0.0) - jnp.log(1.0 + jnp.exp(-jnp.abs(f)))
    upper = (_iota((FOX_CBLK, FOX_CBLK), 0) <= _iota((FOX_CBLK, FOX_CBLK), 1)).astype(BF16)
    carry = jnp.zeros((FOX_HEADS, 1), F32)
    parts = []
    for c in range(t // FOX_CBLK):
        hi, mid, lo = _split3(ls[:, c * FOX_CBLK:(c + 1) * FOX_CBLK])
        cb = _dot(hi, upper) + _dot(mid, upper) + _dot(lo, upper) + carry
        carry = cb[:, FOX_CBLK - 1:FOX_CBLK]
        parts.append(cb)
    csum = jnp.concatenate(parts, axis=1) * LOG2E
    c1, c2, c3 = (v.astype(F32) for v in _split3(csum))
    ones = jnp.ones((3 * FOX_HEADS, t), F32)
    zeros = jnp.zeros((LANES - 6 * FOX_HEADS, t), F32)
    cq_ref[0] = jnp.concatenate([c1, c2, c3, ones, zeros], axis=0).astype(BF16)
    slab_k = jnp.concatenate([ones, -c1, -c2, -c3, zeros], axis=0)
    for c in range(t // LANES):
        sl = slice(c * LANES, (c + 1) * LANES)
        augk_ref[0, sl, :] = slab_k[:, sl].T.astype(BF16)


def _foxprep(gf, bias):
    b, t, _ = gf.shape
    return pl.pallas_call(
        _foxprep_kernel,
        grid=(b,),
        in_specs=[pl.BlockSpec((1, t, LANES), lambda i: (i, 0, 0)),
                  pl.BlockSpec(bias.shape, lambda i: (0, 0))],
        out_specs=[pl.BlockSpec((1, LANES, t), lambda i: (i, 0, 0)),
                   pl.BlockSpec((1, t, LANES), lambda i: (i, 0, 0))],
        out_shape=[jax.ShapeDtypeStruct((b, LANES, t), BF16),
                   jax.ShapeDtypeStruct((b, t, LANES), BF16)],
        compiler_params=pltpu.CompilerParams(dimension_semantics=("arbitrary",)),
        name="fox_prep",
    )(gf, bias)


FOX_TQ = 256
FOX_TK = 256
FOX_NP = FOX_HEADS // 2
FOX_N = 2 * FOX_TQ
FOX_ROWS = HEAD_DIM + 16


def _fox_kernel(q_ref, k_ref, v_ref, cq_ref, augk_ref, tri_ref, o_ref, vt_ref, m_ref, acc_ref, s0_ref, s1_ref,
                mx0_ref, mx1_ref):
    qb = pl.program_id(1)
    t = k_ref.shape[1]

    @pl.when(qb == 0)
    def _():
        for p in range(FOX_NP):
            for c in range(t // LANES):
                sl = slice(c * LANES, (c + 1) * LANES)
                vt = v_ref[0, sl, p * LANES:(p + 1) * LANES].astype(F32).T.astype(BF16)
                for hh in range(2):
                    vt_ref[p, hh, 0:HEAD_DIM, sl] = vt[hh * HEAD_DIM:(hh + 1) * HEAD_DIM]
            for hh in range(2):
                vt_ref[p, hh, HEAD_DIM:FOX_ROWS, :] = _ones_rows(t)

    row = _iota((LANES, FOX_TQ), 0)
    hrow = row % FOX_HEADS
    cq = cq_ref[0].astype(F32)
    qs = []
    for p in range(FOX_NP):
        qt = q_ref[0, :, p * LANES:(p + 1) * LANES].astype(F32).T * (HEAD_DIM ** -0.5 * LOG2E)
        q_top = jnp.concatenate([jnp.where(row < HEAD_DIM, qt, 0.0), jnp.where(row >= HEAD_DIM, qt, 0.0)], axis=1)
        q_aug = jnp.concatenate([jnp.where(hrow == 2 * p, cq, 0.0), jnp.where(hrow == 2 * p + 1, cq, 0.0)], axis=1)
        qs.append(jnp.concatenate([q_top, q_aug], axis=0).astype(BF16))

    pairs = range(FOX_NP)
    s_bufs = (s0_ref, s1_ref)
    mx_bufs = (mx0_ref, mx1_ref)

    def scores(kb):
        k0 = pl.multiple_of(kb * FOX_TK, FOX_TK)
        ak = augk_ref[0, pl.ds(k0, FOX_TK), :]
        return [_dot(jnp.concatenate([k_ref[0, pl.ds(k0, FOX_TK), p * LANES:(p + 1) * LANES], ak], axis=1), qs[p])
                for p in pairs]

    def qk_to(buf, j):
        for p, s in enumerate(scores(j - 1)):
            s_bufs[buf][p] = s
            mx_bufs[buf][p] = jnp.max(s, axis=0, keepdims=True)

    def spv_from(buf, j):
        kb = jnp.where(j == 0, qb, j - 1)
        k0 = pl.multiple_of(kb * FOX_TK, FOX_TK)
        vts = [(vt_ref[p, 0, :, pl.ds(k0, FOX_TK)], vt_ref[p, 1, :, pl.ds(k0, FOX_TK)]) for p in pairs]
        _softmax_pv([s_bufs[buf][p] for p in pairs], vts,
                    [m_ref.at[p] for p in pairs], [acc_ref.at[p] for p in pairs],
                    maxes=[mx_bufs[buf][p] for p in pairs])

    tri = tri_ref[...]
    for p, s in enumerate(scores(qb)):
        s = s + tri
        s0_ref[p] = s
        mx0_ref[p] = jnp.max(s, axis=0, keepdims=True)
        _flash_init(m_ref.at[p], acc_ref.at[p])
    _pipelined_sweep(qb, qk_to, spv_from)
    for p in range(FOX_NP):
        acc = acc_ref[p]
        den = acc[HEAD_DIM:HEAD_DIM + 1]
        o_a = acc[0:HEAD_DIM, 0:FOX_TQ] / den[:, 0:FOX_TQ]
        o_b = acc[0:HEAD_DIM, FOX_TQ:FOX_N] / den[:, FOX_TQ:FOX_N]
        o_ref[0, :, p * LANES:(p + 1) * LANES] = jnp.concatenate([o_a, o_b], axis=0).T.astype(BF16)


def _fox(proj, cq, augk, tri):
    b, t, _ = proj.shape
    nq = t // FOX_TQ
    width = FOX_HEADS * HEAD_DIM
    qf, kf, vf = COL_QF // width, COL_KF // width, COL_VF // width
    return pl.pallas_call(
        _fox_kernel,
        grid=(b, nq),
        in_specs=[
            pl.BlockSpec((1, FOX_TQ, width), lambda i, k: (i, k, qf)),
            pl.BlockSpec((1, t, width), lambda i, k: (i, 0, kf)),
            pl.BlockSpec((1, t, width), lambda i, k: (i, 0, vf)),
            pl.BlockSpec((1, LANES, FOX_TQ), lambda i, k: (i, 0, k)),
            pl.BlockSpec((1, t, LANES), lambda i, k: (i, 0, 0)),
            pl.BlockSpec(tri.shape, lambda i, k: (0, 0)),
        ],
        out_specs=pl.BlockSpec((1, FOX_TQ, width), lambda i, k: (i, k, 0)),
        out_shape=jax.ShapeDtypeStruct((b, t, width), BF16),
        scratch_shapes=[
            pltpu.VMEM((FOX_NP, 2, FOX_ROWS, t), BF16),
            pltpu.VMEM((FOX_NP, 1, FOX_N), F32),
            pltpu.VMEM((FOX_NP, FOX_ROWS, FOX_N), F32),
            pltpu.VMEM((FOX_NP, FOX_TK, FOX_N), F32),
            pltpu.VMEM((FOX_NP, FOX_TK, FOX_N), F32),
            pltpu.VMEM((FOX_NP, 1, FOX_N), F32),
            pltpu.VMEM((FOX_NP, 1, FOX_N), F32),
        ],
        compiler_params=pltpu.CompilerParams(
            dimension_semantics=("arbitrary", "arbitrary"), vmem_limit_bytes=VMEM_LIMIT),
        name="fox_attention",
    )(proj, proj, proj, cq, augk, tri)


def _post_kernel(x_ref, on_ref, of_ref, mg_ref, wn_ref, wf_ref, wo_ref, gm_ref, wu_ref, wd_ref, gfin_ref, o_ref):
    d = x_ref.shape[1]
    ff = wu_ref.shape[1]
    ga = mg_ref[:, 0:d].astype(F32)
    gb = mg_ref[:, d:2 * d].astype(F32)
    y = ga * _dot(on_ref[...], wn_ref[...]) + gb * _dot(of_ref[...], wf_ref[...])
    h = x_ref[...] + _dot(y.astype(BF16), wo_ref[...])
    r = lax.rsqrt(jnp.mean(h * h, axis=-1, keepdims=True) + NORM_EPS)
    v = (h * r * gm_ref[...]).astype(BF16)
    acc = h
    for c0 in range(0, ff, FF_CHUNK):
        up = jnp.maximum(_dot(v, wu_ref[:, c0:c0 + FF_CHUNK]), 0.0)
        acc = acc + _dot((up * up).astype(BF16), wd_ref[c0:c0 + FF_CHUNK, :])
    r2 = lax.rsqrt(jnp.mean(acc * acc, axis=-1, keepdims=True) + NORM_EPS)
    o_ref[...] = acc * r2 * gfin_ref[...]


def _post(x2, on, of, mg, wn, wf, wo, gm, wu, wd, gfin, tm):
    n, d = x2.shape
    row = lambda w: pl.BlockSpec((tm, w), lambda i: (i, 0))
    const = lambda a: pl.BlockSpec(a.shape, lambda i: (0, 0), pipeline_mode=pl.Buffered(1))
    return pl.pallas_call(
        _post_kernel,
        grid=(n // tm,),
        in_specs=[row(d), row(on.shape[1]), row(of.shape[1]), row(2 * d),
                  const(wn), const(wf), const(wo), const(gm), const(wu), const(wd), const(gfin)],
        out_specs=row(d),
        out_shape=jax.ShapeDtypeStruct((n, d), F32),
        compiler_params=pltpu.CompilerParams(
            dimension_semantics=("arbitrary",), vmem_limit_bytes=VMEM_LIMIT),
        name="post_mlp",
    )(x2, on, of, mg, wn, wf, wo, gm, wu, wd, gfin)


def _position_constants(t):
    def key_side(pos, pad):
        a = np.zeros((pos.size, LANES), np.float32)
        for i in range(3):
            a[:, AUG_POS + 2 * i] = pos // 64
            a[:, AUG_POS + 2 * i + 1] = pos % 64
        a[:, AUG_TQ:AUG_TQ + 3] = 1.0
        a[:, AUG_PAD] = pad
        return a

    pos = np.arange(t)
    augk = key_side(pos, 0.0)
    augk[pos, AUG_MASK + pos // SEL_BLOCK] = 1.0
    augw = np.concatenate([key_side(np.zeros(WINDOW, np.int64), 1.0), key_side(pos, 0.0)], axis=0)
    augc = key_side(np.arange(LANES) * CMP_STRIDE + CMP_BLOCK - 1, 0.0)

    nc = (t - CMP_BLOCK) // CMP_STRIDE + 1
    ns = t // SEL_BLOCK
    n = np.arange(LANES)[None, :]
    j = np.arange(ns)[:, None]
    ovt = ((n * CMP_STRIDE <= j * SEL_BLOCK + SEL_BLOCK - 1) & (n * CMP_STRIDE + CMP_BLOCK - 1 >= j * SEL_BLOCK) & (n < nc))

    def tri(tk, tq, anti=False):
        kk = np.arange(tk)[:, None]
        tt = np.arange(tq)[None, :]
        ok = (kk > tt) if anti else (kk <= tt)
        return np.where(ok, 0.0, NEG_BIG).astype(np.float32)

    edge = tri(NSA_TQ, NSA_TQ)
    trisel = np.zeros((SEL_TK // NSA_TQ, SEL_TK, NSA_TQ), np.float32)
    for v in range(SEL_TK // NSA_TQ):
        trisel[v, v * NSA_TQ:(v + 1) * NSA_TQ] = edge
    triwin = np.stack([edge, tri(NSA_TQ, NSA_TQ, anti=True)])
    trifox = np.tile(tri(FOX_TK, FOX_TQ), (1, 2))
    return (jnp.asarray(augk, BF16), jnp.asarray(augw, BF16), jnp.asarray(augc, BF16), jnp.asarray(ovt, BF16),
            jnp.asarray(trisel), jnp.asarray(triwin), jnp.asarray(trifox))


def kernel(x, norm_mix, w_in, cmp_pos_k, cmp_w1_k, cmp_w2_k, cmp_pos_v, cmp_w1_v, cmp_w2_v, fox_f_bias, w_branch_nsa, w_branch_fox, w_merge_gate, b_merge_gate, w_out, norm_mlp, w_up, w_down, norm_final):
    b, t, d = x.shape
    assert w_in.shape[0] == 1, "one layer: the final norm is fused into the MLP kernel"
    assert t % FOX_TQ == 0 and t // SEL_BLOCK == 32 and (t - CMP_BLOCK) // CMP_STRIDE + 1 == LANES - 1
    augk, augw, augc, ovt, trisel, triwin, trifox = _position_constants(t)
    h = x.reshape(b * t, d)
    proj, gf, mg = _inproj(h, norm_mix[0][None, :], jnp.swapaxes(w_in, 1, 2), w_merge_gate,
                           b_merge_gate[0][None, :], ROW_TILE)
    proj = proj.reshape(b, t, PROJ_W)
    gf = gf.reshape(b, t, LANES)

    def w1_blocks(w1):
        wa_ = w1[:CMP_STRIDE * HEAD_DIM].reshape(CMP_STRIDE, HEAD_DIM, CMP_HIDDEN)
        wb_ = w1[CMP_STRIDE * HEAD_DIM:].reshape(CMP_STRIDE, HEAD_DIM, CMP_HIDDEN)
        z = jnp.zeros_like(wa_)
        top = jnp.concatenate([wa_, z, wb_, z], axis=2)
        bot = jnp.concatenate([z, wa_, z, wb_], axis=2)
        return jnp.concatenate([top, bot], axis=1).astype(BF16)

    def w2_pair(w2):
        z = jnp.zeros_like(w2)
        return (jnp.concatenate([w2, z], axis=1).astype(BF16), jnp.concatenate([z, w2], axis=1).astype(BF16))

    def pos_rows(p):
        row = jnp.concatenate([p[:CMP_STRIDE], p[CMP_STRIDE:]], axis=1)[:, None, :]
        return jnp.concatenate([row, jnp.zeros((CMP_STRIDE, 15, 2 * HEAD_DIM), F32)], axis=1).astype(BF16)

    w2k0, w2k1 = w2_pair(cmp_w2_k[0])
    w2v0, w2v1 = w2_pair(cmp_w2_v[0])
    kcmp, vcmp = _compress(proj, w1_blocks(cmp_w1_k[0]), w2k0, w2k1, pos_rows(cmp_pos_k[0]),
                           w1_blocks(cmp_w1_v[0]), w2v0, w2v1, pos_rows(cmp_pos_v[0]))
    o_n = _nsa(proj, gf, kcmp, vcmp, augk, augw, augc, ovt, trisel, triwin)
    cq, faug = _foxprep(gf, fox_f_bias[0][:, None])
    o_f = _fox(proj, cq, faug, trifox)
    out = _post(h, o_n.reshape(b * t, -1), o_f.reshape(b * t, -1), mg,
                w_branch_nsa[0].astype(BF16), w_branch_fox[0].astype(BF16), w_out[0].astype(BF16),
                norm_mlp[0][None, :], w_up[0].astype(BF16), w_down[0].astype(BF16), norm_final[None, :], ROW_TILE)
    return out.reshape(b, t, d)
```

```python
import numpy as np
import jax
import jax.numpy as jnp
from jax import lax
from jax.experimental import pallas as pl
from jax.experimental.pallas import tpu as pltpu

HEAD_DIM = 64
NSA_HEADS = 8
NSA_GROUPS = 2
NSA_HPG = NSA_HEADS // NSA_GROUPS
FOX_HEADS = 8
CMP_BLOCK = 32
CMP_STRIDE = 16
CMP_HIDDEN = 128
SEL_BLOCK = 64
N_SELECT = 16
WINDOW = 512
NORM_EPS = 1e-6
FORCE_SCORE = 1e9

LANES = 128
NEG_BIG = -(2.0 ** 100)
M_INIT = -3.0e38
LOG2E = 1.4426950408889634
VMEM_LIMIT = 56 * 1024 * 1024
ROW_TILE = 512
ROW_SPLIT = 2
FF_CHUNK = 1024

PROJ_W = 2816
COL_QN = 0
COL_QF = 512
COL_KF = 1024
COL_VF = 1536
COL_KV = 2048
GF_GATE = 0
GF_FOX = 24

F32 = jnp.float32
BF16 = jnp.bfloat16


def _dot(a, b):
    return jnp.dot(a, b, preferred_element_type=F32)


def _iota(shape, dim):
    return lax.broadcasted_iota(jnp.int32, shape, dim)


def _split3(x):
    hi = x.astype(BF16)
    r1 = x - hi.astype(F32)
    mid = r1.astype(BF16)
    lo = (r1 - mid.astype(F32)).astype(BF16)
    return hi, mid, lo


W_QN = NSA_HEADS * HEAD_DIM
W_KV = 6 * NSA_GROUPS * HEAD_DIM
W_GATE = 3 * NSA_HEADS
W_FOX = 3 * FOX_HEADS * HEAD_DIM
W_PIECES = ((0, W_QN), (W_QN + W_KV + W_GATE, W_FOX), (W_QN, W_KV))


def _inproj_kernel(x_ref, g_ref, wt_ref, wm32_ref, bm_ref, *refs):
    n_later = (len(refs) - 8) // 2
    later_in, (proj_ref, gf_ref, mg_ref) = refs[:n_later], refs[n_later:n_later + 3]
    later_out = refs[n_later + 3:2 * n_later + 3]
    wq_ref, wf_ref, wkv_ref, wg_ref, wm_ref = refs[2 * n_later + 3:]
    w_refs = (wq_ref, wf_ref, wkv_ref)

    for src, dst in zip(later_in, later_out):
        dst[...] = src[0].astype(BF16)

    @pl.when(pl.program_id(0) == 0)
    def _():
        d = wt_ref.shape[2]
        for (c0, width), dst in zip(W_PIECES, w_refs):
            for k0 in range(0, width, 256):
                dst[:, k0:k0 + 256] = wt_ref[0, c0 + k0:c0 + k0 + 256, :].T.astype(BF16)
        gate0 = W_QN + W_KV
        fox0 = gate0 + W_GATE + W_FOX
        gates = jnp.concatenate([wt_ref[0, gate0:gate0 + W_GATE, :], wt_ref[0, fox0:fox0 + FOX_HEADS, :],
                                 jnp.zeros((LANES - W_GATE - FOX_HEADS, d), F32)], axis=0)
        wg_ref[...] = gates.T.astype(BF16)
        for r0 in range(0, d, 256):
            wm_ref[r0:r0 + 256, :] = wm32_ref[0, r0:r0 + 256, :].astype(BF16)

    sub = x_ref.shape[0] // ROW_SPLIT
    for h in range(ROW_SPLIT):
        rows = slice(h * sub, (h + 1) * sub)
        x = x_ref[rows, :]
        r = lax.rsqrt(jnp.mean(x * x, axis=-1, keepdims=True) + NORM_EPS)
        u = (x * r * g_ref[...]).astype(BF16)
        c0 = 0
        for piece in w_refs:
            for k0 in range(0, piece.shape[1], 768):
                k1 = min(k0 + 768, piece.shape[1])
                proj_ref[rows, c0 + k0:c0 + k1] = _dot(u, piece[:, k0:k1]).astype(BF16)
            c0 += piece.shape[1]
        gf_ref[rows, :] = _dot(u, wg_ref[...])
        for c0 in (0, 1024):
            z = _dot(u, wm_ref[:, c0:c0 + 1024]) + bm_ref[:, c0:c0 + 1024]
            mg_ref[rows, c0:c0 + 1024] = jax.nn.sigmoid(z).astype(BF16)


def _inproj(x2, g, w_in_t, wm, bm, later, tm):
    n, d = x2.shape
    steps = n // tm
    const = lambda i: (0, 0)
    once = dict(pipeline_mode=pl.Buffered(1))
    slabs = [w.shape[1] // steps for w in later]
    assert all(r * steps == w.shape[1] and r % 16 == 0 for w, r in zip(later, slabs)), "bf16 slabs need 16-row multiples"
    return pl.pallas_call(
        _inproj_kernel,
        grid=(n // tm,),
        in_specs=[
            pl.BlockSpec((tm, d), lambda i: (i, 0)),
            pl.BlockSpec((1, d), const),
            pl.BlockSpec(w_in_t.shape, lambda i: (0, 0, 0), **once),
            pl.BlockSpec(wm.shape, lambda i: (0, 0, 0), **once),
            pl.BlockSpec(bm.shape, const),
        ] + [pl.BlockSpec((1, r, w.shape[2]), lambda i: (0, i, 0)) for w, r in zip(later, slabs)],
        out_specs=[
            pl.BlockSpec((tm, PROJ_W), lambda i: (i, 0)),
            pl.BlockSpec((tm, LANES), lambda i: (i, 0)),
            pl.BlockSpec((tm, 2 * d), lambda i: (i, 0)),
        ] + [pl.BlockSpec((r, w.shape[2]), lambda i: (i, 0)) for w, r in zip(later, slabs)],
        out_shape=[
            jax.ShapeDtypeStruct((n, PROJ_W), BF16),
            jax.ShapeDtypeStruct((n, LANES), F32),
            jax.ShapeDtypeStruct((n, 2 * d), BF16),
        ] + [jax.ShapeDtypeStruct(w.shape[1:], BF16) for w in later],
        scratch_shapes=[pltpu.VMEM((d, width), BF16) for _, width in W_PIECES]
        + [pltpu.VMEM((d, LANES), BF16), pltpu.VMEM((d, 2 * d), BF16)],
        compiler_params=pltpu.CompilerParams(
            dimension_semantics=("arbitrary",), vmem_limit_bytes=VMEM_LIMIT),
        name="inproj",
    )(x2, g, w_in_t, wm, bm, *later)


def _compress_kernel(kc_ref, vc_ref, wk_ref, w2k0_ref, w2k1_ref, pk_ref,
                     wv_ref, w2v0_ref, w2v1_ref, pv_ref, kco_ref, vco_ref, xs_ref):
    t = kc_ref.shape[1]
    nblk = t // CMP_STRIDE

    def one(x_ref, w_ref, w2_refs, p_ref, o_ref):
        xs_ref[...] = x_ref[0].astype(F32)
        acc = jnp.zeros((nblk + 16, 4 * LANES), F32)
        for j in range(CMP_STRIDE):
            xj = xs_ref[pl.ds(j, nblk, stride=CMP_STRIDE), :].astype(BF16)
            acc = acc + _dot(jnp.concatenate([xj, p_ref[j]], axis=0), w_ref[j])
        posb = acc[nblk:nblk + 1, 0:LANES] + acc[nblk:nblk + 1, 3 * LANES:4 * LANES]
        out = jnp.zeros((nblk, LANES), F32)
        for g in range(NSA_GROUPS):
            a = acc[0:nblk, g * LANES:(g + 1) * LANES]
            b = acc[0:nblk, (2 + g) * LANES:(3 + g) * LANES]
            h = a + pltpu.roll(b, nblk - 1, 0) + posb
            out = out + _dot(jax.nn.gelu(h).astype(BF16), w2_refs[g][...])
        o_ref[0] = out.astype(BF16)

    one(kc_ref, wk_ref, (w2k0_ref, w2k1_ref), pk_ref, kco_ref)
    one(vc_ref, wv_ref, (w2v0_ref, w2v1_ref), pv_ref, vco_ref)


def _compress(proj, wk, w2k0, w2k1, pk, wv, w2v0, w2v1, pv):
    b, t, _ = proj.shape
    kvb = COL_KV // LANES
    whole = lambda a: pl.BlockSpec(a.shape, lambda i: (0,) * a.ndim)
    ospec = pl.BlockSpec((1, LANES, LANES), lambda i: (i, 0, 0))
    return pl.pallas_call(
        _compress_kernel,
        grid=(b,),
        in_specs=[pl.BlockSpec((1, t, LANES), lambda i: (i, 0, kvb)),
                  pl.BlockSpec((1, t, LANES), lambda i: (i, 0, kvb + 1)),
                  whole(wk), whole(w2k0), whole(w2k1), whole(pk), whole(wv), whole(w2v0), whole(w2v1), whole(pv)],
        out_specs=[ospec, ospec],
        out_shape=[jax.ShapeDtypeStruct((b, LANES, LANES), BF16)] * 2,
        scratch_shapes=[pltpu.VMEM((t, LANES), F32)],
        compiler_params=pltpu.CompilerParams(dimension_semantics=("arbitrary",)),
        name="compress",
    )(proj, proj, wk, w2k0, w2k1, pk, wv, w2v0, w2v1, pv)


def _flash_init(m_ref, acc_ref):
    m_ref[...] = jnp.full(m_ref.shape, M_INIT, F32)
    acc_ref[...] = jnp.zeros(acc_ref.shape, F32)


def _softmax_pv(scores, vt_blks, m_refs=None, acc_refs=None, maxes=None):
    def pv(vt, p):
        if not isinstance(vt, tuple):
            return _dot(vt, p)
        w = p.shape[1] // len(vt)
        return jnp.concatenate([_dot(v, p[:, j * w:(j + 1) * w]) for j, v in enumerate(vt)], axis=1)

    ps, alphas = [], []
    for i, s in enumerate(scores):
        mx = jnp.max(s, axis=0, keepdims=True) if maxes is None else maxes[i]
        if m_refs is None:
            ps.append(jnp.exp2(s - mx).astype(BF16))
            continue
        m_old = m_refs[i][...]
        m_new = jnp.maximum(m_old, mx)
        ps.append(jnp.exp2(s - m_new).astype(BF16))
        alphas.append(jnp.exp2(m_old - m_new))
        m_refs[i][...] = m_new
    if m_refs is None:
        return [pv(vt, p) for vt, p in zip(vt_blks, ps)]
    for vt, p, alpha, acc_ref in zip(vt_blks, ps, alphas, acc_refs):
        acc_ref[...] = acc_ref[...] * alpha + pv(vt, p)
    return None


def _pipelined_sweep(n_rest, qk_to, spv_from):
    def body(jj, carry):
        j = 2 * jj
        qk_to(1, j + 1)
        spv_from(0, j)
        qk_to(0, j + 2)
        spv_from(1, j + 1)
        return carry

    lax.fori_loop(0, n_rest // 2, body, 0)

    @pl.when(n_rest % 2 == 1)
    def _():
        qk_to(1, n_rest)
        spv_from(0, n_rest - 1)
        spv_from(1, n_rest)

    @pl.when(n_rest % 2 == 0)
    def _():
        spv_from(0, n_rest)


def _ones_rows(tk):
    return jnp.where(_iota((16, tk), 0) == 0, 1.0, 0.0).astype(BF16)


NSA_TQ = 128
NSA_N = NSA_HPG * NSA_TQ
SEL_TK = 512
WIN_KEYS = WINDOW + NSA_TQ

AUG_POS = 0
AUG_TQ = 6
AUG_PAD = 9
AUG_MASK = 32


def _bf16_terms(x, n=3):
    out, r = [], np.float32(x)
    for _ in range(n):
        term = np.asarray(r, np.float32).astype(BF16).astype(np.float32)
        out.append(float(term))
        r = np.float32(r - term)
    return out


def _nsa_queries(q_ref, blk):
    t0 = blk * NSA_TQ
    qt = q_ref[0].astype(F32).T * (HEAD_DIM ** -0.5 * LOG2E)
    lane = _iota((1, NSA_N), 1)
    hl = lane // NSA_TQ
    tqf = (t0 + lane % NSA_TQ).astype(F32)
    ridx = _iota((16, NSA_N), 0)
    zeros_h = jnp.zeros((HEAD_DIM, NSA_TQ), F32)
    l2e = _bf16_terms(LOG2E)
    q_top, aug16 = [], []
    for g in range(NSA_GROUPS):
        tops = []
        for h in range(NSA_HPG):
            head = NSA_HPG * g + h
            qh = qt[head * HEAD_DIM:(head + 1) * HEAD_DIM]
            tops.append(jnp.concatenate([qh, zeros_h] if g == 0 else [zeros_h, qh], axis=0))
        q_top.append(jnp.concatenate(tops, axis=1))
        slope = lax.bitcast_convert_type((127 - (NSA_HPG * g + hl + 1)) << 23, F32)
        rows = {}
        for i, term in enumerate(l2e):
            rows[AUG_POS + 2 * i] = slope * (64.0 * term)
            rows[AUG_POS + 2 * i + 1] = slope * term
        for i, term in enumerate(_split3(-(slope * tqf) * LOG2E)):
            rows[AUG_TQ + i] = term.astype(F32)
        a = jnp.zeros((16, NSA_N), F32)
        for r, val in rows.items():
            a = jnp.where(ridx == r, val, a)
        aug16.append(a)
    return q_top, aug16


def _nsa_compressed(s_cmp, blk, vct_ref, ovt, gs):
    tq = blk * NSA_TQ + _iota((1, NSA_N), 1) % NSA_TQ
    n_idx = _iota((LANES, NSA_N), 0)
    valid = (n_idx * CMP_STRIDE + (CMP_BLOCK - 1) <= tq) & (n_idx < LANES - 1)
    o_cmp, imps = [], []
    for g in gs:
        sm = jnp.where(valid, s_cmp[g], NEG_BIG)
        mx = jnp.max(sm, axis=0, keepdims=True)
        e = jnp.where(valid, jnp.exp2(sm - mx), 0.0)
        den = jnp.sum(e, axis=0, keepdims=True)
        pn = e / jnp.where(den > 0.0, den, 1.0)
        o_cmp.append(_dot(vct_ref[g * HEAD_DIM:(g + 1) * HEAD_DIM, :], pn.astype(BF16)))

        psum = pn[:, 0:NSA_TQ]
        for h in range(1, NSA_HPG):
            psum = psum + pn[:, h * NSA_TQ:(h + 1) * NSA_TQ]
        hi, mid, lo = _split3(psum)
        imps.append(_dot(ovt, hi) + _dot(ovt, mid) + _dot(ovt, lo))
    return o_cmp, imps


def _nsa_select(imps, blk, q_top, aug16, gs):
    t0 = blk * NSA_TQ
    ns = imps[0].shape[0]
    tcol = t0 + _iota((8, NSA_TQ), 1)
    q_sel = []
    for g, imp in zip(gs, imps):
        tiles = range(ns // 8)
        j_idx = [8 * r + _iota((8, NSA_TQ), 0) for r in tiles]
        in_past = [j_idx[r] * SEL_BLOCK <= tcol for r in tiles]
        cur = tcol // SEL_BLOCK
        score = []
        for r in tiles:
            forced = (j_idx[r] == 0) | (j_idx[r] == cur) | (j_idx[r] == cur - 1)
            score.append(jnp.where(forced, FORCE_SCORE, jnp.where(in_past[r], imp[8 * r:8 * r + 8], -1.0)))
        rank = [jnp.zeros((8, NSA_TQ), F32) for _ in tiles]
        for i in range(ns):
            si = score[i // 8][i % 8:i % 8 + 1, :]
            for r in tiles:
                if 8 * r > i:
                    beats = si >= score[r]
                elif 8 * r + 7 <= i:
                    beats = si > score[r]
                else:
                    beats = (si > score[r]) | ((si == score[r]) & (j_idx[r] > i))
                rank[r] = rank[r] + jnp.where(beats, 1.0, 0.0)
        maskval = jnp.concatenate([jnp.where((rank[r] < min(N_SELECT, ns)) & in_past[r], 0.0, NEG_BIG) for r in tiles], axis=0)
        mask4 = jnp.concatenate([maskval] * NSA_HPG, axis=1)
        q_sel.append(jnp.concatenate([q_top[g], aug16[g], jnp.zeros((AUG_MASK - 16, NSA_N), F32), mask4,
                                      jnp.zeros((LANES - AUG_MASK - ns, NSA_N), F32)], axis=0).astype(BF16))
    return q_sel


def _nsa_kernel(q_ref, gf_ref, kc_ref, vc_ref, ks_ref, vs_ref, kw_ref, vw_ref,
                augk_ref, augw_ref, augc_ref, ovt_ref, trisel_ref, triwin_ref, o_ref,
                vst_ref, vwt_ref, kwp_ref, vct_ref, gt_ref, m_ref, acc_ref, s0_ref, s1_ref, mx0_ref, mx1_ref):
    qb = pl.program_id(1)
    t = ks_ref.shape[1]
    t0 = qb * NSA_TQ
    groups = range(NSA_GROUPS)
    zeros_tail = jnp.zeros((LANES - 16, NSA_N), F32)

    def cmp_scores(q_top, aug16):
        kcb = jnp.concatenate([kc_ref[0], augc_ref[...]], axis=1)
        return [_dot(kcb, jnp.concatenate([q_top[g], aug16[g], zeros_tail], axis=0).astype(BF16)) for g in groups]

    @pl.when(qb == 0)
    def _():
        vwt_ref[:, 0:WINDOW] = jnp.zeros((LANES, WINDOW), BF16)
        kwp_ref[0:WINDOW, :] = jnp.zeros((WINDOW, LANES), BF16)
        kwp_ref[WINDOW:WINDOW + t, :] = kw_ref[0]
        for c in range(t // LANES):
            sl = slice(c * LANES, (c + 1) * LANES)
            vst_ref[:, sl] = vs_ref[0, sl, :].astype(F32).T.astype(BF16)
            vwt_ref[:, WINDOW + c * LANES:WINDOW + (c + 1) * LANES] = vw_ref[0, sl, :].astype(F32).T.astype(BF16)
        vct_ref[...] = vc_ref[0].astype(F32).T.astype(BF16)

    q_top, aug16 = _nsa_queries(q_ref, qb)
    s_cmp = cmp_scores(q_top, aug16)
    kw0 = pl.multiple_of(t0, NSA_TQ)
    kwin = jnp.concatenate([kwp_ref[pl.ds(kw0, WIN_KEYS), :], augw_ref[pl.ds(kw0, WIN_KEYS), :]], axis=1)
    pad_row = _iota((16, NSA_N), 0) == AUG_PAD
    s_win = [_dot(kwin, jnp.concatenate([q_top[g], jnp.where(pad_row, NEG_BIG, aug16[g]), zeros_tail], axis=0).astype(BF16))
             for g in groups]
    o_cmp, imps = _nsa_compressed(s_cmp, qb, vct_ref, ovt_ref[...], groups)
    q_sel = _nsa_select(imps, qb, q_top, aug16, groups)

    hi_edge = jnp.concatenate([triwin_ref[0]] * NSA_HPG, axis=1)
    lo_edge = jnp.concatenate([triwin_ref[1]] * NSA_HPG, axis=1)
    ones_w = _ones_rows(WIN_KEYS)
    s_win = [jnp.concatenate([s[0:NSA_TQ] + lo_edge, s[NSA_TQ:WINDOW], s[WINDOW:WIN_KEYS] + hi_edge], axis=0)
             for s in s_win]
    vt_win = [jnp.concatenate([vwt_ref[g * HEAD_DIM:(g + 1) * HEAD_DIM, pl.ds(kw0, WIN_KEYS)], ones_w], axis=0)
              for g in groups]
    o_win = [acc[0:HEAD_DIM] / acc[HEAD_DIM:HEAD_DIM + 1] for acc in _softmax_pv(s_win, vt_win)]

    ones_s = _ones_rows(SEL_TK)
    per_chunk = SEL_TK // NSA_TQ
    n_full = qb // per_chunk
    s_bufs = (s0_ref, s1_ref)
    mx_bufs = (mx0_ref, mx1_ref)

    def sel_scores(cb):
        k0 = pl.multiple_of(cb * SEL_TK, SEL_TK)
        kblk = jnp.concatenate([ks_ref[0, pl.ds(k0, SEL_TK), :], augk_ref[pl.ds(k0, SEL_TK), :]], axis=1)
        return [_dot(kblk, q_sel[g]) for g in groups]

    def qk_to(buf, j):
        for g, s in enumerate(sel_scores(j - 1)):
            s_bufs[buf][g] = s
            mx_bufs[buf][g] = jnp.max(s, axis=0, keepdims=True)

    def spv_from(buf, j):
        cb = jnp.where(j == 0, n_full, j - 1)
        k0 = pl.multiple_of(cb * SEL_TK, SEL_TK)
        vts = [jnp.concatenate([vst_ref[g * HEAD_DIM:(g + 1) * HEAD_DIM, pl.ds(k0, SEL_TK)], ones_s], axis=0)
               for g in groups]
        _softmax_pv([s_bufs[buf][g] for g in groups], vts,
                    [m_ref.at[g] for g in groups], [acc_ref.at[g] for g in groups],
                    maxes=[mx_bufs[buf][g] for g in groups])

    diag = trisel_ref[qb % per_chunk]
    diag4 = jnp.concatenate([diag] * NSA_HPG, axis=1)
    for g, s in enumerate(sel_scores(n_full)):
        s = s + diag4
        s0_ref[g] = s
        mx0_ref[g] = jnp.max(s, axis=0, keepdims=True)
        _flash_init(m_ref.at[g], acc_ref.at[g])
    _pipelined_sweep(n_full, qk_to, spv_from)

    gt_ref[...] = gf_ref[0].T
    sig = jax.nn.sigmoid(gt_ref[GF_GATE:GF_GATE + 3 * NSA_HEADS, :])
    outs = []
    for g in groups:
        acc = acc_ref[g]
        o_sel = acc[0:HEAD_DIM] / acc[HEAD_DIM:HEAD_DIM + 1]
        for h in range(NSA_HPG):
            sl = slice(h * NSA_TQ, (h + 1) * NSA_TQ)
            head = NSA_HPG * g + h
            g0 = sig[head:head + 1]
            g1 = sig[NSA_HEADS + head:NSA_HEADS + head + 1]
            g2 = sig[2 * NSA_HEADS + head:2 * NSA_HEADS + head + 1]
            outs.append(g0 * o_cmp[g][:, sl] + g1 * o_sel[:, sl] + g2 * o_win[g][:, sl])
    o_ref[0] = jnp.concatenate(outs, axis=0).T.astype(BF16)


def _nsa(proj, gf, kcmp, vcmp, augk, augw, augc, ovt, trisel, triwin):
    b, t, _ = proj.shape
    nq = t // NSA_TQ
    kvb = COL_KV // LANES
    width = NSA_HEADS * HEAD_DIM
    whole = lambda a: pl.BlockSpec(a.shape, lambda i, k: (0,) * a.ndim)
    kv = lambda idx: pl.BlockSpec((1, t, LANES), lambda i, k: (i, 0, idx))
    return pl.pallas_call(
        _nsa_kernel,
        grid=(b, nq),
        in_specs=[
            pl.BlockSpec((1, NSA_TQ, width), lambda i, k: (i, k, COL_QN // width)),
            pl.BlockSpec((1, NSA_TQ, LANES), lambda i, k: (i, k, 0)),
            pl.BlockSpec((1, LANES, LANES), lambda i, k: (i, 0, 0)),
            pl.BlockSpec((1, LANES, LANES), lambda i, k: (i, 0, 0)),
            kv(kvb + 2), kv(kvb + 3), kv(kvb + 4), kv(kvb + 5),
            whole(augk), whole(augw), whole(augc), whole(ovt), whole(trisel), whole(triwin),
        ],
        out_specs=pl.BlockSpec((1, NSA_TQ, width), lambda i, k: (i, k, 0)),
        out_shape=jax.ShapeDtypeStruct((b, t, width), BF16),
        scratch_shapes=[
            pltpu.VMEM((LANES, t), BF16),
            pltpu.VMEM((LANES, t + WINDOW), BF16),
            pltpu.VMEM((t + WINDOW, LANES), BF16),
            pltpu.VMEM((LANES, LANES), BF16),
            pltpu.VMEM((LANES, NSA_TQ), F32),
            pltpu.VMEM((NSA_GROUPS, 1, NSA_N), F32),
            pltpu.VMEM((NSA_GROUPS, HEAD_DIM + 16, NSA_N), F32),
            pltpu.VMEM((NSA_GROUPS, SEL_TK, NSA_N), F32),
            pltpu.VMEM((NSA_GROUPS, SEL_TK, NSA_N), F32),
            pltpu.VMEM((NSA_GROUPS, 1, NSA_N), F32),
            pltpu.VMEM((NSA_GROUPS, 1, NSA_N), F32),
        ],
        compiler_params=pltpu.CompilerParams(
            dimension_semantics=("arbitrary", "arbitrary"), vmem_limit_bytes=VMEM_LIMIT),
        name="nsa_attention",
    )(proj, gf, kcmp, vcmp, proj, proj, proj, proj, augk, augw, augc, ovt, trisel, triwin)


FOX_CBLK = 256


def _foxprep_kernel(gf_ref, bias_ref, cq_ref, augk_ref):
    t = gf_ref.shape[1]
    cols = []
    for c in range(t // LANES):
        cols.append(gf_ref[0, c * LANES:(c + 1) * LANES, :].T[GF_FOX:GF_FOX + FOX_HEADS])
    f = jnp.concatenate(cols, axis=1) + bias_ref[...]
    ls = jnp.minimum(f, 0.0) - jnp.log(1.0 + jnp.exp(-jnp.abs(f)))
    upper = (_iota((FOX_CBLK, FOX_CBLK), 0) <= _iota((FOX_CBLK, FOX_CBLK), 1)).astype(BF16)
    carry = jnp.zeros((FOX_HEADS, 1), F32)
    parts = []
    for c in range(t // FOX_CBLK):
        hi, mid, lo = _split3(ls[:, c * FOX_CBLK:(c + 1) * FOX_CBLK])
        cb = _dot(hi, upper) + _dot(mid, upper) + _dot(lo, upper) + carry
        carry = cb[:, FOX_CBLK - 1:FOX_CBLK]
        parts.append(cb)
    csum = jnp.concatenate(parts, axis=1) * LOG2E
    c1, c2, c3 = (v.astype(F32) for v in _split3(csum))
    ones = jnp.ones((3 * FOX_HEADS, t), F32)
    zeros = jnp.zeros((LANES - 6 * FOX_HEADS, t), F32)
    cq_ref[0] = jnp.concatenate([c1, c2, c3, ones, zeros], axis=0).astype(BF16)
    slab_k = jnp.concatenate([ones, -c1, -c2, -c3, zeros], axis=0)
    for c in range(t // LANES):
        sl = slice(c * LANES, (c + 1) * LANES)
        augk_ref[0, sl, :] = slab_k[:, sl].T.astype(BF16)


def _foxprep(gf, bias):
    b, t, _ = gf.shape
    return pl.pallas_call(
        _foxprep_kernel,
        grid=(b,),
        in_specs=[pl.BlockSpec((1, t, LANES), lambda i: (i, 0, 0)),
                  pl.BlockSpec(bias.shape, lambda i: (0, 0))],
        out_specs=[pl.BlockSpec((1, LANES, t), lambda i: (i, 0, 0)),
                   pl.BlockSpec((1, t, LANES), lambda i: (i, 0, 0))],
        out_shape=[jax.ShapeDtypeStruct((b, LANES, t), BF16),
                   jax.ShapeDtypeStruct((b, t, LANES), BF16)],
        compiler_params=pltpu.CompilerParams(dimension_semantics=("arbitrary",)),
        name="fox_prep",
    )(gf, bias)


FOX_TQ = 256
FOX_TK = 256
FOX_NP = FOX_HEADS // 2
FOX_N = 2 * FOX_TQ
FOX_ROWS = HEAD_DIM + 16


def _fox_kernel(q_ref, k_ref, v_ref, cq_ref, augk_ref, tri_ref, o_ref, vt_ref, m_ref, acc_ref, s0_ref, s1_ref,
                mx0_ref, mx1_ref):
    qb = pl.program_id(1)
    t = k_ref.shape[1]

    @pl.when(qb == 0)
    def _():
        for p in range(FOX_NP):
            for c in range(t // LANES):
                sl = slice(c * LANES, (c + 1) * LANES)
                vt = v_ref[0, sl, p * LANES:(p + 1) * LANES].astype(F32).T.astype(BF16)
                for hh in range(2):
                    vt_ref[p, hh, 0:HEAD_DIM, sl] = vt[hh * HEAD_DIM:(hh + 1) * HEAD_DIM]
            for hh in range(2):
                vt_ref[p, hh, HEAD_DIM:FOX_ROWS, :] = _ones_rows(t)

    row = _iota((LANES, FOX_TQ), 0)
    hrow = row % FOX_HEADS
    cq = cq_ref[0].astype(F32)
    qs = []
    for p in range(FOX_NP):
        qt = q_ref[0, :, p * LANES:(p + 1) * LANES].astype(F32).T * (HEAD_DIM ** -0.5 * LOG2E)
        q_top = jnp.concatenate([jnp.where(row < HEAD_DIM, qt, 0.0), jnp.where(row >= HEAD_DIM, qt, 0.0)], axis=1)
        q_aug = jnp.concatenate([jnp.where(hrow == 2 * p, cq, 0.0), jnp.where(hrow == 2 * p + 1, cq, 0.0)], axis=1)
        qs.append(jnp.concatenate([q_top, q_aug], axis=0).astype(BF16))

    pairs = range(FOX_NP)
    s_bufs = (s0_ref, s1_ref)
    mx_bufs = (mx0_ref, mx1_ref)

    def scores(kb):
        k0 = pl.multiple_of(kb * FOX_TK, FOX_TK)
        ak = augk_ref[0, pl.ds(k0, FOX_TK), :]
        return [_dot(jnp.concatenate([k_ref[0, pl.ds(k0, FOX_TK), p * LANES:(p + 1) * LANES], ak], axis=1), qs[p])
                for p in pairs]

    def qk_to(buf, j):
        for p, s in enumerate(scores(j - 1)):
            s_bufs[buf][p] = s
            mx_bufs[buf][p] = jnp.max(s, axis=0, keepdims=True)

    def spv_from(buf, j):
        kb = jnp.where(j == 0, qb, j - 1)
        k0 = pl.multiple_of(kb * FOX_TK, FOX_TK)
        vts = [(vt_ref[p, 0, :, pl.ds(k0, FOX_TK)], vt_ref[p, 1, :, pl.ds(k0, FOX_TK)]) for p in pairs]
        _softmax_pv([s_bufs[buf][p] for p in pairs], vts,
                    [m_ref.at[p] for p in pairs], [acc_ref.at[p] for p in pairs],
                    maxes=[mx_bufs[buf][p] for p in pairs])

    tri = tri_ref[...]
    for p, s in enumerate(scores(qb)):
        s = s + tri
        s0_ref[p] = s
        mx0_ref[p] = jnp.max(s, axis=0, keepdims=True)
        _flash_init(m_ref.at[p], acc_ref.at[p])
    _pipelined_sweep(qb, qk_to, spv_from)
    for p in range(FOX_NP):
        acc = acc_ref[p]
        den = acc[HEAD_DIM:HEAD_DIM + 1]
        o_a = acc[0:HEAD_DIM, 0:FOX_TQ] / den[:, 0:FOX_TQ]
        o_b = acc[0:HEAD_DIM, FOX_TQ:FOX_N] / den[:, FOX_TQ:FOX_N]
        o_ref[0, :, p * LANES:(p + 1) * LANES] = jnp.concatenate([o_a, o_b], axis=0).T.astype(BF16)


def _fox(proj, cq, augk, tri):
    b, t, _ = proj.shape
    nq = t // FOX_TQ
    width = FOX_HEADS * HEAD_DIM
    qf, kf, vf = COL_QF // width, COL_KF // width, COL_VF // width
    return pl.pallas_call(
        _fox_kernel,
        grid=(b, nq),
        in_specs=[
            pl.BlockSpec((1, FOX_TQ, width), lambda i, k: (i, k, qf)),
            pl.BlockSpec((1, t, width), lambda i, k: (i, 0, kf)),
            pl.BlockSpec((1, t, width), lambda i, k: (i, 0, vf)),
            pl.BlockSpec((1, LANES, FOX_TQ), lambda i, k: (i, 0, k)),
            pl.BlockSpec((1, t, LANES), lambda i, k: (i, 0, 0)),
            pl.BlockSpec(tri.shape, lambda i, k: (0, 0)),
        ],
        out_specs=pl.BlockSpec((1, FOX_TQ, width), lambda i, k: (i, k, 0)),
        out_shape=jax.ShapeDtypeStruct((b, t, width), BF16),
        scratch_shapes=[
            pltpu.VMEM((FOX_NP, 2, FOX_ROWS, t), BF16),
            pltpu.VMEM((FOX_NP, 1, FOX_N), F32),
            pltpu.VMEM((FOX_NP, FOX_ROWS, FOX_N), F32),
            pltpu.VMEM((FOX_NP, FOX_TK, FOX_N), F32),
            pltpu.VMEM((FOX_NP, FOX_TK, FOX_N), F32),
            pltpu.VMEM((FOX_NP, 1, FOX_N), F32),
            pltpu.VMEM((FOX_NP, 1, FOX_N), F32),
        ],
        compiler_params=pltpu.CompilerParams(
            dimension_semantics=("arbitrary", "arbitrary"), vmem_limit_bytes=VMEM_LIMIT),
        name="fox_attention",
    )(proj, proj, proj, cq, augk, tri)


def _post_kernel(x_ref, on_ref, of_ref, mg_ref, wn_ref, wf_ref, wo_ref, gm_ref, wu_ref, wd_ref, gfin_ref, o_ref):
    d = x_ref.shape[1]
    ff = wu_ref.shape[1]
    ga = mg_ref[:, 0:d].astype(F32)
    gb = mg_ref[:, d:2 * d].astype(F32)
    y = ga * _dot(on_ref[...], wn_ref[...]) + gb * _dot(of_ref[...], wf_ref[...])
    h = x_ref[...] + _dot(y.astype(BF16), wo_ref[...])
    r = lax.rsqrt(jnp.mean(h * h, axis=-1, keepdims=True) + NORM_EPS)
    v = (h * r * gm_ref[...]).astype(BF16)
    acc = h
    for c0 in range(0, ff, FF_CHUNK):
        up = jnp.maximum(_dot(v, wu_ref[:, c0:c0 + FF_CHUNK]), 0.0)
        acc = acc + _dot((up * up).astype(BF16), wd_ref[c0:c0 + FF_CHUNK, :])
    r2 = lax.rsqrt(jnp.mean(acc * acc, axis=-1, keepdims=True) + NORM_EPS)
    o_ref[...] = acc * r2 * gfin_ref[...]


def _post(x2, on, of, mg, wn, wf, wo, gm, wu, wd, gfin, tm):
    n, d = x2.shape
    row = lambda w: pl.BlockSpec((tm, w), lambda i: (i, 0))
    const = lambda a: pl.BlockSpec(a.shape, lambda i: (0, 0), pipeline_mode=pl.Buffered(1))
    return pl.pallas_call(
        _post_kernel,
        grid=(n // tm,),
        in_specs=[row(d), row(on.shape[1]), row(of.shape[1]), row(2 * d),
                  const(wn), const(wf), const(wo), const(gm), const(wu), const(wd), const(gfin)],
        out_specs=row(d),
        out_shape=jax.ShapeDtypeStruct((n, d), F32),
        compiler_params=pltpu.CompilerParams(
            dimension_semantics=("arbitrary",), vmem_limit_bytes=VMEM_LIMIT),
        name="post_mlp",
    )(x2, on, of, mg, wn, wf, wo, gm, wu, wd, gfin)


def _position_constants(t):
    def key_side(pos, pad):
        a = np.zeros((pos.size, LANES), np.float32)
        for i in range(3):
            a[:, AUG_POS + 2 * i] = pos // 64
            a[:, AUG_POS + 2 * i + 1] = pos % 64
        a[:, AUG_TQ:AUG_TQ + 3] = 1.0
        a[:, AUG_PAD] = pad
        return a

    pos = np.arange(t)
    augk = key_side(pos, 0.0)
    augk[pos, AUG_MASK + pos // SEL_BLOCK] = 1.0
    augw = np.concatenate([key_side(np.zeros(WINDOW, np.int64), 1.0), key_side(pos, 0.0)], axis=0)
    augc = key_side(np.arange(LANES) * CMP_STRIDE + CMP_BLOCK - 1, 0.0)

    nc = (t - CMP_BLOCK) // CMP_STRIDE + 1
    ns = t // SEL_BLOCK
    n = np.arange(LANES)[None, :]
    j = np.arange(ns)[:, None]
    ovt = ((n * CMP_STRIDE <= j * SEL_BLOCK + SEL_BLOCK - 1) & (n * CMP_STRIDE + CMP_BLOCK - 1 >= j * SEL_BLOCK) & (n < nc))

    def tri(tk, tq, anti=False):
        kk = np.arange(tk)[:, None]
        tt = np.arange(tq)[None, :]
        ok = (kk > tt) if anti else (kk <= tt)
        return np.where(ok, 0.0, NEG_BIG).astype(np.float32)

    edge = tri(NSA_TQ, NSA_TQ)
    trisel = np.zeros((SEL_TK // NSA_TQ, SEL_TK, NSA_TQ), np.float32)
    for v in range(SEL_TK // NSA_TQ):
        trisel[v, v * NSA_TQ:(v + 1) * NSA_TQ] = edge
    triwin = np.stack([edge, tri(NSA_TQ, NSA_TQ, anti=True)])
    trifox = np.tile(tri(FOX_TK, FOX_TQ), (1, 2))
    return (jnp.asarray(augk, BF16), jnp.asarray(augw, BF16), jnp.asarray(augc, BF16), jnp.asarray(ovt, BF16),
            jnp.asarray(trisel), jnp.asarray(triwin), jnp.asarray(trifox))


def kernel(x, norm_mix, w_in, cmp_pos_k, cmp_w1_k, cmp_w2_k, cmp_pos_v, cmp_w1_v, cmp_w2_v, fox_f_bias, w_branch_nsa, w_branch_fox, w_merge_gate, b_merge_gate, w_out, norm_mlp, w_up, w_down, norm_final):
    b, t, d = x.shape
    assert w_in.shape[0] == 1, "one layer: the final norm is fused into the MLP kernel"
    assert t % FOX_TQ == 0 and t // SEL_BLOCK == 32 and (t - CMP_BLOCK) // CMP_STRIDE + 1 == LANES - 1
    augk, augw, augc, ovt, trisel, triwin, trifox = _position_constants(t)
    h = x.reshape(b * t, d)
    proj, gf, mg, wn, wf, wo, wu, wd = _inproj(h, norm_mix[0][None, :], jnp.swapaxes(w_in, 1, 2), w_merge_gate,
                                               b_merge_gate[0][None, :],
                                               (w_branch_nsa, w_branch_fox, w_out, w_up, w_down), ROW_TILE)
    proj = proj.reshape(b, t, PROJ_W)
    gf = gf.reshape(b, t, LANES)

    def w1_blocks(w1):
        wa_ = w1[:CMP_STRIDE * HEAD_DIM].reshape(CMP_STRIDE, HEAD_DIM, CMP_HIDDEN)
        wb_ = w1[CMP_STRIDE * HEAD_DIM:].reshape(CMP_STRIDE, HEAD_DIM, CMP_HIDDEN)
        z = jnp.zeros_like(wa_)
        top = jnp.concatenate([wa_, z, wb_, z], axis=2)
        bot = jnp.concatenate([z, wa_, z, wb_], axis=2)
        return jnp.concatenate([top, bot], axis=1).astype(BF16)

    def w2_pair(w2):
        z = jnp.zeros_like(w2)
        return (jnp.concatenate([w2, z], axis=1).astype(BF16), jnp.concatenate([z, w2], axis=1).astype(BF16))

    def pos_rows(p):
        row = jnp.concatenate([p[:CMP_STRIDE], p[CMP_STRIDE:]], axis=1)[:, None, :]
        return jnp.concatenate([row, jnp.zeros((CMP_STRIDE, 15, 2 * HEAD_DIM), F32)], axis=1).astype(BF16)

    w2k0, w2k1 = w2_pair(cmp_w2_k[0])
    w2v0, w2v1 = w2_pair(cmp_w2_v[0])
    kcmp, vcmp = _compress(proj, w1_blocks(cmp_w1_k[0]), w2k0, w2k1, pos_rows(cmp_pos_k[0]),
                           w1_blocks(cmp_w1_v[0]), w2v0, w2v1, pos_rows(cmp_pos_v[0]))
    o_n = _nsa(proj, gf, kcmp, vcmp, augk, augw, augc, ovt, trisel, triwin)
    cq, faug = _foxprep(gf, fox_f_bias[0][:, None])
    o_f = _fox(proj, cq, faug, trifox)
    out = _post(h, o_n.reshape(b * t, -1), o_f.reshape(b * t, -1), mg,
                wn, wf, wo, norm_mlp[0][None, :], wu, wd, norm_final[None, :], ROW_TILE)
    return out.reshape(b, t, d)
```

```python
import numpy as np
import jax
import jax.numpy as jnp
from jax import lax
from jax.experimental import pallas as pl
from jax.experimental.pallas import tpu as pltpu

HEAD_DIM = 64
NSA_HEADS = 8
NSA_GROUPS = 2
NSA_HPG = NSA_HEADS // NSA_GROUPS
FOX_HEADS = 8
CMP_BLOCK = 32
CMP_STRIDE = 16
CMP_HIDDEN = 128
SEL_BLOCK = 64
N_SELECT = 16
WINDOW = 512
NORM_EPS = 1e-6
FORCE_SCORE = 1e9

LANES = 128
NEG_BIG = -(2.0 ** 100)
M_INIT = -3.0e38
LOG2E = 1.4426950408889634
VMEM_LIMIT = 56 * 1024 * 1024
ROW_TILE = 512
ROW_SPLIT = 2
FF_CHUNK = 1024

PROJ_W = 2816
COL_QN = 0
COL_QF = 512
COL_KF = 1024
COL_VF = 1536
COL_KV = 2048
GF_GATE = 0
GF_FOX = 24

F32 = jnp.float32
BF16 = jnp.bfloat16


def _dot(a, b):
    return jnp.dot(a, b, preferred_element_type=F32)


def _iota(shape, dim):
    return lax.broadcasted_iota(jnp.int32, shape, dim)


def _split3(x):
    hi = x.astype(BF16)
    r1 = x - hi.astype(F32)
    mid = r1.astype(BF16)
    lo = (r1 - mid.astype(F32)).astype(BF16)
    return hi, mid, lo


W_QN = NSA_HEADS * HEAD_DIM
W_KV = 6 * NSA_GROUPS * HEAD_DIM
W_GATE = 3 * NSA_HEADS
W_FOX = 3 * FOX_HEADS * HEAD_DIM
W_PIECES = ((0, W_QN), (W_QN + W_KV + W_GATE, W_FOX), (W_QN, W_KV))


def _inproj_kernel(x_ref, g_ref, wt_ref, wm32_ref, bm_ref, *refs):
    n_later = (len(refs) - 8) // 2
    later_in, (proj_ref, gf_ref, mg_ref) = refs[:n_later], refs[n_later:n_later + 3]
    later_out = refs[n_later + 3:2 * n_later + 3]
    wq_ref, wf_ref, wkv_ref, wg_ref, wm_ref = refs[2 * n_later + 3:]
    w_refs = (wq_ref, wf_ref, wkv_ref)

    for src, dst in zip(later_in, later_out):
        dst[...] = src[0].astype(BF16)

    @pl.when(pl.program_id(0) == 0)
    def _():
        d = wt_ref.shape[2]
        for (c0, width), dst in zip(W_PIECES, w_refs):
            for k0 in range(0, width, 256):
                dst[:, k0:k0 + 256] = wt_ref[0, c0 + k0:c0 + k0 + 256, :].T.astype(BF16)
        gate0 = W_QN + W_KV
        fox0 = gate0 + W_GATE + W_FOX
        gates = jnp.concatenate([wt_ref[0, gate0:gate0 + W_GATE, :], wt_ref[0, fox0:fox0 + FOX_HEADS, :],
                                 jnp.zeros((LANES - W_GATE - FOX_HEADS, d), F32)], axis=0)
        wg_ref[...] = gates.T.astype(BF16)
        for r0 in range(0, d, 256):
            wm_ref[r0:r0 + 256, :] = wm32_ref[0, r0:r0 + 256, :].astype(BF16)

    sub = x_ref.shape[0] // ROW_SPLIT
    for h in range(ROW_SPLIT):
        rows = slice(h * sub, (h + 1) * sub)
        x = x_ref[rows, :]
        r = lax.rsqrt(jnp.mean(x * x, axis=-1, keepdims=True) + NORM_EPS)
        u = (x * r * g_ref[...]).astype(BF16)
        c0 = 0
        for piece in w_refs:
            for k0 in range(0, piece.shape[1], 768):
                k1 = min(k0 + 768, piece.shape[1])
                proj_ref[rows, c0 + k0:c0 + k1] = _dot(u, piece[:, k0:k1]).astype(BF16)
            c0 += piece.shape[1]
        gf_ref[rows, :] = _dot(u, wg_ref[...])
        for c0 in (0, 1024):
            z = _dot(u, wm_ref[:, c0:c0 + 1024]) + bm_ref[:, c0:c0 + 1024]
            mg_ref[rows, c0:c0 + 1024] = jax.nn.sigmoid(z).astype(BF16)


def _inproj(x2, g, w_in_t, wm, bm, later, tm):
    n, d = x2.shape
    steps = n // tm
    const = lambda i: (0, 0)
    once = dict(pipeline_mode=pl.Buffered(1))
    slabs = [w.shape[1] // steps for w in later]
    assert all(r * steps == w.shape[1] and r % 16 == 0 for w, r in zip(later, slabs)), "bf16 slabs need 16-row multiples"
    return pl.pallas_call(
        _inproj_kernel,
        grid=(n // tm,),
        in_specs=[
            pl.BlockSpec((tm, d), lambda i: (i, 0)),
            pl.BlockSpec((1, d), const),
            pl.BlockSpec(w_in_t.shape, lambda i: (0, 0, 0), **once),
            pl.BlockSpec(wm.shape, lambda i: (0, 0, 0), **once),
            pl.BlockSpec(bm.shape, const),
        ] + [pl.BlockSpec((1, r, w.shape[2]), lambda i: (0, i, 0)) for w, r in zip(later, slabs)],
        out_specs=[
            pl.BlockSpec((tm, PROJ_W), lambda i: (i, 0)),
            pl.BlockSpec((tm, LANES), lambda i: (i, 0)),
            pl.BlockSpec((tm, 2 * d), lambda i: (i, 0)),
        ] + [pl.BlockSpec((r, w.shape[2]), lambda i: (i, 0)) for w, r in zip(later, slabs)],
        out_shape=[
            jax.ShapeDtypeStruct((n, PROJ_W), BF16),
            jax.ShapeDtypeStruct((n, LANES), F32),
            jax.ShapeDtypeStruct((n, 2 * d), BF16),
        ] + [jax.ShapeDtypeStruct(w.shape[1:], BF16) for w in later],
        scratch_shapes=[pltpu.VMEM((d, width), BF16) for _, width in W_PIECES]
        + [pltpu.VMEM((d, LANES), BF16), pltpu.VMEM((d, 2 * d), BF16)],
        compiler_params=pltpu.CompilerParams(
            dimension_semantics=("arbitrary",), vmem_limit_bytes=VMEM_LIMIT),
        name="inproj",
    )(x2, g, w_in_t, wm, bm, *later)


def _compress_kernel(kc_ref, vc_ref, wk_ref, w2k0_ref, w2k1_ref, pk_ref,
                     wv_ref, w2v0_ref, w2v1_ref, pv_ref, kco_ref, vco_ref, xs_ref):
    t = kc_ref.shape[1]
    nblk = t // CMP_STRIDE

    def one(x_ref, w_ref, w2_refs, p_ref, o_ref):
        xs_ref[...] = x_ref[0].astype(F32)
        acc = jnp.zeros((nblk + 16, 4 * LANES), F32)
        for j in range(CMP_STRIDE):
            xj = xs_ref[pl.ds(j, nblk, stride=CMP_STRIDE), :].astype(BF16)
            acc = acc + _dot(jnp.concatenate([xj, p_ref[j]], axis=0), w_ref[j])
        posb = acc[nblk:nblk + 1, 0:LANES] + acc[nblk:nblk + 1, 3 * LANES:4 * LANES]
        out = jnp.zeros((nblk, LANES), F32)
        for g in range(NSA_GROUPS):
            a = acc[0:nblk, g * LANES:(g + 1) * LANES]
            b = acc[0:nblk, (2 + g) * LANES:(3 + g) * LANES]
            h = a + pltpu.roll(b, nblk - 1, 0) + posb
            out = out + _dot(jax.nn.gelu(h).astype(BF16), w2_refs[g][...])
        o_ref[0] = out.astype(BF16)

    one(kc_ref, wk_ref, (w2k0_ref, w2k1_ref), pk_ref, kco_ref)
    one(vc_ref, wv_ref, (w2v0_ref, w2v1_ref), pv_ref, vco_ref)


def _compress(proj, wk, w2k0, w2k1, pk, wv, w2v0, w2v1, pv):
    b, t, _ = proj.shape
    kvb = COL_KV // LANES
    whole = lambda a: pl.BlockSpec(a.shape, lambda i: (0,) * a.ndim)
    ospec = pl.BlockSpec((1, LANES, LANES), lambda i: (i, 0, 0))
    return pl.pallas_call(
        _compress_kernel,
        grid=(b,),
        in_specs=[pl.BlockSpec((1, t, LANES), lambda i: (i, 0, kvb)),
                  pl.BlockSpec((1, t, LANES), lambda i: (i, 0, kvb + 1)),
                  whole(wk), whole(w2k0), whole(w2k1), whole(pk), whole(wv), whole(w2v0), whole(w2v1), whole(pv)],
        out_specs=[ospec, ospec],
        out_shape=[jax.ShapeDtypeStruct((b, LANES, LANES), BF16)] * 2,
        scratch_shapes=[pltpu.VMEM((t, LANES), F32)],
        compiler_params=pltpu.CompilerParams(dimension_semantics=("arbitrary",)),
        name="compress",
    )(proj, proj, wk, w2k0, w2k1, pk, wv, w2v0, w2v1, pv)


def _flash_init(m_ref, acc_ref):
    m_ref[...] = jnp.full(m_ref.shape, M_INIT, F32)
    acc_ref[...] = jnp.zeros(acc_ref.shape, F32)


def _softmax_pv(scores, vt_blks, m_refs=None, acc_refs=None, maxes=None):
    def pv(vt, p):
        if not isinstance(vt, tuple):
            return _dot(vt, p)
        w = p.shape[1] // len(vt)
        return jnp.concatenate([_dot(v, p[:, j * w:(j + 1) * w]) for j, v in enumerate(vt)], axis=1)

    ps, alphas = [], []
    for i, s in enumerate(scores):
        mx = jnp.max(s, axis=0, keepdims=True) if maxes is None else maxes[i]
        if m_refs is None:
            ps.append(jnp.exp2(s - mx).astype(BF16))
            continue
        m_old = m_refs[i][...]
        m_new = jnp.maximum(m_old, mx)
        ps.append(jnp.exp2(s - m_new).astype(BF16))
        alphas.append(jnp.exp2(m_old - m_new))
        m_refs[i][...] = m_new
    if m_refs is None:
        return [pv(vt, p) for vt, p in zip(vt_blks, ps)]
    for vt, p, alpha, acc_ref in zip(vt_blks, ps, alphas, acc_refs):
        acc_ref[...] = acc_ref[...] * alpha + pv(vt, p)
    return None


def _pipelined_sweep(n_rest, qk_to, spv_from):
    def body(jj, carry):
        j = 2 * jj
        qk_to(1, j + 1)
        spv_from(0, j)
        qk_to(0, j + 2)
        spv_from(1, j + 1)
        return carry

    lax.fori_loop(0, n_rest // 2, body, 0)

    @pl.when(n_rest % 2 == 1)
    def _():
        qk_to(1, n_rest)
        spv_from(0, n_rest - 1)
        spv_from(1, n_rest)

    @pl.when(n_rest % 2 == 0)
    def _():
        spv_from(0, n_rest)


def _ones_rows(tk):
    return jnp.where(_iota((16, tk), 0) == 0, 1.0, 0.0).astype(BF16)


NSA_TQ = 128
NSA_N = NSA_HPG * NSA_TQ
SEL_TK = 512
WIN_KEYS = WINDOW + NSA_TQ

AUG_POS = 0
AUG_TQ = 6
AUG_PAD = 9
AUG_MASK = 32


def _bf16_terms(x, n=3):
    out, r = [], np.float32(x)
    for _ in range(n):
        term = np.asarray(r, np.float32).astype(BF16).astype(np.float32)
        out.append(float(term))
        r = np.float32(r - term)
    return out


def _nsa_queries(q_ref, blk):
    t0 = blk * NSA_TQ
    qt = q_ref[0].astype(F32).T * (HEAD_DIM ** -0.5 * LOG2E)
    lane = _iota((1, NSA_N), 1)
    hl = lane // NSA_TQ
    tqf = (t0 + lane % NSA_TQ).astype(F32)
    ridx = _iota((16, NSA_N), 0)
    zeros_h = jnp.zeros((HEAD_DIM, NSA_TQ), F32)
    l2e = _bf16_terms(LOG2E)
    q_top, aug16 = [], []
    for g in range(NSA_GROUPS):
        tops = []
        for h in range(NSA_HPG):
            head = NSA_HPG * g + h
            qh = qt[head * HEAD_DIM:(head + 1) * HEAD_DIM]
            tops.append(jnp.concatenate([qh, zeros_h] if g == 0 else [zeros_h, qh], axis=0))
        q_top.append(jnp.concatenate(tops, axis=1))
        slope = lax.bitcast_convert_type((127 - (NSA_HPG * g + hl + 1)) << 23, F32)
        rows = {}
        for i, term in enumerate(l2e):
            rows[AUG_POS + 2 * i] = slope * (64.0 * term)
            rows[AUG_POS + 2 * i + 1] = slope * term
        for i, term in enumerate(_split3(-(slope * tqf) * LOG2E)):
            rows[AUG_TQ + i] = term.astype(F32)
        a = jnp.zeros((16, NSA_N), F32)
        for r, val in rows.items():
            a = jnp.where(ridx == r, val, a)
        aug16.append(a)
    return q_top, aug16


def _nsa_compressed(s_cmp, blk, vct_ref, ovt, gs):
    tq = blk * NSA_TQ + _iota((1, NSA_N), 1) % NSA_TQ
    n_idx = _iota((LANES, NSA_N), 0)
    valid = (n_idx * CMP_STRIDE + (CMP_BLOCK - 1) <= tq) & (n_idx < LANES - 1)
    o_cmp, imps = [], []
    for g in gs:
        sm = jnp.where(valid, s_cmp[g], NEG_BIG)
        mx = jnp.max(sm, axis=0, keepdims=True)
        e = jnp.where(valid, jnp.exp2(sm - mx), 0.0)
        den = jnp.sum(e, axis=0, keepdims=True)
        pn = e / jnp.where(den > 0.0, den, 1.0)
        o_cmp.append(_dot(vct_ref[g * HEAD_DIM:(g + 1) * HEAD_DIM, :], pn.astype(BF16)))

        psum = pn[:, 0:NSA_TQ]
        for h in range(1, NSA_HPG):
            psum = psum + pn[:, h * NSA_TQ:(h + 1) * NSA_TQ]
        hi, mid, lo = _split3(psum)
        imps.append(_dot(ovt, hi) + _dot(ovt, mid) + _dot(ovt, lo))
    return o_cmp, imps


def _nsa_select(imps, blk, q_top, aug16, gs):
    t0 = blk * NSA_TQ
    ns = imps[0].shape[0]
    tcol = t0 + _iota((8, NSA_TQ), 1)
    q_sel = []
    for g, imp in zip(gs, imps):
        tiles = range(ns // 8)
        j_idx = [8 * r + _iota((8, NSA_TQ), 0) for r in tiles]
        in_past = [j_idx[r] * SEL_BLOCK <= tcol for r in tiles]
        cur = tcol // SEL_BLOCK
        score = []
        for r in tiles:
            forced = (j_idx[r] == 0) | (j_idx[r] == cur) | (j_idx[r] == cur - 1)
            score.append(jnp.where(forced, FORCE_SCORE, jnp.where(in_past[r], imp[8 * r:8 * r + 8], -1.0)))
        rank = [jnp.zeros((8, NSA_TQ), F32) for _ in tiles]
        for i in range(ns):
            si = score[i // 8][i % 8:i % 8 + 1, :]
            for r in tiles:
                if 8 * r > i:
                    beats = si >= score[r]
                elif 8 * r + 7 <= i:
                    beats = si > score[r]
                else:
                    beats = (si > score[r]) | ((si == score[r]) & (j_idx[r] > i))
                rank[r] = rank[r] + jnp.where(beats, 1.0, 0.0)
        maskval = jnp.concatenate([jnp.where((rank[r] < min(N_SELECT, ns)) & in_past[r], 0.0, NEG_BIG) for r in tiles], axis=0)
        mask4 = jnp.concatenate([maskval] * NSA_HPG, axis=1)
        q_sel.append(jnp.concatenate([q_top[g], aug16[g], jnp.zeros((AUG_MASK - 16, NSA_N), F32), mask4,
                                      jnp.zeros((LANES - AUG_MASK - ns, NSA_N), F32)], axis=0).astype(BF16))
    return q_sel


def _nsa_kernel(q_ref, gf_ref, kc_ref, vc_ref, ks_ref, vs_ref, kw_ref, vw_ref,
                augk_ref, augw_ref, augc_ref, ovt_ref, trisel_ref, triwin_ref, o_ref,
                vst_ref, vwt_ref, kwp_ref, vct_ref, gt_ref, m_ref, acc_ref, s0_ref, s1_ref, mx0_ref, mx1_ref):
    qb = pl.program_id(1)
    t = ks_ref.shape[1]
    t0 = qb * NSA_TQ
    groups = range(NSA_GROUPS)
    zeros_tail = jnp.zeros((LANES - 16, NSA_N), F32)

    def cmp_scores(q_top, aug16):
        kcb = jnp.concatenate([kc_ref[0], augc_ref[...]], axis=1)
        return [_dot(kcb, jnp.concatenate([q_top[g], aug16[g], zeros_tail], axis=0).astype(BF16)) for g in groups]

    @pl.when(qb == 0)
    def _():
        vwt_ref[:, 0:WINDOW] = jnp.zeros((LANES, WINDOW), BF16)
        kwp_ref[0:WINDOW, :] = jnp.zeros((WINDOW, LANES), BF16)
        kwp_ref[WINDOW:WINDOW + t, :] = kw_ref[0]
        for c in range(t // LANES):
            sl = slice(c * LANES, (c + 1) * LANES)
            vst_ref[:, sl] = vs_ref[0, sl, :].astype(F32).T.astype(BF16)
            vwt_ref[:, WINDOW + c * LANES:WINDOW + (c + 1) * LANES] = vw_ref[0, sl, :].astype(F32).T.astype(BF16)
        vct_ref[...] = vc_ref[0].astype(F32).T.astype(BF16)

    q_top, aug16 = _nsa_queries(q_ref, qb)
    s_cmp = cmp_scores(q_top, aug16)
    kw0 = pl.multiple_of(t0, NSA_TQ)
    kwin = jnp.concatenate([kwp_ref[pl.ds(kw0, WIN_KEYS), :], augw_ref[pl.ds(kw0, WIN_KEYS), :]], axis=1)
    pad_row = _iota((16, NSA_N), 0) == AUG_PAD
    s_win = [_dot(kwin, jnp.concatenate([q_top[g], jnp.where(pad_row, NEG_BIG, aug16[g]), zeros_tail], axis=0).astype(BF16))
             for g in groups]
    o_cmp, imps = _nsa_compressed(s_cmp, qb, vct_ref, ovt_ref[...], groups)
    q_sel = _nsa_select(imps, qb, q_top, aug16, groups)

    hi_edge = jnp.concatenate([triwin_ref[0]] * NSA_HPG, axis=1)
    lo_edge = jnp.concatenate([triwin_ref[1]] * NSA_HPG, axis=1)
    ones_w = _ones_rows(WIN_KEYS)
    s_win = [jnp.concatenate([s[0:NSA_TQ] + lo_edge, s[NSA_TQ:WINDOW], s[WINDOW:WIN_KEYS] + hi_edge], axis=0)
             for s in s_win]
    vt_win = [jnp.concatenate([vwt_ref[g * HEAD_DIM:(g + 1) * HEAD_DIM, pl.ds(kw0, WIN_KEYS)], ones_w], axis=0)
              for g in groups]
    o_win = [acc[0:HEAD_DIM] / acc[HEAD_DIM:HEAD_DIM + 1] for acc in _softmax_pv(s_win, vt_win)]

    ones_s = _ones_rows(SEL_TK)
    per_chunk = SEL_TK // NSA_TQ
    n_full = qb // per_chunk
    s_bufs = (s0_ref, s1_ref)
    mx_bufs = (mx0_ref, mx1_ref)

    def sel_scores(cb):
        k0 = pl.multiple_of(cb * SEL_TK, SEL_TK)
        kblk = jnp.concatenate([ks_ref[0, pl.ds(k0, SEL_TK), :], augk_ref[pl.ds(k0, SEL_TK), :]], axis=1)
        return [_dot(kblk, q_sel[g]) for g in groups]

    def qk_to(buf, j):
        for g, s in enumerate(sel_scores(j - 1)):
            s_bufs[buf][g] = s
            mx_bufs[buf][g] = jnp.max(s, axis=0, keepdims=True)

    def spv_from(buf, j):
        cb = jnp.where(j == 0, n_full, j - 1)
        k0 = pl.multiple_of(cb * SEL_TK, SEL_TK)
        vts = [jnp.concatenate([vst_ref[g * HEAD_DIM:(g + 1) * HEAD_DIM, pl.ds(k0, SEL_TK)], ones_s], axis=0)
               for g in groups]
        _softmax_pv([s_bufs[buf][g] for g in groups], vts,
                    [m_ref.at[g] for g in groups], [acc_ref.at[g] for g in groups],
                    maxes=[mx_bufs[buf][g] for g in groups])

    diag = trisel_ref[qb % per_chunk]
    diag4 = jnp.concatenate([diag] * NSA_HPG, axis=1)
    for g, s in enumerate(sel_scores(n_full)):
        s = s + diag4
        s0_ref[g] = s
        mx0_ref[g] = jnp.max(s, axis=0, keepdims=True)
        _flash_init(m_ref.at[g], acc_ref.at[g])
    _pipelined_sweep(n_full, qk_to, spv_from)

    gt_ref[...] = gf_ref[0].T
    sig = jax.nn.sigmoid(gt_ref[GF_GATE:GF_GATE + 3 * NSA_HEADS, :])
    outs = []
    for g in groups:
        acc = acc_ref[g]
        o_sel = acc[0:HEAD_DIM] / acc[HEAD_DIM:HEAD_DIM + 1]
        for h in range(NSA_HPG):
            sl = slice(h * NSA_TQ, (h + 1) * NSA_TQ)
            head = NSA_HPG * g + h
            g0 = sig[head:head + 1]
            g1 = sig[NSA_HEADS + head:NSA_HEADS + head + 1]
            g2 = sig[2 * NSA_HEADS + head:2 * NSA_HEADS + head + 1]
            outs.append(g0 * o_cmp[g][:, sl] + g1 * o_sel[:, sl] + g2 * o_win[g][:, sl])
    o_ref[0] = jnp.concatenate(outs, axis=0).T.astype(BF16)


def _nsa(proj, gf, kcmp, vcmp, augk, augw, augc, ovt, trisel, triwin):
    b, t, _ = proj.shape
    nq = t // NSA_TQ
    kvb = COL_KV // LANES
    width = NSA_HEADS * HEAD_DIM
    whole = lambda a: pl.BlockSpec(a.shape, lambda i, k: (0,) * a.ndim)
    kv = lambda idx: pl.BlockSpec((1, t, LANES), lambda i, k: (i, 0, idx))
    return pl.pallas_call(
        _nsa_kernel,
        grid=(b, nq),
        in_specs=[
            pl.BlockSpec((1, NSA_TQ, width), lambda i, k: (i, k, COL_QN // width)),
            pl.BlockSpec((1, NSA_TQ, LANES), lambda i, k: (i, k, 0)),
            pl.BlockSpec((1, LANES, LANES), lambda i, k: (i, 0, 0)),
            pl.BlockSpec((1, LANES, LANES), lambda i, k: (i, 0, 0)),
            kv(kvb + 2), kv(kvb + 3), kv(kvb + 4), kv(kvb + 5),
            whole(augk), whole(augw), whole(augc), whole(ovt), whole(trisel), whole(triwin),
        ],
        out_specs=pl.BlockSpec((1, NSA_TQ, width), lambda i, k: (i, k, 0)),
        out_shape=jax.ShapeDtypeStruct((b, t, width), BF16),
        scratch_shapes=[
            pltpu.VMEM((LANES, t), BF16),
            pltpu.VMEM((LANES, t + WINDOW), BF16),
            pltpu.VMEM((t + WINDOW, LANES), BF16),
            pltpu.VMEM((LANES, LANES), BF16),
            pltpu.VMEM((LANES, NSA_TQ), F32),
            pltpu.VMEM((NSA_GROUPS, 1, NSA_N), F32),
            pltpu.VMEM((NSA_GROUPS, HEAD_DIM + 16, NSA_N), F32),
            pltpu.VMEM((NSA_GROUPS, SEL_TK, NSA_N), F32),
            pltpu.VMEM((NSA_GROUPS, SEL_TK, NSA_N), F32),
            pltpu.VMEM((NSA_GROUPS, 1, NSA_N), F32),
            pltpu.VMEM((NSA_GROUPS, 1, NSA_N), F32),
        ],
        compiler_params=pltpu.CompilerParams(
            dimension_semantics=("arbitrary", "arbitrary"), vmem_limit_bytes=VMEM_LIMIT),
        name="nsa_attention",
    )(proj, gf, kcmp, vcmp, proj, proj, proj, proj, augk, augw, augc, ovt, trisel, triwin)


FOX_CBLK = 256


def _foxprep_kernel(gf_ref, bias_ref, cq_ref, augk_ref):
    t = gf_ref.shape[1]
    cols = []
    for c in range(t // LANES):
        cols.append(gf_ref[0, c * LANES:(c + 1) * LANES, :].T[GF_FOX:GF_FOX + FOX_HEADS])
    f = jnp.concatenate(cols, axis=1) + bias_ref[...]
    ls = jnp.minimum(f, 0.0) - jnp.log(1.0 + jnp.exp(-jnp.abs(f)))
    upper = (_iota((FOX_CBLK, FOX_CBLK), 0) <= _iota((FOX_CBLK, FOX_CBLK), 1)).astype(BF16)
    carry = jnp.zeros((FOX_HEADS, 1), F32)
    parts = []
    for c in range(t // FOX_CBLK):
        hi, mid, lo = _split3(ls[:, c * FOX_CBLK:(c + 1) * FOX_CBLK])
        cb = _dot(hi, upper) + _dot(mid, upper) + _dot(lo, upper) + carry
        carry = cb[:, FOX_CBLK - 1:FOX_CBLK]
        parts.append(cb)
    csum = jnp.concatenate(parts, axis=1) * LOG2E
    c1, c2, c3 = (v.astype(F32) for v in _split3(csum))
    ones = jnp.ones((3 * FOX_HEADS, t), F32)
    zeros = jnp.zeros((LANES - 6 * FOX_HEADS, t), F32)
    cq_ref[0] = jnp.concatenate([c1, c2, c3, ones, zeros], axis=0).astype(BF16)
    slab_k = jnp.concatenate([ones, -c1, -c2, -c3, zeros], axis=0)
    for c in range(t // LANES):
        sl = slice(c * LANES, (c + 1) * LANES)
        augk_ref[0, sl, :] = slab_k[:, sl].T.astype(BF16)


def _foxprep(gf, bias):
    b, t, _ = gf.shape
    return pl.pallas_call(
        _foxprep_kernel,
        grid=(b,),
        in_specs=[pl.BlockSpec((1, t, LANES), lambda i: (i, 0, 0)),
                  pl.BlockSpec(bias.shape, lambda i: (0, 0))],
        out_specs=[pl.BlockSpec((1, LANES, t), lambda i: (i, 0, 0)),
                   pl.BlockSpec((1, t, LANES), lambda i: (i, 0, 0))],
        out_shape=[jax.ShapeDtypeStruct((b, LANES, t), BF16),
                   jax.ShapeDtypeStruct((b, t, LANES), BF16)],
        compiler_params=pltpu.CompilerParams(dimension_semantics=("arbitrary",)),
        name="fox_prep",
    )(gf, bias)


FOX_TQ = 256
FOX_TK = 256
FOX_NP = FOX_HEADS // 2
FOX_N = 2 * FOX_TQ
FOX_ROWS = HEAD_DIM + 16


def _fox_kernel(q_ref, k_ref, v_ref, cq_ref, augk_ref, tri_ref, o_ref, vt_ref, qs_ref, m_ref, acc_ref,
                s0_ref, s1_ref, sd_ref, mx0_ref, mx1_ref, mxd_ref):
    qb = pl.program_id(1)
    nq = pl.num_programs(1)
    t = k_ref.shape[1]
    pairs = range(FOX_NP)
    s_bufs = (s0_ref, s1_ref, sd_ref)
    mx_bufs = (mx0_ref, mx1_ref, mxd_ref)
    diag = 2

    def scores(kb, qs):
        k0 = pl.multiple_of(kb * FOX_TK, FOX_TK)
        ak = augk_ref[0, pl.ds(k0, FOX_TK), :]
        return [_dot(jnp.concatenate([k_ref[0, pl.ds(k0, FOX_TK), p * LANES:(p + 1) * LANES], ak], axis=1), qs(p))
                for p in pairs]

    def prepare(blk, slot):
        q0 = pl.multiple_of(blk * FOX_TQ, FOX_TQ)
        row = _iota((LANES, FOX_TQ), 0)
        hrow = row % FOX_HEADS
        cq = cq_ref[0, :, pl.ds(q0, FOX_TQ)].astype(F32)
        for p in pairs:
            qt = q_ref[0, pl.ds(q0, FOX_TQ), p * LANES:(p + 1) * LANES].astype(F32).T * (HEAD_DIM ** -0.5 * LOG2E)
            q_top = jnp.concatenate([jnp.where(row < HEAD_DIM, qt, 0.0), jnp.where(row >= HEAD_DIM, qt, 0.0)], axis=1)
            q_aug = jnp.concatenate([jnp.where(hrow == 2 * p, cq, 0.0), jnp.where(hrow == 2 * p + 1, cq, 0.0)],
                                    axis=1)
            qs_ref[slot, p] = jnp.concatenate([q_top, q_aug], axis=0).astype(BF16)
        tri = tri_ref[...]
        for p, s in enumerate(scores(blk, lambda p: qs_ref[slot, p])):
            s = s + tri
            sd_ref[p] = s
            mxd_ref[p] = jnp.max(s, axis=0, keepdims=True)

    def qk_to(buf, kb):
        for p, s in enumerate(scores(kb, lambda p: qs_ref[qb % 2, p])):
            s_bufs[buf][p] = s
            mx_bufs[buf][p] = jnp.max(s, axis=0, keepdims=True)

    def spv_from(buf, kb):
        k0 = pl.multiple_of(kb * FOX_TK, FOX_TK)
        vts = [(vt_ref[p, 0, :, pl.ds(k0, FOX_TK)], vt_ref[p, 1, :, pl.ds(k0, FOX_TK)]) for p in pairs]
        _softmax_pv([s_bufs[buf][p] for p in pairs], vts,
                    [m_ref.at[p] for p in pairs], [acc_ref.at[p] for p in pairs],
                    maxes=[mx_bufs[buf][p] for p in pairs])

    def finish():
        for p in pairs:
            acc = acc_ref[p]
            den = acc[HEAD_DIM:HEAD_DIM + 1]
            o_a = acc[0:HEAD_DIM, 0:FOX_TQ] / den[:, 0:FOX_TQ]
            o_b = acc[0:HEAD_DIM, FOX_TQ:FOX_N] / den[:, FOX_TQ:FOX_N]
            o_ref[0, :, p * LANES:(p + 1) * LANES] = jnp.concatenate([o_a, o_b], axis=0).T.astype(BF16)
        prepare(jnp.minimum(qb + 1, nq - 1), (qb + 1) % 2)

    @pl.when(qb == 0)
    def _():
        for p in pairs:
            for c in range(t // LANES):
                sl = slice(c * LANES, (c + 1) * LANES)
                vt = v_ref[0, sl, p * LANES:(p + 1) * LANES].astype(F32).T.astype(BF16)
                for hh in range(2):
                    vt_ref[p, hh, 0:HEAD_DIM, sl] = vt[hh * HEAD_DIM:(hh + 1) * HEAD_DIM]
            for hh in range(2):
                vt_ref[p, hh, HEAD_DIM:FOX_ROWS, :] = _ones_rows(t)
        prepare(0, 0)

    for p in pairs:
        _flash_init(m_ref.at[p], acc_ref.at[p])

    @pl.when(qb == 0)
    def _():
        spv_from(diag, qb)
        finish()

    @pl.when(qb > 0)
    def _():
        qk_to(0, 0)
        spv_from(diag, qb)

    def body(jj, carry):
        kb = 2 * jj
        qk_to(1, kb + 1)
        spv_from(0, kb)
        qk_to(0, kb + 2)
        spv_from(1, kb + 1)
        return carry

    lax.fori_loop(0, jnp.maximum(qb - 1, 0) // 2, body, 0)

    @pl.when(qb % 2 == 1)
    def _():
        spv_from(0, qb - 1)
        finish()

    @pl.when((qb > 0) & (qb % 2 == 0))
    def _():
        qk_to(1, qb - 1)
        spv_from(0, qb - 2)
        spv_from(1, qb - 1)
        finish()


def _fox(proj, cq, augk, tri):
    b, t, _ = proj.shape
    nq = t // FOX_TQ
    width = FOX_HEADS * HEAD_DIM
    qf, kf, vf = COL_QF // width, COL_KF // width, COL_VF // width
    return pl.pallas_call(
        _fox_kernel,
        grid=(b, nq),
        in_specs=[
            pl.BlockSpec((1, t, width), lambda i, k: (i, 0, qf)),
            pl.BlockSpec((1, t, width), lambda i, k: (i, 0, kf)),
            pl.BlockSpec((1, t, width), lambda i, k: (i, 0, vf)),
            pl.BlockSpec((1, LANES, t), lambda i, k: (i, 0, 0)),
            pl.BlockSpec((1, t, LANES), lambda i, k: (i, 0, 0)),
            pl.BlockSpec(tri.shape, lambda i, k: (0, 0)),
        ],
        out_specs=pl.BlockSpec((1, FOX_TQ, width), lambda i, k: (i, k, 0)),
        out_shape=jax.ShapeDtypeStruct((b, t, width), BF16),
        scratch_shapes=[
            pltpu.VMEM((FOX_NP, 2, FOX_ROWS, t), BF16),
            pltpu.VMEM((2, FOX_NP, 2 * LANES, FOX_N), BF16),
            pltpu.VMEM((FOX_NP, 1, FOX_N), F32),
            pltpu.VMEM((FOX_NP, FOX_ROWS, FOX_N), F32),
            pltpu.VMEM((FOX_NP, FOX_TK, FOX_N), F32),
            pltpu.VMEM((FOX_NP, FOX_TK, FOX_N), F32),
            pltpu.VMEM((FOX_NP, FOX_TK, FOX_N), F32),
            pltpu.VMEM((FOX_NP, 1, FOX_N), F32),
            pltpu.VMEM((FOX_NP, 1, FOX_N), F32),
            pltpu.VMEM((FOX_NP, 1, FOX_N), F32),
        ],
        compiler_params=pltpu.CompilerParams(
            dimension_semantics=("arbitrary", "arbitrary"), vmem_limit_bytes=VMEM_LIMIT),
        name="fox_attention",
    )(proj, proj, proj, cq, augk, tri)


def _post_kernel(x_ref, on_ref, of_ref, mg_ref, wn_ref, wf_ref, wo_ref, gm_ref, wu_ref, wd_ref, gfin_ref, o_ref):
    d = x_ref.shape[1]
    ff = wu_ref.shape[1]
    ga = mg_ref[:, 0:d].astype(F32)
    gb = mg_ref[:, d:2 * d].astype(F32)
    y = ga * _dot(on_ref[...], wn_ref[...]) + gb * _dot(of_ref[...], wf_ref[...])
    h = x_ref[...] + _dot(y.astype(BF16), wo_ref[...])
    r = lax.rsqrt(jnp.mean(h * h, axis=-1, keepdims=True) + NORM_EPS)
    v = (h * r * gm_ref[...]).astype(BF16)
    acc = h
    for c0 in range(0, ff, FF_CHUNK):
        up = jnp.maximum(_dot(v, wu_ref[:, c0:c0 + FF_CHUNK]), 0.0)
        acc = acc + _dot((up * up).astype(BF16), wd_ref[c0:c0 + FF_CHUNK, :])
    r2 = lax.rsqrt(jnp.mean(acc * acc, axis=-1, keepdims=True) + NORM_EPS)
    o_ref[...] = acc * r2 * gfin_ref[...]


def _post(x2, on, of, mg, wn, wf, wo, gm, wu, wd, gfin, tm):
    n, d = x2.shape
    row = lambda w: pl.BlockSpec((tm, w), lambda i: (i, 0))
    const = lambda a: pl.BlockSpec(a.shape, lambda i: (0, 0), pipeline_mode=pl.Buffered(1))
    return pl.pallas_call(
        _post_kernel,
        grid=(n // tm,),
        in_specs=[row(d), row(on.shape[1]), row(of.shape[1]), row(2 * d),
                  const(wn), const(wf), const(wo), const(gm), const(wu), const(wd), const(gfin)],
        out_specs=row(d),
        out_shape=jax.ShapeDtypeStruct((n, d), F32),
        compiler_params=pltpu.CompilerParams(
            dimension_semantics=("arbitrary",), vmem_limit_bytes=VMEM_LIMIT),
        name="post_mlp",
    )(x2, on, of, mg, wn, wf, wo, gm, wu, wd, gfin)


def _position_constants(t):
    def key_side(pos, pad):
        a = np.zeros((pos.size, LANES), np.float32)
        for i in range(3):
            a[:, AUG_POS + 2 * i] = pos // 64
            a[:, AUG_POS + 2 * i + 1] = pos % 64
        a[:, AUG_TQ:AUG_TQ + 3] = 1.0
        a[:, AUG_PAD] = pad
        return a

    pos = np.arange(t)
    augk = key_side(pos, 0.0)
    augk[pos, AUG_MASK + pos // SEL_BLOCK] = 1.0
    augw = np.concatenate([key_side(np.zeros(WINDOW, np.int64), 1.0), key_side(pos, 0.0)], axis=0)
    augc = key_side(np.arange(LANES) * CMP_STRIDE + CMP_BLOCK - 1, 0.0)

    nc = (t - CMP_BLOCK) // CMP_STRIDE + 1
    ns = t // SEL_BLOCK
    n = np.arange(LANES)[None, :]
    j = np.arange(ns)[:, None]
    ovt = ((n * CMP_STRIDE <= j * SEL_BLOCK + SEL_BLOCK - 1) & (n * CMP_STRIDE + CMP_BLOCK - 1 >= j * SEL_BLOCK) & (n < nc))

    def tri(tk, tq, anti=False):
        kk = np.arange(tk)[:, None]
        tt = np.arange(tq)[None, :]
        ok = (kk > tt) if anti else (kk <= tt)
        return np.where(ok, 0.0, NEG_BIG).astype(np.float32)

    edge = tri(NSA_TQ, NSA_TQ)
    trisel = np.zeros((SEL_TK // NSA_TQ, SEL_TK, NSA_TQ), np.float32)
    for v in range(SEL_TK // NSA_TQ):
        trisel[v, v * NSA_TQ:(v + 1) * NSA_TQ] = edge
    triwin = np.stack([edge, tri(NSA_TQ, NSA_TQ, anti=True)])
    trifox = np.tile(tri(FOX_TK, FOX_TQ), (1, 2))
    return (jnp.asarray(augk, BF16), jnp.asarray(augw, BF16), jnp.asarray(augc, BF16), jnp.asarray(ovt, BF16),
            jnp.asarray(trisel), jnp.asarray(triwin), jnp.asarray(trifox))


def kernel(x, norm_mix, w_in, cmp_pos_k, cmp_w1_k, cmp_w2_k, cmp_pos_v, cmp_w1_v, cmp_w2_v, fox_f_bias, w_branch_nsa, w_branch_fox, w_merge_gate, b_merge_gate, w_out, norm_mlp, w_up, w_down, norm_final):
    b, t, d = x.shape
    assert w_in.shape[0] == 1, "one layer: the final norm is fused into the MLP kernel"
    assert t % FOX_TQ == 0 and t // SEL_BLOCK == 32 and (t - CMP_BLOCK) // CMP_STRIDE + 1 == LANES - 1
    augk, augw, augc, ovt, trisel, triwin, trifox = _position_constants(t)
    h = x.reshape(b * t, d)
    proj, gf, mg, wn, wf, wo, wu, wd = _inproj(h, norm_mix[0][None, :], jnp.swapaxes(w_in, 1, 2), w_merge_gate,
                                               b_merge_gate[0][None, :],
                                               (w_branch_nsa, w_branch_fox, w_out, w_up, w_down), ROW_TILE)
    proj = proj.reshape(b, t, PROJ_W)
    gf = gf.reshape(b, t, LANES)

    def w1_blocks(w1):
        wa_ = w1[:CMP_STRIDE * HEAD_DIM].reshape(CMP_STRIDE, HEAD_DIM, CMP_HIDDEN)
        wb_ = w1[CMP_STRIDE * HEAD_DIM:].reshape(CMP_STRIDE, HEAD_DIM, CMP_HIDDEN)
        z = jnp.zeros_like(wa_)
        top = jnp.concatenate([wa_, z, wb_, z], axis=2)
        bot = jnp.concatenate([z, wa_, z, wb_], axis=2)
        return jnp.concatenate([top, bot], axis=1).astype(BF16)

    def w2_pair(w2):
        z = jnp.zeros_like(w2)
        return (jnp.concatenate([w2, z], axis=1).astype(BF16), jnp.concatenate([z, w2], axis=1).astype(BF16))

    def pos_rows(p):
        row = jnp.concatenate([p[:CMP_STRIDE], p[CMP_STRIDE:]], axis=1)[:, None, :]
        return jnp.concatenate([row, jnp.zeros((CMP_STRIDE, 15, 2 * HEAD_DIM), F32)], axis=1).astype(BF16)

    w2k0, w2k1 = w2_pair(cmp_w2_k[0])
    w2v0, w2v1 = w2_pair(cmp_w2_v[0])
    kcmp, vcmp = _compress(proj, w1_blocks(cmp_w1_k[0]), w2k0, w2k1, pos_rows(cmp_pos_k[0]),
                           w1_blocks(cmp_w1_v[0]), w2v0, w2v1, pos_rows(cmp_pos_v[0]))
    o_n = _nsa(proj, gf, kcmp, vcmp, augk, augw, augc, ovt, trisel, triwin)
    cq, faug = _foxprep(gf, fox_f_bias[0][:, None])
    o_f = _fox(proj, cq, faug, trifox)
    out = _post(h, o_n.reshape(b * t, -1), o_f.reshape(b * t, -1), mg,
                wn, wf, wo, norm_mlp[0][None, :], wu, wd, norm_final[None, :], ROW_TILE)
    return out.reshape(b, t, d)
```

```python
import numpy as np
import jax
import jax.numpy as jnp
from jax import lax
from jax.experimental import pallas as pl
from jax.experimental.pallas import tpu as pltpu

HEAD_DIM = 64
NSA_HEADS = 8
NSA_GROUPS = 2
NSA_HPG = NSA_HEADS // NSA_GROUPS
FOX_HEADS = 8
CMP_BLOCK = 32
CMP_STRIDE = 16
CMP_HIDDEN = 128
SEL_BLOCK = 64
N_SELECT = 16
WINDOW = 512
NORM_EPS = 1e-6
FORCE_SCORE = 1e9

LANES = 128
NEG_BIG = -(2.0 ** 100)
M_INIT = -3.0e38
LOG2E = 1.4426950408889634
VMEM_LIMIT = 56 * 1024 * 1024
ROW_TILE = 512
ROW_SPLIT = 2
FF_CHUNK = 1024

PROJ_W = 2816
COL_QN = 0
COL_QF = 512
COL_KF = 1024
COL_VF = 1536
COL_KV = 2048
GF_GATE = 0
GF_FOX = 24

F32 = jnp.float32
BF16 = jnp.bfloat16


def _dot(a, b):
    return jnp.dot(a, b, preferred_element_type=F32)


def _iota(shape, dim):
    return lax.broadcasted_iota(jnp.int32, shape, dim)


def _split3(x):
    hi = x.astype(BF16)
    r1 = x - hi.astype(F32)
    mid = r1.astype(BF16)
    lo = (r1 - mid.astype(F32)).astype(BF16)
    return hi, mid, lo


W_QN = NSA_HEADS * HEAD_DIM
W_KV = 6 * NSA_GROUPS * HEAD_DIM
W_GATE = 3 * NSA_HEADS
W_FOX = 3 * FOX_HEADS * HEAD_DIM
W_PIECES = ((0, W_QN), (W_QN + W_KV + W_GATE, W_FOX), (W_QN, W_KV))


def _inproj_kernel(x_ref, g_ref, wt_ref, wm32_ref, bm_ref, *refs):
    n_later = (len(refs) - 8) // 2
    later_in, (proj_ref, gf_ref, mg_ref) = refs[:n_later], refs[n_later:n_later + 3]
    later_out = refs[n_later + 3:2 * n_later + 3]
    wq_ref, wf_ref, wkv_ref, wg_ref, wm_ref = refs[2 * n_later + 3:]
    w_refs = (wq_ref, wf_ref, wkv_ref)

    for src, dst in zip(later_in, later_out):
        dst[...] = src[0].astype(BF16)

    @pl.when(pl.program_id(0) == 0)
    def _():
        d = wt_ref.shape[2]
        for (c0, width), dst in zip(W_PIECES, w_refs):
            for k0 in range(0, width, 256):
                dst[:, k0:k0 + 256] = wt_ref[0, c0 + k0:c0 + k0 + 256, :].T.astype(BF16)
        gate0 = W_QN + W_KV
        fox0 = gate0 + W_GATE + W_FOX
        gates = jnp.concatenate([wt_ref[0, gate0:gate0 + W_GATE, :], wt_ref[0, fox0:fox0 + FOX_HEADS, :],
                                 jnp.zeros((LANES - W_GATE - FOX_HEADS, d), F32)], axis=0)
        wg_ref[...] = gates.T.astype(BF16)
        for r0 in range(0, d, 256):
            wm_ref[r0:r0 + 256, :] = wm32_ref[0, r0:r0 + 256, :].astype(BF16)

    sub = x_ref.shape[0] // ROW_SPLIT
    for h in range(ROW_SPLIT):
        rows = slice(h * sub, (h + 1) * sub)
        x = x_ref[rows, :]
        r = lax.rsqrt(jnp.mean(x * x, axis=-1, keepdims=True) + NORM_EPS)
        u = (x * r * g_ref[...]).astype(BF16)
        c0 = 0
        for piece in w_refs:
            for k0 in range(0, piece.shape[1], 768):
                k1 = min(k0 + 768, piece.shape[1])
                proj_ref[rows, c0 + k0:c0 + k1] = _dot(u, piece[:, k0:k1]).astype(BF16)
            c0 += piece.shape[1]
        gf_ref[rows, :] = _dot(u, wg_ref[...])
        for c0 in (0, 1024):
            z = _dot(u, wm_ref[:, c0:c0 + 1024]) + bm_ref[:, c0:c0 + 1024]
            mg_ref[rows, c0:c0 + 1024] = jax.nn.sigmoid(z).astype(BF16)


def _inproj(x2, g, w_in_t, wm, bm, later, tm):
    n, d = x2.shape
    steps = n // tm
    const = lambda i: (0, 0)
    once = dict(pipeline_mode=pl.Buffered(1))
    slabs = [w.shape[1] // steps for w in later]
    assert all(r * steps == w.shape[1] and r % 16 == 0 for w, r in zip(later, slabs)), "bf16 slabs need 16-row multiples"
    return pl.pallas_call(
        _inproj_kernel,
        grid=(n // tm,),
        in_specs=[
            pl.BlockSpec((tm, d), lambda i: (i, 0)),
            pl.BlockSpec((1, d), const),
            pl.BlockSpec(w_in_t.shape, lambda i: (0, 0, 0), **once),
            pl.BlockSpec(wm.shape, lambda i: (0, 0, 0), **once),
            pl.BlockSpec(bm.shape, const),
        ] + [pl.BlockSpec((1, r, w.shape[2]), lambda i: (0, i, 0)) for w, r in zip(later, slabs)],
        out_specs=[
            pl.BlockSpec((tm, PROJ_W), lambda i: (i, 0)),
            pl.BlockSpec((tm, LANES), lambda i: (i, 0)),
            pl.BlockSpec((tm, 2 * d), lambda i: (i, 0)),
        ] + [pl.BlockSpec((r, w.shape[2]), lambda i: (i, 0)) for w, r in zip(later, slabs)],
        out_shape=[
            jax.ShapeDtypeStruct((n, PROJ_W), BF16),
            jax.ShapeDtypeStruct((n, LANES), F32),
            jax.ShapeDtypeStruct((n, 2 * d), BF16),
        ] + [jax.ShapeDtypeStruct(w.shape[1:], BF16) for w in later],
        scratch_shapes=[pltpu.VMEM((d, width), BF16) for _, width in W_PIECES]
        + [pltpu.VMEM((d, LANES), BF16), pltpu.VMEM((d, 2 * d), BF16)],
        compiler_params=pltpu.CompilerParams(
            dimension_semantics=("arbitrary",), vmem_limit_bytes=VMEM_LIMIT),
        name="inproj",
    )(x2, g, w_in_t, wm, bm, *later)


def _compress_kernel(kc_ref, vc_ref, wk_ref, w2k0_ref, w2k1_ref, pk_ref,
                     wv_ref, w2v0_ref, w2v1_ref, pv_ref, kco_ref, vco_ref, xs_ref):
    t = kc_ref.shape[1]
    nblk = t // CMP_STRIDE

    def one(x_ref, w_ref, w2_refs, p_ref, o_ref):
        xs_ref[...] = x_ref[0].astype(F32)
        acc = jnp.zeros((nblk + 16, 4 * LANES), F32)
        for j in range(CMP_STRIDE):
            xj = xs_ref[pl.ds(j, nblk, stride=CMP_STRIDE), :].astype(BF16)
            acc = acc + _dot(jnp.concatenate([xj, p_ref[j]], axis=0), w_ref[j])
        posb = acc[nblk:nblk + 1, 0:LANES] + acc[nblk:nblk + 1, 3 * LANES:4 * LANES]
        out = jnp.zeros((nblk, LANES), F32)
        for g in range(NSA_GROUPS):
            a = acc[0:nblk, g * LANES:(g + 1) * LANES]
            b = acc[0:nblk, (2 + g) * LANES:(3 + g) * LANES]
            h = a + pltpu.roll(b, nblk - 1, 0) + posb
            out = out + _dot(jax.nn.gelu(h).astype(BF16), w2_refs[g][...])
        o_ref[0] = out.astype(BF16)

    one(kc_ref, wk_ref, (w2k0_ref, w2k1_ref), pk_ref, kco_ref)
    one(vc_ref, wv_ref, (w2v0_ref, w2v1_ref), pv_ref, vco_ref)


def _compress(proj, wk, w2k0, w2k1, pk, wv, w2v0, w2v1, pv):
    b, t, _ = proj.shape
    kvb = COL_KV // LANES
    whole = lambda a: pl.BlockSpec(a.shape, lambda i: (0,) * a.ndim)
    ospec = pl.BlockSpec((1, LANES, LANES), lambda i: (i, 0, 0))
    return pl.pallas_call(
        _compress_kernel,
        grid=(b,),
        in_specs=[pl.BlockSpec((1, t, LANES), lambda i: (i, 0, kvb)),
                  pl.BlockSpec((1, t, LANES), lambda i: (i, 0, kvb + 1)),
                  whole(wk), whole(w2k0), whole(w2k1), whole(pk), whole(wv), whole(w2v0), whole(w2v1), whole(pv)],
        out_specs=[ospec, ospec],
        out_shape=[jax.ShapeDtypeStruct((b, LANES, LANES), BF16)] * 2,
        scratch_shapes=[pltpu.VMEM((t, LANES), F32)],
        compiler_params=pltpu.CompilerParams(dimension_semantics=("arbitrary",)),
        name="compress",
    )(proj, proj, wk, w2k0, w2k1, pk, wv, w2v0, w2v1, pv)


def _flash_init(m_ref, acc_ref):
    m_ref[...] = jnp.full(m_ref.shape, M_INIT, F32)
    acc_ref[...] = jnp.zeros(acc_ref.shape, F32)


def _softmax_pv(scores, vt_blks, m_refs=None, acc_refs=None, maxes=None):
    def pv(vt, p):
        if not isinstance(vt, tuple):
            return _dot(vt, p)
        w = p.shape[1] // len(vt)
        return jnp.concatenate([_dot(v, p[:, j * w:(j + 1) * w]) for j, v in enumerate(vt)], axis=1)

    ps, alphas = [], []
    for i, s in enumerate(scores):
        mx = jnp.max(s, axis=0, keepdims=True) if maxes is None else maxes[i]
        if m_refs is None:
            ps.append(jnp.exp2(s - mx).astype(BF16))
            continue
        m_old = m_refs[i][...]
        m_new = jnp.maximum(m_old, mx)
        ps.append(jnp.exp2(s - m_new).astype(BF16))
        alphas.append(jnp.exp2(m_old - m_new))
        m_refs[i][...] = m_new
    if m_refs is None:
        return [pv(vt, p) for vt, p in zip(vt_blks, ps)]
    for vt, p, alpha, acc_ref in zip(vt_blks, ps, alphas, acc_refs):
        acc_ref[...] = acc_ref[...] * alpha + pv(vt, p)
    return None


def _pipelined_sweep(n_rest, qk_to, spv_from, finish):
    def body(jj, carry):
        j = 2 * jj
        qk_to(1, j + 1)
        spv_from(0, j)
        qk_to(0, j + 2)
        spv_from(1, j + 1)
        return carry

    lax.fori_loop(0, n_rest // 2, body, 0)

    @pl.when(n_rest % 2 == 1)
    def _():
        qk_to(1, n_rest)
        spv_from(0, n_rest - 1)
        spv_from(1, n_rest)
        finish()

    @pl.when(n_rest % 2 == 0)
    def _():
        spv_from(0, n_rest)
        finish()


def _ones_rows(tk):
    return jnp.where(_iota((16, tk), 0) == 0, 1.0, 0.0).astype(BF16)


NSA_TQ = 128
NSA_N = NSA_HPG * NSA_TQ
SEL_TK = 512
WIN_KEYS = WINDOW + NSA_TQ

AUG_POS = 0
AUG_TQ = 6
AUG_PAD = 9
AUG_MASK = 32


def _bf16_terms(x, n=3):
    out, r = [], np.float32(x)
    for _ in range(n):
        term = np.asarray(r, np.float32).astype(BF16).astype(np.float32)
        out.append(float(term))
        r = np.float32(r - term)
    return out


def _nsa_queries(q_ref, blk):
    t0 = blk * NSA_TQ
    qt = q_ref[0].astype(F32).T * (HEAD_DIM ** -0.5 * LOG2E)
    lane = _iota((1, NSA_N), 1)
    hl = lane // NSA_TQ
    tqf = (t0 + lane % NSA_TQ).astype(F32)
    ridx = _iota((16, NSA_N), 0)
    zeros_h = jnp.zeros((HEAD_DIM, NSA_TQ), F32)
    l2e = _bf16_terms(LOG2E)
    q_top, aug16 = [], []
    for g in range(NSA_GROUPS):
        tops = []
        for h in range(NSA_HPG):
            head = NSA_HPG * g + h
            qh = qt[head * HEAD_DIM:(head + 1) * HEAD_DIM]
            tops.append(jnp.concatenate([qh, zeros_h] if g == 0 else [zeros_h, qh], axis=0))
        q_top.append(jnp.concatenate(tops, axis=1))
        slope = lax.bitcast_convert_type((127 - (NSA_HPG * g + hl + 1)) << 23, F32)
        rows = {}
        for i, term in enumerate(l2e):
            rows[AUG_POS + 2 * i] = slope * (64.0 * term)
            rows[AUG_POS + 2 * i + 1] = slope * term
        for i, term in enumerate(_split3(-(slope * tqf) * LOG2E)):
            rows[AUG_TQ + i] = term.astype(F32)
        a = jnp.zeros((16, NSA_N), F32)
        for r, val in rows.items():
            a = jnp.where(ridx == r, val, a)
        aug16.append(a)
    return q_top, aug16


def _nsa_compressed(s_cmp, blk, vct_ref, ovt, gs):
    tq = blk * NSA_TQ + _iota((1, NSA_N), 1) % NSA_TQ
    n_idx = _iota((LANES, NSA_N), 0)
    valid = (n_idx * CMP_STRIDE + (CMP_BLOCK - 1) <= tq) & (n_idx < LANES - 1)
    o_cmp, imps = [], []
    for g in gs:
        sm = jnp.where(valid, s_cmp[g], NEG_BIG)
        mx = jnp.max(sm, axis=0, keepdims=True)
        e = jnp.where(valid, jnp.exp2(sm - mx), 0.0)
        den = jnp.sum(e, axis=0, keepdims=True)
        pn = e / jnp.where(den > 0.0, den, 1.0)
        o_cmp.append(_dot(vct_ref[g * HEAD_DIM:(g + 1) * HEAD_DIM, :], pn.astype(BF16)))

        psum = pn[:, 0:NSA_TQ]
        for h in range(1, NSA_HPG):
            psum = psum + pn[:, h * NSA_TQ:(h + 1) * NSA_TQ]
        hi, mid, lo = _split3(psum)
        imps.append(_dot(ovt, hi) + _dot(ovt, mid) + _dot(ovt, lo))
    return o_cmp, imps


def _nsa_select(imps, blk, q_top, aug16, gs):
    t0 = blk * NSA_TQ
    ns = imps[0].shape[0]
    tcol = t0 + _iota((8, NSA_TQ), 1)
    q_sel = []
    for g, imp in zip(gs, imps):
        tiles = range(ns // 8)
        j_idx = [8 * r + _iota((8, NSA_TQ), 0) for r in tiles]
        in_past = [j_idx[r] * SEL_BLOCK <= tcol for r in tiles]
        cur = tcol // SEL_BLOCK
        score = []
        for r in tiles:
            forced = (j_idx[r] == 0) | (j_idx[r] == cur) | (j_idx[r] == cur - 1)
            score.append(jnp.where(forced, FORCE_SCORE, jnp.where(in_past[r], imp[8 * r:8 * r + 8], -1.0)))
        rank = [jnp.zeros((8, NSA_TQ), F32) for _ in tiles]
        for i in range(ns):
            si = score[i // 8][i % 8:i % 8 + 1, :]
            for r in tiles:
                if 8 * r > i:
                    beats = si >= score[r]
                elif 8 * r + 7 <= i:
                    beats = si > score[r]
                else:
                    beats = (si > score[r]) | ((si == score[r]) & (j_idx[r] > i))
                rank[r] = rank[r] + jnp.where(beats, 1.0, 0.0)
        maskval = jnp.concatenate([jnp.where((rank[r] < min(N_SELECT, ns)) & in_past[r], 0.0, NEG_BIG) for r in tiles], axis=0)
        mask4 = jnp.concatenate([maskval] * NSA_HPG, axis=1)
        q_sel.append(jnp.concatenate([q_top[g], aug16[g], jnp.zeros((AUG_MASK - 16, NSA_N), F32), mask4,
                                      jnp.zeros((LANES - AUG_MASK - ns, NSA_N), F32)], axis=0).astype(BF16))
    return q_sel


def _nsa_kernel(q_ref, gf_ref, kc_ref, vc_ref, ks_ref, vs_ref, kw_ref, vw_ref,
                augk_ref, augw_ref, augc_ref, ovt_ref, trisel_ref, triwin_ref, o_ref,
                vst_ref, vwt_ref, kwp_ref, vct_ref, gt_ref, m_ref, acc_ref, s0_ref, s1_ref, mx0_ref, mx1_ref):
    qb = pl.program_id(1)
    t = ks_ref.shape[1]
    t0 = qb * NSA_TQ
    groups = range(NSA_GROUPS)
    zeros_tail = jnp.zeros((LANES - 16, NSA_N), F32)

    def cmp_scores(q_top, aug16):
        kcb = jnp.concatenate([kc_ref[0], augc_ref[...]], axis=1)
        return [_dot(kcb, jnp.concatenate([q_top[g], aug16[g], zeros_tail], axis=0).astype(BF16)) for g in groups]

    @pl.when(qb == 0)
    def _():
        vwt_ref[:, 0:WINDOW] = jnp.zeros((LANES, WINDOW), BF16)
        kwp_ref[0:WINDOW, :] = jnp.zeros((WINDOW, LANES), BF16)
        kwp_ref[WINDOW:WINDOW + t, :] = kw_ref[0]
        for c in range(t // LANES):
            sl = slice(c * LANES, (c + 1) * LANES)
            vst_ref[:, sl] = vs_ref[0, sl, :].astype(F32).T.astype(BF16)
            vwt_ref[:, WINDOW + c * LANES:WINDOW + (c + 1) * LANES] = vw_ref[0, sl, :].astype(F32).T.astype(BF16)
        vct_ref[...] = vc_ref[0].astype(F32).T.astype(BF16)

    q_top, aug16 = _nsa_queries(q_ref, qb)
    s_cmp = cmp_scores(q_top, aug16)
    kw0 = pl.multiple_of(t0, NSA_TQ)
    kwin = jnp.concatenate([kwp_ref[pl.ds(kw0, WIN_KEYS), :], augw_ref[pl.ds(kw0, WIN_KEYS), :]], axis=1)
    pad_row = _iota((16, NSA_N), 0) == AUG_PAD
    s_win = [_dot(kwin, jnp.concatenate([q_top[g], jnp.where(pad_row, NEG_BIG, aug16[g]), zeros_tail], axis=0).astype(BF16))
             for g in groups]
    o_cmp, imps = _nsa_compressed(s_cmp, qb, vct_ref, ovt_ref[...], groups)
    q_sel = _nsa_select(imps, qb, q_top, aug16, groups)

    hi_edge = jnp.concatenate([triwin_ref[0]] * NSA_HPG, axis=1)
    lo_edge = jnp.concatenate([triwin_ref[1]] * NSA_HPG, axis=1)
    ones_w = _ones_rows(WIN_KEYS)
    s_win = [jnp.concatenate([s[0:NSA_TQ] + lo_edge, s[NSA_TQ:WINDOW], s[WINDOW:WIN_KEYS] + hi_edge], axis=0)
             for s in s_win]
    vt_win = [jnp.concatenate([vwt_ref[g * HEAD_DIM:(g + 1) * HEAD_DIM, pl.ds(kw0, WIN_KEYS)], ones_w], axis=0)
              for g in groups]
    o_win = [acc[0:HEAD_DIM] / acc[HEAD_DIM:HEAD_DIM + 1] for acc in _softmax_pv(s_win, vt_win)]

    ones_s = _ones_rows(SEL_TK)
    per_chunk = SEL_TK // NSA_TQ
    n_full = qb // per_chunk
    s_bufs = (s0_ref, s1_ref)
    mx_bufs = (mx0_ref, mx1_ref)

    def sel_scores(cb):
        k0 = pl.multiple_of(cb * SEL_TK, SEL_TK)
        kblk = jnp.concatenate([ks_ref[0, pl.ds(k0, SEL_TK), :], augk_ref[pl.ds(k0, SEL_TK), :]], axis=1)
        return [_dot(kblk, q_sel[g]) for g in groups]

    def qk_to(buf, j):
        for g, s in enumerate(sel_scores(j - 1)):
            s_bufs[buf][g] = s
            mx_bufs[buf][g] = jnp.max(s, axis=0, keepdims=True)

    def spv_from(buf, j):
        cb = jnp.where(j == 0, n_full, j - 1)
        k0 = pl.multiple_of(cb * SEL_TK, SEL_TK)
        vts = [jnp.concatenate([vst_ref[g * HEAD_DIM:(g + 1) * HEAD_DIM, pl.ds(k0, SEL_TK)], ones_s], axis=0)
               for g in groups]
        _softmax_pv([s_bufs[buf][g] for g in groups], vts,
                    [m_ref.at[g] for g in groups], [acc_ref.at[g] for g in groups],
                    maxes=[mx_bufs[buf][g] for g in groups])

    diag = trisel_ref[qb % per_chunk]
    diag4 = jnp.concatenate([diag] * NSA_HPG, axis=1)
    for g, s in enumerate(sel_scores(n_full)):
        s = s + diag4
        s0_ref[g] = s
        mx0_ref[g] = jnp.max(s, axis=0, keepdims=True)
        _flash_init(m_ref.at[g], acc_ref.at[g])

    def gated_output():
        gt_ref[...] = gf_ref[0].T
        sig = jax.nn.sigmoid(gt_ref[GF_GATE:GF_GATE + 3 * NSA_HEADS, :])
        outs = []
        for g in groups:
            acc = acc_ref[g]
            o_sel = acc[0:HEAD_DIM] / acc[HEAD_DIM:HEAD_DIM + 1]
            for h in range(NSA_HPG):
                sl = slice(h * NSA_TQ, (h + 1) * NSA_TQ)
                head = NSA_HPG * g + h
                g0 = sig[head:head + 1]
                g1 = sig[NSA_HEADS + head:NSA_HEADS + head + 1]
                g2 = sig[2 * NSA_HEADS + head:2 * NSA_HEADS + head + 1]
                outs.append(g0 * o_cmp[g][:, sl] + g1 * o_sel[:, sl] + g2 * o_win[g][:, sl])
        o_ref[0] = jnp.concatenate(outs, axis=0).T.astype(BF16)

    _pipelined_sweep(n_full, qk_to, spv_from, gated_output)


def _nsa(proj, gf, kcmp, vcmp, augk, augw, augc, ovt, trisel, triwin):
    b, t, _ = proj.shape
    nq = t // NSA_TQ
    kvb = COL_KV // LANES
    width = NSA_HEADS * HEAD_DIM
    whole = lambda a: pl.BlockSpec(a.shape, lambda i, k: (0,) * a.ndim)
    kv = lambda idx: pl.BlockSpec((1, t, LANES), lambda i, k: (i, 0, idx))
    return pl.pallas_call(
        _nsa_kernel,
        grid=(b, nq),
        in_specs=[
            pl.BlockSpec((1, NSA_TQ, width), lambda i, k: (i, k, COL_QN // width)),
            pl.BlockSpec((1, NSA_TQ, LANES), lambda i, k: (i, k, 0)),
            pl.BlockSpec((1, LANES, LANES), lambda i, k: (i, 0, 0)),
            pl.BlockSpec((1, LANES, LANES), lambda i, k: (i, 0, 0)),
            kv(kvb + 2), kv(kvb + 3), kv(kvb + 4), kv(kvb + 5),
            whole(augk), whole(augw), whole(augc), whole(ovt), whole(trisel), whole(triwin),
        ],
        out_specs=pl.BlockSpec((1, NSA_TQ, width), lambda i, k: (i, k, 0)),
        out_shape=jax.ShapeDtypeStruct((b, t, width), BF16),
        scratch_shapes=[
            pltpu.VMEM((LANES, t), BF16),
            pltpu.VMEM((LANES, t + WINDOW), BF16),
            pltpu.VMEM((t + WINDOW, LANES), BF16),
            pltpu.VMEM((LANES, LANES), BF16),
            pltpu.VMEM((LANES, NSA_TQ), F32),
            pltpu.VMEM((NSA_GROUPS, 1, NSA_N), F32),
            pltpu.VMEM((NSA_GROUPS, HEAD_DIM + 16, NSA_N), F32),
            pltpu.VMEM((NSA_GROUPS, SEL_TK, NSA_N), F32),
            pltpu.VMEM((NSA_GROUPS, SEL_TK, NSA_N), F32),
            pltpu.VMEM((NSA_GROUPS, 1, NSA_N), F32),
            pltpu.VMEM((NSA_GROUPS, 1, NSA_N), F32),
        ],
        compiler_params=pltpu.CompilerParams(
            dimension_semantics=("arbitrary", "arbitrary"), vmem_limit_bytes=VMEM_LIMIT),
        name="nsa_attention",
    )(proj, gf, kcmp, vcmp, proj, proj, proj, proj, augk, augw, augc, ovt, trisel, triwin)


FOX_CBLK = 256


def _foxprep_kernel(gf_ref, bias_ref, cq_ref, augk_ref):
    t = gf_ref.shape[1]
    cols = []
    for c in range(t // LANES):
        cols.append(gf_ref[0, c * LANES:(c + 1) * LANES, :].T[GF_FOX:GF_FOX + FOX_HEADS])
    f = jnp.concatenate(cols, axis=1) + bias_ref[...]
    ls = jnp.minimum(f, 0.0) - jnp.log(1.0 + jnp.exp(-jnp.abs(f)))
    upper = (_iota((FOX_CBLK, FOX_CBLK), 0) <= _iota((FOX_CBLK, FOX_CBLK), 1)).astype(BF16)
    carry = jnp.zeros((FOX_HEADS, 1), F32)
    parts = []
    for c in range(t // FOX_CBLK):
        hi, mid, lo = _split3(ls[:, c * FOX_CBLK:(c + 1) * FOX_CBLK])
        cb = _dot(hi, upper) + _dot(mid, upper) + _dot(lo, upper) + carry
        carry = cb[:, FOX_CBLK - 1:FOX_CBLK]
        parts.append(cb)
    csum = jnp.concatenate(parts, axis=1) * LOG2E
    c1, c2, c3 = (v.astype(F32) for v in _split3(csum))
    ones = jnp.ones((3 * FOX_HEADS, t), F32)
    zeros = jnp.zeros((LANES - 6 * FOX_HEADS, t), F32)
    cq_ref[0] = jnp.concatenate([c1, c2, c3, ones, zeros], axis=0).astype(BF16)
    slab_k = jnp.concatenate([ones, -c1, -c2, -c3, zeros], axis=0)
    for c in range(t // LANES):
        sl = slice(c * LANES, (c + 1) * LANES)
        augk_ref[0, sl, :] = slab_k[:, sl].T.astype(BF16)


def _foxprep(gf, bias):
    b, t, _ = gf.shape
    return pl.pallas_call(
        _foxprep_kernel,
        grid=(b,),
        in_specs=[pl.BlockSpec((1, t, LANES), lambda i: (i, 0, 0)),
                  pl.BlockSpec(bias.shape, lambda i: (0, 0))],
        out_specs=[pl.BlockSpec((1, LANES, t), lambda i: (i, 0, 0)),
                   pl.BlockSpec((1, t, LANES), lambda i: (i, 0, 0))],
        out_shape=[jax.ShapeDtypeStruct((b, LANES, t), BF16),
                   jax.ShapeDtypeStruct((b, t, LANES), BF16)],
        compiler_params=pltpu.CompilerParams(dimension_semantics=("arbitrary",)),
        name="fox_prep",
    )(gf, bias)


FOX_TQ = 256
FOX_TK = 256
FOX_NP = FOX_HEADS // 2
FOX_N = 2 * FOX_TQ
FOX_ROWS = HEAD_DIM + 16


def _fox_kernel(q_ref, k_ref, v_ref, cq_ref, augk_ref, tri_ref, o_ref, vt_ref, qs_ref, m_ref, acc_ref,
                s0_ref, s1_ref, sd_ref, mx0_ref, mx1_ref, mxd_ref):
    qb = pl.program_id(1)
    nq = pl.num_programs(1)
    t = k_ref.shape[1]
    pairs = range(FOX_NP)
    s_bufs = (s0_ref, s1_ref, sd_ref)
    mx_bufs = (mx0_ref, mx1_ref, mxd_ref)
    diag = 2

    def scores(kb, qs):
        k0 = pl.multiple_of(kb * FOX_TK, FOX_TK)
        ak = augk_ref[0, pl.ds(k0, FOX_TK), :]
        return [_dot(jnp.concatenate([k_ref[0, pl.ds(k0, FOX_TK), p * LANES:(p + 1) * LANES], ak], axis=1), qs(p))
                for p in pairs]

    def prepare(blk, slot):
        q0 = pl.multiple_of(blk * FOX_TQ, FOX_TQ)
        row = _iota((LANES, FOX_TQ), 0)
        hrow = row % FOX_HEADS
        cq = cq_ref[0, :, pl.ds(q0, FOX_TQ)].astype(F32)
        for p in pairs:
            qt = q_ref[0, pl.ds(q0, FOX_TQ), p * LANES:(p + 1) * LANES].astype(F32).T * (HEAD_DIM ** -0.5 * LOG2E)
            q_top = jnp.concatenate([jnp.where(row < HEAD_DIM, qt, 0.0), jnp.where(row >= HEAD_DIM, qt, 0.0)], axis=1)
            q_aug = jnp.concatenate([jnp.where(hrow == 2 * p, cq, 0.0), jnp.where(hrow == 2 * p + 1, cq, 0.0)],
                                    axis=1)
            qs_ref[slot, p] = jnp.concatenate([q_top, q_aug], axis=0).astype(BF16)
        tri = tri_ref[...]
        for p, s in enumerate(scores(blk, lambda p: qs_ref[slot, p])):
            s = s + tri
            sd_ref[p] = s
            mxd_ref[p] = jnp.max(s, axis=0, keepdims=True)

    def qk_to(buf, kb):
        for p, s in enumerate(scores(kb, lambda p: qs_ref[qb % 2, p])):
            s_bufs[buf][p] = s
            mx_bufs[buf][p] = jnp.max(s, axis=0, keepdims=True)

    def spv_from(buf, kb):
        k0 = pl.multiple_of(kb * FOX_TK, FOX_TK)
        vts = [(vt_ref[p, 0, :, pl.ds(k0, FOX_TK)], vt_ref[p, 1, :, pl.ds(k0, FOX_TK)]) for p in pairs]
        _softmax_pv([s_bufs[buf][p] for p in pairs], vts,
                    [m_ref.at[p] for p in pairs], [acc_ref.at[p] for p in pairs],
                    maxes=[mx_bufs[buf][p] for p in pairs])

    def finish():
        for p in pairs:
            acc = acc_ref[p]
            den = acc[HEAD_DIM:HEAD_DIM + 1]
            o_a = acc[0:HEAD_DIM, 0:FOX_TQ] / den[:, 0:FOX_TQ]
            o_b = acc[0:HEAD_DIM, FOX_TQ:FOX_N] / den[:, FOX_TQ:FOX_N]
            o_ref[0, :, p * LANES:(p + 1) * LANES] = jnp.concatenate([o_a, o_b], axis=0).T.astype(BF16)
        prepare(jnp.minimum(qb + 1, nq - 1), (qb + 1) % 2)

    @pl.when(qb == 0)
    def _():
        for p in pairs:
            for c in range(t // LANES):
                sl = slice(c * LANES, (c + 1) * LANES)
                vt = v_ref[0, sl, p * LANES:(p + 1) * LANES].astype(F32).T.astype(BF16)
                for hh in range(2):
                    vt_ref[p, hh, 0:HEAD_DIM, sl] = vt[hh * HEAD_DIM:(hh + 1) * HEAD_DIM]
            for hh in range(2):
                vt_ref[p, hh, HEAD_DIM:FOX_ROWS, :] = _ones_rows(t)
        prepare(0, 0)

    for p in pairs:
        _flash_init(m_ref.at[p], acc_ref.at[p])

    @pl.when(qb == 0)
    def _():
        spv_from(diag, qb)
        finish()

    @pl.when(qb > 0)
    def _():
        qk_to(0, 0)
        spv_from(diag, qb)

    def body(jj, carry):
        kb = 2 * jj
        qk_to(1, kb + 1)
        spv_from(0, kb)
        qk_to(0, kb + 2)
        spv_from(1, kb + 1)
        return carry

    lax.fori_loop(0, jnp.maximum(qb - 1, 0) // 2, body, 0)

    @pl.when(qb % 2 == 1)
    def _():
        spv_from(0, qb - 1)
        finish()

    @pl.when((qb > 0) & (qb % 2 == 0))
    def _():
        qk_to(1, qb - 1)
        spv_from(0, qb - 2)

    @pl.when((qb >= 2) & ((qb + 1) % 2 == 1))
    def _():
        spv_from(1, qb - 1)
        finish()


def _fox(proj, cq, augk, tri):
    b, t, _ = proj.shape
    nq = t // FOX_TQ
    width = FOX_HEADS * HEAD_DIM
    qf, kf, vf = COL_QF // width, COL_KF // width, COL_VF // width
    return pl.pallas_call(
        _fox_kernel,
        grid=(b, nq),
        in_specs=[
            pl.BlockSpec((1, t, width), lambda i, k: (i, 0, qf)),
            pl.BlockSpec((1, t, width), lambda i, k: (i, 0, kf)),
            pl.BlockSpec((1, t, width), lambda i, k: (i, 0, vf)),
            pl.BlockSpec((1, LANES, t), lambda i, k: (i, 0, 0)),
            pl.BlockSpec((1, t, LANES), lambda i, k: (i, 0, 0)),
            pl.BlockSpec(tri.shape, lambda i, k: (0, 0)),
        ],
        out_specs=pl.BlockSpec((1, FOX_TQ, width), lambda i, k: (i, k, 0)),
        out_shape=jax.ShapeDtypeStruct((b, t, width), BF16),
        scratch_shapes=[
            pltpu.VMEM((FOX_NP, 2, FOX_ROWS, t), BF16),
            pltpu.VMEM((2, FOX_NP, 2 * LANES, FOX_N), BF16),
            pltpu.VMEM((FOX_NP, 1, FOX_N), F32),
            pltpu.VMEM((FOX_NP, FOX_ROWS, FOX_N), F32),
            pltpu.VMEM((FOX_NP, FOX_TK, FOX_N), F32),
            pltpu.VMEM((FOX_NP, FOX_TK, FOX_N), F32),
            pltpu.VMEM((FOX_NP, FOX_TK, FOX_N), F32),
            pltpu.VMEM((FOX_NP, 1, FOX_N), F32),
            pltpu.VMEM((FOX_NP, 1, FOX_N), F32),
            pltpu.VMEM((FOX_NP, 1, FOX_N), F32),
        ],
        compiler_params=pltpu.CompilerParams(
            dimension_semantics=("arbitrary", "arbitrary"), vmem_limit_bytes=VMEM_LIMIT),
        name="fox_attention",
    )(proj, proj, proj, cq, augk, tri)


def _post_kernel(x_ref, on_ref, of_ref, mg_ref, wn_ref, wf_ref, wo_ref, gm_ref, wu_ref, wd_ref, gfin_ref, o_ref):
    d = x_ref.shape[1]
    ff = wu_ref.shape[1]
    ga = mg_ref[:, 0:d].astype(F32)
    gb = mg_ref[:, d:2 * d].astype(F32)
    y = ga * _dot(on_ref[...], wn_ref[...]) + gb * _dot(of_ref[...], wf_ref[...])
    h = x_ref[...] + _dot(y.astype(BF16), wo_ref[...])
    r = lax.rsqrt(jnp.mean(h * h, axis=-1, keepdims=True) + NORM_EPS)
    v = (h * r * gm_ref[...]).astype(BF16)
    acc = h
    for c0 in range(0, ff, FF_CHUNK):
        up = jnp.maximum(_dot(v, wu_ref[:, c0:c0 + FF_CHUNK]), 0.0)
        acc = acc + _dot((up * up).astype(BF16), wd_ref[c0:c0 + FF_CHUNK, :])
    r2 = lax.rsqrt(jnp.mean(acc * acc, axis=-1, keepdims=True) + NORM_EPS)
    o_ref[...] = acc * r2 * gfin_ref[...]


def _post(x2, on, of, mg, wn, wf, wo, gm, wu, wd, gfin, tm):
    n, d = x2.shape
    row = lambda w: pl.BlockSpec((tm, w), lambda i: (i, 0))
    const = lambda a: pl.BlockSpec(a.shape, lambda i: (0, 0), pipeline_mode=pl.Buffered(1))
    return pl.pallas_call(
        _post_kernel,
        grid=(n // tm,),
        in_specs=[row(d), row(on.shape[1]), row(of.shape[1]), row(2 * d),
                  const(wn), const(wf), const(wo), const(gm), const(wu), const(wd), const(gfin)],
        out_specs=row(d),
        out_shape=jax.ShapeDtypeStruct((n, d), F32),
        compiler_params=pltpu.CompilerParams(
            dimension_semantics=("arbitrary",), vmem_limit_bytes=VMEM_LIMIT),
        name="post_mlp",
    )(x2, on, of, mg, wn, wf, wo, gm, wu, wd, gfin)


def _position_constants(t):
    def key_side(pos, pad):
        a = np.zeros((pos.size, LANES), np.float32)
        for i in range(3):
            a[:, AUG_POS + 2 * i] = pos // 64
            a[:, AUG_POS + 2 * i + 1] = pos % 64
        a[:, AUG_TQ:AUG_TQ + 3] = 1.0
        a[:, AUG_PAD] = pad
        return a

    pos = np.arange(t)
    augk = key_side(pos, 0.0)
    augk[pos, AUG_MASK + pos // SEL_BLOCK] = 1.0
    augw = np.concatenate([key_side(np.zeros(WINDOW, np.int64), 1.0), key_side(pos, 0.0)], axis=0)
    augc = key_side(np.arange(LANES) * CMP_STRIDE + CMP_BLOCK - 1, 0.0)

    nc = (t - CMP_BLOCK) // CMP_STRIDE + 1
    ns = t // SEL_BLOCK
    n = np.arange(LANES)[None, :]
    j = np.arange(ns)[:, None]
    ovt = ((n * CMP_STRIDE <= j * SEL_BLOCK + SEL_BLOCK - 1) & (n * CMP_STRIDE + CMP_BLOCK - 1 >= j * SEL_BLOCK) & (n < nc))

    def tri(tk, tq, anti=False):
        kk = np.arange(tk)[:, None]
        tt = np.arange(tq)[None, :]
        ok = (kk > tt) if anti else (kk <= tt)
        return np.where(ok, 0.0, NEG_BIG).astype(np.float32)

    edge = tri(NSA_TQ, NSA_TQ)
    trisel = np.zeros((SEL_TK // NSA_TQ, SEL_TK, NSA_TQ), np.float32)
    for v in range(SEL_TK // NSA_TQ):
        trisel[v, v * NSA_TQ:(v + 1) * NSA_TQ] = edge
    triwin = np.stack([edge, tri(NSA_TQ, NSA_TQ, anti=True)])
    trifox = np.tile(tri(FOX_TK, FOX_TQ), (1, 2))
    return (jnp.asarray(augk, BF16), jnp.asarray(augw, BF16), jnp.asarray(augc, BF16), jnp.asarray(ovt, BF16),
            jnp.asarray(trisel), jnp.asarray(triwin), jnp.asarray(trifox))


def kernel(x, norm_mix, w_in, cmp_pos_k, cmp_w1_k, cmp_w2_k, cmp_pos_v, cmp_w1_v, cmp_w2_v, fox_f_bias, w_branch_nsa, w_branch_fox, w_merge_gate, b_merge_gate, w_out, norm_mlp, w_up, w_down, norm_final):
    b, t, d = x.shape
    assert w_in.shape[0] == 1, "one layer: the final norm is fused into the MLP kernel"
    assert t % FOX_TQ == 0 and t // SEL_BLOCK == 32 and (t - CMP_BLOCK) // CMP_STRIDE + 1 == LANES - 1
    augk, augw, augc, ovt, trisel, triwin, trifox = _position_constants(t)
    h = x.reshape(b * t, d)
    proj, gf, mg, wn, wf, wo, wu, wd = _inproj(h, norm_mix[0][None, :], jnp.swapaxes(w_in, 1, 2), w_merge_gate,
                                               b_merge_gate[0][None, :],
                                               (w_branch_nsa, w_branch_fox, w_out, w_up, w_down), ROW_TILE)
    proj = proj.reshape(b, t, PROJ_W)
    gf = gf.reshape(b, t, LANES)

    def w1_blocks(w1):
        wa_ = w1[:CMP_STRIDE * HEAD_DIM].reshape(CMP_STRIDE, HEAD_DIM, CMP_HIDDEN)
        wb_ = w1[CMP_STRIDE * HEAD_DIM:].reshape(CMP_STRIDE, HEAD_DIM, CMP_HIDDEN)
        z = jnp.zeros_like(wa_)
        top = jnp.concatenate([wa_, z, wb_, z], axis=2)
        bot = jnp.concatenate([z, wa_, z, wb_], axis=2)
        return jnp.concatenate([top, bot], axis=1).astype(BF16)

    def w2_pair(w2):
        z = jnp.zeros_like(w2)
        return (jnp.concatenate([w2, z], axis=1).astype(BF16), jnp.concatenate([z, w2], axis=1).astype(BF16))

    def pos_rows(p):
        row = jnp.concatenate([p[:CMP_STRIDE], p[CMP_STRIDE:]], axis=1)[:, None, :]
        return jnp.concatenate([row, jnp.zeros((CMP_STRIDE, 15, 2 * HEAD_DIM), F32)], axis=1).astype(BF16)

    w2k0, w2k1 = w2_pair(cmp_w2_k[0])
    w2v0, w2v1 = w2_pair(cmp_w2_v[0])
    kcmp, vcmp = _compress(proj, w1_blocks(cmp_w1_k[0]), w2k0, w2k1, pos_rows(cmp_pos_k[0]),
                           w1_blocks(cmp_w1_v[0]), w2v0, w2v1, pos_rows(cmp_pos_v[0]))
    o_n = _nsa(proj, gf, kcmp, vcmp, augk, augw, augc, ovt, trisel, triwin)
    cq, faug = _foxprep(gf, fox_f_bias[0][:, None])
    o_f = _fox(proj, cq, faug, trifox)
    out = _post(h, o_n.reshape(b * t, -1), o_f.reshape(b * t, -1), mg,
                wn, wf, wo, norm_mlp[0][None, :], wu, wd, norm_final[None, :], ROW_TILE)
    return out.reshape(b, t, d)
```

```python
import numpy as np
import jax
import jax.numpy as jnp
from jax import lax
from jax.experimental import pallas as pl
from jax.experimental.pallas import tpu as pltpu

HEAD_DIM = 64
NSA_HEADS = 8
NSA_GROUPS = 2
NSA_HPG = NSA_HEADS // NSA_GROUPS
FOX_HEADS = 8
CMP_BLOCK = 32
CMP_STRIDE = 16
CMP_HIDDEN = 128
SEL_BLOCK = 64
N_SELECT = 16
WINDOW = 512
NORM_EPS = 1e-6
FORCE_SCORE = 1e9

LANES = 128
NEG_BIG = -(2.0 ** 100)
M_INIT = -3.0e38
LOG2E = 1.4426950408889634
VMEM_LIMIT = 56 * 1024 * 1024
ROW_TILE = 512
ROW_SPLIT = 2
FF_CHUNK = 1024

PROJ_W = 2816
COL_QN = 0
COL_QF = 512
COL_KF = 1024
COL_VF = 1536
COL_KV = 2048
GF_GATE = 0
GF_FOX = 24

F32 = jnp.float32
BF16 = jnp.bfloat16


def _dot(a, b):
    return jnp.dot(a, b, preferred_element_type=F32)


def _iota(shape, dim):
    return lax.broadcasted_iota(jnp.int32, shape, dim)


def _split3(x):
    hi = x.astype(BF16)
    r1 = x - hi.astype(F32)
    mid = r1.astype(BF16)
    lo = (r1 - mid.astype(F32)).astype(BF16)
    return hi, mid, lo


W_QN = NSA_HEADS * HEAD_DIM
W_KV = 6 * NSA_GROUPS * HEAD_DIM
W_GATE = 3 * NSA_HEADS
W_FOX = 3 * FOX_HEADS * HEAD_DIM
W_PIECES = ((0, W_QN), (W_QN + W_KV + W_GATE, W_FOX), (W_QN, W_KV))


def _inproj_kernel(x_ref, g_ref, wt_ref, wm32_ref, bm_ref, *refs):
    n_later = (len(refs) - 8) // 2
    later_in, (proj_ref, gf_ref, mg_ref) = refs[:n_later], refs[n_later:n_later + 3]
    later_out = refs[n_later + 3:2 * n_later + 3]
    wq_ref, wf_ref, wkv_ref, wg_ref, wm_ref = refs[2 * n_later + 3:]
    w_refs = (wq_ref, wf_ref, wkv_ref)

    for src, dst in zip(later_in, later_out):
        dst[...] = src[0].astype(BF16)

    @pl.when(pl.program_id(0) == 0)
    def _():
        d = wt_ref.shape[2]
        for (c0, width), dst in zip(W_PIECES, w_refs):
            for k0 in range(0, width, 256):
                dst[:, k0:k0 + 256] = wt_ref[0, c0 + k0:c0 + k0 + 256, :].T.astype(BF16)
        gate0 = W_QN + W_KV
        fox0 = gate0 + W_GATE + W_FOX
        gates = jnp.concatenate([wt_ref[0, gate0:gate0 + W_GATE, :], wt_ref[0, fox0:fox0 + FOX_HEADS, :],
                                 jnp.zeros((LANES - W_GATE - FOX_HEADS, d), F32)], axis=0)
        wg_ref[...] = gates.T.astype(BF16)
        for r0 in range(0, d, 256):
            wm_ref[r0:r0 + 256, :] = wm32_ref[0, r0:r0 + 256, :].astype(BF16)

    sub = x_ref.shape[0] // ROW_SPLIT
    for h in range(ROW_SPLIT):
        rows = slice(h * sub, (h + 1) * sub)
        x = x_ref[rows, :]
        r = lax.rsqrt(jnp.mean(x * x, axis=-1, keepdims=True) + NORM_EPS)
        u = (x * r * g_ref[...]).astype(BF16)
        c0 = 0
        for piece in w_refs:
            for k0 in range(0, piece.shape[1], 768):
                k1 = min(k0 + 768, piece.shape[1])
                proj_ref[rows, c0 + k0:c0 + k1] = _dot(u, piece[:, k0:k1]).astype(BF16)
            c0 += piece.shape[1]
        gf_ref[rows, :] = _dot(u, wg_ref[...])
        for c0 in (0, 1024):
            z = _dot(u, wm_ref[:, c0:c0 + 1024]) + bm_ref[:, c0:c0 + 1024]
            mg_ref[rows, c0:c0 + 1024] = jax.nn.sigmoid(z).astype(BF16)


def _inproj(x2, g, w_in_t, wm, bm, later, tm):
    n, d = x2.shape
    steps = n // tm
    const = lambda i: (0, 0)
    once = dict(pipeline_mode=pl.Buffered(1))
    slabs = [w.shape[1] // steps for w in later]
    assert all(r * steps == w.shape[1] and r % 16 == 0 for w, r in zip(later, slabs)), "bf16 slabs need 16-row multiples"
    return pl.pallas_call(
        _inproj_kernel,
        grid=(n // tm,),
        in_specs=[
            pl.BlockSpec((tm, d), lambda i: (i, 0)),
            pl.BlockSpec((1, d), const),
            pl.BlockSpec(w_in_t.shape, lambda i: (0, 0, 0), **once),
            pl.BlockSpec(wm.shape, lambda i: (0, 0, 0), **once),
            pl.BlockSpec(bm.shape, const),
        ] + [pl.BlockSpec((1, r, w.shape[2]), lambda i: (0, i, 0)) for w, r in zip(later, slabs)],
        out_specs=[
            pl.BlockSpec((tm, PROJ_W), lambda i: (i, 0)),
            pl.BlockSpec((tm, LANES), lambda i: (i, 0)),
            pl.BlockSpec((tm, 2 * d), lambda i: (i, 0)),
        ] + [pl.BlockSpec((r, w.shape[2]), lambda i: (i, 0)) for w, r in zip(later, slabs)],
        out_shape=[
            jax.ShapeDtypeStruct((n, PROJ_W), BF16),
            jax.ShapeDtypeStruct((n, LANES), F32),
            jax.ShapeDtypeStruct((n, 2 * d), BF16),
        ] + [jax.ShapeDtypeStruct(w.shape[1:], BF16) for w in later],
        scratch_shapes=[pltpu.VMEM((d, width), BF16) for _, width in W_PIECES]
        + [pltpu.VMEM((d, LANES), BF16), pltpu.VMEM((d, 2 * d), BF16)],
        compiler_params=pltpu.CompilerParams(
            dimension_semantics=("arbitrary",), vmem_limit_bytes=VMEM_LIMIT),
        name="inproj",
    )(x2, g, w_in_t, wm, bm, *later)


def _compress_kernel(kc_ref, vc_ref, w1k_ref, w2k_ref, pk_ref, w1v_ref, w2v_ref, pv_ref, kco_ref, vco_ref,
                     xs_ref, w1s_ref, w2s_ref, ps_ref):
    t = kc_ref.shape[1]
    nblk = t // CMP_STRIDE

    @pl.when(pl.program_id(0) == 0)
    def _():
        r = _iota((HEAD_DIM, LANES), 0)
        c = _iota((HEAD_DIM, LANES), 1)
        halves = [jnp.where(c == r + g * HEAD_DIM, 1.0, 0.0).astype(BF16) for g in range(NSA_GROUPS)]
        z = jnp.zeros((HEAD_DIM, CMP_HIDDEN), BF16)
        first_row = _iota((16, LANES), 0) == 0
        for i, (w1_ref, w2_ref, p_ref) in enumerate(((w1k_ref, w2k_ref, pk_ref), (w1v_ref, w2v_ref, pv_ref))):
            for j in range(CMP_STRIDE):
                wa = w1_ref[0, j * HEAD_DIM:(j + 1) * HEAD_DIM, :].astype(BF16)
                wb = w1_ref[0, (CMP_STRIDE + j) * HEAD_DIM:(CMP_STRIDE + j + 1) * HEAD_DIM, :].astype(BF16)
                w1s_ref[i, j] = jnp.concatenate([jnp.concatenate([wa, z, wb, z], axis=1),
                                                 jnp.concatenate([z, wa, z, wb], axis=1)], axis=0)
            w2 = w2_ref[0].astype(BF16)
            for g in range(NSA_GROUPS):
                w2s_ref[i, g] = _dot(w2, halves[g]).astype(BF16)
            pos = p_ref[0].astype(BF16)
            prow = _dot(pos[0:CMP_STRIDE], halves[0]) + _dot(pos[CMP_STRIDE:CMP_BLOCK], halves[1])
            for j in range(CMP_STRIDE):
                ps_ref[i, j] = jnp.where(first_row, prow[j:j + 1, :], 0.0).astype(BF16)

    def one(x_ref, w_ref, w2_refs, p_ref, o_ref):
        xs_ref[...] = x_ref[0].astype(F32)
        acc = jnp.zeros((nblk + 16, 4 * LANES), F32)
        for j in range(CMP_STRIDE):
            xj = xs_ref[pl.ds(j, nblk, stride=CMP_STRIDE), :].astype(BF16)
            acc = acc + _dot(jnp.concatenate([xj, p_ref[j]], axis=0), w_ref[j])
        posb = acc[nblk:nblk + 1, 0:LANES] + acc[nblk:nblk + 1, 3 * LANES:4 * LANES]
        out = jnp.zeros((nblk, LANES), F32)
        for g in range(NSA_GROUPS):
            a = acc[0:nblk, g * LANES:(g + 1) * LANES]
            b = acc[0:nblk, (2 + g) * LANES:(3 + g) * LANES]
            h = a + pltpu.roll(b, nblk - 1, 0) + posb
            out = out + _dot(jax.nn.gelu(h).astype(BF16), w2_refs[g][...])
        o_ref[0] = out.astype(BF16)

    one(kc_ref, w1s_ref.at[0], (w2s_ref.at[0, 0], w2s_ref.at[0, 1]), ps_ref.at[0], kco_ref)
    one(vc_ref, w1s_ref.at[1], (w2s_ref.at[1, 0], w2s_ref.at[1, 1]), ps_ref.at[1], vco_ref)


def _compress(proj, w1k, w2k, pk, w1v, w2v, pv):
    b, t, _ = proj.shape
    kvb = COL_KV // LANES
    assert w1k.shape == (1, CMP_BLOCK * HEAD_DIM, CMP_HIDDEN) and w2k.shape == (1, CMP_HIDDEN, HEAD_DIM)
    assert pk.shape == (1, CMP_BLOCK, HEAD_DIM) and CMP_BLOCK == 2 * CMP_STRIDE and CMP_HIDDEN == LANES
    whole = lambda a: pl.BlockSpec(a.shape, lambda i: (0,) * a.ndim)
    ospec = pl.BlockSpec((1, LANES, LANES), lambda i: (i, 0, 0))
    return pl.pallas_call(
        _compress_kernel,
        grid=(b,),
        in_specs=[pl.BlockSpec((1, t, LANES), lambda i: (i, 0, kvb)),
                  pl.BlockSpec((1, t, LANES), lambda i: (i, 0, kvb + 1)),
                  whole(w1k), whole(w2k), whole(pk), whole(w1v), whole(w2v), whole(pv)],
        out_specs=[ospec, ospec],
        out_shape=[jax.ShapeDtypeStruct((b, LANES, LANES), BF16)] * 2,
        scratch_shapes=[pltpu.VMEM((t, LANES), F32),
                        pltpu.VMEM((2, CMP_STRIDE, 2 * HEAD_DIM, 4 * CMP_HIDDEN), BF16),
                        pltpu.VMEM((2, NSA_GROUPS, CMP_HIDDEN, LANES), BF16),
                        pltpu.VMEM((2, CMP_STRIDE, 16, LANES), BF16)],
        compiler_params=pltpu.CompilerParams(dimension_semantics=("arbitrary",)),
        name="compress",
    )(proj, proj, w1k, w2k, pk, w1v, w2v, pv)


def _flash_init(m_ref, acc_ref):
    m_ref[...] = jnp.full(m_ref.shape, M_INIT, F32)
    acc_ref[...] = jnp.zeros(acc_ref.shape, F32)


def _softmax_pv(scores, vt_blks, m_refs=None, acc_refs=None, maxes=None):
    def pv(vt, p):
        if not isinstance(vt, tuple):
            return _dot(vt, p)
        w = p.shape[1] // len(vt)
        return jnp.concatenate([_dot(v, p[:, j * w:(j + 1) * w]) for j, v in enumerate(vt)], axis=1)

    ps, alphas = [], []
    for i, s in enumerate(scores):
        mx = jnp.max(s, axis=0, keepdims=True) if maxes is None else maxes[i]
        if m_refs is None:
            ps.append(jnp.exp2(s - mx).astype(BF16))
            continue
        m_old = m_refs[i][...]
        m_new = jnp.maximum(m_old, mx)
        ps.append(jnp.exp2(s - m_new).astype(BF16))
        alphas.append(jnp.exp2(m_old - m_new))
        m_refs[i][...] = m_new
    if m_refs is None:
        return [pv(vt, p) for vt, p in zip(vt_blks, ps)]
    for vt, p, alpha, acc_ref in zip(vt_blks, ps, alphas, acc_refs):
        acc_ref[...] = acc_ref[...] * alpha + pv(vt, p)
    return None


def _pipelined_sweep(n_rest, qk_to, spv_from, finish):
    def body(jj, carry):
        j = 2 * jj
        qk_to(1, j + 1)
        spv_from(0, j)
        qk_to(0, j + 2)
        spv_from(1, j + 1)
        return carry

    lax.fori_loop(0, n_rest // 2, body, 0)

    @pl.when(n_rest % 2 == 1)
    def _():
        qk_to(1, n_rest)
        spv_from(0, n_rest - 1)
        spv_from(1, n_rest)
        finish()

    @pl.when(n_rest % 2 == 0)
    def _():
        spv_from(0, n_rest)
        finish()


def _ones_rows(tk):
    return jnp.where(_iota((16, tk), 0) == 0, 1.0, 0.0).astype(BF16)


NSA_TQ = 128
NSA_N = NSA_HPG * NSA_TQ
SEL_TK = 512
WIN_KEYS = WINDOW + NSA_TQ

AUG_POS = 0
AUG_TQ = 6
AUG_PAD = 9
AUG_MASK = 32


def _bf16_terms(x, n=3):
    out, r = [], np.float32(x)
    for _ in range(n):
        term = np.asarray(r, np.float32).astype(BF16).astype(np.float32)
        out.append(float(term))
        r = np.float32(r - term)
    return out


def _nsa_queries(q_ref, blk):
    t0 = blk * NSA_TQ
    qt = q_ref[0].astype(F32).T * (HEAD_DIM ** -0.5 * LOG2E)
    lane = _iota((1, NSA_N), 1)
    hl = lane // NSA_TQ
    tqf = (t0 + lane % NSA_TQ).astype(F32)
    ridx = _iota((16, NSA_N), 0)
    zeros_h = jnp.zeros((HEAD_DIM, NSA_TQ), F32)
    l2e = _bf16_terms(LOG2E)
    q_top, aug16 = [], []
    for g in range(NSA_GROUPS):
        tops = []
        for h in range(NSA_HPG):
            head = NSA_HPG * g + h
            qh = qt[head * HEAD_DIM:(head + 1) * HEAD_DIM]
            tops.append(jnp.concatenate([qh, zeros_h] if g == 0 else [zeros_h, qh], axis=0))
        q_top.append(jnp.concatenate(tops, axis=1))
        slope = lax.bitcast_convert_type((127 - (NSA_HPG * g + hl + 1)) << 23, F32)
        rows = {}
        for i, term in enumerate(l2e):
            rows[AUG_POS + 2 * i] = slope * (64.0 * term)
            rows[AUG_POS + 2 * i + 1] = slope * term
        for i, term in enumerate(_split3(-(slope * tqf) * LOG2E)):
            rows[AUG_TQ + i] = term.astype(F32)
        a = jnp.zeros((16, NSA_N), F32)
        for r, val in rows.items():
            a = jnp.where(ridx == r, val, a)
        aug16.append(a)
    return q_top, aug16


def _nsa_compressed(s_cmp, blk, vct_ref, ovt, gs):
    tq = blk * NSA_TQ + _iota((1, NSA_N), 1) % NSA_TQ
    n_idx = _iota((LANES, NSA_N), 0)
    valid = (n_idx * CMP_STRIDE + (CMP_BLOCK - 1) <= tq) & (n_idx < LANES - 1)
    o_cmp, imps = [], []
    for g in gs:
        sm = jnp.where(valid, s_cmp[g], NEG_BIG)
        mx = jnp.max(sm, axis=0, keepdims=True)
        e = jnp.where(valid, jnp.exp2(sm - mx), 0.0)
        den = jnp.sum(e, axis=0, keepdims=True)
        pn = e / jnp.where(den > 0.0, den, 1.0)
        o_cmp.append(_dot(vct_ref[g * HEAD_DIM:(g + 1) * HEAD_DIM, :], pn.astype(BF16)))

        psum = pn[:, 0:NSA_TQ]
        for h in range(1, NSA_HPG):
            psum = psum + pn[:, h * NSA_TQ:(h + 1) * NSA_TQ]
        hi, mid, lo = _split3(psum)
        imps.append(_dot(ovt, hi) + _dot(ovt, mid) + _dot(ovt, lo))
    return o_cmp, imps


def _nsa_select(imps, blk, q_top, aug16, gs):
    t0 = blk * NSA_TQ
    ns = imps[0].shape[0]
    tcol = t0 + _iota((8, NSA_TQ), 1)
    q_sel = []
    for g, imp in zip(gs, imps):
        tiles = range(ns // 8)
        j_idx = [8 * r + _iota((8, NSA_TQ), 0) for r in tiles]
        in_past = [j_idx[r] * SEL_BLOCK <= tcol for r in tiles]
        cur = tcol // SEL_BLOCK
        score = []
        for r in tiles:
            forced = (j_idx[r] == 0) | (j_idx[r] == cur) | (j_idx[r] == cur - 1)
            score.append(jnp.where(forced, FORCE_SCORE, jnp.where(in_past[r], imp[8 * r:8 * r + 8], -1.0)))
        rank = [jnp.zeros((8, NSA_TQ), F32) for _ in tiles]
        for i in range(ns):
            si = score[i // 8][i % 8:i % 8 + 1, :]
            for r in tiles:
                if 8 * r > i:
                    beats = si >= score[r]
                elif 8 * r + 7 <= i:
                    beats = si > score[r]
                else:
                    beats = (si > score[r]) | ((si == score[r]) & (j_idx[r] > i))
                rank[r] = rank[r] + jnp.where(beats, 1.0, 0.0)
        maskval = jnp.concatenate([jnp.where((rank[r] < min(N_SELECT, ns)) & in_past[r], 0.0, NEG_BIG) for r in tiles], axis=0)
        mask4 = jnp.concatenate([maskval] * NSA_HPG, axis=1)
        q_sel.append(jnp.concatenate([q_top[g], aug16[g], jnp.zeros((AUG_MASK - 16, NSA_N), F32), mask4,
                                      jnp.zeros((LANES - AUG_MASK - ns, NSA_N), F32)], axis=0).astype(BF16))
    return q_sel


def _nsa_kernel(q_ref, gf_ref, kc_ref, vc_ref, ks_ref, vs_ref, kw_ref, vw_ref,
                augk_ref, augw_ref, augc_ref, ovt_ref, trisel_ref, triwin_ref, o_ref,
                vst_ref, vwt_ref, kwp_ref, vct_ref, gt_ref, m_ref, acc_ref, s0_ref, s1_ref, mx0_ref, mx1_ref):
    qb = pl.program_id(1)
    t = ks_ref.shape[1]
    t0 = qb * NSA_TQ
    groups = range(NSA_GROUPS)
    zeros_tail = jnp.zeros((LANES - 16, NSA_N), F32)

    def cmp_scores(q_top, aug16):
        kcb = jnp.concatenate([kc_ref[0], augc_ref[...]], axis=1)
        return [_dot(kcb, jnp.concatenate([q_top[g], aug16[g], zeros_tail], axis=0).astype(BF16)) for g in groups]

    @pl.when(qb == 0)
    def _():
        vwt_ref[:, 0:WINDOW] = jnp.zeros((LANES, WINDOW), BF16)
        kwp_ref[0:WINDOW, :] = jnp.zeros((WINDOW, LANES), BF16)
        kwp_ref[WINDOW:WINDOW + t, :] = kw_ref[0]
        for c in range(t // LANES):
            sl = slice(c * LANES, (c + 1) * LANES)
            vst_ref[:, sl] = vs_ref[0, sl, :].astype(F32).T.astype(BF16)
            vwt_ref[:, WINDOW + c * LANES:WINDOW + (c + 1) * LANES] = vw_ref[0, sl, :].astype(F32).T.astype(BF16)
        vct_ref[...] = vc_ref[0].astype(F32).T.astype(BF16)

    q_top, aug16 = _nsa_queries(q_ref, qb)
    s_cmp = cmp_scores(q_top, aug16)
    kw0 = pl.multiple_of(t0, NSA_TQ)
    kwin = jnp.concatenate([kwp_ref[pl.ds(kw0, WIN_KEYS), :], augw_ref[pl.ds(kw0, WIN_KEYS), :]], axis=1)
    pad_row = _iota((16, NSA_N), 0) == AUG_PAD
    s_win = [_dot(kwin, jnp.concatenate([q_top[g], jnp.where(pad_row, NEG_BIG, aug16[g]), zeros_tail], axis=0).astype(BF16))
             for g in groups]
    o_cmp, imps = _nsa_compressed(s_cmp, qb, vct_ref, ovt_ref[...], groups)
    q_sel = _nsa_select(imps, qb, q_top, aug16, groups)

    hi_edge = jnp.concatenate([triwin_ref[0]] * NSA_HPG, axis=1)
    lo_edge = jnp.concatenate([triwin_ref[1]] * NSA_HPG, axis=1)
    ones_w = _ones_rows(WIN_KEYS)
    s_win = [jnp.concatenate([s[0:NSA_TQ] + lo_edge, s[NSA_TQ:WINDOW], s[WINDOW:WIN_KEYS] + hi_edge], axis=0)
             for s in s_win]
    vt_win = [jnp.concatenate([vwt_ref[g * HEAD_DIM:(g + 1) * HEAD_DIM, pl.ds(kw0, WIN_KEYS)], ones_w], axis=0)
              for g in groups]
    o_win = [acc[0:HEAD_DIM] / acc[HEAD_DIM:HEAD_DIM + 1] for acc in _softmax_pv(s_win, vt_win)]

    ones_s = _ones_rows(SEL_TK)
    per_chunk = SEL_TK // NSA_TQ
    n_full = qb // per_chunk
    s_bufs = (s0_ref, s1_ref)
    mx_bufs = (mx0_ref, mx1_ref)

    def sel_scores(cb):
        k0 = pl.multiple_of(cb * SEL_TK, SEL_TK)
        kblk = jnp.concatenate([ks_ref[0, pl.ds(k0, SEL_TK), :], augk_ref[pl.ds(k0, SEL_TK), :]], axis=1)
        return [_dot(kblk, q_sel[g]) for g in groups]

    def qk_to(buf, j):
        for g, s in enumerate(sel_scores(j - 1)):
            s_bufs[buf][g] = s
            mx_bufs[buf][g] = jnp.max(s, axis=0, keepdims=True)

    def spv_from(buf, j):
        cb = jnp.where(j == 0, n_full, j - 1)
        k0 = pl.multiple_of(cb * SEL_TK, SEL_TK)
        vts = [jnp.concatenate([vst_ref[g * HEAD_DIM:(g + 1) * HEAD_DIM, pl.ds(k0, SEL_TK)], ones_s], axis=0)
               for g in groups]
        _softmax_pv([s_bufs[buf][g] for g in groups], vts,
                    [m_ref.at[g] for g in groups], [acc_ref.at[g] for g in groups],
                    maxes=[mx_bufs[buf][g] for g in groups])

    diag = trisel_ref[qb % per_chunk]
    diag4 = jnp.concatenate([diag] * NSA_HPG, axis=1)
    for g, s in enumerate(sel_scores(n_full)):
        s = s + diag4
        s0_ref[g] = s
        mx0_ref[g] = jnp.max(s, axis=0, keepdims=True)
        _flash_init(m_ref.at[g], acc_ref.at[g])

    def gated_output():
        gt_ref[...] = gf_ref[0].T
        sig = jax.nn.sigmoid(gt_ref[GF_GATE:GF_GATE + 3 * NSA_HEADS, :])
        outs = []
        for g in groups:
            acc = acc_ref[g]
            o_sel = acc[0:HEAD_DIM] / acc[HEAD_DIM:HEAD_DIM + 1]
            for h in range(NSA_HPG):
                sl = slice(h * NSA_TQ, (h + 1) * NSA_TQ)
                head = NSA_HPG * g + h
                g0 = sig[head:head + 1]
                g1 = sig[NSA_HEADS + head:NSA_HEADS + head + 1]
                g2 = sig[2 * NSA_HEADS + head:2 * NSA_HEADS + head + 1]
                outs.append(g0 * o_cmp[g][:, sl] + g1 * o_sel[:, sl] + g2 * o_win[g][:, sl])
        o_ref[0] = jnp.concatenate(outs, axis=0).T.astype(BF16)

    _pipelined_sweep(n_full, qk_to, spv_from, gated_output)


def _nsa(proj, gf, kcmp, vcmp, augk, augw, augc, ovt, trisel, triwin):
    b, t, _ = proj.shape
    nq = t // NSA_TQ
    kvb = COL_KV // LANES
    width = NSA_HEADS * HEAD_DIM
    whole = lambda a: pl.BlockSpec(a.shape, lambda i, k: (0,) * a.ndim)
    kv = lambda idx: pl.BlockSpec((1, t, LANES), lambda i, k: (i, 0, idx))
    return pl.pallas_call(
        _nsa_kernel,
        grid=(b, nq),
        in_specs=[
            pl.BlockSpec((1, NSA_TQ, width), lambda i, k: (i, k, COL_QN // width)),
            pl.BlockSpec((1, NSA_TQ, LANES), lambda i, k: (i, k, 0)),
            pl.BlockSpec((1, LANES, LANES), lambda i, k: (i, 0, 0)),
            pl.BlockSpec((1, LANES, LANES), lambda i, k: (i, 0, 0)),
            kv(kvb + 2), kv(kvb + 3), kv(kvb + 4), kv(kvb + 5),
            whole(augk), whole(augw), whole(augc), whole(ovt), whole(trisel), whole(triwin),
        ],
        out_specs=pl.BlockSpec((1, NSA_TQ, width), lambda i, k: (i, k, 0)),
        out_shape=jax.ShapeDtypeStruct((b, t, width), BF16),
        scratch_shapes=[
            pltpu.VMEM((LANES, t), BF16),
            pltpu.VMEM((LANES, t + WINDOW), BF16),
            pltpu.VMEM((t + WINDOW, LANES), BF16),
            pltpu.VMEM((LANES, LANES), BF16),
            pltpu.VMEM((LANES, NSA_TQ), F32),
            pltpu.VMEM((NSA_GROUPS, 1, NSA_N), F32),
            pltpu.VMEM((NSA_GROUPS, HEAD_DIM + 16, NSA_N), F32),
            pltpu.VMEM((NSA_GROUPS, SEL_TK, NSA_N), F32),
            pltpu.VMEM((NSA_GROUPS, SEL_TK, NSA_N), F32),
            pltpu.VMEM((NSA_GROUPS, 1, NSA_N), F32),
            pltpu.VMEM((NSA_GROUPS, 1, NSA_N), F32),
        ],
        compiler_params=pltpu.CompilerParams(
            dimension_semantics=("arbitrary", "arbitrary"), vmem_limit_bytes=VMEM_LIMIT),
        name="nsa_attention",
    )(proj, gf, kcmp, vcmp, proj, proj, proj, proj, augk, augw, augc, ovt, trisel, triwin)


FOX_CBLK = 256


def _foxprep_kernel(gf_ref, bias_ref, cq_ref, augk_ref):
    t = gf_ref.shape[1]
    cols = []
    for c in range(t // LANES):
        cols.append(gf_ref[0, c * LANES:(c + 1) * LANES, :].T[GF_FOX:GF_FOX + FOX_HEADS])
    f = jnp.concatenate(cols, axis=1) + bias_ref[...]
    ls = jnp.minimum(f, 0.0) - jnp.log(1.0 + jnp.exp(-jnp.abs(f)))
    upper = (_iota((FOX_CBLK, FOX_CBLK), 0) <= _iota((FOX_CBLK, FOX_CBLK), 1)).astype(BF16)
    carry = jnp.zeros((FOX_HEADS, 1), F32)
    parts = []
    for c in range(t // FOX_CBLK):
        hi, mid, lo = _split3(ls[:, c * FOX_CBLK:(c + 1) * FOX_CBLK])
        cb = _dot(hi, upper) + _dot(mid, upper) + _dot(lo, upper) + carry
        carry = cb[:, FOX_CBLK - 1:FOX_CBLK]
        parts.append(cb)
    csum = jnp.concatenate(parts, axis=1) * LOG2E
    c1, c2, c3 = (v.astype(F32) for v in _split3(csum))
    ones = jnp.ones((3 * FOX_HEADS, t), F32)
    zeros = jnp.zeros((LANES - 6 * FOX_HEADS, t), F32)
    cq_ref[0] = jnp.concatenate([c1, c2, c3, ones, zeros], axis=0).astype(BF16)
    slab_k = jnp.concatenate([ones, -c1, -c2, -c3, zeros], axis=0)
    for c in range(t // LANES):
        sl = slice(c * LANES, (c + 1) * LANES)
        augk_ref[0, sl, :] = slab_k[:, sl].T.astype(BF16)


def _foxprep(gf, bias):
    b, t, _ = gf.shape
    return pl.pallas_call(
        _foxprep_kernel,
        grid=(b,),
        in_specs=[pl.BlockSpec((1, t, LANES), lambda i: (i, 0, 0)),
                  pl.BlockSpec(bias.shape, lambda i: (0, 0))],
        out_specs=[pl.BlockSpec((1, LANES, t), lambda i: (i, 0, 0)),
                   pl.BlockSpec((1, t, LANES), lambda i: (i, 0, 0))],
        out_shape=[jax.ShapeDtypeStruct((b, LANES, t), BF16),
                   jax.ShapeDtypeStruct((b, t, LANES), BF16)],
        compiler_params=pltpu.CompilerParams(dimension_semantics=("arbitrary",)),
        name="fox_prep",
    )(gf, bias)


FOX_TQ = 256
FOX_TK = 256
FOX_NP = FOX_HEADS // 2
FOX_N = 2 * FOX_TQ
FOX_ROWS = HEAD_DIM + 16


def _fox_kernel(q_ref, k_ref, v_ref, cq_ref, augk_ref, tri_ref, o_ref, vt_ref, qs_ref, m_ref, acc_ref,
                s0_ref, s1_ref, sd_ref, mx0_ref, mx1_ref, mxd_ref):
    qb = pl.program_id(1)
    nq = pl.num_programs(1)
    t = k_ref.shape[1]
    pairs = range(FOX_NP)
    s_bufs = (s0_ref, s1_ref, sd_ref)
    mx_bufs = (mx0_ref, mx1_ref, mxd_ref)
    diag = 2

    def scores(kb, qs):
        k0 = pl.multiple_of(kb * FOX_TK, FOX_TK)
        ak = augk_ref[0, pl.ds(k0, FOX_TK), :]
        return [_dot(jnp.concatenate([k_ref[0, pl.ds(k0, FOX_TK), p * LANES:(p + 1) * LANES], ak], axis=1), qs(p))
                for p in pairs]

    def prepare(blk, slot):
        q0 = pl.multiple_of(blk * FOX_TQ, FOX_TQ)
        row = _iota((LANES, FOX_TQ), 0)
        hrow = row % FOX_HEADS
        cq = cq_ref[0, :, pl.ds(q0, FOX_TQ)].astype(F32)
        for p in pairs:
            qt = q_ref[0, pl.ds(q0, FOX_TQ), p * LANES:(p + 1) * LANES].astype(F32).T * (HEAD_DIM ** -0.5 * LOG2E)
            q_top = jnp.concatenate([jnp.where(row < HEAD_DIM, qt, 0.0), jnp.where(row >= HEAD_DIM, qt, 0.0)], axis=1)
            q_aug = jnp.concatenate([jnp.where(hrow == 2 * p, cq, 0.0), jnp.where(hrow == 2 * p + 1, cq, 0.0)],
                                    axis=1)
            qs_ref[slot, p] = jnp.concatenate([q_top, q_aug], axis=0).astype(BF16)
        tri = tri_ref[...]
        for p, s in enumerate(scores(blk, lambda p: qs_ref[slot, p])):
            s = s + tri
            sd_ref[p] = s
            mxd_ref[p] = jnp.max(s, axis=0, keepdims=True)

    def qk_to(buf, kb):
        for p, s in enumerate(scores(kb, lambda p: qs_ref[qb % 2, p])):
            s_bufs[buf][p] = s
            mx_bufs[buf][p] = jnp.max(s, axis=0, keepdims=True)

    def spv_from(buf, kb):
        k0 = pl.multiple_of(kb * FOX_TK, FOX_TK)
        vts = [(vt_ref[p, 0, :, pl.ds(k0, FOX_TK)], vt_ref[p, 1, :, pl.ds(k0, FOX_TK)]) for p in pairs]
        _softmax_pv([s_bufs[buf][p] for p in pairs], vts,
                    [m_ref.at[p] for p in pairs], [acc_ref.at[p] for p in pairs],
                    maxes=[mx_bufs[buf][p] for p in pairs])

    def finish():
        for p in pairs:
            acc = acc_ref[p]
            den = acc[HEAD_DIM:HEAD_DIM + 1]
            o_a = acc[0:HEAD_DIM, 0:FOX_TQ] / den[:, 0:FOX_TQ]
            o_b = acc[0:HEAD_DIM, FOX_TQ:FOX_N] / den[:, FOX_TQ:FOX_N]
            o_ref[0, :, p * LANES:(p + 1) * LANES] = jnp.concatenate([o_a, o_b], axis=0).T.astype(BF16)
        prepare(jnp.minimum(qb + 1, nq - 1), (qb + 1) % 2)

    @pl.when(qb == 0)
    def _():
        for p in pairs:
            for c in range(t // LANES):
                sl = slice(c * LANES, (c + 1) * LANES)
                vt = v_ref[0, sl, p * LANES:(p + 1) * LANES].astype(F32).T.astype(BF16)
                for hh in range(2):
                    vt_ref[p, hh, 0:HEAD_DIM, sl] = vt[hh * HEAD_DIM:(hh + 1) * HEAD_DIM]
            for hh in range(2):
                vt_ref[p, hh, HEAD_DIM:FOX_ROWS, :] = _ones_rows(t)
        prepare(0, 0)

    for p in pairs:
        _flash_init(m_ref.at[p], acc_ref.at[p])

    @pl.when(qb == 0)
    def _():
        spv_from(diag, qb)
        finish()

    @pl.when(qb > 0)
    def _():
        qk_to(0, 0)
        spv_from(diag, qb)

    def body(jj, carry):
        kb = 2 * jj
        qk_to(1, kb + 1)
        spv_from(0, kb)
        qk_to(0, kb + 2)
        spv_from(1, kb + 1)
        return carry

    lax.fori_loop(0, jnp.maximum(qb - 1, 0) // 2, body, 0)

    @pl.when(qb % 2 == 1)
    def _():
        spv_from(0, qb - 1)
        finish()

    @pl.when((qb > 0) & (qb % 2 == 0))
    def _():
        qk_to(1, qb - 1)
        spv_from(0, qb - 2)
        spv_from(1, qb - 1)
        finish()


def _fox(proj, cq, augk, tri):
    b, t, _ = proj.shape
    nq = t // FOX_TQ
    width = FOX_HEADS * HEAD_DIM
    qf, kf, vf = COL_QF // width, COL_KF // width, COL_VF // width
    return pl.pallas_call(
        _fox_kernel,
        grid=(b, nq),
        in_specs=[
            pl.BlockSpec((1, t, width), lambda i, k: (i, 0, qf)),
            pl.BlockSpec((1, t, width), lambda i, k: (i, 0, kf)),
            pl.BlockSpec((1, t, width), lambda i, k: (i, 0, vf)),
            pl.BlockSpec((1, LANES, t), lambda i, k: (i, 0, 0)),
            pl.BlockSpec((1, t, LANES), lambda i, k: (i, 0, 0)),
            pl.BlockSpec(tri.shape, lambda i, k: (0, 0)),
        ],
        out_specs=pl.BlockSpec((1, FOX_TQ, width), lambda i, k: (i, k, 0)),
        out_shape=jax.ShapeDtypeStruct((b, t, width), BF16),
        scratch_shapes=[
            pltpu.VMEM((FOX_NP, 2, FOX_ROWS, t), BF16),
            pltpu.VMEM((2, FOX_NP, 2 * LANES, FOX_N), BF16),
            pltpu.VMEM((FOX_NP, 1, FOX_N), F32),
            pltpu.VMEM((FOX_NP, FOX_ROWS, FOX_N), F32),
            pltpu.VMEM((FOX_NP, FOX_TK, FOX_N), F32),
            pltpu.VMEM((FOX_NP, FOX_TK, FOX_N), F32),
            pltpu.VMEM((FOX_NP, FOX_TK, FOX_N), F32),
            pltpu.VMEM((FOX_NP, 1, FOX_N), F32),
            pltpu.VMEM((FOX_NP, 1, FOX_N), F32),
            pltpu.VMEM((FOX_NP, 1, FOX_N), F32),
        ],
        compiler_params=pltpu.CompilerParams(
            dimension_semantics=("arbitrary", "arbitrary"), vmem_limit_bytes=VMEM_LIMIT),
        name="fox_attention",
    )(proj, proj, proj, cq, augk, tri)


def _post_kernel(x_ref, on_ref, of_ref, mg_ref, wn_ref, wf_ref, wo_ref, gm_ref, wu_ref, wd_ref, gfin_ref, o_ref):
    d = x_ref.shape[1]
    ff = wu_ref.shape[1]
    ga = mg_ref[:, 0:d].astype(F32)
    gb = mg_ref[:, d:2 * d].astype(F32)
    y = ga * _dot(on_ref[...], wn_ref[...]) + gb * _dot(of_ref[...], wf_ref[...])
    h = x_ref[...] + _dot(y.astype(BF16), wo_ref[...])
    r = lax.rsqrt(jnp.mean(h * h, axis=-1, keepdims=True) + NORM_EPS)
    v = (h * r * gm_ref[...]).astype(BF16)
    acc = h
    for c0 in range(0, ff, FF_CHUNK):
        up = jnp.maximum(_dot(v, wu_ref[:, c0:c0 + FF_CHUNK]), 0.0)
        acc = acc + _dot((up * up).astype(BF16), wd_ref[c0:c0 + FF_CHUNK, :])
    r2 = lax.rsqrt(jnp.mean(acc * acc, axis=-1, keepdims=True) + NORM_EPS)
    o_ref[...] = acc * r2 * gfin_ref[...]


def _post(x2, on, of, mg, wn, wf, wo, gm, wu, wd, gfin, tm):
    n, d = x2.shape
    row = lambda w: pl.BlockSpec((tm, w), lambda i: (i, 0))
    const = lambda a: pl.BlockSpec(a.shape, lambda i: (0, 0), pipeline_mode=pl.Buffered(1))
    return pl.pallas_call(
        _post_kernel,
        grid=(n // tm,),
        in_specs=[row(d), row(on.shape[1]), row(of.shape[1]), row(2 * d),
                  const(wn), const(wf), const(wo), const(gm), const(wu), const(wd), const(gfin)],
        out_specs=row(d),
        out_shape=jax.ShapeDtypeStruct((n, d), F32),
        compiler_params=pltpu.CompilerParams(
            dimension_semantics=("arbitrary",), vmem_limit_bytes=VMEM_LIMIT),
        name="post_mlp",
    )(x2, on, of, mg, wn, wf, wo, gm, wu, wd, gfin)


def _position_constants(t):
    def key_side(pos, pad):
        a = np.zeros((pos.size, LANES), np.float32)
        for i in range(3):
            a[:, AUG_POS + 2 * i] = pos // 64
            a[:, AUG_POS + 2 * i + 1] = pos % 64
        a[:, AUG_TQ:AUG_TQ + 3] = 1.0
        a[:, AUG_PAD] = pad
        return a

    pos = np.arange(t)
    augk = key_side(pos, 0.0)
    augk[pos, AUG_MASK + pos // SEL_BLOCK] = 1.0
    augw = np.concatenate([key_side(np.zeros(WINDOW, np.int64), 1.0), key_side(pos, 0.0)], axis=0)
    augc = key_side(np.arange(LANES) * CMP_STRIDE + CMP_BLOCK - 1, 0.0)

    nc = (t - CMP_BLOCK) // CMP_STRIDE + 1
    ns = t // SEL_BLOCK
    n = np.arange(LANES)[None, :]
    j = np.arange(ns)[:, None]
    ovt = ((n * CMP_STRIDE <= j * SEL_BLOCK + SEL_BLOCK - 1) & (n * CMP_STRIDE + CMP_BLOCK - 1 >= j * SEL_BLOCK) & (n < nc))

    def tri(tk, tq, anti=False):
        kk = np.arange(tk)[:, None]
        tt = np.arange(tq)[None, :]
        ok = (kk > tt) if anti else (kk <= tt)
        return np.where(ok, 0.0, NEG_BIG).astype(np.float32)

    edge = tri(NSA_TQ, NSA_TQ)
    trisel = np.zeros((SEL_TK // NSA_TQ, SEL_TK, NSA_TQ), np.float32)
    for v in range(SEL_TK // NSA_TQ):
        trisel[v, v * NSA_TQ:(v + 1) * NSA_TQ] = edge
    triwin = np.stack([edge, tri(NSA_TQ, NSA_TQ, anti=True)])
    trifox = np.tile(tri(FOX_TK, FOX_TQ), (1, 2))
    return (jnp.asarray(augk, BF16), jnp.asarray(augw, BF16), jnp.asarray(augc, BF16), jnp.asarray(ovt, BF16),
            jnp.asarray(trisel), jnp.asarray(triwin), jnp.asarray(trifox))


def kernel(x, norm_mix, w_in, cmp_pos_k, cmp_w1_k, cmp_w2_k, cmp_pos_v, cmp_w1_v, cmp_w2_v, fox_f_bias, w_branch_nsa, w_branch_fox, w_merge_gate, b_merge_gate, w_out, norm_mlp, w_up, w_down, norm_final):
    b, t, d = x.shape
    assert w_in.shape[0] == 1, "one layer: the final norm is fused into the MLP kernel"
    assert t % FOX_TQ == 0 and t // SEL_BLOCK == 32 and (t - CMP_BLOCK) // CMP_STRIDE + 1 == LANES - 1
    augk, augw, augc, ovt, trisel, triwin, trifox = _position_constants(t)
    h = x.reshape(b * t, d)
    proj, gf, mg, wn, wf, wo, wu, wd = _inproj(h, norm_mix[0][None, :], jnp.swapaxes(w_in, 1, 2), w_merge_gate,
                                               b_merge_gate[0][None, :],
                                               (w_branch_nsa, w_branch_fox, w_out, w_up, w_down), ROW_TILE)
    proj = proj.reshape(b, t, PROJ_W)
    gf = gf.reshape(b, t, LANES)

    kcmp, vcmp = _compress(proj, cmp_w1_k, cmp_w2_k, cmp_pos_k, cmp_w1_v, cmp_w2_v, cmp_pos_v)
    o_n = _nsa(proj, gf, kcmp, vcmp, augk, augw, augc, ovt, trisel, triwin)
    cq, faug = _foxprep(gf, fox_f_bias[0][:, None])
    o_f = _fox(proj, cq, faug, trifox)
    out = _post(h, o_n.reshape(b * t, -1), o_f.reshape(b * t, -1), mg,
                wn, wf, wo, norm_mlp[0][None, :], wu, wd, norm_final[None, :], ROW_TILE)
    return out.reshape(b, t, d)
```

```python
import numpy as np
import jax
import jax.numpy as jnp
from jax import lax
from jax.experimental import pallas as pl
from jax.experimental.pallas import tpu as pltpu

HEAD_DIM = 64
NSA_HEADS = 8
NSA_GROUPS = 2
NSA_HPG = NSA_HEADS // NSA_GROUPS
FOX_HEADS = 8
CMP_BLOCK = 32
CMP_STRIDE = 16
CMP_HIDDEN = 128
SEL_BLOCK = 64
N_SELECT = 16
WINDOW = 512
NORM_EPS = 1e-6
FORCE_SCORE = 1e9

LANES = 128
NEG_BIG = -(2.0 ** 100)
M_INIT = -3.0e38
LOG2E = 1.4426950408889634
VMEM_LIMIT = 56 * 1024 * 1024
ROW_TILE = 512
ROW_SPLIT = 2
FF_CHUNK = 1024

PROJ_W = 2816
COL_QN = 0
COL_QF = 512
COL_KF = 1024
COL_VF = 1536
COL_KV = 2048
GF_GATE = 0
GF_FOX = 24

F32 = jnp.float32
BF16 = jnp.bfloat16


def _dot(a, b):
    return jnp.dot(a, b, preferred_element_type=F32)


def _iota(shape, dim):
    return lax.broadcasted_iota(jnp.int32, shape, dim)


def _split3(x):
    hi = x.astype(BF16)
    r1 = x - hi.astype(F32)
    mid = r1.astype(BF16)
    lo = (r1 - mid.astype(F32)).astype(BF16)
    return hi, mid, lo


W_QN = NSA_HEADS * HEAD_DIM
W_KV = 6 * NSA_GROUPS * HEAD_DIM
W_GATE = 3 * NSA_HEADS
W_FOX = 3 * FOX_HEADS * HEAD_DIM
W_PIECES = ((0, W_QN), (W_QN + W_KV + W_GATE, W_FOX), (W_QN, W_KV))


def _inproj_kernel(x_ref, g_ref, wt_ref, wm32_ref, bm_ref, *refs):
    n_later = (len(refs) - 8) // 2
    later_in, (proj_ref, gf_ref, mg_ref) = refs[:n_later], refs[n_later:n_later + 3]
    later_out = refs[n_later + 3:2 * n_later + 3]
    wq_ref, wf_ref, wkv_ref, wg_ref, wm_ref = refs[2 * n_later + 3:]
    w_refs = (wq_ref, wf_ref, wkv_ref)

    for src, dst in zip(later_in, later_out):
        dst[...] = src[0].astype(BF16)

    @pl.when(pl.program_id(0) == 0)
    def _():
        d = wt_ref.shape[2]
        for (c0, width), dst in zip(W_PIECES, w_refs):
            for k0 in range(0, width, 256):
                dst[:, k0:k0 + 256] = wt_ref[0, c0 + k0:c0 + k0 + 256, :].T.astype(BF16)
        gate0 = W_QN + W_KV
        fox0 = gate0 + W_GATE + W_FOX
        gates = jnp.concatenate([wt_ref[0, gate0:gate0 + W_GATE, :], wt_ref[0, fox0:fox0 + FOX_HEADS, :],
                                 jnp.zeros((LANES - W_GATE - FOX_HEADS, d), F32)], axis=0)
        wg_ref[...] = gates.T.astype(BF16)
        for r0 in range(0, d, 256):
            wm_ref[r0:r0 + 256, :] = wm32_ref[0, r0:r0 + 256, :].astype(BF16)

    sub = x_ref.shape[0] // ROW_SPLIT
    for h in range(ROW_SPLIT):
        rows = slice(h * sub, (h + 1) * sub)
        x = x_ref[rows, :]
        r = lax.rsqrt(jnp.mean(x * x, axis=-1, keepdims=True) + NORM_EPS)
        u = (x * r * g_ref[...]).astype(BF16)
        c0 = 0
        for piece in w_refs:
            for k0 in range(0, piece.shape[1], 768):
                k1 = min(k0 + 768, piece.shape[1])
                proj_ref[rows, c0 + k0:c0 + k1] = _dot(u, piece[:, k0:k1]).astype(BF16)
            c0 += piece.shape[1]
        gf_ref[rows, :] = _dot(u, wg_ref[...])
        for c0 in (0, 1024):
            z = _dot(u, wm_ref[:, c0:c0 + 1024]) + bm_ref[:, c0:c0 + 1024]
            mg_ref[rows, c0:c0 + 1024] = jax.nn.sigmoid(z).astype(BF16)


def _inproj(x2, g, w_in_t, wm, bm, later, tm):
    n, d = x2.shape
    steps = n // tm
    const = lambda i: (0, 0)
    once = dict(pipeline_mode=pl.Buffered(1))
    slabs = [w.shape[1] // steps for w in later]
    assert all(r * steps == w.shape[1] and r % 16 == 0 for w, r in zip(later, slabs)), "bf16 slabs need 16-row multiples"
    return pl.pallas_call(
        _inproj_kernel,
        grid=(n // tm,),
        in_specs=[
            pl.BlockSpec((tm, d), lambda i: (i, 0)),
            pl.BlockSpec((1, d), const),
            pl.BlockSpec(w_in_t.shape, lambda i: (0, 0, 0), **once),
            pl.BlockSpec(wm.shape, lambda i: (0, 0, 0), **once),
            pl.BlockSpec(bm.shape, const),
        ] + [pl.BlockSpec((1, r, w.shape[2]), lambda i: (0, i, 0)) for w, r in zip(later, slabs)],
        out_specs=[
            pl.BlockSpec((tm, PROJ_W), lambda i: (i, 0)),
            pl.BlockSpec((tm, LANES), lambda i: (i, 0)),
            pl.BlockSpec((tm, 2 * d), lambda i: (i, 0)),
        ] + [pl.BlockSpec((r, w.shape[2]), lambda i: (i, 0)) for w, r in zip(later, slabs)],
        out_shape=[
            jax.ShapeDtypeStruct((n, PROJ_W), BF16),
            jax.ShapeDtypeStruct((n, LANES), F32),
            jax.ShapeDtypeStruct((n, 2 * d), BF16),
        ] + [jax.ShapeDtypeStruct(w.shape[1:], BF16) for w in later],
        scratch_shapes=[pltpu.VMEM((d, width), BF16) for _, width in W_PIECES]
        + [pltpu.VMEM((d, LANES), BF16), pltpu.VMEM((d, 2 * d), BF16)],
        compiler_params=pltpu.CompilerParams(
            dimension_semantics=("arbitrary",), vmem_limit_bytes=VMEM_LIMIT),
        name="inproj",
    )(x2, g, w_in_t, wm, bm, *later)


def _compress_kernel(kc_ref, vc_ref, w1k_ref, w2k_ref, pk_ref, w1v_ref, w2v_ref, pv_ref, kco_ref, vco_ref,
                     xs_ref, w1s_ref, w2s_ref, ps_ref):
    t = kc_ref.shape[1]
    nblk = t // CMP_STRIDE

    @pl.when(pl.program_id(0) == 0)
    def _():
        r = _iota((HEAD_DIM, LANES), 0)
        c = _iota((HEAD_DIM, LANES), 1)
        halves = [jnp.where(c == r + g * HEAD_DIM, 1.0, 0.0).astype(BF16) for g in range(NSA_GROUPS)]
        z = jnp.zeros((HEAD_DIM, CMP_HIDDEN), BF16)
        first_row = _iota((16, LANES), 0) == 0
        for i, (w1_ref, w2_ref, p_ref) in enumerate(((w1k_ref, w2k_ref, pk_ref), (w1v_ref, w2v_ref, pv_ref))):
            for j in range(CMP_STRIDE):
                wa = w1_ref[0, j * HEAD_DIM:(j + 1) * HEAD_DIM, :].astype(BF16)
                wb = w1_ref[0, (CMP_STRIDE + j) * HEAD_DIM:(CMP_STRIDE + j + 1) * HEAD_DIM, :].astype(BF16)
                w1s_ref[i, j] = jnp.concatenate([jnp.concatenate([wa, z, wb, z], axis=1),
                                                 jnp.concatenate([z, wa, z, wb], axis=1)], axis=0)
            w2 = w2_ref[0].T.astype(BF16)
            for g in range(NSA_GROUPS):
                w2s_ref[i, g] = _dot(w2, halves[g]).astype(BF16)
            pos = p_ref[0].astype(BF16)
            prow = _dot(pos[0:CMP_STRIDE], halves[0]) + _dot(pos[CMP_STRIDE:CMP_BLOCK], halves[1])
            for j in range(CMP_STRIDE):
                ps_ref[i, j] = jnp.where(first_row, prow[j:j + 1, :], 0.0).astype(BF16)

    def one(x_ref, w_ref, w2_refs, p_ref, o_ref):
        xs_ref[...] = x_ref[0].astype(F32)
        acc = jnp.zeros((nblk + 16, 4 * LANES), F32)
        for j in range(CMP_STRIDE):
            xj = xs_ref[pl.ds(j, nblk, stride=CMP_STRIDE), :].astype(BF16)
            acc = acc + _dot(jnp.concatenate([xj, p_ref[j]], axis=0), w_ref[j])
        posb = acc[nblk:nblk + 1, 0:LANES] + acc[nblk:nblk + 1, 3 * LANES:4 * LANES]
        out = jnp.zeros((nblk, LANES), F32)
        for g in range(NSA_GROUPS):
            a = acc[0:nblk, g * LANES:(g + 1) * LANES]
            b = acc[0:nblk, (2 + g) * LANES:(3 + g) * LANES]
            h = a + pltpu.roll(b, nblk - 1, 0) + posb
            out = out + _dot(jax.nn.gelu(h).astype(BF16), w2_refs[g][...])
        o_ref[0] = out.astype(BF16)

    one(kc_ref, w1s_ref.at[0], (w2s_ref.at[0, 0], w2s_ref.at[0, 1]), ps_ref.at[0], kco_ref)
    one(vc_ref, w1s_ref.at[1], (w2s_ref.at[1, 0], w2s_ref.at[1, 1]), ps_ref.at[1], vco_ref)


def _compress(proj, w1k, w2k, pk, w1v, w2v, pv):
    b, t, _ = proj.shape
    kvb = COL_KV // LANES
    assert w1k.shape == (1, CMP_BLOCK * HEAD_DIM, CMP_HIDDEN) and w2k.shape == (1, HEAD_DIM, CMP_HIDDEN)
    assert pk.shape == (1, CMP_BLOCK, HEAD_DIM) and CMP_BLOCK == 2 * CMP_STRIDE and CMP_HIDDEN == LANES
    whole = lambda a: pl.BlockSpec(a.shape, lambda i: (0,) * a.ndim)
    ospec = pl.BlockSpec((1, LANES, LANES), lambda i: (i, 0, 0))
    return pl.pallas_call(
        _compress_kernel,
        grid=(b,),
        in_specs=[pl.BlockSpec((1, t, LANES), lambda i: (i, 0, kvb)),
                  pl.BlockSpec((1, t, LANES), lambda i: (i, 0, kvb + 1)),
                  whole(w1k), whole(w2k), whole(pk), whole(w1v), whole(w2v), whole(pv)],
        out_specs=[ospec, ospec],
        out_shape=[jax.ShapeDtypeStruct((b, LANES, LANES), BF16)] * 2,
        scratch_shapes=[pltpu.VMEM((t, LANES), F32),
                        pltpu.VMEM((2, CMP_STRIDE, 2 * HEAD_DIM, 4 * CMP_HIDDEN), BF16),
                        pltpu.VMEM((2, NSA_GROUPS, CMP_HIDDEN, LANES), BF16),
                        pltpu.VMEM((2, CMP_STRIDE, 16, LANES), BF16)],
        compiler_params=pltpu.CompilerParams(dimension_semantics=("arbitrary",)),
        name="compress",
    )(proj, proj, w1k, w2k, pk, w1v, w2v, pv)


def _flash_init(m_ref, acc_ref):
    m_ref[...] = jnp.full(m_ref.shape, M_INIT, F32)
    acc_ref[...] = jnp.zeros(acc_ref.shape, F32)


def _softmax_pv(scores, vt_blks, m_refs=None, acc_refs=None, maxes=None):
    def pv(vt, p):
        if not isinstance(vt, tuple):
            return _dot(vt, p)
        w = p.shape[1] // len(vt)
        return jnp.concatenate([_dot(v, p[:, j * w:(j + 1) * w]) for j, v in enumerate(vt)], axis=1)

    ps, alphas = [], []
    for i, s in enumerate(scores):
        mx = jnp.max(s, axis=0, keepdims=True) if maxes is None else maxes[i]
        if m_refs is None:
            ps.append(jnp.exp2(s - mx).astype(BF16))
            continue
        m_old = m_refs[i][...]
        m_new = jnp.maximum(m_old, mx)
        ps.append(jnp.exp2(s - m_new).astype(BF16))
        alphas.append(jnp.exp2(m_old - m_new))
        m_refs[i][...] = m_new
    if m_refs is None:
        return [pv(vt, p) for vt, p in zip(vt_blks, ps)]
    for vt, p, alpha, acc_ref in zip(vt_blks, ps, alphas, acc_refs):
        acc_ref[...] = acc_ref[...] * alpha + pv(vt, p)
    return None


def _pipelined_sweep(n_rest, qk_to, spv_from, finish):
    def body(jj, carry):
        j = 2 * jj
        qk_to(1, j + 1)
        spv_from(0, j)
        qk_to(0, j + 2)
        spv_from(1, j + 1)
        return carry

    lax.fori_loop(0, n_rest // 2, body, 0)

    @pl.when(n_rest % 2 == 1)
    def _():
        qk_to(1, n_rest)
        spv_from(0, n_rest - 1)
        spv_from(1, n_rest)
        finish()

    @pl.when(n_rest % 2 == 0)
    def _():
        spv_from(0, n_rest)
        finish()


def _ones_rows(tk):
    return jnp.where(_iota((16, tk), 0) == 0, 1.0, 0.0).astype(BF16)


NSA_TQ = 128
NSA_N = NSA_HPG * NSA_TQ
SEL_TK = 512
WIN_KEYS = WINDOW + NSA_TQ

AUG_POS = 0
AUG_TQ = 6
AUG_PAD = 9
AUG_MASK = 32


def _bf16_terms(x, n=3):
    out, r = [], np.float32(x)
    for _ in range(n):
        term = np.asarray(r, np.float32).astype(BF16).astype(np.float32)
        out.append(float(term))
        r = np.float32(r - term)
    return out


def _nsa_queries(q_ref, blk):
    t0 = blk * NSA_TQ
    qt = q_ref[0].astype(F32).T * (HEAD_DIM ** -0.5 * LOG2E)
    lane = _iota((1, NSA_N), 1)
    hl = lane // NSA_TQ
    tqf = (t0 + lane % NSA_TQ).astype(F32)
    ridx = _iota((16, NSA_N), 0)
    zeros_h = jnp.zeros((HEAD_DIM, NSA_TQ), F32)
    l2e = _bf16_terms(LOG2E)
    q_top, aug16 = [], []
    for g in range(NSA_GROUPS):
        tops = []
        for h in range(NSA_HPG):
            head = NSA_HPG * g + h
            qh = qt[head * HEAD_DIM:(head + 1) * HEAD_DIM]
            tops.append(jnp.concatenate([qh, zeros_h] if g == 0 else [zeros_h, qh], axis=0))
        q_top.append(jnp.concatenate(tops, axis=1))
        slope = lax.bitcast_convert_type((127 - (NSA_HPG * g + hl + 1)) << 23, F32)
        rows = {}
        for i, term in enumerate(l2e):
            rows[AUG_POS + 2 * i] = slope * (64.0 * term)
            rows[AUG_POS + 2 * i + 1] = slope * term
        for i, term in enumerate(_split3(-(slope * tqf) * LOG2E)):
            rows[AUG_TQ + i] = term.astype(F32)
        a = jnp.zeros((16, NSA_N), F32)
        for r, val in rows.items():
            a = jnp.where(ridx == r, val, a)
        aug16.append(a)
    return q_top, aug16


def _nsa_compressed(s_cmp, blk, vct_ref, ovt, gs):
    tq = blk * NSA_TQ + _iota((1, NSA_N), 1) % NSA_TQ
    n_idx = _iota((LANES, NSA_N), 0)
    valid = (n_idx * CMP_STRIDE + (CMP_BLOCK - 1) <= tq) & (n_idx < LANES - 1)
    o_cmp, imps = [], []
    for g in gs:
        sm = jnp.where(valid, s_cmp[g], NEG_BIG)
        mx = jnp.max(sm, axis=0, keepdims=True)
        e = jnp.where(valid, jnp.exp2(sm - mx), 0.0)
        den = jnp.sum(e, axis=0, keepdims=True)
        pn = e / jnp.where(den > 0.0, den, 1.0)
        o_cmp.append(_dot(vct_ref[g * HEAD_DIM:(g + 1) * HEAD_DIM, :], pn.astype(BF16)))

        psum = pn[:, 0:NSA_TQ]
        for h in range(1, NSA_HPG):
            psum = psum + pn[:, h * NSA_TQ:(h + 1) * NSA_TQ]
        hi, mid, lo = _split3(psum)
        imps.append(_dot(ovt, hi) + _dot(ovt, mid) + _dot(ovt, lo))
    return o_cmp, imps


def _nsa_select(imps, blk, q_top, aug16, gs):
    t0 = blk * NSA_TQ
    ns = imps[0].shape[0]
    tcol = t0 + _iota((8, NSA_TQ), 1)
    q_sel = []
    for g, imp in zip(gs, imps):
        tiles = range(ns // 8)
        j_idx = [8 * r + _iota((8, NSA_TQ), 0) for r in tiles]
        in_past = [j_idx[r] * SEL_BLOCK <= tcol for r in tiles]
        cur = tcol // SEL_BLOCK
        score = []
        for r in tiles:
            forced = (j_idx[r] == 0) | (j_idx[r] == cur) | (j_idx[r] == cur - 1)
            score.append(jnp.where(forced, FORCE_SCORE, jnp.where(in_past[r], imp[8 * r:8 * r + 8], -1.0)))
        rank = [jnp.zeros((8, NSA_TQ), F32) for _ in tiles]
        for i in range(ns):
            si = score[i // 8][i % 8:i % 8 + 1, :]
            for r in tiles:
                if 8 * r > i:
                    beats = si >= score[r]
                elif 8 * r + 7 <= i:
                    beats = si > score[r]
                else:
                    beats = (si > score[r]) | ((si == score[r]) & (j_idx[r] > i))
                rank[r] = rank[r] + jnp.where(beats, 1.0, 0.0)
        maskval = jnp.concatenate([jnp.where((rank[r] < min(N_SELECT, ns)) & in_past[r], 0.0, NEG_BIG) for r in tiles], axis=0)
        mask4 = jnp.concatenate([maskval] * NSA_HPG, axis=1)
        q_sel.append(jnp.concatenate([q_top[g], aug16[g], jnp.zeros((AUG_MASK - 16, NSA_N), F32), mask4,
                                      jnp.zeros((LANES - AUG_MASK - ns, NSA_N), F32)], axis=0).astype(BF16))
    return q_sel


def _nsa_kernel(q_ref, gf_ref, kc_ref, vc_ref, ks_ref, vs_ref, kw_ref, vw_ref,
                augk_ref, augw_ref, augc_ref, ovt_ref, trisel_ref, triwin_ref, o_ref,
                vst_ref, vwt_ref, kwp_ref, vct_ref, gt_ref, m_ref, acc_ref, s0_ref, s1_ref, mx0_ref, mx1_ref):
    qb = pl.program_id(1)
    t = ks_ref.shape[1]
    t0 = qb * NSA_TQ
    groups = range(NSA_GROUPS)
    zeros_tail = jnp.zeros((LANES - 16, NSA_N), F32)

    def cmp_scores(q_top, aug16):
        kcb = jnp.concatenate([kc_ref[0], augc_ref[...]], axis=1)
        return [_dot(kcb, jnp.concatenate([q_top[g], aug16[g], zeros_tail], axis=0).astype(BF16)) for g in groups]

    @pl.when(qb == 0)
    def _():
        vwt_ref[:, 0:WINDOW] = jnp.zeros((LANES, WINDOW), BF16)
        kwp_ref[0:WINDOW, :] = jnp.zeros((WINDOW, LANES), BF16)
        kwp_ref[WINDOW:WINDOW + t, :] = kw_ref[0]
        for c in range(t // LANES):
            sl = slice(c * LANES, (c + 1) * LANES)
            vst_ref[:, sl] = vs_ref[0, sl, :].astype(F32).T.astype(BF16)
            vwt_ref[:, WINDOW + c * LANES:WINDOW + (c + 1) * LANES] = vw_ref[0, sl, :].astype(F32).T.astype(BF16)
        vct_ref[...] = vc_ref[0].astype(F32).T.astype(BF16)

    q_top, aug16 = _nsa_queries(q_ref, qb)
    s_cmp = cmp_scores(q_top, aug16)
    kw0 = pl.multiple_of(t0, NSA_TQ)
    kwin = jnp.concatenate([kwp_ref[pl.ds(kw0, WIN_KEYS), :], augw_ref[pl.ds(kw0, WIN_KEYS), :]], axis=1)
    pad_row = _iota((16, NSA_N), 0) == AUG_PAD
    s_win = [_dot(kwin, jnp.concatenate([q_top[g], jnp.where(pad_row, NEG_BIG, aug16[g]), zeros_tail], axis=0).astype(BF16))
             for g in groups]
    o_cmp, imps = _nsa_compressed(s_cmp, qb, vct_ref, ovt_ref[...], groups)
    q_sel = _nsa_select(imps, qb, q_top, aug16, groups)

    hi_edge = jnp.concatenate([triwin_ref[0]] * NSA_HPG, axis=1)
    lo_edge = jnp.concatenate([triwin_ref[1]] * NSA_HPG, axis=1)
    ones_w = _ones_rows(WIN_KEYS)
    s_win = [jnp.concatenate([s[0:NSA_TQ] + lo_edge, s[NSA_TQ:WINDOW], s[WINDOW:WIN_KEYS] + hi_edge], axis=0)
             for s in s_win]
    vt_win = [jnp.concatenate([vwt_ref[g * HEAD_DIM:(g + 1) * HEAD_DIM, pl.ds(kw0, WIN_KEYS)], ones_w], axis=0)
              for g in groups]
    o_win = [acc[0:HEAD_DIM] / acc[HEAD_DIM:HEAD_DIM + 1] for acc in _softmax_pv(s_win, vt_win)]

    ones_s = _ones_rows(SEL_TK)
    per_chunk = SEL_TK // NSA_TQ
    n_full = qb // per_chunk
    s_bufs = (s0_ref, s1_ref)
    mx_bufs = (mx0_ref, mx1_ref)

    def sel_scores(cb):
        k0 = pl.multiple_of(cb * SEL_TK, SEL_TK)
        kblk = jnp.concatenate([ks_ref[0, pl.ds(k0, SEL_TK), :], augk_ref[pl.ds(k0, SEL_TK), :]], axis=1)
        return [_dot(kblk, q_sel[g]) for g in groups]

    def qk_to(buf, j):
        for g, s in enumerate(sel_scores(j - 1)):
            s_bufs[buf][g] = s
            mx_bufs[buf][g] = jnp.max(s, axis=0, keepdims=True)

    def spv_from(buf, j):
        cb = jnp.where(j == 0, n_full, j - 1)
        k0 = pl.multiple_of(cb * SEL_TK, SEL_TK)
        vts = [jnp.concatenate([vst_ref[g * HEAD_DIM:(g + 1) * HEAD_DIM, pl.ds(k0, SEL_TK)], ones_s], axis=0)
               for g in groups]
        _softmax_pv([s_bufs[buf][g] for g in groups], vts,
                    [m_ref.at[g] for g in groups], [acc_ref.at[g] for g in groups],
                    maxes=[mx_bufs[buf][g] for g in groups])

    diag = trisel_ref[qb % per_chunk]
    diag4 = jnp.concatenate([diag] * NSA_HPG, axis=1)
    for g, s in enumerate(sel_scores(n_full)):
        s = s + diag4
        s0_ref[g] = s
        mx0_ref[g] = jnp.max(s, axis=0, keepdims=True)
        _flash_init(m_ref.at[g], acc_ref.at[g])

    def gated_output():
        gt_ref[...] = gf_ref[0].T
        sig = jax.nn.sigmoid(gt_ref[GF_GATE:GF_GATE + 3 * NSA_HEADS, :])
        outs = []
        for g in groups:
            acc = acc_ref[g]
            o_sel = acc[0:HEAD_DIM] / acc[HEAD_DIM:HEAD_DIM + 1]
            for h in range(NSA_HPG):
                sl = slice(h * NSA_TQ, (h + 1) * NSA_TQ)
                head = NSA_HPG * g + h
                g0 = sig[head:head + 1]
                g1 = sig[NSA_HEADS + head:NSA_HEADS + head + 1]
                g2 = sig[2 * NSA_HEADS + head:2 * NSA_HEADS + head + 1]
                outs.append(g0 * o_cmp[g][:, sl] + g1 * o_sel[:, sl] + g2 * o_win[g][:, sl])
        o_ref[0] = jnp.concatenate(outs, axis=0).T.astype(BF16)

    _pipelined_sweep(n_full, qk_to, spv_from, gated_output)


def _nsa(proj, gf, kcmp, vcmp, augk, augw, augc, ovt, trisel, triwin):
    b, t, _ = proj.shape
    nq = t // NSA_TQ
    kvb = COL_KV // LANES
    width = NSA_HEADS * HEAD_DIM
    whole = lambda a: pl.BlockSpec(a.shape, lambda i, k: (0,) * a.ndim)
    kv = lambda idx: pl.BlockSpec((1, t, LANES), lambda i, k: (i, 0, idx))
    return pl.pallas_call(
        _nsa_kernel,
        grid=(b, nq),
        in_specs=[
            pl.BlockSpec((1, NSA_TQ, width), lambda i, k: (i, k, COL_QN // width)),
            pl.BlockSpec((1, NSA_TQ, LANES), lambda i, k: (i, k, 0)),
            pl.BlockSpec((1, LANES, LANES), lambda i, k: (i, 0, 0)),
            pl.BlockSpec((1, LANES, LANES), lambda i, k: (i, 0, 0)),
            kv(kvb + 2), kv(kvb + 3), kv(kvb + 4), kv(kvb + 5),
            whole(augk), whole(augw), whole(augc), whole(ovt), whole(trisel), whole(triwin),
        ],
        out_specs=pl.BlockSpec((1, NSA_TQ, width), lambda i, k: (i, k, 0)),
        out_shape=jax.ShapeDtypeStruct((b, t, width), BF16),
        scratch_shapes=[
            pltpu.VMEM((LANES, t), BF16),
            pltpu.VMEM((LANES, t + WINDOW), BF16),
            pltpu.VMEM((t + WINDOW, LANES), BF16),
            pltpu.VMEM((LANES, LANES), BF16),
            pltpu.VMEM((LANES, NSA_TQ), F32),
            pltpu.VMEM((NSA_GROUPS, 1, NSA_N), F32),
            pltpu.VMEM((NSA_GROUPS, HEAD_DIM + 16, NSA_N), F32),
            pltpu.VMEM((NSA_GROUPS, SEL_TK, NSA_N), F32),
            pltpu.VMEM((NSA_GROUPS, SEL_TK, NSA_N), F32),
            pltpu.VMEM((NSA_GROUPS, 1, NSA_N), F32),
            pltpu.VMEM((NSA_GROUPS, 1, NSA_N), F32),
        ],
        compiler_params=pltpu.CompilerParams(
            dimension_semantics=("arbitrary", "arbitrary"), vmem_limit_bytes=VMEM_LIMIT),
        name="nsa_attention",
    )(proj, gf, kcmp, vcmp, proj, proj, proj, proj, augk, augw, augc, ovt, trisel, triwin)


FOX_CBLK = 256


def _foxprep_kernel(gf_ref, bias_ref, cq_ref, augk_ref):
    t = gf_ref.shape[1]
    cols = []
    for c in range(t // LANES):
        cols.append(gf_ref[0, c * LANES:(c + 1) * LANES, :].T[GF_FOX:GF_FOX + FOX_HEADS])
    hrow = _iota((FOX_HEADS, t), 0)
    bias = jnp.zeros((FOX_HEADS, t), F32)
    for h in range(FOX_HEADS):
        bias = jnp.where(hrow == h, bias_ref[0, h], bias)
    f = jnp.concatenate(cols, axis=1) + bias
    ls = jnp.minimum(f, 0.0) - jnp.log(1.0 + jnp.exp(-jnp.abs(f)))
    upper = (_iota((FOX_CBLK, FOX_CBLK), 0) <= _iota((FOX_CBLK, FOX_CBLK), 1)).astype(BF16)
    carry = jnp.zeros((FOX_HEADS, 1), F32)
    parts = []
    for c in range(t // FOX_CBLK):
        hi, mid, lo = _split3(ls[:, c * FOX_CBLK:(c + 1) * FOX_CBLK])
        cb = _dot(hi, upper) + _dot(mid, upper) + _dot(lo, upper) + carry
        carry = cb[:, FOX_CBLK - 1:FOX_CBLK]
        parts.append(cb)
    csum = jnp.concatenate(parts, axis=1) * LOG2E
    c1, c2, c3 = (v.astype(F32) for v in _split3(csum))
    ones = jnp.ones((3 * FOX_HEADS, t), F32)
    zeros = jnp.zeros((LANES - 6 * FOX_HEADS, t), F32)
    cq_ref[0] = jnp.concatenate([c1, c2, c3, ones, zeros], axis=0).astype(BF16)
    slab_k = jnp.concatenate([ones, -c1, -c2, -c3, zeros], axis=0)
    for c in range(t // LANES):
        sl = slice(c * LANES, (c + 1) * LANES)
        augk_ref[0, sl, :] = slab_k[:, sl].T.astype(BF16)


def _foxprep(gf, bias):
    b, t, _ = gf.shape
    return pl.pallas_call(
        _foxprep_kernel,
        grid=(b,),
        in_specs=[pl.BlockSpec((1, t, LANES), lambda i: (i, 0, 0)),
                  pl.BlockSpec(memory_space=pltpu.SMEM)],
        out_specs=[pl.BlockSpec((1, LANES, t), lambda i: (i, 0, 0)),
                   pl.BlockSpec((1, t, LANES), lambda i: (i, 0, 0))],
        out_shape=[jax.ShapeDtypeStruct((b, LANES, t), BF16),
                   jax.ShapeDtypeStruct((b, t, LANES), BF16)],
        compiler_params=pltpu.CompilerParams(dimension_semantics=("arbitrary",)),
        name="fox_prep",
    )(gf, bias)


FOX_TQ = 256
FOX_TK = 256
FOX_NP = FOX_HEADS // 2
FOX_N = 2 * FOX_TQ
FOX_ROWS = HEAD_DIM + 16


def _fox_kernel(q_ref, k_ref, v_ref, cq_ref, augk_ref, tri_ref, o_ref, vt_ref, qs_ref, m_ref, acc_ref,
                s0_ref, s1_ref, sd_ref, mx0_ref, mx1_ref, mxd_ref):
    qb = pl.program_id(1)
    nq = pl.num_programs(1)
    t = k_ref.shape[1]
    pairs = range(FOX_NP)
    s_bufs = (s0_ref, s1_ref, sd_ref)
    mx_bufs = (mx0_ref, mx1_ref, mxd_ref)
    diag = 2

    def scores(kb, qs):
        k0 = pl.multiple_of(kb * FOX_TK, FOX_TK)
        ak = augk_ref[0, pl.ds(k0, FOX_TK), :]
        return [_dot(jnp.concatenate([k_ref[0, pl.ds(k0, FOX_TK), p * LANES:(p + 1) * LANES], ak], axis=1), qs(p))
                for p in pairs]

    def prepare(blk, slot):
        q0 = pl.multiple_of(blk * FOX_TQ, FOX_TQ)
        row = _iota((LANES, FOX_TQ), 0)
        hrow = row % FOX_HEADS
        cq = cq_ref[0, :, pl.ds(q0, FOX_TQ)].astype(F32)
        for p in pairs:
            qt = q_ref[0, pl.ds(q0, FOX_TQ), p * LANES:(p + 1) * LANES].astype(F32).T * (HEAD_DIM ** -0.5 * LOG2E)
            q_top = jnp.concatenate([jnp.where(row < HEAD_DIM, qt, 0.0), jnp.where(row >= HEAD_DIM, qt, 0.0)], axis=1)
            q_aug = jnp.concatenate([jnp.where(hrow == 2 * p, cq, 0.0), jnp.where(hrow == 2 * p + 1, cq, 0.0)],
                                    axis=1)
            qs_ref[slot, p] = jnp.concatenate([q_top, q_aug], axis=0).astype(BF16)
        tri = tri_ref[...]
        for p, s in enumerate(scores(blk, lambda p: qs_ref[slot, p])):
            s = s + tri
            sd_ref[p] = s
            mxd_ref[p] = jnp.max(s, axis=0, keepdims=True)

    def qk_to(buf, kb):
        for p, s in enumerate(scores(kb, lambda p: qs_ref[qb % 2, p])):
            s_bufs[buf][p] = s
            mx_bufs[buf][p] = jnp.max(s, axis=0, keepdims=True)

    def spv_from(buf, kb):
        k0 = pl.multiple_of(kb * FOX_TK, FOX_TK)
        vts = [(vt_ref[p, 0, :, pl.ds(k0, FOX_TK)], vt_ref[p, 1, :, pl.ds(k0, FOX_TK)]) for p in pairs]
        _softmax_pv([s_bufs[buf][p] for p in pairs], vts,
                    [m_ref.at[p] for p in pairs], [acc_ref.at[p] for p in pairs],
                    maxes=[mx_bufs[buf][p] for p in pairs])

    def finish():
        for p in pairs:
            acc = acc_ref[p]
            den = acc[HEAD_DIM:HEAD_DIM + 1]
            o_a = acc[0:HEAD_DIM, 0:FOX_TQ] / den[:, 0:FOX_TQ]
            o_b = acc[0:HEAD_DIM, FOX_TQ:FOX_N] / den[:, FOX_TQ:FOX_N]
            o_ref[0, :, p * LANES:(p + 1) * LANES] = jnp.concatenate([o_a, o_b], axis=0).T.astype(BF16)
        prepare(jnp.minimum(qb + 1, nq - 1), (qb + 1) % 2)

    @pl.when(qb == 0)
    def _():
        for p in pairs:
            for c in range(t // LANES):
                sl = slice(c * LANES, (c + 1) * LANES)
                vt = v_ref[0, sl, p * LANES:(p + 1) * LANES].astype(F32).T.astype(BF16)
                for hh in range(2):
                    vt_ref[p, hh, 0:HEAD_DIM, sl] = vt[hh * HEAD_DIM:(hh + 1) * HEAD_DIM]
            for hh in range(2):
                vt_ref[p, hh, HEAD_DIM:FOX_ROWS, :] = _ones_rows(t)
        prepare(0, 0)

    for p in pairs:
        _flash_init(m_ref.at[p], acc_ref.at[p])

    @pl.when(qb == 0)
    def _():
        spv_from(diag, qb)
        finish()

    @pl.when(qb > 0)
    def _():
        qk_to(0, 0)
        spv_from(diag, qb)

    def body(jj, carry):
        kb = 2 * jj
        qk_to(1, kb + 1)
        spv_from(0, kb)
        qk_to(0, kb + 2)
        spv_from(1, kb + 1)
        return carry

    lax.fori_loop(0, jnp.maximum(qb - 1, 0) // 2, body, 0)

    @pl.when(qb % 2 == 1)
    def _():
        spv_from(0, qb - 1)
        finish()

    @pl.when((qb > 0) & (qb % 2 == 0))
    def _():
        qk_to(1, qb - 1)
        spv_from(0, qb - 2)
        spv_from(1, qb - 1)
        finish()


def _fox(proj, cq, augk, tri):
    b, t, _ = proj.shape
    nq = t // FOX_TQ
    width = FOX_HEADS * HEAD_DIM
    qf, kf, vf = COL_QF // width, COL_KF // width, COL_VF // width
    return pl.pallas_call(
        _fox_kernel,
        grid=(b, nq),
        in_specs=[
            pl.BlockSpec((1, t, width), lambda i, k: (i, 0, qf)),
            pl.BlockSpec((1, t, width), lambda i, k: (i, 0, kf)),
            pl.BlockSpec((1, t, width), lambda i, k: (i, 0, vf)),
            pl.BlockSpec((1, LANES, t), lambda i, k: (i, 0, 0)),
            pl.BlockSpec((1, t, LANES), lambda i, k: (i, 0, 0)),
            pl.BlockSpec(tri.shape, lambda i, k: (0, 0)),
        ],
        out_specs=pl.BlockSpec((1, FOX_TQ, width), lambda i, k: (i, k, 0)),
        out_shape=jax.ShapeDtypeStruct((b, t, width), BF16),
        scratch_shapes=[
            pltpu.VMEM((FOX_NP, 2, FOX_ROWS, t), BF16),
            pltpu.VMEM((2, FOX_NP, 2 * LANES, FOX_N), BF16),
            pltpu.VMEM((FOX_NP, 1, FOX_N), F32),
            pltpu.VMEM((FOX_NP, FOX_ROWS, FOX_N), F32),
            pltpu.VMEM((FOX_NP, FOX_TK, FOX_N), F32),
            pltpu.VMEM((FOX_NP, FOX_TK, FOX_N), F32),
            pltpu.VMEM((FOX_NP, FOX_TK, FOX_N), F32),
            pltpu.VMEM((FOX_NP, 1, FOX_N), F32),
            pltpu.VMEM((FOX_NP, 1, FOX_N), F32),
            pltpu.VMEM((FOX_NP, 1, FOX_N), F32),
        ],
        compiler_params=pltpu.CompilerParams(
            dimension_semantics=("arbitrary", "arbitrary"), vmem_limit_bytes=VMEM_LIMIT),
        name="fox_attention",
    )(proj, proj, proj, cq, augk, tri)


def _post_kernel(x_ref, on_ref, of_ref, mg_ref, wn_ref, wf_ref, wo_ref, gm_ref, wu_ref, wd_ref, gfin_ref, o_ref):
    d = x_ref.shape[1]
    ff = wu_ref.shape[1]
    ga = mg_ref[:, 0:d].astype(F32)
    gb = mg_ref[:, d:2 * d].astype(F32)
    y = ga * _dot(on_ref[...], wn_ref[...]) + gb * _dot(of_ref[...], wf_ref[...])
    h = x_ref[...] + _dot(y.astype(BF16), wo_ref[...])
    r = lax.rsqrt(jnp.mean(h * h, axis=-1, keepdims=True) + NORM_EPS)
    v = (h * r * gm_ref[...]).astype(BF16)
    acc = h
    for c0 in range(0, ff, FF_CHUNK):
        up = jnp.maximum(_dot(v, wu_ref[:, c0:c0 + FF_CHUNK]), 0.0)
        acc = acc + _dot((up * up).astype(BF16), wd_ref[c0:c0 + FF_CHUNK, :])
    r2 = lax.rsqrt(jnp.mean(acc * acc, axis=-1, keepdims=True) + NORM_EPS)
    o_ref[...] = acc * r2 * gfin_ref[...]


def _post(x2, on, of, mg, wn, wf, wo, gm, wu, wd, gfin, tm):
    n, d = x2.shape
    row = lambda w: pl.BlockSpec((tm, w), lambda i: (i, 0))
    const = lambda a: pl.BlockSpec(a.shape, lambda i: (0, 0), pipeline_mode=pl.Buffered(1))
    return pl.pallas_call(
        _post_kernel,
        grid=(n // tm,),
        in_specs=[row(d), row(on.shape[1]), row(of.shape[1]), row(2 * d),
                  const(wn), const(wf), const(wo), const(gm), const(wu), const(wd), const(gfin)],
        out_specs=row(d),
        out_shape=jax.ShapeDtypeStruct((n, d), F32),
        compiler_params=pltpu.CompilerParams(
            dimension_semantics=("arbitrary",), vmem_limit_bytes=VMEM_LIMIT),
        name="post_mlp",
    )(x2, on, of, mg, wn, wf, wo, gm, wu, wd, gfin)


def _position_constants(t):
    def key_side(pos, pad):
        a = np.zeros((pos.size, LANES), np.float32)
        for i in range(3):
            a[:, AUG_POS + 2 * i] = pos // 64
            a[:, AUG_POS + 2 * i + 1] = pos % 64
        a[:, AUG_TQ:AUG_TQ + 3] = 1.0
        a[:, AUG_PAD] = pad
        return a

    pos = np.arange(t)
    augk = key_side(pos, 0.0)
    augk[pos, AUG_MASK + pos // SEL_BLOCK] = 1.0
    augw = np.concatenate([key_side(np.zeros(WINDOW, np.int64), 1.0), key_side(pos, 0.0)], axis=0)
    augc = key_side(np.arange(LANES) * CMP_STRIDE + CMP_BLOCK - 1, 0.0)

    nc = (t - CMP_BLOCK) // CMP_STRIDE + 1
    ns = t // SEL_BLOCK
    n = np.arange(LANES)[None, :]
    j = np.arange(ns)[:, None]
    ovt = ((n * CMP_STRIDE <= j * SEL_BLOCK + SEL_BLOCK - 1) & (n * CMP_STRIDE + CMP_BLOCK - 1 >= j * SEL_BLOCK) & (n < nc))

    def tri(tk, tq, anti=False):
        kk = np.arange(tk)[:, None]
        tt = np.arange(tq)[None, :]
        ok = (kk > tt) if anti else (kk <= tt)
        return np.where(ok, 0.0, NEG_BIG).astype(np.float32)

    edge = tri(NSA_TQ, NSA_TQ)
    trisel = np.zeros((SEL_TK // NSA_TQ, SEL_TK, NSA_TQ), np.float32)
    for v in range(SEL_TK // NSA_TQ):
        trisel[v, v * NSA_TQ:(v + 1) * NSA_TQ] = edge
    triwin = np.stack([edge, tri(NSA_TQ, NSA_TQ, anti=True)])
    trifox = np.tile(tri(FOX_TK, FOX_TQ), (1, 2))
    return (jnp.asarray(augk, BF16), jnp.asarray(augw, BF16), jnp.asarray(augc, BF16), jnp.asarray(ovt, BF16),
            jnp.asarray(trisel), jnp.asarray(triwin), jnp.asarray(trifox))


def kernel(x, norm_mix, w_in, cmp_pos_k, cmp_w1_k, cmp_w2_k, cmp_pos_v, cmp_w1_v, cmp_w2_v, fox_f_bias, w_branch_nsa, w_branch_fox, w_merge_gate, b_merge_gate, w_out, norm_mlp, w_up, w_down, norm_final):
    b, t, d = x.shape
    assert w_in.shape[0] == 1, "one layer: the final norm is fused into the MLP kernel"
    assert t % FOX_TQ == 0 and t // SEL_BLOCK == 32 and (t - CMP_BLOCK) // CMP_STRIDE + 1 == LANES - 1
    augk, augw, augc, ovt, trisel, triwin, trifox = _position_constants(t)
    h = x.reshape(b * t, d)
    proj, gf, mg, wn, wf, wo, wu, wd = _inproj(h, norm_mix[0][None, :], jnp.swapaxes(w_in, 1, 2), w_merge_gate,
                                               b_merge_gate[0][None, :],
                                               (w_branch_nsa, w_branch_fox, w_out, w_up, w_down), ROW_TILE)
    proj = proj.reshape(b, t, PROJ_W)
    gf = gf.reshape(b, t, LANES)

    kcmp, vcmp = _compress(proj, cmp_w1_k, jnp.swapaxes(cmp_w2_k, 1, 2), cmp_pos_k,
                           cmp_w1_v, jnp.swapaxes(cmp_w2_v, 1, 2), cmp_pos_v)
    o_n = _nsa(proj, gf, kcmp, vcmp, augk, augw, augc, ovt, trisel, triwin)
    cq, faug = _foxprep(gf, fox_f_bias)
    o_f = _fox(proj, cq, faug, trifox)
    out = _post(h, o_n.reshape(b * t, -1), o_f.reshape(b * t, -1), mg,
                wn, wf, wo, norm_mlp[0][None, :], wu, wd, norm_final[None, :], ROW_TILE)
    return out.reshape(b, t, d)
```

```python
import numpy as np
import jax
import jax.numpy as jnp
from jax import lax
from jax.experimental import pallas as pl
from jax.experimental.pallas import tpu as pltpu

HEAD_DIM = 64
NSA_HEADS = 8
NSA_GROUPS = 2
NSA_HPG = NSA_HEADS // NSA_GROUPS
FOX_HEADS = 8
CMP_BLOCK = 32
CMP_STRIDE = 16
CMP_HIDDEN = 128
SEL_BLOCK = 64
N_SELECT = 16
WINDOW = 512
NORM_EPS = 1e-6
FORCE_SCORE = 1e9

LANES = 128
MXU_TILE = 256
NEG_BIG = -(2.0 ** 100)
M_INIT = -3.0e38
LOG2E = 1.4426950408889634
VMEM_LIMIT = 56 * 1024 * 1024
ROW_TILE = 512
ROW_SPLIT = 2
PROJ_CHUNK = 3 * MXU_TILE
GATE_CHUNK = 4 * MXU_TILE
FF_CHUNK = 4 * MXU_TILE

COL_QN = 0
COL_QF = COL_QN + NSA_HEADS * HEAD_DIM
COL_KF = COL_QF + FOX_HEADS * HEAD_DIM
COL_VF = COL_KF + FOX_HEADS * HEAD_DIM
COL_KV = COL_VF + FOX_HEADS * HEAD_DIM
PROJ_W = COL_KV + 6 * NSA_GROUPS * HEAD_DIM
GF_GATE = 0
GF_FOX = 24

F32 = jnp.float32
BF16 = jnp.bfloat16


def _dot(a, b):
    return jnp.dot(a, b, preferred_element_type=F32)


def _iota(shape, dim):
    return lax.broadcasted_iota(jnp.int32, shape, dim)


def _split3(x):
    hi = x.astype(BF16)
    r1 = x - hi.astype(F32)
    mid = r1.astype(BF16)
    lo = (r1 - mid.astype(F32)).astype(BF16)
    return hi, mid, lo


W_QN = NSA_HEADS * HEAD_DIM
W_KV = 6 * NSA_GROUPS * HEAD_DIM
W_GATE = 3 * NSA_HEADS
W_FOX = 3 * FOX_HEADS * HEAD_DIM
W_PIECES = ((0, W_QN), (W_QN + W_KV + W_GATE, W_FOX), (W_QN, W_KV))


def _inproj_kernel(x_ref, g_ref, wt_ref, wm32_ref, bm_ref, *refs):
    n_later = (len(refs) - 8) // 2
    later_in, (proj_ref, gf_ref, mg_ref) = refs[:n_later], refs[n_later:n_later + 3]
    later_out = refs[n_later + 3:2 * n_later + 3]
    wq_ref, wf_ref, wkv_ref, wg_ref, wm_ref = refs[2 * n_later + 3:]
    w_refs = (wq_ref, wf_ref, wkv_ref)

    for src, dst in zip(later_in, later_out):
        dst[...] = src[0].astype(BF16)

    @pl.when(pl.program_id(0) == 0)
    def _():
        d = wt_ref.shape[2]
        for (c0, width), dst in zip(W_PIECES, w_refs):
            for k0 in range(0, width, MXU_TILE):
                dst[:, k0:k0 + MXU_TILE] = wt_ref[0, c0 + k0:c0 + k0 + MXU_TILE, :].T.astype(BF16)
        gate0 = W_QN + W_KV
        fox0 = gate0 + W_GATE + W_FOX
        gates = jnp.concatenate([wt_ref[0, gate0:gate0 + W_GATE, :], wt_ref[0, fox0:fox0 + FOX_HEADS, :],
                                 jnp.zeros((LANES - W_GATE - FOX_HEADS, d), F32)], axis=0)
        wg_ref[...] = gates.T.astype(BF16)
        for r0 in range(0, d, MXU_TILE):
            wm_ref[r0:r0 + MXU_TILE, :] = wm32_ref[0, r0:r0 + MXU_TILE, :].astype(BF16)

    sub = x_ref.shape[0] // ROW_SPLIT
    for h in range(ROW_SPLIT):
        rows = slice(h * sub, (h + 1) * sub)
        x = x_ref[rows, :]
        r = lax.rsqrt(jnp.mean(x * x, axis=-1, keepdims=True) + NORM_EPS)
        u = (x * r * g_ref[...]).astype(BF16)
        c0 = 0
        for piece in w_refs:
            for k0 in range(0, piece.shape[1], PROJ_CHUNK):
                k1 = min(k0 + PROJ_CHUNK, piece.shape[1])
                proj_ref[rows, c0 + k0:c0 + k1] = _dot(u, piece[:, k0:k1]).astype(BF16)
            c0 += piece.shape[1]
        gf_ref[rows, :] = _dot(u, wg_ref[...])
        for c0 in range(0, wm_ref.shape[1], GATE_CHUNK):
            z = _dot(u, wm_ref[:, c0:c0 + GATE_CHUNK]) + bm_ref[:, c0:c0 + GATE_CHUNK]
            mg_ref[rows, c0:c0 + GATE_CHUNK] = jax.nn.sigmoid(z).astype(BF16)


def _inproj(x2, g, w_in_t, wm, bm, later, tm):
    n, d = x2.shape
    steps = n // tm
    const = lambda i: (0, 0)
    once = dict(pipeline_mode=pl.Buffered(1))
    slabs = [w.shape[1] // steps for w in later]
    assert all(r * steps == w.shape[1] and r % 16 == 0 for w, r in zip(later, slabs)), "bf16 slabs need 16-row multiples"
    return pl.pallas_call(
        _inproj_kernel,
        grid=(n // tm,),
        in_specs=[
            pl.BlockSpec((tm, d), lambda i: (i, 0)),
            pl.BlockSpec((1, d), const),
            pl.BlockSpec(w_in_t.shape, lambda i: (0, 0, 0), **once),
            pl.BlockSpec(wm.shape, lambda i: (0, 0, 0), **once),
            pl.BlockSpec(bm.shape, const),
        ] + [pl.BlockSpec((1, r, w.shape[2]), lambda i: (0, i, 0)) for w, r in zip(later, slabs)],
        out_specs=[
            pl.BlockSpec((tm, PROJ_W), lambda i: (i, 0)),
            pl.BlockSpec((tm, LANES), lambda i: (i, 0)),
            pl.BlockSpec((tm, 2 * d), lambda i: (i, 0)),
        ] + [pl.BlockSpec((r, w.shape[2]), lambda i: (i, 0)) for w, r in zip(later, slabs)],
        out_shape=[
            jax.ShapeDtypeStruct((n, PROJ_W), BF16),
            jax.ShapeDtypeStruct((n, LANES), F32),
            jax.ShapeDtypeStruct((n, 2 * d), BF16),
        ] + [jax.ShapeDtypeStruct(w.shape[1:], BF16) for w in later],
        scratch_shapes=[pltpu.VMEM((d, width), BF16) for _, width in W_PIECES]
        + [pltpu.VMEM((d, LANES), BF16), pltpu.VMEM((d, 2 * d), BF16)],
        compiler_params=pltpu.CompilerParams(
            dimension_semantics=("arbitrary",), vmem_limit_bytes=VMEM_LIMIT),
        name="inproj",
    )(x2, g, w_in_t, wm, bm, *later)


def _compress_kernel(kc_ref, vc_ref, w1k_ref, w2k_ref, pk_ref, w1v_ref, w2v_ref, pv_ref, kco_ref, vco_ref,
                     xs_ref, w1s_ref, w2s_ref, ps_ref):
    t = kc_ref.shape[1]
    nblk = t // CMP_STRIDE

    @pl.when(pl.program_id(0) == 0)
    def _():
        r = _iota((HEAD_DIM, LANES), 0)
        c = _iota((HEAD_DIM, LANES), 1)
        halves = [jnp.where(c == r + g * HEAD_DIM, 1.0, 0.0).astype(BF16) for g in range(NSA_GROUPS)]
        z = jnp.zeros((HEAD_DIM, CMP_HIDDEN), BF16)
        first_row = _iota((16, LANES), 0) == 0
        for i, (w1_ref, w2_ref, p_ref) in enumerate(((w1k_ref, w2k_ref, pk_ref), (w1v_ref, w2v_ref, pv_ref))):
            for j in range(CMP_STRIDE):
                wa = w1_ref[0, j * HEAD_DIM:(j + 1) * HEAD_DIM, :].astype(BF16)
                wb = w1_ref[0, (CMP_STRIDE + j) * HEAD_DIM:(CMP_STRIDE + j + 1) * HEAD_DIM, :].astype(BF16)
                w1s_ref[i, j] = jnp.concatenate([jnp.concatenate([wa, z, wb, z], axis=1),
                                                 jnp.concatenate([z, wa, z, wb], axis=1)], axis=0)
            w2 = w2_ref[0].T.astype(BF16)
            for g in range(NSA_GROUPS):
                w2s_ref[i, g] = _dot(w2, halves[g]).astype(BF16)
            pos = p_ref[0].astype(BF16)
            prow = _dot(pos[0:CMP_STRIDE], halves[0]) + _dot(pos[CMP_STRIDE:CMP_BLOCK], halves[1])
            for j in range(CMP_STRIDE):
                ps_ref[i, j] = jnp.where(first_row, prow[j:j + 1, :], 0.0).astype(BF16)

    def one(x_ref, w_ref, w2_refs, p_ref, o_ref):
        xs_ref[...] = x_ref[0].astype(F32)
        acc = jnp.zeros((nblk + 16, 4 * LANES), F32)
        for j in range(CMP_STRIDE):
            xj = xs_ref[pl.ds(j, nblk, stride=CMP_STRIDE), :].astype(BF16)
            acc = acc + _dot(jnp.concatenate([xj, p_ref[j]], axis=0), w_ref[j])
        posb = acc[nblk:nblk + 1, 0:LANES] + acc[nblk:nblk + 1, 3 * LANES:4 * LANES]
        out = jnp.zeros((nblk, LANES), F32)
        for g in range(NSA_GROUPS):
            a = acc[0:nblk, g * LANES:(g + 1) * LANES]
            b = acc[0:nblk, (2 + g) * LANES:(3 + g) * LANES]
            h = a + pltpu.roll(b, nblk - 1, 0) + posb
            out = out + _dot(jax.nn.gelu(h).astype(BF16), w2_refs[g][...])
        o_ref[0] = out.astype(BF16)

    one(kc_ref, w1s_ref.at[0], (w2s_ref.at[0, 0], w2s_ref.at[0, 1]), ps_ref.at[0], kco_ref)
    one(vc_ref, w1s_ref.at[1], (w2s_ref.at[1, 0], w2s_ref.at[1, 1]), ps_ref.at[1], vco_ref)


def _compress(proj, w1k, w2k, pk, w1v, w2v, pv):
    b, t, _ = proj.shape
    kvb = COL_KV // LANES
    assert w1k.shape == (1, CMP_BLOCK * HEAD_DIM, CMP_HIDDEN) and w2k.shape == (1, HEAD_DIM, CMP_HIDDEN)
    assert pk.shape == (1, CMP_BLOCK, HEAD_DIM) and CMP_BLOCK == 2 * CMP_STRIDE and CMP_HIDDEN == LANES
    whole = lambda a: pl.BlockSpec(a.shape, lambda i: (0,) * a.ndim)
    ospec = pl.BlockSpec((1, LANES, LANES), lambda i: (i, 0, 0))
    return pl.pallas_call(
        _compress_kernel,
        grid=(b,),
        in_specs=[pl.BlockSpec((1, t, LANES), lambda i: (i, 0, kvb)),
                  pl.BlockSpec((1, t, LANES), lambda i: (i, 0, kvb + 1)),
                  whole(w1k), whole(w2k), whole(pk), whole(w1v), whole(w2v), whole(pv)],
        out_specs=[ospec, ospec],
        out_shape=[jax.ShapeDtypeStruct((b, LANES, LANES), BF16)] * 2,
        scratch_shapes=[pltpu.VMEM((t, LANES), F32),
                        pltpu.VMEM((2, CMP_STRIDE, 2 * HEAD_DIM, 4 * CMP_HIDDEN), BF16),
                        pltpu.VMEM((2, NSA_GROUPS, CMP_HIDDEN, LANES), BF16),
                        pltpu.VMEM((2, CMP_STRIDE, 16, LANES), BF16)],
        compiler_params=pltpu.CompilerParams(dimension_semantics=("arbitrary",)),
        name="compress",
    )(proj, proj, w1k, w2k, pk, w1v, w2v, pv)


def _flash_init(m_ref, acc_ref):
    m_ref[...] = jnp.full(m_ref.shape, M_INIT, F32)
    acc_ref[...] = jnp.zeros(acc_ref.shape, F32)


def _softmax_pv(scores, vt_blks, m_refs=None, acc_refs=None, maxes=None):
    def pv(vt, p):
        if not isinstance(vt, tuple):
            return _dot(vt, p)
        w = p.shape[1] // len(vt)
        return jnp.concatenate([_dot(v, p[:, j * w:(j + 1) * w]) for j, v in enumerate(vt)], axis=1)

    ps, alphas = [], []
    for i, s in enumerate(scores):
        mx = jnp.max(s, axis=0, keepdims=True) if maxes is None else maxes[i]
        if m_refs is None:
            ps.append(jnp.exp2(s - mx).astype(BF16))
            continue
        m_old = m_refs[i][...]
        m_new = jnp.maximum(m_old, mx)
        ps.append(jnp.exp2(s - m_new).astype(BF16))
        alphas.append(jnp.exp2(m_old - m_new))
        m_refs[i][...] = m_new
    if m_refs is None:
        return [pv(vt, p) for vt, p in zip(vt_blks, ps)]
    for vt, p, alpha, acc_ref in zip(vt_blks, ps, alphas, acc_refs):
        acc_ref[...] = acc_ref[...] * alpha + pv(vt, p)
    return None


def _pipelined_sweep(n_rest, qk_to, spv_from, finish):
    def body(jj, carry):
        j = 2 * jj
        qk_to(1, j + 1)
        spv_from(0, j)
        qk_to(0, j + 2)
        spv_from(1, j + 1)
        return carry

    lax.fori_loop(0, n_rest // 2, body, 0)

    @pl.when(n_rest % 2 == 1)
    def _():
        qk_to(1, n_rest)
        spv_from(0, n_rest - 1)
        spv_from(1, n_rest)
        finish()

    @pl.when(n_rest % 2 == 0)
    def _():
        spv_from(0, n_rest)
        finish()


def _ones_rows(tk):
    return jnp.where(_iota((16, tk), 0) == 0, 1.0, 0.0).astype(BF16)


NSA_TQ = 128
NSA_N = NSA_HPG * NSA_TQ
SEL_TK = 512
WIN_KEYS = WINDOW + NSA_TQ

AUG_POS = 0
AUG_TQ = 6
AUG_PAD = 9
AUG_MASK = 32


def _bf16_terms(x, n=3):
    out, r = [], np.float32(x)
    for _ in range(n):
        term = np.asarray(r, np.float32).astype(BF16).astype(np.float32)
        out.append(float(term))
        r = np.float32(r - term)
    return out


def _nsa_queries(q_tile, blk):
    t0 = blk * NSA_TQ
    qt = q_tile.astype(F32).T * (HEAD_DIM ** -0.5 * LOG2E)
    lane = _iota((1, NSA_N), 1)
    hl = lane // NSA_TQ
    tqf = (t0 + lane % NSA_TQ).astype(F32)
    ridx = _iota((16, NSA_N), 0)
    zeros_h = jnp.zeros((HEAD_DIM, NSA_TQ), F32)
    l2e = _bf16_terms(LOG2E)
    q_top, aug16 = [], []
    for g in range(NSA_GROUPS):
        tops = []
        for h in range(NSA_HPG):
            head = NSA_HPG * g + h
            qh = qt[head * HEAD_DIM:(head + 1) * HEAD_DIM]
            tops.append(jnp.concatenate([qh, zeros_h] if g == 0 else [zeros_h, qh], axis=0))
        q_top.append(jnp.concatenate(tops, axis=1))
        slope = lax.bitcast_convert_type((127 - (NSA_HPG * g + hl + 1)) << 23, F32)
        rows = {}
        for i, term in enumerate(l2e):
            rows[AUG_POS + 2 * i] = slope * (64.0 * term)
            rows[AUG_POS + 2 * i + 1] = slope * term
        for i, term in enumerate(_split3(-(slope * tqf) * LOG2E)):
            rows[AUG_TQ + i] = term.astype(F32)
        a = jnp.zeros((16, NSA_N), F32)
        for r, val in rows.items():
            a = jnp.where(ridx == r, val, a)
        aug16.append(a)
    return q_top, aug16


def _nsa_compressed(s_cmp, blk, vct_ref, ovt, gs):
    tq = blk * NSA_TQ + _iota((1, NSA_N), 1) % NSA_TQ
    n_idx = _iota((LANES, NSA_N), 0)
    valid = (n_idx * CMP_STRIDE + (CMP_BLOCK - 1) <= tq) & (n_idx < LANES - 1)
    o_cmp, imps = [], []
    for g in gs:
        sm = jnp.where(valid, s_cmp[g], NEG_BIG)
        mx = jnp.max(sm, axis=0, keepdims=True)
        e = jnp.where(valid, jnp.exp2(sm - mx), 0.0)
        den = jnp.sum(e, axis=0, keepdims=True)
        pn = e / jnp.where(den > 0.0, den, 1.0)
        o_cmp.append(_dot(vct_ref[g * HEAD_DIM:(g + 1) * HEAD_DIM, :], pn.astype(BF16)))

        psum = pn[:, 0:NSA_TQ]
        for h in range(1, NSA_HPG):
            psum = psum + pn[:, h * NSA_TQ:(h + 1) * NSA_TQ]
        hi, mid, lo = _split3(psum)
        imps.append(_dot(ovt, hi) + _dot(ovt, mid) + _dot(ovt, lo))
    return o_cmp, imps


def _nsa_select(imps, blk, q_top, aug16, gs):
    t0 = blk * NSA_TQ
    ns = imps[0].shape[0]
    tcol = t0 + _iota((8, NSA_TQ), 1)
    q_sel = []
    for g, imp in zip(gs, imps):
        tiles = range(ns // 8)
        j_idx = [8 * r + _iota((8, NSA_TQ), 0) for r in tiles]
        in_past = [j_idx[r] * SEL_BLOCK <= tcol for r in tiles]
        cur = tcol // SEL_BLOCK
        score = []
        for r in tiles:
            forced = (j_idx[r] == 0) | (j_idx[r] == cur) | (j_idx[r] == cur - 1)
            score.append(jnp.where(forced, FORCE_SCORE, jnp.where(in_past[r], imp[8 * r:8 * r + 8], -1.0)))
        rank = [jnp.zeros((8, NSA_TQ), F32) for _ in tiles]
        for i in range(ns):
            si = score[i // 8][i % 8:i % 8 + 1, :]
            for r in tiles:
                if 8 * r > i:
                    beats = si >= score[r]
                elif 8 * r + 7 <= i:
                    beats = si > score[r]
                else:
                    beats = (si > score[r]) | ((si == score[r]) & (j_idx[r] > i))
                rank[r] = rank[r] + jnp.where(beats, 1.0, 0.0)
        maskval = jnp.concatenate([jnp.where((rank[r] < min(N_SELECT, ns)) & in_past[r], 0.0, NEG_BIG) for r in tiles], axis=0)
        mask4 = jnp.concatenate([maskval] * NSA_HPG, axis=1)
        q_sel.append(jnp.concatenate([q_top[g], aug16[g], jnp.zeros((AUG_MASK - 16, NSA_N), BF16), mask4.astype(BF16),
                                      jnp.zeros((LANES - AUG_MASK - ns, NSA_N), BF16)], axis=0))
    return q_sel


def _nsa_kernel(q_ref, gf_ref, kc_ref, vc_ref, ks_ref, vs_ref, kw_ref, vw_ref,
                augk_ref, augw_ref, augc_ref, ovt_ref, trisel_ref, triwin_ref, o_ref,
                vst_ref, vwt_ref, kwp_ref, vct_ref, gt_ref, qa_ref, m_ref, acc_ref, s0_ref, s1_ref, mx0_ref, mx1_ref):
    qb = pl.program_id(1)
    nq = pl.num_programs(1)
    t = ks_ref.shape[1]
    t0 = qb * NSA_TQ
    groups = range(NSA_GROUPS)
    zeros_tail = jnp.zeros((LANES - 16, NSA_N), BF16)

    def prepare(blk):
        q_top, aug16 = _nsa_queries(q_ref[0, pl.ds(pl.multiple_of(blk * NSA_TQ, NSA_TQ), NSA_TQ), :], blk)
        for g in groups:
            qa_ref[g] = jnp.concatenate([q_top[g], aug16[g]], axis=0).astype(BF16)

    def cmp_scores(q_top, aug16):
        kcb = jnp.concatenate([kc_ref[0], augc_ref[...]], axis=1)
        return [_dot(kcb, jnp.concatenate([q_top[g], aug16[g], zeros_tail], axis=0)) for g in groups]

    @pl.when(qb == 0)
    def _():
        vwt_ref[:, 0:WINDOW] = jnp.zeros((LANES, WINDOW), BF16)
        kwp_ref[0:WINDOW, :] = jnp.zeros((WINDOW, LANES), BF16)
        kwp_ref[WINDOW:WINDOW + t, :] = kw_ref[0]
        for c in range(t // LANES):
            sl = slice(c * LANES, (c + 1) * LANES)
            vst_ref[:, sl] = vs_ref[0, sl, :].astype(F32).T.astype(BF16)
            vwt_ref[:, WINDOW + c * LANES:WINDOW + (c + 1) * LANES] = vw_ref[0, sl, :].astype(F32).T.astype(BF16)
        vct_ref[...] = vc_ref[0].astype(F32).T.astype(BF16)
        prepare(0)

    q_top = [qa_ref[g, 0:LANES, :] for g in groups]
    aug16 = [qa_ref[g, LANES:LANES + 16, :] for g in groups]
    s_cmp = cmp_scores(q_top, aug16)
    kw0 = pl.multiple_of(t0, NSA_TQ)
    kwin = jnp.concatenate([kwp_ref[pl.ds(kw0, WIN_KEYS), :], augw_ref[pl.ds(kw0, WIN_KEYS), :]], axis=1)
    pad_row = _iota((16, NSA_N), 0) == AUG_PAD
    neg_big = jnp.full((16, NSA_N), NEG_BIG, BF16)
    s_win = [_dot(kwin, jnp.concatenate([q_top[g], jnp.where(pad_row, neg_big, aug16[g]), zeros_tail], axis=0))
             for g in groups]
    o_cmp, imps = _nsa_compressed(s_cmp, qb, vct_ref, ovt_ref[...], groups)
    q_sel = _nsa_select(imps, qb, q_top, aug16, groups)

    hi_edge = jnp.concatenate([triwin_ref[0]] * NSA_HPG, axis=1)
    lo_edge = jnp.concatenate([triwin_ref[1]] * NSA_HPG, axis=1)
    ones_w = _ones_rows(WIN_KEYS)
    s_win = [jnp.concatenate([s[0:NSA_TQ] + lo_edge, s[NSA_TQ:WINDOW], s[WINDOW:WIN_KEYS] + hi_edge], axis=0)
             for s in s_win]
    vt_win = [jnp.concatenate([vwt_ref[g * HEAD_DIM:(g + 1) * HEAD_DIM, pl.ds(kw0, WIN_KEYS)], ones_w], axis=0)
              for g in groups]
    o_win = [acc[0:HEAD_DIM] / acc[HEAD_DIM:HEAD_DIM + 1] for acc in _softmax_pv(s_win, vt_win)]

    ones_s = _ones_rows(SEL_TK)
    per_chunk = SEL_TK // NSA_TQ
    n_full = qb // per_chunk
    s_bufs = (s0_ref, s1_ref)
    mx_bufs = (mx0_ref, mx1_ref)

    def sel_scores(cb):
        k0 = pl.multiple_of(cb * SEL_TK, SEL_TK)
        kblk = jnp.concatenate([ks_ref[0, pl.ds(k0, SEL_TK), :], augk_ref[pl.ds(k0, SEL_TK), :]], axis=1)
        return [_dot(kblk, q_sel[g]) for g in groups]

    def qk_to(buf, j):
        for g, s in enumerate(sel_scores(j - 1)):
            s_bufs[buf][g] = s
            mx_bufs[buf][g] = jnp.max(s, axis=0, keepdims=True)

    def spv_from(buf, j):
        cb = jnp.where(j == 0, n_full, j - 1)
        k0 = pl.multiple_of(cb * SEL_TK, SEL_TK)
        vts = [jnp.concatenate([vst_ref[g * HEAD_DIM:(g + 1) * HEAD_DIM, pl.ds(k0, SEL_TK)], ones_s], axis=0)
               for g in groups]
        _softmax_pv([s_bufs[buf][g] for g in groups], vts,
                    [m_ref.at[g] for g in groups], [acc_ref.at[g] for g in groups],
                    maxes=[mx_bufs[buf][g] for g in groups])

    diag = trisel_ref[qb % per_chunk]
    diag4 = jnp.concatenate([diag] * NSA_HPG, axis=1)
    for g, s in enumerate(sel_scores(n_full)):
        s = s + diag4
        s0_ref[g] = s
        mx0_ref[g] = jnp.max(s, axis=0, keepdims=True)
        _flash_init(m_ref.at[g], acc_ref.at[g])

    def gated_output():
        gt_ref[...] = gf_ref[0].T
        sig = jax.nn.sigmoid(gt_ref[GF_GATE:GF_GATE + 3 * NSA_HEADS, :])
        outs = []
        for g in groups:
            acc = acc_ref[g]
            o_sel = acc[0:HEAD_DIM] / acc[HEAD_DIM:HEAD_DIM + 1]
            for h in range(NSA_HPG):
                sl = slice(h * NSA_TQ, (h + 1) * NSA_TQ)
                head = NSA_HPG * g + h
                g0 = sig[head:head + 1]
                g1 = sig[NSA_HEADS + head:NSA_HEADS + head + 1]
                g2 = sig[2 * NSA_HEADS + head:2 * NSA_HEADS + head + 1]
                outs.append(g0 * o_cmp[g][:, sl] + g1 * o_sel[:, sl] + g2 * o_win[g][:, sl])
        o_ref[0] = jnp.concatenate(outs, axis=0).T.astype(BF16)
        prepare(jnp.minimum(qb + 1, nq - 1))

    _pipelined_sweep(n_full, qk_to, spv_from, gated_output)


def _nsa(proj, gf, kcmp, vcmp, augk, augw, augc, ovt, trisel, triwin):
    b, t, _ = proj.shape
    nq = t // NSA_TQ
    kvb = COL_KV // LANES
    width = NSA_HEADS * HEAD_DIM
    whole = lambda a: pl.BlockSpec(a.shape, lambda i, k: (0,) * a.ndim)
    kv = lambda idx: pl.BlockSpec((1, t, LANES), lambda i, k: (i, 0, idx))
    return pl.pallas_call(
        _nsa_kernel,
        grid=(b, nq),
        in_specs=[
            pl.BlockSpec((1, t, width), lambda i, k: (i, 0, COL_QN // width)),
            pl.BlockSpec((1, NSA_TQ, LANES), lambda i, k: (i, k, 0)),
            pl.BlockSpec((1, LANES, LANES), lambda i, k: (i, 0, 0)),
            pl.BlockSpec((1, LANES, LANES), lambda i, k: (i, 0, 0)),
            kv(kvb + 2), kv(kvb + 3), kv(kvb + 4), kv(kvb + 5),
            whole(augk), whole(augw), whole(augc), whole(ovt), whole(trisel), whole(triwin),
        ],
        out_specs=pl.BlockSpec((1, NSA_TQ, width), lambda i, k: (i, k, 0)),
        out_shape=jax.ShapeDtypeStruct((b, t, width), BF16),
        scratch_shapes=[
            pltpu.VMEM((LANES, t), BF16),
            pltpu.VMEM((LANES, t + WINDOW), BF16),
            pltpu.VMEM((t + WINDOW, LANES), BF16),
            pltpu.VMEM((LANES, LANES), BF16),
            pltpu.VMEM((LANES, NSA_TQ), F32),
            pltpu.VMEM((NSA_GROUPS, LANES + 16, NSA_N), BF16),
            pltpu.VMEM((NSA_GROUPS, 1, NSA_N), F32),
            pltpu.VMEM((NSA_GROUPS, HEAD_DIM + 16, NSA_N), F32),
            pltpu.VMEM((NSA_GROUPS, SEL_TK, NSA_N), F32),
            pltpu.VMEM((NSA_GROUPS, SEL_TK, NSA_N), F32),
            pltpu.VMEM((NSA_GROUPS, 1, NSA_N), F32),
            pltpu.VMEM((NSA_GROUPS, 1, NSA_N), F32),
        ],
        compiler_params=pltpu.CompilerParams(
            dimension_semantics=("arbitrary", "arbitrary"), vmem_limit_bytes=VMEM_LIMIT),
        name="nsa_attention",
    )(proj, gf, kcmp, vcmp, proj, proj, proj, proj, augk, augw, augc, ovt, trisel, triwin)


FOX_CBLK = MXU_TILE


def _foxprep_kernel(gf_ref, bias_ref, cq_ref, augk_ref):
    t = gf_ref.shape[1]
    cols = []
    for c in range(t // LANES):
        cols.append(gf_ref[0, c * LANES:(c + 1) * LANES, :].T[GF_FOX:GF_FOX + FOX_HEADS])
    hrow = _iota((FOX_HEADS, t), 0)
    bias = jnp.zeros((FOX_HEADS, t), F32)
    for h in range(FOX_HEADS):
        bias = jnp.where(hrow == h, bias_ref[0, h], bias)
    f = jnp.concatenate(cols, axis=1) + bias
    ls = jnp.minimum(f, 0.0) - jnp.log(1.0 + jnp.exp(-jnp.abs(f)))
    upper = (_iota((FOX_CBLK, FOX_CBLK), 0) <= _iota((FOX_CBLK, FOX_CBLK), 1)).astype(BF16)
    carry = jnp.zeros((FOX_HEADS, 1), F32)
    parts = []
    for c in range(t // FOX_CBLK):
        hi, mid, lo = _split3(ls[:, c * FOX_CBLK:(c + 1) * FOX_CBLK])
        cb = _dot(hi, upper) + _dot(mid, upper) + _dot(lo, upper) + carry
        carry = cb[:, FOX_CBLK - 1:FOX_CBLK]
        parts.append(cb)
    csum = jnp.concatenate(parts, axis=1) * LOG2E
    c1, c2, c3 = (v.astype(F32) for v in _split3(csum))
    ones = jnp.ones((3 * FOX_HEADS, t), F32)
    zeros = jnp.zeros((LANES - 6 * FOX_HEADS, t), F32)
    cq_ref[0] = jnp.concatenate([c1, c2, c3, ones, zeros], axis=0).astype(BF16)
    slab_k = jnp.concatenate([ones, -c1, -c2, -c3, zeros], axis=0)
    for c in range(t // LANES):
        sl = slice(c * LANES, (c + 1) * LANES)
        augk_ref[0, sl, :] = slab_k[:, sl].T.astype(BF16)


def _foxprep(gf, bias):
    b, t, _ = gf.shape
    return pl.pallas_call(
        _foxprep_kernel,
        grid=(b,),
        in_specs=[pl.BlockSpec((1, t, LANES), lambda i: (i, 0, 0)),
                  pl.BlockSpec(memory_space=pltpu.SMEM)],
        out_specs=[pl.BlockSpec((1, LANES, t), lambda i: (i, 0, 0)),
                   pl.BlockSpec((1, t, LANES), lambda i: (i, 0, 0))],
        out_shape=[jax.ShapeDtypeStruct((b, LANES, t), BF16),
                   jax.ShapeDtypeStruct((b, t, LANES), BF16)],
        compiler_params=pltpu.CompilerParams(dimension_semantics=("arbitrary",)),
        name="fox_prep",
    )(gf, bias)


FOX_TQ = MXU_TILE
FOX_TK = MXU_TILE
FOX_NP = FOX_HEADS // 2
FOX_N = 2 * FOX_TQ
FOX_ROWS = HEAD_DIM + 16


def _fox_kernel(q_ref, k_ref, v_ref, cq_ref, augk_ref, tri_ref, o_ref, vt_ref, qs_ref, m_ref, acc_ref,
                s0_ref, s1_ref, sd_ref, mx0_ref, mx1_ref, mxd_ref):
    qb = pl.program_id(1)
    nq = pl.num_programs(1)
    t = k_ref.shape[1]
    pairs = range(FOX_NP)
    s_bufs = (s0_ref, s1_ref, sd_ref)
    mx_bufs = (mx0_ref, mx1_ref, mxd_ref)
    diag = 2

    def scores(kb, qs):
        k0 = pl.multiple_of(kb * FOX_TK, FOX_TK)
        ak = augk_ref[0, pl.ds(k0, FOX_TK), :]
        return [_dot(jnp.concatenate([k_ref[0, pl.ds(k0, FOX_TK), p * LANES:(p + 1) * LANES], ak], axis=1), qs(p))
                for p in pairs]

    def prepare(blk, slot):
        q0 = pl.multiple_of(blk * FOX_TQ, FOX_TQ)
        row = _iota((LANES, FOX_TQ), 0)
        hrow = row % FOX_HEADS
        cq = cq_ref[0, :, pl.ds(q0, FOX_TQ)].astype(F32)
        for p in pairs:
            qt = q_ref[0, pl.ds(q0, FOX_TQ), p * LANES:(p + 1) * LANES].astype(F32).T * (HEAD_DIM ** -0.5 * LOG2E)
            q_top = jnp.concatenate([jnp.where(row < HEAD_DIM, qt, 0.0), jnp.where(row >= HEAD_DIM, qt, 0.0)], axis=1)
            q_aug = jnp.concatenate([jnp.where(hrow == 2 * p, cq, 0.0), jnp.where(hrow == 2 * p + 1, cq, 0.0)],
                                    axis=1)
            qs_ref[slot, p] = jnp.concatenate([q_top, q_aug], axis=0).astype(BF16)
        tri = tri_ref[...]
        for p, s in enumerate(scores(blk, lambda p: qs_ref[slot, p])):
            s = s + tri
            sd_ref[p] = s
            mxd_ref[p] = jnp.max(s, axis=0, keepdims=True)

    def qk_to(buf, kb):
        for p, s in enumerate(scores(kb, lambda p: qs_ref[qb % 2, p])):
            s_bufs[buf][p] = s
            mx_bufs[buf][p] = jnp.max(s, axis=0, keepdims=True)

    def spv_from(buf, kb):
        k0 = pl.multiple_of(kb * FOX_TK, FOX_TK)
        vts = [(vt_ref[p, 0, :, pl.ds(k0, FOX_TK)], vt_ref[p, 1, :, pl.ds(k0, FOX_TK)]) for p in pairs]
        _softmax_pv([s_bufs[buf][p] for p in pairs], vts,
                    [m_ref.at[p] for p in pairs], [acc_ref.at[p] for p in pairs],
                    maxes=[mx_bufs[buf][p] for p in pairs])

    def finish():
        for p in pairs:
            acc = acc_ref[p]
            den = acc[HEAD_DIM:HEAD_DIM + 1]
            o_a = acc[0:HEAD_DIM, 0:FOX_TQ] / den[:, 0:FOX_TQ]
            o_b = acc[0:HEAD_DIM, FOX_TQ:FOX_N] / den[:, FOX_TQ:FOX_N]
            o_ref[0, :, p * LANES:(p + 1) * LANES] = jnp.concatenate([o_a, o_b], axis=0).T.astype(BF16)
        prepare(jnp.minimum(qb + 1, nq - 1), (qb + 1) % 2)

    @pl.when(qb == 0)
    def _():
        for p in pairs:
            for c in range(t // LANES):
                sl = slice(c * LANES, (c + 1) * LANES)
                vt = v_ref[0, sl, p * LANES:(p + 1) * LANES].astype(F32).T.astype(BF16)
                for hh in range(2):
                    vt_ref[p, hh, 0:HEAD_DIM, sl] = vt[hh * HEAD_DIM:(hh + 1) * HEAD_DIM]
            for hh in range(2):
                vt_ref[p, hh, HEAD_DIM:FOX_ROWS, :] = _ones_rows(t)
        prepare(0, 0)

    for p in pairs:
        _flash_init(m_ref.at[p], acc_ref.at[p])

    @pl.when(qb == 0)
    def _():
        spv_from(diag, qb)
        finish()

    @pl.when(qb > 0)
    def _():
        qk_to(0, 0)
        spv_from(diag, qb)

    def body(jj, carry):
        kb = 2 * jj
        qk_to(1, kb + 1)
        spv_from(0, kb)
        qk_to(0, kb + 2)
        spv_from(1, kb + 1)
        return carry

    lax.fori_loop(0, jnp.maximum(qb - 1, 0) // 2, body, 0)

    @pl.when(qb % 2 == 1)
    def _():
        spv_from(0, qb - 1)
        finish()

    @pl.when((qb > 0) & (qb % 2 == 0))
    def _():
        qk_to(1, qb - 1)
        spv_from(0, qb - 2)
        spv_from(1, qb - 1)
        finish()


def _fox(proj, cq, augk, tri):
    b, t, _ = proj.shape
    nq = t // FOX_TQ
    width = FOX_HEADS * HEAD_DIM
    qf, kf, vf = COL_QF // width, COL_KF // width, COL_VF // width
    return pl.pallas_call(
        _fox_kernel,
        grid=(b, nq),
        in_specs=[
            pl.BlockSpec((1, t, width), lambda i, k: (i, 0, qf)),
            pl.BlockSpec((1, t, width), lambda i, k: (i, 0, kf)),
            pl.BlockSpec((1, t, width), lambda i, k: (i, 0, vf)),
            pl.BlockSpec((1, LANES, t), lambda i, k: (i, 0, 0)),
            pl.BlockSpec((1, t, LANES), lambda i, k: (i, 0, 0)),
            pl.BlockSpec(tri.shape, lambda i, k: (0, 0)),
        ],
        out_specs=pl.BlockSpec((1, FOX_TQ, width), lambda i, k: (i, k, 0)),
        out_shape=jax.ShapeDtypeStruct((b, t, width), BF16),
        scratch_shapes=[
            pltpu.VMEM((FOX_NP, 2, FOX_ROWS, t), BF16),
            pltpu.VMEM((2, FOX_NP, 2 * LANES, FOX_N), BF16),
            pltpu.VMEM((FOX_NP, 1, FOX_N), F32),
            pltpu.VMEM((FOX_NP, FOX_ROWS, FOX_N), F32),
            pltpu.VMEM((FOX_NP, FOX_TK, FOX_N), F32),
            pltpu.VMEM((FOX_NP, FOX_TK, FOX_N), F32),
            pltpu.VMEM((FOX_NP, FOX_TK, FOX_N), F32),
            pltpu.VMEM((FOX_NP, 1, FOX_N), F32),
            pltpu.VMEM((FOX_NP, 1, FOX_N), F32),
            pltpu.VMEM((FOX_NP, 1, FOX_N), F32),
        ],
        compiler_params=pltpu.CompilerParams(
            dimension_semantics=("arbitrary", "arbitrary"), vmem_limit_bytes=VMEM_LIMIT),
        name="fox_attention",
    )(proj, proj, proj, cq, augk, tri)


def _post_kernel(x_ref, on_ref, of_ref, mg_ref, wn_ref, wf_ref, wo_ref, gm_ref, wu_ref, wd_ref, gfin_ref, o_ref):
    d = x_ref.shape[1]
    ff = wu_ref.shape[1]
    ga = mg_ref[:, 0:d].astype(F32)
    gb = mg_ref[:, d:2 * d].astype(F32)
    y = ga * _dot(on_ref[...], wn_ref[...]) + gb * _dot(of_ref[...], wf_ref[...])
    h = x_ref[...] + _dot(y.astype(BF16), wo_ref[...])
    r = lax.rsqrt(jnp.mean(h * h, axis=-1, keepdims=True) + NORM_EPS)
    v = (h * r * gm_ref[...]).astype(BF16)
    acc = h
    for c0 in range(0, ff, FF_CHUNK):
        up = jnp.maximum(_dot(v, wu_ref[:, c0:c0 + FF_CHUNK]), 0.0)
        acc = acc + _dot((up * up).astype(BF16), wd_ref[c0:c0 + FF_CHUNK, :])
    r2 = lax.rsqrt(jnp.mean(acc * acc, axis=-1, keepdims=True) + NORM_EPS)
    o_ref[...] = acc * r2 * gfin_ref[...]


def _post(x2, on, of, mg, wn, wf, wo, gm, wu, wd, gfin, tm):
    n, d = x2.shape
    row = lambda w: pl.BlockSpec((tm, w), lambda i: (i, 0))
    const = lambda a: pl.BlockSpec(a.shape, lambda i: (0, 0), pipeline_mode=pl.Buffered(1))
    return pl.pallas_call(
        _post_kernel,
        grid=(n // tm,),
        in_specs=[row(d), row(on.shape[1]), row(of.shape[1]), row(2 * d),
                  const(wn), const(wf), const(wo), const(gm), const(wu), const(wd), const(gfin)],
        out_specs=row(d),
        out_shape=jax.ShapeDtypeStruct((n, d), F32),
        compiler_params=pltpu.CompilerParams(
            dimension_semantics=("arbitrary",), vmem_limit_bytes=VMEM_LIMIT),
        name="post_mlp",
    )(x2, on, of, mg, wn, wf, wo, gm, wu, wd, gfin)


def _position_constants(t):
    def key_side(pos, pad):
        a = np.zeros((pos.size, LANES), np.float32)
        for i in range(3):
            a[:, AUG_POS + 2 * i] = pos // 64
            a[:, AUG_POS + 2 * i + 1] = pos % 64
        a[:, AUG_TQ:AUG_TQ + 3] = 1.0
        a[:, AUG_PAD] = pad
        return a

    pos = np.arange(t)
    augk = key_side(pos, 0.0)
    augk[pos, AUG_MASK + pos // SEL_BLOCK] = 1.0
    augw = np.concatenate([key_side(np.zeros(WINDOW, np.int64), 1.0), key_side(pos, 0.0)], axis=0)
    augc = key_side(np.arange(LANES) * CMP_STRIDE + CMP_BLOCK - 1, 0.0)

    nc = (t - CMP_BLOCK) // CMP_STRIDE + 1
    ns = t // SEL_BLOCK
    n = np.arange(LANES)[None, :]
    j = np.arange(ns)[:, None]
    ovt = ((n * CMP_STRIDE <= j * SEL_BLOCK + SEL_BLOCK - 1) & (n * CMP_STRIDE + CMP_BLOCK - 1 >= j * SEL_BLOCK) & (n < nc))

    def tri(tk, tq, anti=False):
        kk = np.arange(tk)[:, None]
        tt = np.arange(tq)[None, :]
        ok = (kk > tt) if anti else (kk <= tt)
        return np.where(ok, 0.0, NEG_BIG).astype(np.float32)

    edge = tri(NSA_TQ, NSA_TQ)
    trisel = np.zeros((SEL_TK // NSA_TQ, SEL_TK, NSA_TQ), np.float32)
    for v in range(SEL_TK // NSA_TQ):
        trisel[v, v * NSA_TQ:(v + 1) * NSA_TQ] = edge
    triwin = np.stack([edge, tri(NSA_TQ, NSA_TQ, anti=True)])
    trifox = np.tile(tri(FOX_TK, FOX_TQ), (1, 2))
    return (jnp.asarray(augk, BF16), jnp.asarray(augw, BF16), jnp.asarray(augc, BF16), jnp.asarray(ovt, BF16),
            jnp.asarray(trisel), jnp.asarray(triwin), jnp.asarray(trifox))


def kernel(x, norm_mix, w_in, cmp_pos_k, cmp_w1_k, cmp_w2_k, cmp_pos_v, cmp_w1_v, cmp_w2_v, fox_f_bias, w_branch_nsa, w_branch_fox, w_merge_gate, b_merge_gate, w_out, norm_mlp, w_up, w_down, norm_final):
    b, t, d = x.shape
    assert w_in.shape[0] == 1, "one layer: the final norm is fused into the MLP kernel"
    assert t % FOX_TQ == 0 and t // SEL_BLOCK == 32 and (t - CMP_BLOCK) // CMP_STRIDE + 1 == LANES - 1
    augk, augw, augc, ovt, trisel, triwin, trifox = _position_constants(t)
    h = x.reshape(b * t, d)
    proj, gf, mg, wn, wf, wo, wu, wd = _inproj(h, norm_mix[0][None, :], jnp.swapaxes(w_in, 1, 2), w_merge_gate,
                                               b_merge_gate[0][None, :],
                                               (w_branch_nsa, w_branch_fox, w_out, w_up, w_down), ROW_TILE)
    proj = proj.reshape(b, t, PROJ_W)
    gf = gf.reshape(b, t, LANES)

    kcmp, vcmp = _compress(proj, cmp_w1_k, jnp.swapaxes(cmp_w2_k, 1, 2), cmp_pos_k,
                           cmp_w1_v, jnp.swapaxes(cmp_w2_v, 1, 2), cmp_pos_v)
    o_n = _nsa(proj, gf, kcmp, vcmp, augk, augw, augc, ovt, trisel, triwin)
    cq, faug = _foxprep(gf, fox_f_bias)
    o_f = _fox(proj, cq, faug, trifox)
    out = _post(h, o_n.reshape(b * t, -1), o_f.reshape(b * t, -1), mg,
                wn, wf, wo, norm_mlp[0][None, :], wu, wd, norm_final[None, :], ROW_TILE)
    return out.reshape(b, t, d)
```

```python
import numpy as np
import jax
import jax.numpy as jnp
from jax import lax
from jax.experimental import pallas as pl
from jax.experimental.pallas import tpu as pltpu

HEAD_DIM = 64
NSA_HEADS = 8
NSA_GROUPS = 2
NSA_HPG = NSA_HEADS // NSA_GROUPS
FOX_HEADS = 8
CMP_BLOCK = 32
CMP_STRIDE = 16
CMP_HIDDEN = 128
SEL_BLOCK = 64
N_SELECT = 16
WINDOW = 512
NORM_EPS = 1e-6
FORCE_SCORE = 1e9

LANES = 128
MXU_TILE = 256
NEG_BIG = -(2.0 ** 100)
M_INIT = -3.0e38
LOG2E = 1.4426950408889634
VMEM_LIMIT = 56 * 1024 * 1024
ROW_TILE = 512
ROW_SPLIT = 2
PROJ_CHUNK = 3 * MXU_TILE
GATE_CHUNK = 4 * MXU_TILE
FF_CHUNK = 4 * MXU_TILE

COL_QN = 0
COL_QF = COL_QN + NSA_HEADS * HEAD_DIM
COL_KF = COL_QF + FOX_HEADS * HEAD_DIM
COL_VF = COL_KF + FOX_HEADS * HEAD_DIM
COL_KV = COL_VF + FOX_HEADS * HEAD_DIM
PROJ_W = COL_KV + 6 * NSA_GROUPS * HEAD_DIM
GF_GATE = 0
GF_FOX = 24

F32 = jnp.float32
BF16 = jnp.bfloat16


def _dot(a, b):
    return jnp.dot(a, b, preferred_element_type=F32)


def _iota(shape, dim):
    return lax.broadcasted_iota(jnp.int32, shape, dim)


def _split3(x):
    hi = x.astype(BF16)
    r1 = x - hi.astype(F32)
    mid = r1.astype(BF16)
    lo = (r1 - mid.astype(F32)).astype(BF16)
    return hi, mid, lo


W_QN = NSA_HEADS * HEAD_DIM
W_KV = 6 * NSA_GROUPS * HEAD_DIM
W_GATE = 3 * NSA_HEADS
W_FOX = 3 * FOX_HEADS * HEAD_DIM
W_PIECES = ((0, W_QN), (W_QN + W_KV + W_GATE, W_FOX), (W_QN, W_KV))


def _inproj_kernel(x_ref, g_ref, wt_ref, wm32_ref, bm_ref, *refs):
    n_later = (len(refs) - 8) // 2
    later_in, (proj_ref, gf_ref, mg_ref) = refs[:n_later], refs[n_later:n_later + 3]
    later_out = refs[n_later + 3:2 * n_later + 3]
    wq_ref, wf_ref, wkv_ref, wg_ref, wm_ref = refs[2 * n_later + 3:]
    w_refs = (wq_ref, wf_ref, wkv_ref)

    for src, dst in zip(later_in, later_out):
        dst[...] = src[0].astype(BF16)

    @pl.when(pl.program_id(0) == 0)
    def _():
        d = wt_ref.shape[2]
        for (c0, width), dst in zip(W_PIECES, w_refs):
            for k0 in range(0, width, MXU_TILE):
                dst[:, k0:k0 + MXU_TILE] = wt_ref[0, c0 + k0:c0 + k0 + MXU_TILE, :].T.astype(BF16)
        gate0 = W_QN + W_KV
        fox0 = gate0 + W_GATE + W_FOX
        gates = jnp.concatenate([wt_ref[0, gate0:gate0 + W_GATE, :], wt_ref[0, fox0:fox0 + FOX_HEADS, :],
                                 jnp.zeros((LANES - W_GATE - FOX_HEADS, d), F32)], axis=0)
        wg_ref[...] = gates.T.astype(BF16)
        for r0 in range(0, d, MXU_TILE):
            wm_ref[r0:r0 + MXU_TILE, :] = wm32_ref[0, r0:r0 + MXU_TILE, :].astype(BF16)

    sub = x_ref.shape[0] // ROW_SPLIT
    for h in range(ROW_SPLIT):
        rows = slice(h * sub, (h + 1) * sub)
        x = x_ref[rows, :]
        r = lax.rsqrt(jnp.mean(x * x, axis=-1, keepdims=True) + NORM_EPS)
        u = (x * r * g_ref[...]).astype(BF16)
        c0 = 0
        for piece in w_refs:
            for k0 in range(0, piece.shape[1], PROJ_CHUNK):
                k1 = min(k0 + PROJ_CHUNK, piece.shape[1])
                proj_ref[rows, c0 + k0:c0 + k1] = _dot(u, piece[:, k0:k1]).astype(BF16)
            c0 += piece.shape[1]
        gf_ref[rows, :] = _dot(u, wg_ref[...])
        for c0 in range(0, wm_ref.shape[1], GATE_CHUNK):
            z = _dot(u, wm_ref[:, c0:c0 + GATE_CHUNK]) + bm_ref[:, c0:c0 + GATE_CHUNK]
            mg_ref[rows, c0:c0 + GATE_CHUNK] = jax.nn.sigmoid(z).astype(BF16)


def _inproj(x2, g, w_in_t, wm, bm, later, tm):
    n, d = x2.shape
    steps = n // tm
    const = lambda i: (0, 0)
    once = dict(pipeline_mode=pl.Buffered(1))
    slabs = [w.shape[1] // steps for w in later]
    assert all(r * steps == w.shape[1] and r % 16 == 0 for w, r in zip(later, slabs)), "bf16 slabs need 16-row multiples"
    return pl.pallas_call(
        _inproj_kernel,
        grid=(n // tm,),
        in_specs=[
            pl.BlockSpec((tm, d), lambda i: (i, 0)),
            pl.BlockSpec((1, d), const),
            pl.BlockSpec(w_in_t.shape, lambda i: (0, 0, 0), **once),
            pl.BlockSpec(wm.shape, lambda i: (0, 0, 0), **once),
            pl.BlockSpec(bm.shape, const),
        ] + [pl.BlockSpec((1, r, w.shape[2]), lambda i: (0, i, 0)) for w, r in zip(later, slabs)],
        out_specs=[
            pl.BlockSpec((tm, PROJ_W), lambda i: (i, 0)),
            pl.BlockSpec((tm, LANES), lambda i: (i, 0)),
            pl.BlockSpec((tm, 2 * d), lambda i: (i, 0)),
        ] + [pl.BlockSpec((r, w.shape[2]), lambda i: (i, 0)) for w, r in zip(later, slabs)],
        out_shape=[
            jax.ShapeDtypeStruct((n, PROJ_W), BF16),
            jax.ShapeDtypeStruct((n, LANES), F32),
            jax.ShapeDtypeStruct((n, 2 * d), BF16),
        ] + [jax.ShapeDtypeStruct(w.shape[1:], BF16) for w in later],
        scratch_shapes=[pltpu.VMEM((d, width), BF16) for _, width in W_PIECES]
        + [pltpu.VMEM((d, LANES), BF16), pltpu.VMEM((d, 2 * d), BF16)],
        compiler_params=pltpu.CompilerParams(
            dimension_semantics=("arbitrary",), vmem_limit_bytes=VMEM_LIMIT),
        name="inproj",
    )(x2, g, w_in_t, wm, bm, *later)


def _compress_kernel(kc_ref, vc_ref, w1k_ref, w2k_ref, pk_ref, w1v_ref, w2v_ref, pv_ref, kco_ref, vco_ref,
                     xs_ref, w1s_ref, w2s_ref, ps_ref):
    t = kc_ref.shape[1]
    nblk = t // CMP_STRIDE

    @pl.when(pl.program_id(0) == 0)
    def _():
        r = _iota((HEAD_DIM, LANES), 0)
        c = _iota((HEAD_DIM, LANES), 1)
        halves = [jnp.where(c == r + g * HEAD_DIM, 1.0, 0.0).astype(BF16) for g in range(NSA_GROUPS)]
        z = jnp.zeros((HEAD_DIM, CMP_HIDDEN), BF16)
        first_row = _iota((16, LANES), 0) == 0
        for i, (w1_ref, w2_ref, p_ref) in enumerate(((w1k_ref, w2k_ref, pk_ref), (w1v_ref, w2v_ref, pv_ref))):
            for j in range(CMP_STRIDE):
                wa = w1_ref[0, j * HEAD_DIM:(j + 1) * HEAD_DIM, :].astype(BF16)
                wb = w1_ref[0, (CMP_STRIDE + j) * HEAD_DIM:(CMP_STRIDE + j + 1) * HEAD_DIM, :].astype(BF16)
                w1s_ref[i, j] = jnp.concatenate([jnp.concatenate([wa, z, wb, z], axis=1),
                                                 jnp.concatenate([z, wa, z, wb], axis=1)], axis=0)
            w2 = w2_ref[0].T.astype(BF16)
            for g in range(NSA_GROUPS):
                w2s_ref[i, g] = _dot(w2, halves[g]).astype(BF16)
            pos = p_ref[0].astype(BF16)
            prow = _dot(pos[0:CMP_STRIDE], halves[0]) + _dot(pos[CMP_STRIDE:CMP_BLOCK], halves[1])
            for j in range(CMP_STRIDE):
                ps_ref[i, j] = jnp.where(first_row, prow[j:j + 1, :], 0.0).astype(BF16)

    def one(x_ref, w_ref, w2_refs, p_ref, o_ref):
        xs_ref[...] = x_ref[0].astype(F32)
        acc = jnp.zeros((nblk + 16, 4 * LANES), F32)
        for j in range(CMP_STRIDE):
            xj = xs_ref[pl.ds(j, nblk, stride=CMP_STRIDE), :].astype(BF16)
            acc = acc + _dot(jnp.concatenate([xj, p_ref[j]], axis=0), w_ref[j])
        posb = acc[nblk:nblk + 1, 0:LANES] + acc[nblk:nblk + 1, 3 * LANES:4 * LANES]
        out = jnp.zeros((nblk, LANES), F32)
        for g in range(NSA_GROUPS):
            a = acc[0:nblk, g * LANES:(g + 1) * LANES]
            b = acc[0:nblk, (2 + g) * LANES:(3 + g) * LANES]
            h = a + pltpu.roll(b, nblk - 1, 0) + posb
            out = out + _dot(jax.nn.gelu(h).astype(BF16), w2_refs[g][...])
        o_ref[0] = out.astype(BF16)

    one(kc_ref, w1s_ref.at[0], (w2s_ref.at[0, 0], w2s_ref.at[0, 1]), ps_ref.at[0], kco_ref)
    one(vc_ref, w1s_ref.at[1], (w2s_ref.at[1, 0], w2s_ref.at[1, 1]), ps_ref.at[1], vco_ref)


def _compress(proj, w1k, w2k, pk, w1v, w2v, pv):
    b, t, _ = proj.shape
    kvb = COL_KV // LANES
    assert w1k.shape == (1, CMP_BLOCK * HEAD_DIM, CMP_HIDDEN) and w2k.shape == (1, HEAD_DIM, CMP_HIDDEN)
    assert pk.shape == (1, CMP_BLOCK, HEAD_DIM) and CMP_BLOCK == 2 * CMP_STRIDE and CMP_HIDDEN == LANES
    whole = lambda a: pl.BlockSpec(a.shape, lambda i: (0,) * a.ndim)
    ospec = pl.BlockSpec((1, LANES, LANES), lambda i: (i, 0, 0))
    return pl.pallas_call(
        _compress_kernel,
        grid=(b,),
        in_specs=[pl.BlockSpec((1, t, LANES), lambda i: (i, 0, kvb)),
                  pl.BlockSpec((1, t, LANES), lambda i: (i, 0, kvb + 1)),
                  whole(w1k), whole(w2k), whole(pk), whole(w1v), whole(w2v), whole(pv)],
        out_specs=[ospec, ospec],
        out_shape=[jax.ShapeDtypeStruct((b, LANES, LANES), BF16)] * 2,
        scratch_shapes=[pltpu.VMEM((t, LANES), F32),
                        pltpu.VMEM((2, CMP_STRIDE, 2 * HEAD_DIM, 4 * CMP_HIDDEN), BF16),
                        pltpu.VMEM((2, NSA_GROUPS, CMP_HIDDEN, LANES), BF16),
                        pltpu.VMEM((2, CMP_STRIDE, 16, LANES), BF16)],
        compiler_params=pltpu.CompilerParams(dimension_semantics=("arbitrary",)),
        name="compress",
    )(proj, proj, w1k, w2k, pk, w1v, w2v, pv)


def _flash_init(m_ref, acc_ref):
    m_ref[...] = jnp.full(m_ref.shape, M_INIT, F32)
    acc_ref[...] = jnp.zeros(acc_ref.shape, F32)


def _softmax_pv(scores, vt_blks, m_refs=None, acc_refs=None, maxes=None, first=False):
    def pv(vt, p):
        if not isinstance(vt, tuple):
            return _dot(vt, p)
        w = p.shape[1] // len(vt)
        return jnp.concatenate([_dot(v, p[:, j * w:(j + 1) * w]) for j, v in enumerate(vt)], axis=1)

    ps, alphas = [], []
    for i, s in enumerate(scores):
        mx = jnp.max(s, axis=0, keepdims=True) if maxes is None else maxes[i]
        if m_refs is None or first:
            ps.append(jnp.exp2(s - mx).astype(BF16))
            if first:
                m_refs[i][...] = mx
            continue
        m_old = m_refs[i][...]
        m_new = jnp.maximum(m_old, mx)
        ps.append(jnp.exp2(s - m_new).astype(BF16))
        alphas.append(jnp.exp2(m_old - m_new))
        m_refs[i][...] = m_new
    if m_refs is None:
        return [pv(vt, p) for vt, p in zip(vt_blks, ps)]
    if first:
        for vt, p, acc_ref in zip(vt_blks, ps, acc_refs):
            acc_ref[...] = pv(vt, p)
        return None
    for vt, p, alpha, acc_ref in zip(vt_blks, ps, alphas, acc_refs):
        acc_ref[...] = acc_ref[...] * alpha + pv(vt, p)
    return None


def _pipelined_sweep(n_rest, qk_to, spv_from, finish):
    def body(jj, carry):
        j = 2 * jj
        qk_to(1, j + 1)
        spv_from(0, j)
        qk_to(0, j + 2)
        spv_from(1, j + 1)
        return carry

    lax.fori_loop(0, n_rest // 2, body, 0)

    @pl.when(n_rest % 2 == 1)
    def _():
        qk_to(1, n_rest)
        spv_from(0, n_rest - 1)
        spv_from(1, n_rest)
        finish()

    @pl.when(n_rest % 2 == 0)
    def _():
        spv_from(0, n_rest)
        finish()


def _ones_rows(tk):
    return jnp.where(_iota((16, tk), 0) == 0, 1.0, 0.0).astype(BF16)


NSA_TQ = 128
NSA_N = NSA_HPG * NSA_TQ
SEL_TK = 512
WIN_KEYS = WINDOW + NSA_TQ

AUG_POS = 0
AUG_TQ = 6
AUG_PAD = 9
AUG_MASK = 32


def _bf16_terms(x, n=3):
    out, r = [], np.float32(x)
    for _ in range(n):
        term = np.asarray(r, np.float32).astype(BF16).astype(np.float32)
        out.append(float(term))
        r = np.float32(r - term)
    return out


def _nsa_queries(q_tile, blk):
    t0 = blk * NSA_TQ
    qt = q_tile.astype(F32).T * (HEAD_DIM ** -0.5 * LOG2E)
    lane = _iota((1, NSA_N), 1)
    hl = lane // NSA_TQ
    tqf = (t0 + lane % NSA_TQ).astype(F32)
    ridx = _iota((16, NSA_N), 0)
    zeros_h = jnp.zeros((HEAD_DIM, NSA_TQ), F32)
    l2e = _bf16_terms(LOG2E)
    q_top, aug16 = [], []
    for g in range(NSA_GROUPS):
        tops = []
        for h in range(NSA_HPG):
            head = NSA_HPG * g + h
            qh = qt[head * HEAD_DIM:(head + 1) * HEAD_DIM]
            tops.append(jnp.concatenate([qh, zeros_h] if g == 0 else [zeros_h, qh], axis=0))
        q_top.append(jnp.concatenate(tops, axis=1))
        slope = lax.bitcast_convert_type((127 - (NSA_HPG * g + hl + 1)) << 23, F32)
        rows = {}
        for i, term in enumerate(l2e):
            rows[AUG_POS + 2 * i] = slope * (64.0 * term)
            rows[AUG_POS + 2 * i + 1] = slope * term
        for i, term in enumerate(_split3(-(slope * tqf) * LOG2E)):
            rows[AUG_TQ + i] = term.astype(F32)
        a = jnp.zeros((16, NSA_N), F32)
        for r, val in rows.items():
            a = jnp.where(ridx == r, val, a)
        aug16.append(a)
    return q_top, aug16


def _nsa_compressed(s_cmp, blk, vct_ref, ovt, gs):
    tq = blk * NSA_TQ + _iota((1, NSA_N), 1) % NSA_TQ
    n_idx = _iota((LANES, NSA_N), 0)
    valid = (n_idx * CMP_STRIDE + (CMP_BLOCK - 1) <= tq) & (n_idx < LANES - 1)
    o_cmp, imps = [], []
    for g in gs:
        sm = jnp.where(valid, s_cmp[g], NEG_BIG)
        mx = jnp.max(sm, axis=0, keepdims=True)
        e = jnp.where(valid, jnp.exp2(sm - mx), 0.0)
        den = jnp.sum(e, axis=0, keepdims=True)
        pn = e / jnp.where(den > 0.0, den, 1.0)
        o_cmp.append(_dot(vct_ref[g * HEAD_DIM:(g + 1) * HEAD_DIM, :], pn.astype(BF16)))

        psum = pn[:, 0:NSA_TQ]
        for h in range(1, NSA_HPG):
            psum = psum + pn[:, h * NSA_TQ:(h + 1) * NSA_TQ]
        hi, mid, lo = _split3(psum)
        imps.append(_dot(ovt, hi) + _dot(ovt, mid) + _dot(ovt, lo))
    return o_cmp, imps


def _nsa_select(imps, blk, q_top, aug16, gs):
    t0 = blk * NSA_TQ
    ns = imps[0].shape[0]
    tcol = t0 + _iota((8, NSA_TQ), 1)
    q_sel = []
    for g, imp in zip(gs, imps):
        tiles = range(ns // 8)
        j_idx = [8 * r + _iota((8, NSA_TQ), 0) for r in tiles]
        in_past = [j_idx[r] * SEL_BLOCK <= tcol for r in tiles]
        cur = tcol // SEL_BLOCK
        score = []
        for r in tiles:
            forced = (j_idx[r] == 0) | (j_idx[r] == cur) | (j_idx[r] == cur - 1)
            score.append(jnp.where(forced, FORCE_SCORE, jnp.where(in_past[r], imp[8 * r:8 * r + 8], -1.0)))
        rank = [jnp.zeros((8, NSA_TQ), F32) for _ in tiles]
        for i in range(ns):
            si = score[i // 8][i % 8:i % 8 + 1, :]
            for r in tiles:
                if 8 * r > i:
                    beats = si >= score[r]
                elif 8 * r + 7 <= i:
                    beats = si > score[r]
                else:
                    beats = (si > score[r]) | ((si == score[r]) & (j_idx[r] > i))
                rank[r] = rank[r] + jnp.where(beats, 1.0, 0.0)
        maskval = jnp.concatenate([jnp.where((rank[r] < min(N_SELECT, ns)) & in_past[r], 0.0, NEG_BIG) for r in tiles], axis=0)
        mask4 = jnp.concatenate([maskval] * NSA_HPG, axis=1)
        q_sel.append(jnp.concatenate([q_top[g], aug16[g], jnp.zeros((AUG_MASK - 16, NSA_N), BF16), mask4.astype(BF16),
                                      jnp.zeros((LANES - AUG_MASK - ns, NSA_N), BF16)], axis=0))
    return q_sel


def _nsa_kernel(q_ref, gf_ref, kc_ref, vc_ref, ks_ref, vs_ref, kw_ref, vw_ref,
                augk_ref, augw_ref, augc_ref, ovt_ref, trisel_ref, triwin_ref, o_ref,
                vst_ref, vwt_ref, kwp_ref, vct_ref, gt_ref, qa_ref, scmp_ref,
                m_ref, acc_ref, s0_ref, s1_ref, mx0_ref, mx1_ref):
    qb = pl.program_id(1)
    nq = pl.num_programs(1)
    t = ks_ref.shape[1]
    t0 = qb * NSA_TQ
    groups = range(NSA_GROUPS)
    zeros_tail = jnp.zeros((LANES - 16, NSA_N), BF16)

    def prepare(blk):
        q_top, aug16 = _nsa_queries(q_ref[0, pl.ds(pl.multiple_of(blk * NSA_TQ, NSA_TQ), NSA_TQ), :], blk)
        kcb = jnp.concatenate([kc_ref[0], augc_ref[...]], axis=1)
        for g in groups:
            qa = jnp.concatenate([q_top[g], aug16[g]], axis=0).astype(BF16)
            qa_ref[g] = qa
            scmp_ref[g] = _dot(kcb, jnp.concatenate([qa, zeros_tail], axis=0))

    @pl.when(qb == 0)
    def _():
        vwt_ref[:, 0:WINDOW] = jnp.zeros((LANES, WINDOW), BF16)
        kwp_ref[0:WINDOW, :] = jnp.zeros((WINDOW, LANES), BF16)
        kwp_ref[WINDOW:WINDOW + t, :] = kw_ref[0]
        for c in range(t // LANES):
            sl = slice(c * LANES, (c + 1) * LANES)
            vst_ref[:, sl] = vs_ref[0, sl, :].astype(F32).T.astype(BF16)
            vwt_ref[:, WINDOW + c * LANES:WINDOW + (c + 1) * LANES] = vw_ref[0, sl, :].astype(F32).T.astype(BF16)
        vct_ref[...] = vc_ref[0].astype(F32).T.astype(BF16)
        prepare(0)

    q_top = [qa_ref[g, 0:LANES, :] for g in groups]
    aug16 = [qa_ref[g, LANES:LANES + 16, :] for g in groups]
    s_cmp = [scmp_ref[g] for g in groups]
    kw0 = pl.multiple_of(t0, NSA_TQ)
    kwin = jnp.concatenate([kwp_ref[pl.ds(kw0, WIN_KEYS), :], augw_ref[pl.ds(kw0, WIN_KEYS), :]], axis=1)
    pad_row = _iota((16, NSA_N), 0) == AUG_PAD
    neg_big = jnp.full((16, NSA_N), NEG_BIG, BF16)
    s_win = [_dot(kwin, jnp.concatenate([q_top[g], jnp.where(pad_row, neg_big, aug16[g]), zeros_tail], axis=0))
             for g in groups]
    o_cmp, imps = _nsa_compressed(s_cmp, qb, vct_ref, ovt_ref[...], groups)
    q_sel = _nsa_select(imps, qb, q_top, aug16, groups)

    hi_edge = jnp.concatenate([triwin_ref[0]] * NSA_HPG, axis=1)
    lo_edge = jnp.concatenate([triwin_ref[1]] * NSA_HPG, axis=1)
    ones_w = _ones_rows(WIN_KEYS)
    s_win = [jnp.concatenate([s[0:NSA_TQ] + lo_edge, s[NSA_TQ:WINDOW], s[WINDOW:WIN_KEYS] + hi_edge], axis=0)
             for s in s_win]
    vt_win = [jnp.concatenate([vwt_ref[g * HEAD_DIM:(g + 1) * HEAD_DIM, pl.ds(kw0, WIN_KEYS)], ones_w], axis=0)
              for g in groups]
    o_win = [acc[0:HEAD_DIM] / acc[HEAD_DIM:HEAD_DIM + 1] for acc in _softmax_pv(s_win, vt_win)]

    ones_s = _ones_rows(SEL_TK)
    per_chunk = SEL_TK // NSA_TQ
    n_full = qb // per_chunk
    s_bufs = (s0_ref, s1_ref)
    mx_bufs = (mx0_ref, mx1_ref)

    def sel_scores(cb):
        k0 = pl.multiple_of(cb * SEL_TK, SEL_TK)
        kblk = jnp.concatenate([ks_ref[0, pl.ds(k0, SEL_TK), :], augk_ref[pl.ds(k0, SEL_TK), :]], axis=1)
        return [_dot(kblk, q_sel[g]) for g in groups]

    def qk_to(buf, j):
        for g, s in enumerate(sel_scores(j - 1)):
            s_bufs[buf][g] = s
            mx_bufs[buf][g] = jnp.max(s, axis=0, keepdims=True)

    def spv_from(buf, j):
        cb = jnp.where(j == 0, n_full, j - 1)
        k0 = pl.multiple_of(cb * SEL_TK, SEL_TK)
        vts = [jnp.concatenate([vst_ref[g * HEAD_DIM:(g + 1) * HEAD_DIM, pl.ds(k0, SEL_TK)], ones_s], axis=0)
               for g in groups]
        _softmax_pv([s_bufs[buf][g] for g in groups], vts,
                    [m_ref.at[g] for g in groups], [acc_ref.at[g] for g in groups],
                    maxes=[mx_bufs[buf][g] for g in groups])

    diag = trisel_ref[qb % per_chunk]
    diag4 = jnp.concatenate([diag] * NSA_HPG, axis=1)
    for g, s in enumerate(sel_scores(n_full)):
        s = s + diag4
        s0_ref[g] = s
        mx0_ref[g] = jnp.max(s, axis=0, keepdims=True)
        _flash_init(m_ref.at[g], acc_ref.at[g])

    def gated_output():
        gt_ref[...] = gf_ref[0].T
        sig = jax.nn.sigmoid(gt_ref[GF_GATE:GF_GATE + 3 * NSA_HEADS, :])
        outs = []
        for g in groups:
            acc = acc_ref[g]
            o_sel = acc[0:HEAD_DIM] / acc[HEAD_DIM:HEAD_DIM + 1]
            for h in range(NSA_HPG):
                sl = slice(h * NSA_TQ, (h + 1) * NSA_TQ)
                head = NSA_HPG * g + h
                g0 = sig[head:head + 1]
                g1 = sig[NSA_HEADS + head:NSA_HEADS + head + 1]
                g2 = sig[2 * NSA_HEADS + head:2 * NSA_HEADS + head + 1]
                outs.append(g0 * o_cmp[g][:, sl] + g1 * o_sel[:, sl] + g2 * o_win[g][:, sl])
        o_ref[0] = jnp.concatenate(outs, axis=0).T.astype(BF16)
        prepare(jnp.minimum(qb + 1, nq - 1))

    _pipelined_sweep(n_full, qk_to, spv_from, gated_output)


def _nsa(proj, gf, kcmp, vcmp, augk, augw, augc, ovt, trisel, triwin):
    b, t, _ = proj.shape
    nq = t // NSA_TQ
    kvb = COL_KV // LANES
    width = NSA_HEADS * HEAD_DIM
    whole = lambda a: pl.BlockSpec(a.shape, lambda i, k: (0,) * a.ndim)
    kv = lambda idx: pl.BlockSpec((1, t, LANES), lambda i, k: (i, 0, idx))
    return pl.pallas_call(
        _nsa_kernel,
        grid=(b, nq),
        in_specs=[
            pl.BlockSpec((1, t, width), lambda i, k: (i, 0, COL_QN // width)),
            pl.BlockSpec((1, NSA_TQ, LANES), lambda i, k: (i, k, 0)),
            pl.BlockSpec((1, LANES, LANES), lambda i, k: (i, 0, 0)),
            pl.BlockSpec((1, LANES, LANES), lambda i, k: (i, 0, 0)),
            kv(kvb + 2), kv(kvb + 3), kv(kvb + 4), kv(kvb + 5),
            whole(augk), whole(augw), whole(augc), whole(ovt), whole(trisel), whole(triwin),
        ],
        out_specs=pl.BlockSpec((1, NSA_TQ, width), lambda i, k: (i, k, 0)),
        out_shape=jax.ShapeDtypeStruct((b, t, width), BF16),
        scratch_shapes=[
            pltpu.VMEM((LANES, t), BF16),
            pltpu.VMEM((LANES, t + WINDOW), BF16),
            pltpu.VMEM((t + WINDOW, LANES), BF16),
            pltpu.VMEM((LANES, LANES), BF16),
            pltpu.VMEM((LANES, NSA_TQ), F32),
            pltpu.VMEM((NSA_GROUPS, LANES + 16, NSA_N), BF16),
            pltpu.VMEM((NSA_GROUPS, LANES, NSA_N), F32),
            pltpu.VMEM((NSA_GROUPS, 1, NSA_N), F32),
            pltpu.VMEM((NSA_GROUPS, HEAD_DIM + 16, NSA_N), F32),
            pltpu.VMEM((NSA_GROUPS, SEL_TK, NSA_N), F32),
            pltpu.VMEM((NSA_GROUPS, SEL_TK, NSA_N), F32),
            pltpu.VMEM((NSA_GROUPS, 1, NSA_N), F32),
            pltpu.VMEM((NSA_GROUPS, 1, NSA_N), F32),
        ],
        compiler_params=pltpu.CompilerParams(
            dimension_semantics=("arbitrary", "arbitrary"), vmem_limit_bytes=VMEM_LIMIT),
        name="nsa_attention",
    )(proj, gf, kcmp, vcmp, proj, proj, proj, proj, augk, augw, augc, ovt, trisel, triwin)


FOX_CBLK = MXU_TILE


def _foxprep_kernel(gf_ref, bias_ref, cq_ref, augk_ref):
    t = gf_ref.shape[1]
    cols = []
    for c in range(t // LANES):
        cols.append(gf_ref[0, c * LANES:(c + 1) * LANES, :].T[GF_FOX:GF_FOX + FOX_HEADS])
    hrow = _iota((FOX_HEADS, t), 0)
    bias = jnp.zeros((FOX_HEADS, t), F32)
    for h in range(FOX_HEADS):
        bias = jnp.where(hrow == h, bias_ref[0, h], bias)
    f = jnp.concatenate(cols, axis=1) + bias
    ls = jnp.minimum(f, 0.0) - jnp.log(1.0 + jnp.exp(-jnp.abs(f)))
    upper = (_iota((FOX_CBLK, FOX_CBLK), 0) <= _iota((FOX_CBLK, FOX_CBLK), 1)).astype(BF16)
    carry = jnp.zeros((FOX_HEADS, 1), F32)
    parts = []
    for c in range(t // FOX_CBLK):
        hi, mid, lo = _split3(ls[:, c * FOX_CBLK:(c + 1) * FOX_CBLK])
        cb = _dot(hi, upper) + _dot(mid, upper) + _dot(lo, upper) + carry
        carry = cb[:, FOX_CBLK - 1:FOX_CBLK]
        parts.append(cb)
    csum = jnp.concatenate(parts, axis=1) * LOG2E
    c1, c2, c3 = (v.astype(F32) for v in _split3(csum))
    ones = jnp.ones((3 * FOX_HEADS, t), F32)
    zeros = jnp.zeros((LANES - 6 * FOX_HEADS, t), F32)
    cq_ref[0] = jnp.concatenate([c1, c2, c3, ones, zeros], axis=0).astype(BF16)
    slab_k = jnp.concatenate([ones, -c1, -c2, -c3, zeros], axis=0)
    for c in range(t // LANES):
        sl = slice(c * LANES, (c + 1) * LANES)
        augk_ref[0, sl, :] = slab_k[:, sl].T.astype(BF16)


def _foxprep(gf, bias):
    b, t, _ = gf.shape
    return pl.pallas_call(
        _foxprep_kernel,
        grid=(b,),
        in_specs=[pl.BlockSpec((1, t, LANES), lambda i: (i, 0, 0)),
                  pl.BlockSpec(memory_space=pltpu.SMEM)],
        out_specs=[pl.BlockSpec((1, LANES, t), lambda i: (i, 0, 0)),
                   pl.BlockSpec((1, t, LANES), lambda i: (i, 0, 0))],
        out_shape=[jax.ShapeDtypeStruct((b, LANES, t), BF16),
                   jax.ShapeDtypeStruct((b, t, LANES), BF16)],
        compiler_params=pltpu.CompilerParams(dimension_semantics=("arbitrary",)),
        name="fox_prep",
    )(gf, bias)


FOX_TQ = MXU_TILE
FOX_TK = MXU_TILE
FOX_NP = FOX_HEADS // 2
FOX_N = 2 * FOX_TQ
FOX_ROWS = HEAD_DIM + 16


def _fox_kernel(q_ref, k_ref, v_ref, cq_ref, augk_ref, tri_ref, o_ref, vt_ref, qs_ref, m_ref, acc_ref,
                s0_ref, s1_ref, sd_ref, mx0_ref, mx1_ref, mxd_ref):
    qb = pl.program_id(1)
    nq = pl.num_programs(1)
    t = k_ref.shape[1]
    pairs = range(FOX_NP)
    s_bufs = (s0_ref, s1_ref, sd_ref)
    mx_bufs = (mx0_ref, mx1_ref, mxd_ref)
    diag = 2

    def scores(kb, qs):
        k0 = pl.multiple_of(kb * FOX_TK, FOX_TK)
        ak = augk_ref[0, pl.ds(k0, FOX_TK), :]
        return [_dot(jnp.concatenate([k_ref[0, pl.ds(k0, FOX_TK), p * LANES:(p + 1) * LANES], ak], axis=1), qs(p))
                for p in pairs]

    def prepare(blk, slot):
        q0 = pl.multiple_of(blk * FOX_TQ, FOX_TQ)
        row = _iota((LANES, FOX_TQ), 0)
        hrow = row % FOX_HEADS
        cq = cq_ref[0, :, pl.ds(q0, FOX_TQ)].astype(F32)
        for p in pairs:
            qt = q_ref[0, pl.ds(q0, FOX_TQ), p * LANES:(p + 1) * LANES].astype(F32).T * (HEAD_DIM ** -0.5 * LOG2E)
            q_top = jnp.concatenate([jnp.where(row < HEAD_DIM, qt, 0.0), jnp.where(row >= HEAD_DIM, qt, 0.0)], axis=1)
            q_aug = jnp.concatenate([jnp.where(hrow == 2 * p, cq, 0.0), jnp.where(hrow == 2 * p + 1, cq, 0.0)],
                                    axis=1)
            qs_ref[slot, p] = jnp.concatenate([q_top, q_aug], axis=0).astype(BF16)
        tri = tri_ref[...]
        for p, s in enumerate(scores(blk, lambda p: qs_ref[slot, p])):
            s = s + tri
            sd_ref[p] = s
            mxd_ref[p] = jnp.max(s, axis=0, keepdims=True)

    def qk_to(buf, kb):
        for p, s in enumerate(scores(kb, lambda p: qs_ref[qb % 2, p])):
            s_bufs[buf][p] = s
            mx_bufs[buf][p] = jnp.max(s, axis=0, keepdims=True)

    def spv_from(buf, kb, first=False):
        k0 = pl.multiple_of(kb * FOX_TK, FOX_TK)
        vts = [(vt_ref[p, 0, :, pl.ds(k0, FOX_TK)], vt_ref[p, 1, :, pl.ds(k0, FOX_TK)]) for p in pairs]
        _softmax_pv([s_bufs[buf][p] for p in pairs], vts,
                    [m_ref.at[p] for p in pairs], [acc_ref.at[p] for p in pairs],
                    maxes=[mx_bufs[buf][p] for p in pairs], first=first)

    def finish():
        for p in pairs:
            acc = acc_ref[p]
            den = acc[HEAD_DIM:HEAD_DIM + 1]
            o_a = acc[0:HEAD_DIM, 0:FOX_TQ] / den[:, 0:FOX_TQ]
            o_b = acc[0:HEAD_DIM, FOX_TQ:FOX_N] / den[:, FOX_TQ:FOX_N]
            o_ref[0, :, p * LANES:(p + 1) * LANES] = jnp.concatenate([o_a, o_b], axis=0).T.astype(BF16)
        prepare(jnp.minimum(qb + 1, nq - 1), (qb + 1) % 2)

    @pl.when(qb == 0)
    def _():
        for p in pairs:
            for c in range(t // LANES):
                sl = slice(c * LANES, (c + 1) * LANES)
                vt = v_ref[0, sl, p * LANES:(p + 1) * LANES].astype(F32).T.astype(BF16)
                for hh in range(2):
                    vt_ref[p, hh, 0:HEAD_DIM, sl] = vt[hh * HEAD_DIM:(hh + 1) * HEAD_DIM]
            for hh in range(2):
                vt_ref[p, hh, HEAD_DIM:FOX_ROWS, :] = _ones_rows(t)
        prepare(0, 0)

    @pl.when(qb == 0)
    def _():
        spv_from(diag, qb, first=True)
        finish()

    @pl.when(qb > 0)
    def _():
        qk_to(0, 0)
        spv_from(diag, qb, first=True)

    def body(jj, carry):
        kb = 2 * jj
        qk_to(1, kb + 1)
        spv_from(0, kb)
        qk_to(0, kb + 2)
        spv_from(1, kb + 1)
        return carry

    lax.fori_loop(0, jnp.maximum(qb - 1, 0) // 2, body, 0)

    @pl.when(qb % 2 == 1)
    def _():
        spv_from(0, qb - 1)
        finish()

    @pl.when((qb > 0) & (qb % 2 == 0))
    def _():
        qk_to(1, qb - 1)
        spv_from(0, qb - 2)
        spv_from(1, qb - 1)
        finish()


def _fox(proj, cq, augk, tri):
    b, t, _ = proj.shape
    nq = t // FOX_TQ
    width = FOX_HEADS * HEAD_DIM
    qf, kf, vf = COL_QF // width, COL_KF // width, COL_VF // width
    return pl.pallas_call(
        _fox_kernel,
        grid=(b, nq),
        in_specs=[
            pl.BlockSpec((1, t, width), lambda i, k: (i, 0, qf)),
            pl.BlockSpec((1, t, width), lambda i, k: (i, 0, kf)),
            pl.BlockSpec((1, t, width), lambda i, k: (i, 0, vf)),
            pl.BlockSpec((1, LANES, t), lambda i, k: (i, 0, 0)),
            pl.BlockSpec((1, t, LANES), lambda i, k: (i, 0, 0)),
            pl.BlockSpec(tri.shape, lambda i, k: (0, 0)),
        ],
        out_specs=pl.BlockSpec((1, FOX_TQ, width), lambda i, k: (i, k, 0)),
        out_shape=jax.ShapeDtypeStruct((b, t, width), BF16),
        scratch_shapes=[
            pltpu.VMEM((FOX_NP, 2, FOX_ROWS, t), BF16),
            pltpu.VMEM((2, FOX_NP, 2 * LANES, FOX_N), BF16),
            pltpu.VMEM((FOX_NP, 1, FOX_N), F32),
            pltpu.VMEM((FOX_NP, FOX_ROWS, FOX_N), F32),
            pltpu.VMEM((FOX_NP, FOX_TK, FOX_N), F32),
            pltpu.VMEM((FOX_NP, FOX_TK, FOX_N), F32),
            pltpu.VMEM((FOX_NP, FOX_TK, FOX_N), F32),
            pltpu.VMEM((FOX_NP, 1, FOX_N), F32),
            pltpu.VMEM((FOX_NP, 1, FOX_N), F32),
            pltpu.VMEM((FOX_NP, 1, FOX_N), F32),
        ],
        compiler_params=pltpu.CompilerParams(
            dimension_semantics=("arbitrary", "arbitrary"), vmem_limit_bytes=VMEM_LIMIT),
        name="fox_attention",
    )(proj, proj, proj, cq, augk, tri)


def _post_kernel(x_ref, on_ref, of_ref, mg_ref, wn_ref, wf_ref, wo_ref, gm_ref, wu_ref, wd_ref, gfin_ref, o_ref):
    d = x_ref.shape[1]
    ff = wu_ref.shape[1]
    ga = mg_ref[:, 0:d].astype(F32)
    gb = mg_ref[:, d:2 * d].astype(F32)
    y = ga * _dot(on_ref[...], wn_ref[...]) + gb * _dot(of_ref[...], wf_ref[...])
    h = x_ref[...] + _dot(y.astype(BF16), wo_ref[...])
    r = lax.rsqrt(jnp.mean(h * h, axis=-1, keepdims=True) + NORM_EPS)
    v = (h * r * gm_ref[...]).astype(BF16)
    acc = h
    for c0 in range(0, ff, FF_CHUNK):
        up = jnp.maximum(_dot(v, wu_ref[:, c0:c0 + FF_CHUNK]), 0.0)
        acc = acc + _dot((up * up).astype(BF16), wd_ref[c0:c0 + FF_CHUNK, :])
    r2 = lax.rsqrt(jnp.mean(acc * acc, axis=-1, keepdims=True) + NORM_EPS)
    o_ref[...] = acc * r2 * gfin_ref[...]


def _post(x2, on, of, mg, wn, wf, wo, gm, wu, wd, gfin, tm):
    n, d = x2.shape
    row = lambda w: pl.BlockSpec((tm, w), lambda i: (i, 0))
    const = lambda a: pl.BlockSpec(a.shape, lambda i: (0, 0), pipeline_mode=pl.Buffered(1))
    return pl.pallas_call(
        _post_kernel,
        grid=(n // tm,),
        in_specs=[row(d), row(on.shape[1]), row(of.shape[1]), row(2 * d),
                  const(wn), const(wf), const(wo), const(gm), const(wu), const(wd), const(gfin)],
        out_specs=row(d),
        out_shape=jax.ShapeDtypeStruct((n, d), F32),
        compiler_params=pltpu.CompilerParams(
            dimension_semantics=("arbitrary",), vmem_limit_bytes=VMEM_LIMIT),
        name="post_mlp",
    )(x2, on, of, mg, wn, wf, wo, gm, wu, wd, gfin)


def _position_constants(t):
    def key_side(pos, pad):
        a = np.zeros((pos.size, LANES), np.float32)
        for i in range(3):
            a[:, AUG_POS + 2 * i] = pos // 64
            a[:, AUG_POS + 2 * i + 1] = pos % 64
        a[:, AUG_TQ:AUG_TQ + 3] = 1.0
        a[:, AUG_PAD] = pad
        return a

    pos = np.arange(t)
    augk = key_side(pos, 0.0)
    augk[pos, AUG_MASK + pos // SEL_BLOCK] = 1.0
    augw = np.concatenate([key_side(np.zeros(WINDOW, np.int64), 1.0), key_side(pos, 0.0)], axis=0)
    augc = key_side(np.arange(LANES) * CMP_STRIDE + CMP_BLOCK - 1, 0.0)

    nc = (t - CMP_BLOCK) // CMP_STRIDE + 1
    ns = t // SEL_BLOCK
    n = np.arange(LANES)[None, :]
    j = np.arange(ns)[:, None]
    ovt = ((n * CMP_STRIDE <= j * SEL_BLOCK + SEL_BLOCK - 1) & (n * CMP_STRIDE + CMP_BLOCK - 1 >= j * SEL_BLOCK) & (n < nc))

    def tri(tk, tq, anti=False):
        kk = np.arange(tk)[:, None]
        tt = np.arange(tq)[None, :]
        ok = (kk > tt) if anti else (kk <= tt)
        return np.where(ok, 0.0, NEG_BIG).astype(np.float32)

    edge = tri(NSA_TQ, NSA_TQ)
    trisel = np.zeros((SEL_TK // NSA_TQ, SEL_TK, NSA_TQ), np.float32)
    for v in range(SEL_TK // NSA_TQ):
        trisel[v, v * NSA_TQ:(v + 1) * NSA_TQ] = edge
    triwin = np.stack([edge, tri(NSA_TQ, NSA_TQ, anti=True)])
    trifox = np.tile(tri(FOX_TK, FOX_TQ), (1, 2))
    return (jnp.asarray(augk, BF16), jnp.asarray(augw, BF16), jnp.asarray(augc, BF16), jnp.asarray(ovt, BF16),
            jnp.asarray(trisel), jnp.asarray(triwin), jnp.asarray(trifox))


def kernel(x, norm_mix, w_in, cmp_pos_k, cmp_w1_k, cmp_w2_k, cmp_pos_v, cmp_w1_v, cmp_w2_v, fox_f_bias, w_branch_nsa, w_branch_fox, w_merge_gate, b_merge_gate, w_out, norm_mlp, w_up, w_down, norm_final):
    b, t, d = x.shape
    assert w_in.shape[0] == 1, "one layer: the final norm is fused into the MLP kernel"
    assert t % FOX_TQ == 0 and t // SEL_BLOCK == 32 and (t - CMP_BLOCK) // CMP_STRIDE + 1 == LANES - 1
    augk, augw, augc, ovt, trisel, triwin, trifox = _position_constants(t)
    h = x.reshape(b * t, d)
    proj, gf, mg, wn, wf, wo, wu, wd = _inproj(h, norm_mix[0][None, :], jnp.swapaxes(w_in, 1, 2), w_merge_gate,
                                               b_merge_gate[0][None, :],
                                               (w_branch_nsa, w_branch_fox, w_out, w_up, w_down), ROW_TILE)
    proj = proj.reshape(b, t, PROJ_W)
    gf = gf.reshape(b, t, LANES)

    kcmp, vcmp = _compress(proj, cmp_w1_k, jnp.swapaxes(cmp_w2_k, 1, 2), cmp_pos_k,
                           cmp_w1_v, jnp.swapaxes(cmp_w2_v, 1, 2), cmp_pos_v)
    o_n = _nsa(proj, gf, kcmp, vcmp, augk, augw, augc, ovt, trisel, triwin)
    cq, faug = _foxprep(gf, fox_f_bias)
    o_f = _fox(proj, cq, faug, trifox)
    out = _post(h, o_n.reshape(b * t, -1), o_f.reshape(b * t, -1), mg,
                wn, wf, wo, norm_mlp[0][None, :], wu, wd, norm_final[None, :], ROW_TILE)
    return out.reshape(b, t, d)
```

```python
import numpy as np
import jax
import jax.numpy as jnp
from jax import lax
from jax.experimental import pallas as pl
from jax.experimental.pallas import tpu as pltpu

HEAD_DIM = 64
NSA_HEADS = 8
NSA_GROUPS = 2
NSA_HPG = NSA_HEADS // NSA_GROUPS
FOX_HEADS = 8
CMP_BLOCK = 32
CMP_STRIDE = 16
CMP_HIDDEN = 128
SEL_BLOCK = 64
N_SELECT = 16
WINDOW = 512
NORM_EPS = 1e-6
FORCE_SCORE = 1e9

LANES = 128
MXU_TILE = 256
NEG_BIG = -(2.0 ** 100)
M_INIT = -3.0e38
LOG2E = 1.4426950408889634
VMEM_LIMIT = 56 * 1024 * 1024
ROW_TILE = 512
ROW_SPLIT = 2
PROJ_CHUNK = 3 * MXU_TILE
GATE_CHUNK = 4 * MXU_TILE
FF_CHUNK = 4 * MXU_TILE

COL_QN = 0
COL_QF = COL_QN + NSA_HEADS * HEAD_DIM
COL_KF = COL_QF + FOX_HEADS * HEAD_DIM
COL_VF = COL_KF + FOX_HEADS * HEAD_DIM
COL_KV = COL_VF + FOX_HEADS * HEAD_DIM
PROJ_W = COL_KV + 6 * NSA_GROUPS * HEAD_DIM
GF_GATE = 0
GF_FOX = 24

F32 = jnp.float32
BF16 = jnp.bfloat16


def _dot(a, b):
    return jnp.dot(a, b, preferred_element_type=F32)


def _iota(shape, dim):
    return lax.broadcasted_iota(jnp.int32, shape, dim)


def _split3(x):
    hi = x.astype(BF16)
    r1 = x - hi.astype(F32)
    mid = r1.astype(BF16)
    lo = (r1 - mid.astype(F32)).astype(BF16)
    return hi, mid, lo


W_QN = NSA_HEADS * HEAD_DIM
W_KV = 6 * NSA_GROUPS * HEAD_DIM
W_GATE = 3 * NSA_HEADS
W_FOX = 3 * FOX_HEADS * HEAD_DIM
W_PIECES = ((0, W_QN), (W_QN + W_KV + W_GATE, W_FOX), (W_QN, W_KV))


def _inproj_kernel(x_ref, g_ref, wt_ref, wm32_ref, bm_ref, *refs):
    n_later = (len(refs) - 8) // 2
    later_in, (proj_ref, gf_ref, mg_ref) = refs[:n_later], refs[n_later:n_later + 3]
    later_out = refs[n_later + 3:2 * n_later + 3]
    wq_ref, wf_ref, wkv_ref, wg_ref, wm_ref = refs[2 * n_later + 3:]
    w_refs = (wq_ref, wf_ref, wkv_ref)

    for src, dst in zip(later_in, later_out):
        dst[...] = src[0].astype(BF16)

    @pl.when(pl.program_id(0) == 0)
    def _():
        d = wt_ref.shape[2]
        for (c0, width), dst in zip(W_PIECES, w_refs):
            for k0 in range(0, width, MXU_TILE):
                dst[:, k0:k0 + MXU_TILE] = wt_ref[0, c0 + k0:c0 + k0 + MXU_TILE, :].T.astype(BF16)
        gate0 = W_QN + W_KV
        fox0 = gate0 + W_GATE + W_FOX
        gates = jnp.concatenate([wt_ref[0, gate0:gate0 + W_GATE, :], wt_ref[0, fox0:fox0 + FOX_HEADS, :],
                                 jnp.zeros((LANES - W_GATE - FOX_HEADS, d), F32)], axis=0)
        wg_ref[...] = gates.T.astype(BF16)
        for r0 in range(0, d, MXU_TILE):
            wm_ref[r0:r0 + MXU_TILE, :] = wm32_ref[0, r0:r0 + MXU_TILE, :].astype(BF16)

    sub = x_ref.shape[0] // ROW_SPLIT
    for h in range(ROW_SPLIT):
        rows = slice(h * sub, (h + 1) * sub)
        x = x_ref[rows, :]
        r = lax.rsqrt(jnp.mean(x * x, axis=-1, keepdims=True) + NORM_EPS)
        u = (x * r * g_ref[...]).astype(BF16)
        c0 = 0
        for piece in w_refs:
            for k0 in range(0, piece.shape[1], PROJ_CHUNK):
                k1 = min(k0 + PROJ_CHUNK, piece.shape[1])
                proj_ref[rows, c0 + k0:c0 + k1] = _dot(u, piece[:, k0:k1]).astype(BF16)
            c0 += piece.shape[1]
        gf_ref[rows, :] = _dot(u, wg_ref[...])
        for c0 in range(0, wm_ref.shape[1], GATE_CHUNK):
            z = _dot(u, wm_ref[:, c0:c0 + GATE_CHUNK]) + bm_ref[:, c0:c0 + GATE_CHUNK]
            mg_ref[rows, c0:c0 + GATE_CHUNK] = jax.nn.sigmoid(z).astype(BF16)


def _inproj(x2, g, w_in_t, wm, bm, later, tm):
    n, d = x2.shape
    steps = n // tm
    const = lambda i: (0, 0)
    once = dict(pipeline_mode=pl.Buffered(1))
    slabs = [w.shape[1] // steps for w in later]
    assert all(r * steps == w.shape[1] and r % 16 == 0 for w, r in zip(later, slabs)), "bf16 slabs need 16-row multiples"
    return pl.pallas_call(
        _inproj_kernel,
        grid=(n // tm,),
        in_specs=[
            pl.BlockSpec((tm, d), lambda i: (i, 0)),
            pl.BlockSpec((1, d), const),
            pl.BlockSpec(w_in_t.shape, lambda i: (0, 0, 0), **once),
            pl.BlockSpec(wm.shape, lambda i: (0, 0, 0), **once),
            pl.BlockSpec(bm.shape, const),
        ] + [pl.BlockSpec((1, r, w.shape[2]), lambda i: (0, i, 0)) for w, r in zip(later, slabs)],
        out_specs=[
            pl.BlockSpec((tm, PROJ_W), lambda i: (i, 0)),
            pl.BlockSpec((tm, LANES), lambda i: (i, 0)),
            pl.BlockSpec((tm, 2 * d), lambda i: (i, 0)),
        ] + [pl.BlockSpec((r, w.shape[2]), lambda i: (i, 0)) for w, r in zip(later, slabs)],
        out_shape=[
            jax.ShapeDtypeStruct((n, PROJ_W), BF16),
            jax.ShapeDtypeStruct((n, LANES), F32),
            jax.ShapeDtypeStruct((n, 2 * d), BF16),
        ] + [jax.ShapeDtypeStruct(w.shape[1:], BF16) for w in later],
        scratch_shapes=[pltpu.VMEM((d, width), BF16) for _, width in W_PIECES]
        + [pltpu.VMEM((d, LANES), BF16), pltpu.VMEM((d, 2 * d), BF16)],
        compiler_params=pltpu.CompilerParams(
            dimension_semantics=("arbitrary",), vmem_limit_bytes=VMEM_LIMIT),
        name="inproj",
    )(x2, g, w_in_t, wm, bm, *later)


def _compress_kernel(kc_ref, vc_ref, w1k_ref, w2k_ref, pk_ref, w1v_ref, w2v_ref, pv_ref, kco_ref, vco_ref,
                     xs_ref, w1s_ref, w2s_ref, ps_ref):
    t = kc_ref.shape[1]
    nblk = t // CMP_STRIDE

    @pl.when(pl.program_id(0) == 0)
    def _():
        r = _iota((HEAD_DIM, LANES), 0)
        c = _iota((HEAD_DIM, LANES), 1)
        halves = [jnp.where(c == r + g * HEAD_DIM, 1.0, 0.0).astype(BF16) for g in range(NSA_GROUPS)]
        z = jnp.zeros((HEAD_DIM, CMP_HIDDEN), BF16)
        first_row = _iota((16, LANES), 0) == 0
        for i, (w1_ref, w2_ref, p_ref) in enumerate(((w1k_ref, w2k_ref, pk_ref), (w1v_ref, w2v_ref, pv_ref))):
            for j in range(CMP_STRIDE):
                wa = w1_ref[0, j * HEAD_DIM:(j + 1) * HEAD_DIM, :].astype(BF16)
                wb = w1_ref[0, (CMP_STRIDE + j) * HEAD_DIM:(CMP_STRIDE + j + 1) * HEAD_DIM, :].astype(BF16)
                w1s_ref[i, j] = jnp.concatenate([jnp.concatenate([wa, z, wb, z], axis=1),
                                                 jnp.concatenate([z, wa, z, wb], axis=1)], axis=0)
            w2 = w2_ref[0].T.astype(BF16)
            for g in range(NSA_GROUPS):
                w2s_ref[i, g] = _dot(w2, halves[g]).astype(BF16)
            pos = p_ref[0].astype(BF16)
            prow = _dot(pos[0:CMP_STRIDE], halves[0]) + _dot(pos[CMP_STRIDE:CMP_BLOCK], halves[1])
            for j in range(CMP_STRIDE):
                ps_ref[i, j] = jnp.where(first_row, prow[j:j + 1, :], 0.0).astype(BF16)

    def one(x_ref, w_ref, w2_refs, p_ref, o_ref):
        xs_ref[...] = x_ref[0].astype(F32)
        acc = jnp.zeros((nblk + 16, 4 * LANES), F32)
        for j in range(CMP_STRIDE):
            xj = xs_ref[pl.ds(j, nblk, stride=CMP_STRIDE), :].astype(BF16)
            acc = acc + _dot(jnp.concatenate([xj, p_ref[j]], axis=0), w_ref[j])
        posb = acc[nblk:nblk + 1, 0:LANES] + acc[nblk:nblk + 1, 3 * LANES:4 * LANES]
        out = jnp.zeros((nblk, LANES), F32)
        for g in range(NSA_GROUPS):
            a = acc[0:nblk, g * LANES:(g + 1) * LANES]
            b = acc[0:nblk, (2 + g) * LANES:(3 + g) * LANES]
            h = a + pltpu.roll(b, nblk - 1, 0) + posb
            out = out + _dot(jax.nn.gelu(h).astype(BF16), w2_refs[g][...])
        o_ref[0] = out.astype(BF16)

    one(kc_ref, w1s_ref.at[0], (w2s_ref.at[0, 0], w2s_ref.at[0, 1]), ps_ref.at[0], kco_ref)
    one(vc_ref, w1s_ref.at[1], (w2s_ref.at[1, 0], w2s_ref.at[1, 1]), ps_ref.at[1], vco_ref)


def _compress(proj, w1k, w2k, pk, w1v, w2v, pv):
    b, t, _ = proj.shape
    kvb = COL_KV // LANES
    assert w1k.shape == (1, CMP_BLOCK * HEAD_DIM, CMP_HIDDEN) and w2k.shape == (1, HEAD_DIM, CMP_HIDDEN)
    assert pk.shape == (1, CMP_BLOCK, HEAD_DIM) and CMP_BLOCK == 2 * CMP_STRIDE and CMP_HIDDEN == LANES
    whole = lambda a: pl.BlockSpec(a.shape, lambda i: (0,) * a.ndim)
    ospec = pl.BlockSpec((1, LANES, LANES), lambda i: (i, 0, 0))
    return pl.pallas_call(
        _compress_kernel,
        grid=(b,),
        in_specs=[pl.BlockSpec((1, t, LANES), lambda i: (i, 0, kvb)),
                  pl.BlockSpec((1, t, LANES), lambda i: (i, 0, kvb + 1)),
                  whole(w1k), whole(w2k), whole(pk), whole(w1v), whole(w2v), whole(pv)],
        out_specs=[ospec, ospec],
        out_shape=[jax.ShapeDtypeStruct((b, LANES, LANES), BF16)] * 2,
        scratch_shapes=[pltpu.VMEM((t, LANES), F32),
                        pltpu.VMEM((2, CMP_STRIDE, 2 * HEAD_DIM, 4 * CMP_HIDDEN), BF16),
                        pltpu.VMEM((2, NSA_GROUPS, CMP_HIDDEN, LANES), BF16),
                        pltpu.VMEM((2, CMP_STRIDE, 16, LANES), BF16)],
        compiler_params=pltpu.CompilerParams(dimension_semantics=("arbitrary",)),
        name="compress",
    )(proj, proj, w1k, w2k, pk, w1v, w2v, pv)


def _flash_init(m_ref, acc_ref):
    m_ref[...] = jnp.full(m_ref.shape, M_INIT, F32)
    acc_ref[...] = jnp.zeros(acc_ref.shape, F32)


def _softmax_pv(scores, vt_blks, m_refs=None, acc_refs=None, maxes=None, first=False):
    def pv(vt, p):
        if not isinstance(vt, tuple):
            return _dot(vt, p)
        w = p.shape[1] // len(vt)
        return jnp.concatenate([_dot(v, p[:, j * w:(j + 1) * w]) for j, v in enumerate(vt)], axis=1)

    ps, alphas = [], []
    for i, s in enumerate(scores):
        mx = jnp.max(s, axis=0, keepdims=True) if maxes is None else maxes[i]
        if m_refs is None or first:
            ps.append(jnp.exp2(s - mx).astype(BF16))
            if first:
                m_refs[i][...] = mx
            continue
        m_old = m_refs[i][...]
        m_new = jnp.maximum(m_old, mx)
        ps.append(jnp.exp2(s - m_new).astype(BF16))
        alphas.append(jnp.exp2(m_old - m_new))
        m_refs[i][...] = m_new
    if m_refs is None:
        return [pv(vt, p) for vt, p in zip(vt_blks, ps)]
    if first:
        for vt, p, acc_ref in zip(vt_blks, ps, acc_refs):
            acc_ref[...] = pv(vt, p)
        return None
    for vt, p, alpha, acc_ref in zip(vt_blks, ps, alphas, acc_refs):
        acc_ref[...] = acc_ref[...] * alpha + pv(vt, p)
    return None


def _pipelined_sweep(n_rest, qk_to, spv_from, finish):
    def body(jj, carry):
        j = 2 * jj
        qk_to(1, j + 1)
        spv_from(0, j)
        qk_to(0, j + 2)
        spv_from(1, j + 1)
        return carry

    lax.fori_loop(0, n_rest // 2, body, 0)

    @pl.when(n_rest % 2 == 1)
    def _():
        qk_to(1, n_rest)
        spv_from(0, n_rest - 1)
        spv_from(1, n_rest)
        finish()

    @pl.when(n_rest % 2 == 0)
    def _():
        spv_from(0, n_rest)
        finish()


def _ones_rows(tk):
    return jnp.where(_iota((16, tk), 0) == 0, 1.0, 0.0).astype(BF16)


NSA_TQ = 128
NSA_N = NSA_HPG * NSA_TQ
SEL_TK = 512
WIN_KEYS = WINDOW + NSA_TQ

AUG_POS = 0
AUG_TQ = 6
AUG_PAD = 9
AUG_MASK = 32


def _bf16_terms(x, n=3):
    out, r = [], np.float32(x)
    for _ in range(n):
        term = np.asarray(r, np.float32).astype(BF16).astype(np.float32)
        out.append(float(term))
        r = np.float32(r - term)
    return out


def _nsa_queries(q_tile, blk):
    t0 = blk * NSA_TQ
    qt = q_tile.astype(F32).T * (HEAD_DIM ** -0.5 * LOG2E)
    lane = _iota((1, NSA_N), 1)
    hl = lane // NSA_TQ
    tqf = (t0 + lane % NSA_TQ).astype(F32)
    ridx = _iota((16, NSA_N), 0)
    zeros_h = jnp.zeros((HEAD_DIM, NSA_TQ), F32)
    l2e = _bf16_terms(LOG2E)
    q_top, aug16 = [], []
    for g in range(NSA_GROUPS):
        tops = []
        for h in range(NSA_HPG):
            head = NSA_HPG * g + h
            qh = qt[head * HEAD_DIM:(head + 1) * HEAD_DIM]
            tops.append(jnp.concatenate([qh, zeros_h] if g == 0 else [zeros_h, qh], axis=0))
        q_top.append(jnp.concatenate(tops, axis=1))
        slope = lax.bitcast_convert_type((127 - (NSA_HPG * g + hl + 1)) << 23, F32)
        rows = {}
        for i, term in enumerate(l2e):
            rows[AUG_POS + 2 * i] = slope * (64.0 * term)
            rows[AUG_POS + 2 * i + 1] = slope * term
        for i, term in enumerate(_split3(-(slope * tqf) * LOG2E)):
            rows[AUG_TQ + i] = term.astype(F32)
        a = jnp.zeros((16, NSA_N), F32)
        for r, val in rows.items():
            a = jnp.where(ridx == r, val, a)
        aug16.append(a)
    return q_top, aug16


def _nsa_compressed(s_cmp, blk, vct_ref, ovt, gs):
    tq = blk * NSA_TQ + _iota((1, NSA_N), 1) % NSA_TQ
    n_idx = _iota((LANES, NSA_N), 0)
    valid = (n_idx * CMP_STRIDE + (CMP_BLOCK - 1) <= tq) & (n_idx < LANES - 1)
    o_cmp, imps = [], []
    for g in gs:
        sm = jnp.where(valid, s_cmp[g], NEG_BIG)
        mx = jnp.max(sm, axis=0, keepdims=True)
        e = jnp.where(valid, jnp.exp2(sm - mx), 0.0)
        den = jnp.sum(e, axis=0, keepdims=True)
        pn = e / jnp.where(den > 0.0, den, 1.0)
        o_cmp.append(_dot(vct_ref[g * HEAD_DIM:(g + 1) * HEAD_DIM, :], pn.astype(BF16)))

        psum = pn[:, 0:NSA_TQ]
        for h in range(1, NSA_HPG):
            psum = psum + pn[:, h * NSA_TQ:(h + 1) * NSA_TQ]
        hi, mid, lo = _split3(psum)
        imps.append(_dot(ovt, hi) + _dot(ovt, mid) + _dot(ovt, lo))
    return o_cmp, imps


def _nsa_select(imps, blk, q_top, aug16, gs):
    t0 = blk * NSA_TQ
    ns = imps[0].shape[0]
    tcol = t0 + _iota((8, NSA_TQ), 1)
    q_sel = []
    for g, imp in zip(gs, imps):
        tiles = range(ns // 8)
        j_idx = [8 * r + _iota((8, NSA_TQ), 0) for r in tiles]
        in_past = [j_idx[r] * SEL_BLOCK <= tcol for r in tiles]
        cur = tcol // SEL_BLOCK
        score = []
        for r in tiles:
            forced = (j_idx[r] == 0) | (j_idx[r] == cur) | (j_idx[r] == cur - 1)
            score.append(jnp.where(forced, FORCE_SCORE, jnp.where(in_past[r], imp[8 * r:8 * r + 8], -1.0)))
        rank = [jnp.zeros((8, NSA_TQ), F32) for _ in tiles]
        for i in range(ns):
            si = score[i // 8][i % 8:i % 8 + 1, :]
            for r in tiles:
                if 8 * r > i:
                    beats = si >= score[r]
                elif 8 * r + 7 <= i:
                    beats = si > score[r]
                else:
                    beats = (si > score[r]) | ((si == score[r]) & (j_idx[r] > i))
                rank[r] = rank[r] + jnp.where(beats, 1.0, 0.0)
        maskval = jnp.concatenate([jnp.where((rank[r] < min(N_SELECT, ns)) & in_past[r], 0.0, NEG_BIG) for r in tiles], axis=0)
        mask4 = jnp.concatenate([maskval] * NSA_HPG, axis=1)
        q_sel.append(jnp.concatenate([q_top[g], aug16[g], jnp.zeros((AUG_MASK - 16, NSA_N), BF16), mask4.astype(BF16),
                                      jnp.zeros((LANES - AUG_MASK - ns, NSA_N), BF16)], axis=0))
    return q_sel


def _nsa_kernel(q_ref, gf_ref, kc_ref, vc_ref, ks_ref, vs_ref, kw_ref, vw_ref,
                augk_ref, augw_ref, augc_ref, ovt_ref, trisel_ref, triwin_ref, o_ref,
                vst_ref, vwt_ref, kwp_ref, vct_ref, gt_ref, qa_ref, m_ref, acc_ref, s0_ref, s1_ref, mx0_ref, mx1_ref):
    qb = pl.program_id(1)
    nq = pl.num_programs(1)
    t = ks_ref.shape[1]
    t0 = qb * NSA_TQ
    groups = range(NSA_GROUPS)
    zeros_tail = jnp.zeros((LANES - 16, NSA_N), BF16)

    def prepare(blk):
        q_top, aug16 = _nsa_queries(q_ref[0, pl.ds(pl.multiple_of(blk * NSA_TQ, NSA_TQ), NSA_TQ), :], blk)
        for g in groups:
            qa_ref[g] = jnp.concatenate([q_top[g], aug16[g]], axis=0).astype(BF16)

    def cmp_scores(q_top, aug16):
        kcb = jnp.concatenate([kc_ref[0], augc_ref[...]], axis=1)
        return [_dot(kcb, jnp.concatenate([q_top[g], aug16[g], zeros_tail], axis=0)) for g in groups]

    @pl.when(qb == 0)
    def _():
        vwt_ref[:, 0:WINDOW] = jnp.zeros((LANES, WINDOW), BF16)
        kwp_ref[0:WINDOW, :] = jnp.zeros((WINDOW, LANES), BF16)
        kwp_ref[WINDOW:WINDOW + t, :] = kw_ref[0]
        for c in range(t // LANES):
            sl = slice(c * LANES, (c + 1) * LANES)
            vst_ref[:, sl] = vs_ref[0, sl, :].astype(F32).T.astype(BF16)
            vwt_ref[:, WINDOW + c * LANES:WINDOW + (c + 1) * LANES] = vw_ref[0, sl, :].astype(F32).T.astype(BF16)
        vct_ref[...] = vc_ref[0].astype(F32).T.astype(BF16)
        prepare(0)

    q_top = [qa_ref[g, 0:LANES, :] for g in groups]
    aug16 = [qa_ref[g, LANES:LANES + 16, :] for g in groups]
    s_cmp = cmp_scores(q_top, aug16)
    kw0 = pl.multiple_of(t0, NSA_TQ)
    kwin = jnp.concatenate([kwp_ref[pl.ds(kw0, WIN_KEYS), :], augw_ref[pl.ds(kw0, WIN_KEYS), :]], axis=1)
    pad_row = _iota((16, NSA_N), 0) == AUG_PAD
    neg_big = jnp.full((16, NSA_N), NEG_BIG, BF16)
    s_win = [_dot(kwin, jnp.concatenate([q_top[g], jnp.where(pad_row, neg_big, aug16[g]), zeros_tail], axis=0))
             for g in groups]
    o_cmp, imps = _nsa_compressed(s_cmp, qb, vct_ref, ovt_ref[...], groups)
    q_sel = _nsa_select(imps, qb, q_top, aug16, groups)

    hi_edge = jnp.concatenate([triwin_ref[0]] * NSA_HPG, axis=1)
    lo_edge = jnp.concatenate([triwin_ref[1]] * NSA_HPG, axis=1)
    ones_w = _ones_rows(WIN_KEYS)
    s_win = [jnp.concatenate([s[0:NSA_TQ] + lo_edge, s[NSA_TQ:WINDOW], s[WINDOW:WIN_KEYS] + hi_edge], axis=0)
             for s in s_win]
    vt_win = [jnp.concatenate([vwt_ref[g * HEAD_DIM:(g + 1) * HEAD_DIM, pl.ds(kw0, WIN_KEYS)], ones_w], axis=0)
              for g in groups]
    o_win = [acc[0:HEAD_DIM] / acc[HEAD_DIM:HEAD_DIM + 1] for acc in _softmax_pv(s_win, vt_win)]

    ones_s = _ones_rows(SEL_TK)
    per_chunk = SEL_TK // NSA_TQ
    n_full = qb // per_chunk
    s_bufs = (s0_ref, s1_ref)
    mx_bufs = (mx0_ref, mx1_ref)

    def sel_scores(cb):
        k0 = pl.multiple_of(cb * SEL_TK, SEL_TK)
        kblk = jnp.concatenate([ks_ref[0, pl.ds(k0, SEL_TK), :], augk_ref[pl.ds(k0, SEL_TK), :]], axis=1)
        return [_dot(kblk, q_sel[g]) for g in groups]

    def qk_to(buf, j):
        for g, s in enumerate(sel_scores(j - 1)):
            s_bufs[buf][g] = s
            mx_bufs[buf][g] = jnp.max(s, axis=0, keepdims=True)

    def spv_from(buf, j):
        cb = jnp.where(j == 0, n_full, j - 1)
        k0 = pl.multiple_of(cb * SEL_TK, SEL_TK)
        vts = [jnp.concatenate([vst_ref[g * HEAD_DIM:(g + 1) * HEAD_DIM, pl.ds(k0, SEL_TK)], ones_s], axis=0)
               for g in groups]
        _softmax_pv([s_bufs[buf][g] for g in groups], vts,
                    [m_ref.at[g] for g in groups], [acc_ref.at[g] for g in groups],
                    maxes=[mx_bufs[buf][g] for g in groups])

    diag = trisel_ref[qb % per_chunk]
    diag4 = jnp.concatenate([diag] * NSA_HPG, axis=1)
    for g, s in enumerate(sel_scores(n_full)):
        s = s + diag4
        s0_ref[g] = s
        mx0_ref[g] = jnp.max(s, axis=0, keepdims=True)
        _flash_init(m_ref.at[g], acc_ref.at[g])

    def gated_output():
        gt_ref[...] = gf_ref[0].T
        sig = jax.nn.sigmoid(gt_ref[GF_GATE:GF_GATE + 3 * NSA_HEADS, :])
        outs = []
        for g in groups:
            acc = acc_ref[g]
            o_sel = acc[0:HEAD_DIM] / acc[HEAD_DIM:HEAD_DIM + 1]
            for h in range(NSA_HPG):
                sl = slice(h * NSA_TQ, (h + 1) * NSA_TQ)
                head = NSA_HPG * g + h
                g0 = sig[head:head + 1]
                g1 = sig[NSA_HEADS + head:NSA_HEADS + head + 1]
                g2 = sig[2 * NSA_HEADS + head:2 * NSA_HEADS + head + 1]
                outs.append(g0 * o_cmp[g][:, sl] + g1 * o_sel[:, sl] + g2 * o_win[g][:, sl])
        o_ref[0] = jnp.concatenate(outs, axis=0).T.astype(BF16)
        prepare(jnp.minimum(qb + 1, nq - 1))

    _pipelined_sweep(n_full, qk_to, spv_from, gated_output)


def _nsa(proj, gf, kcmp, vcmp, augk, augw, augc, ovt, trisel, triwin):
    b, t, _ = proj.shape
    nq = t // NSA_TQ
    kvb = COL_KV // LANES
    width = NSA_HEADS * HEAD_DIM
    whole = lambda a: pl.BlockSpec(a.shape, lambda i, k: (0,) * a.ndim)
    kv = lambda idx: pl.BlockSpec((1, t, LANES), lambda i, k: (i, 0, idx))
    return pl.pallas_call(
        _nsa_kernel,
        grid=(b, nq),
        in_specs=[
            pl.BlockSpec((1, t, width), lambda i, k: (i, 0, COL_QN // width)),
            pl.BlockSpec((1, NSA_TQ, LANES), lambda i, k: (i, k, 0)),
            pl.BlockSpec((1, LANES, LANES), lambda i, k: (i, 0, 0)),
            pl.BlockSpec((1, LANES, LANES), lambda i, k: (i, 0, 0)),
            kv(kvb + 2), kv(kvb + 3), kv(kvb + 4), kv(kvb + 5),
            whole(augk), whole(augw), whole(augc), whole(ovt), whole(trisel), whole(triwin),
        ],
        out_specs=pl.BlockSpec((1, NSA_TQ, width), lambda i, k: (i, k, 0)),
        out_shape=jax.ShapeDtypeStruct((b, t, width), BF16),
        scratch_shapes=[
            pltpu.VMEM((LANES, t), BF16),
            pltpu.VMEM((LANES, t + WINDOW), BF16),
            pltpu.VMEM((t + WINDOW, LANES), BF16),
            pltpu.VMEM((LANES, LANES), BF16),
            pltpu.VMEM((LANES, NSA_TQ), F32),
            pltpu.VMEM((NSA_GROUPS, LANES + 16, NSA_N), BF16),
            pltpu.VMEM((NSA_GROUPS, 1, NSA_N), F32),
            pltpu.VMEM((NSA_GROUPS, HEAD_DIM + 16, NSA_N), F32),
            pltpu.VMEM((NSA_GROUPS, SEL_TK, NSA_N), F32),
            pltpu.VMEM((NSA_GROUPS, SEL_TK, NSA_N), F32),
            pltpu.VMEM((NSA_GROUPS, 1, NSA_N), F32),
            pltpu.VMEM((NSA_GROUPS, 1, NSA_N), F32),
        ],
        compiler_params=pltpu.CompilerParams(
            dimension_semantics=("arbitrary", "arbitrary"), vmem_limit_bytes=VMEM_LIMIT),
        name="nsa_attention",
    )(proj, gf, kcmp, vcmp, proj, proj, proj, proj, augk, augw, augc, ovt, trisel, triwin)


FOX_CBLK = MXU_TILE


def _foxprep_kernel(gf_ref, bias_ref, cq_ref, augk_ref):
    t = gf_ref.shape[1]
    cols = []
    for c in range(t // LANES):
        cols.append(gf_ref[0, c * LANES:(c + 1) * LANES, :].T[GF_FOX:GF_FOX + FOX_HEADS])
    hrow = _iota((FOX_HEADS, t), 0)
    bias = jnp.zeros((FOX_HEADS, t), F32)
    for h in range(FOX_HEADS):
        bias = jnp.where(hrow == h, bias_ref[0, h], bias)
    f = jnp.concatenate(cols, axis=1) + bias
    ls = jnp.minimum(f, 0.0) - jnp.log(1.0 + jnp.exp(-jnp.abs(f)))
    upper = (_iota((FOX_CBLK, FOX_CBLK), 0) <= _iota((FOX_CBLK, FOX_CBLK), 1)).astype(BF16)
    carry = jnp.zeros((FOX_HEADS, 1), F32)
    parts = []
    for c in range(t // FOX_CBLK):
        hi, mid, lo = _split3(ls[:, c * FOX_CBLK:(c + 1) * FOX_CBLK])
        cb = _dot(hi, upper) + _dot(mid, upper) + _dot(lo, upper) + carry
        carry = cb[:, FOX_CBLK - 1:FOX_CBLK]
        parts.append(cb)
    csum = jnp.concatenate(parts, axis=1) * LOG2E
    c1, c2, c3 = (v.astype(F32) for v in _split3(csum))
    ones = jnp.ones((3 * FOX_HEADS, t), F32)
    zeros = jnp.zeros((LANES - 6 * FOX_HEADS, t), F32)
    cq_ref[0] = jnp.concatenate([c1, c2, c3, ones, zeros], axis=0).astype(BF16)
    slab_k = jnp.concatenate([ones, -c1, -c2, -c3, zeros], axis=0)
    for c in range(t // LANES):
        sl = slice(c * LANES, (c + 1) * LANES)
        augk_ref[0, sl, :] = slab_k[:, sl].T.astype(BF16)


def _foxprep(gf, bias):
    b, t, _ = gf.shape
    return pl.pallas_call(
        _foxprep_kernel,
        grid=(b,),
        in_specs=[pl.BlockSpec((1, t, LANES), lambda i: (i, 0, 0)),
                  pl.BlockSpec(memory_space=pltpu.SMEM)],
        out_specs=[pl.BlockSpec((1, LANES, t), lambda i: (i, 0, 0)),
                   pl.BlockSpec((1, t, LANES), lambda i: (i, 0, 0))],
        out_shape=[jax.ShapeDtypeStruct((b, LANES, t), BF16),
                   jax.ShapeDtypeStruct((b, t, LANES), BF16)],
        compiler_params=pltpu.CompilerParams(dimension_semantics=("arbitrary",)),
        name="fox_prep",
    )(gf, bias)


FOX_TQ = MXU_TILE
FOX_TK = MXU_TILE
FOX_NP = FOX_HEADS // 2
FOX_N = 2 * FOX_TQ
FOX_ROWS = HEAD_DIM + 16


def _fox_kernel(q_ref, k_ref, v_ref, cq_ref, augk_ref, tri_ref, o_ref, vt_ref, qs_ref, m_ref, acc_ref,
                s0_ref, s1_ref, sd_ref, mx0_ref, mx1_ref, mxd_ref):
    qb = pl.program_id(1)
    nq = pl.num_programs(1)
    t = k_ref.shape[1]
    pairs = range(FOX_NP)
    s_bufs = (s0_ref, s1_ref, sd_ref)
    mx_bufs = (mx0_ref, mx1_ref, mxd_ref)
    diag = 2

    def scores(kb, qs):
        k0 = pl.multiple_of(kb * FOX_TK, FOX_TK)
        ak = augk_ref[0, pl.ds(k0, FOX_TK), :]
        return [_dot(jnp.concatenate([k_ref[0, pl.ds(k0, FOX_TK), p * LANES:(p + 1) * LANES], ak], axis=1), qs(p))
                for p in pairs]

    def prepare(blk, slot):
        q0 = pl.multiple_of(blk * FOX_TQ, FOX_TQ)
        row = _iota((LANES, FOX_TQ), 0)
        hrow = row % FOX_HEADS
        cq = cq_ref[0, :, pl.ds(q0, FOX_TQ)].astype(F32)
        for p in pairs:
            qt = q_ref[0, pl.ds(q0, FOX_TQ), p * LANES:(p + 1) * LANES].astype(F32).T * (HEAD_DIM ** -0.5 * LOG2E)
            q_top = jnp.concatenate([jnp.where(row < HEAD_DIM, qt, 0.0), jnp.where(row >= HEAD_DIM, qt, 0.0)], axis=1)
            q_aug = jnp.concatenate([jnp.where(hrow == 2 * p, cq, 0.0), jnp.where(hrow == 2 * p + 1, cq, 0.0)],
                                    axis=1)
            qs_ref[slot, p] = jnp.concatenate([q_top, q_aug], axis=0).astype(BF16)
        tri = tri_ref[...]
        for p, s in enumerate(scores(blk, lambda p: qs_ref[slot, p])):
            s = s + tri
            sd_ref[p] = s
            mxd_ref[p] = jnp.max(s, axis=0, keepdims=True)

    def qk_to(buf, kb):
        for p, s in enumerate(scores(kb, lambda p: qs_ref[qb % 2, p])):
            s_bufs[buf][p] = s
            mx_bufs[buf][p] = jnp.max(s, axis=0, keepdims=True)

    def spv_from(buf, kb, first=False):
        k0 = pl.multiple_of(kb * FOX_TK, FOX_TK)
        vts = [(vt_ref[p, 0, :, pl.ds(k0, FOX_TK)], vt_ref[p, 1, :, pl.ds(k0, FOX_TK)]) for p in pairs]
        _softmax_pv([s_bufs[buf][p] for p in pairs], vts,
                    [m_ref.at[p] for p in pairs], [acc_ref.at[p] for p in pairs],
                    maxes=[mx_bufs[buf][p] for p in pairs], first=first)

    def finish():
        for p in pairs:
            acc = acc_ref[p]
            den = acc[HEAD_DIM:HEAD_DIM + 1]
            o_a = acc[0:HEAD_DIM, 0:FOX_TQ] / den[:, 0:FOX_TQ]
            o_b = acc[0:HEAD_DIM, FOX_TQ:FOX_N] / den[:, FOX_TQ:FOX_N]
            o_ref[0, :, p * LANES:(p + 1) * LANES] = jnp.concatenate([o_a, o_b], axis=0).T.astype(BF16)
        prepare(jnp.minimum(qb + 1, nq - 1), (qb + 1) % 2)

    @pl.when(qb == 0)
    def _():
        for p in pairs:
            for c in range(t // LANES):
                sl = slice(c * LANES, (c + 1) * LANES)
                vt = v_ref[0, sl, p * LANES:(p + 1) * LANES].astype(F32).T.astype(BF16)
                for hh in range(2):
                    vt_ref[p, hh, 0:HEAD_DIM, sl] = vt[hh * HEAD_DIM:(hh + 1) * HEAD_DIM]
            for hh in range(2):
                vt_ref[p, hh, HEAD_DIM:FOX_ROWS, :] = _ones_rows(t)
        prepare(0, 0)

    @pl.when(qb == 0)
    def _():
        spv_from(diag, qb, first=True)
        finish()

    @pl.when(qb > 0)
    def _():
        qk_to(0, 0)
        spv_from(diag, qb, first=True)

    def body(jj, carry):
        kb = 2 * jj
        qk_to(1, kb + 1)
        spv_from(0, kb)
        qk_to(0, kb + 2)
        spv_from(1, kb + 1)
        return carry

    lax.fori_loop(0, jnp.maximum(qb - 1, 0) // 2, body, 0)

    @pl.when(qb % 2 == 1)
    def _():
        spv_from(0, qb - 1)
        finish()

    @pl.when((qb > 0) & (qb % 2 == 0))
    def _():
        qk_to(1, qb - 1)
        spv_from(0, qb - 2)
        spv_from(1, qb - 1)
        finish()


def _fox(proj, cq, augk, tri):
    b, t, _ = proj.shape
    nq = t // FOX_TQ
    width = FOX_HEADS * HEAD_DIM
    qf, kf, vf = COL_QF // width, COL_KF // width, COL_VF // width
    return pl.pallas_call(
        _fox_kernel,
        grid=(b, nq),
        in_specs=[
            pl.BlockSpec((1, t, width), lambda i, k: (i, 0, qf)),
            pl.BlockSpec((1, t, width), lambda i, k: (i, 0, kf)),
            pl.BlockSpec((1, t, width), lambda i, k: (i, 0, vf)),
            pl.BlockSpec((1, LANES, t), lambda i, k: (i, 0, 0)),
            pl.BlockSpec((1, t, LANES), lambda i, k: (i, 0, 0)),
            pl.BlockSpec(tri.shape, lambda i, k: (0, 0)),
        ],
        out_specs=pl.BlockSpec((1, FOX_TQ, width), lambda i, k: (i, k, 0)),
        out_shape=jax.ShapeDtypeStruct((b, t, width), BF16),
        scratch_shapes=[
            pltpu.VMEM((FOX_NP, 2, FOX_ROWS, t), BF16),
            pltpu.VMEM((2, FOX_NP, 2 * LANES, FOX_N), BF16),
            pltpu.VMEM((FOX_NP, 1, FOX_N), F32),
            pltpu.VMEM((FOX_NP, FOX_ROWS, FOX_N), F32),
            pltpu.VMEM((FOX_NP, FOX_TK, FOX_N), F32),
            pltpu.VMEM((FOX_NP, FOX_TK, FOX_N), F32),
            pltpu.VMEM((FOX_NP, FOX_TK, FOX_N), F32),
            pltpu.VMEM((FOX_NP, 1, FOX_N), F32),
            pltpu.VMEM((FOX_NP, 1, FOX_N), F32),
            pltpu.VMEM((FOX_NP, 1, FOX_N), F32),
        ],
        compiler_params=pltpu.CompilerParams(
            dimension_semantics=("arbitrary", "arbitrary"), vmem_limit_bytes=VMEM_LIMIT),
        name="fox_attention",
    )(proj, proj, proj, cq, augk, tri)


def _post_kernel(x_ref, on_ref, of_ref, mg_ref, wn_ref, wf_ref, wo_ref, gm_ref, wu_ref, wd_ref, gfin_ref, o_ref):
    d = x_ref.shape[1]
    ff = wu_ref.shape[1]
    ga = mg_ref[:, 0:d].astype(F32)
    gb = mg_ref[:, d:2 * d].astype(F32)
    y = ga * _dot(on_ref[...], wn_ref[...]) + gb * _dot(of_ref[...], wf_ref[...])
    h = x_ref[...] + _dot(y.astype(BF16), wo_ref[...])
    r = lax.rsqrt(jnp.mean(h * h, axis=-1, keepdims=True) + NORM_EPS)
    v = (h * r * gm_ref[...]).astype(BF16)
    acc = h
    for c0 in range(0, ff, FF_CHUNK):
        up = jnp.maximum(_dot(v, wu_ref[:, c0:c0 + FF_CHUNK]), 0.0)
        acc = acc + _dot((up * up).astype(BF16), wd_ref[c0:c0 + FF_CHUNK, :])
    r2 = lax.rsqrt(jnp.mean(acc * acc, axis=-1, keepdims=True) + NORM_EPS)
    o_ref[...] = acc * r2 * gfin_ref[...]


def _post(x2, on, of, mg, wn, wf, wo, gm, wu, wd, gfin, tm):
    n, d = x2.shape
    row = lambda w: pl.BlockSpec((tm, w), lambda i: (i, 0))
    const = lambda a: pl.BlockSpec(a.shape, lambda i: (0, 0), pipeline_mode=pl.Buffered(1))
    return pl.pallas_call(
        _post_kernel,
        grid=(n // tm,),
        in_specs=[row(d), row(on.shape[1]), row(of.shape[1]), row(2 * d),
                  const(wn), const(wf), const(wo), const(gm), const(wu), const(wd), const(gfin)],
        out_specs=row(d),
        out_shape=jax.ShapeDtypeStruct((n, d), F32),
        compiler_params=pltpu.CompilerParams(
            dimension_semantics=("arbitrary",), vmem_limit_bytes=VMEM_LIMIT),
        name="post_mlp",
    )(x2, on, of, mg, wn, wf, wo, gm, wu, wd, gfin)


def _position_constants(t):
    def key_side(pos, pad):
        a = np.zeros((pos.size, LANES), np.float32)
        for i in range(3):
            a[:, AUG_POS + 2 * i] = pos // 64
            a[:, AUG_POS + 2 * i + 1] = pos % 64
        a[:, AUG_TQ:AUG_TQ + 3] = 1.0
        a[:, AUG_PAD] = pad
        return a

    pos = np.arange(t)
    augk = key_side(pos, 0.0)
    augk[pos, AUG_MASK + pos // SEL_BLOCK] = 1.0
    augw = np.concatenate([key_side(np.zeros(WINDOW, np.int64), 1.0), key_side(pos, 0.0)], axis=0)
    augc = key_side(np.arange(LANES) * CMP_STRIDE + CMP_BLOCK - 1, 0.0)

    nc = (t - CMP_BLOCK) // CMP_STRIDE + 1
    ns = t // SEL_BLOCK
    n = np.arange(LANES)[None, :]
    j = np.arange(ns)[:, None]
    ovt = ((n * CMP_STRIDE <= j * SEL_BLOCK + SEL_BLOCK - 1) & (n * CMP_STRIDE + CMP_BLOCK - 1 >= j * SEL_BLOCK) & (n < nc))

    def tri(tk, tq, anti=False):
        kk = np.arange(tk)[:, None]
        tt = np.arange(tq)[None, :]
        ok = (kk > tt) if anti else (kk <= tt)
        return np.where(ok, 0.0, NEG_BIG).astype(np.float32)

    edge = tri(NSA_TQ, NSA_TQ)
    trisel = np.zeros((SEL_TK // NSA_TQ, SEL_TK, NSA_TQ), np.float32)
    for v in range(SEL_TK // NSA_TQ):
        trisel[v, v * NSA_TQ:(v + 1) * NSA_TQ] = edge
    triwin = np.stack([edge, tri(NSA_TQ, NSA_TQ, anti=True)])
    trifox = np.tile(tri(FOX_TK, FOX_TQ), (1, 2))
    return (jnp.asarray(augk, BF16), jnp.asarray(augw, BF16), jnp.asarray(augc, BF16), jnp.asarray(ovt, BF16),
            jnp.asarray(trisel), jnp.asarray(triwin), jnp.asarray(trifox))


def kernel(x, norm_mix, w_in, cmp_pos_k, cmp_w1_k, cmp_w2_k, cmp_pos_v, cmp_w1_v, cmp_w2_v, fox_f_bias, w_branch_nsa, w_branch_fox, w_merge_gate, b_merge_gate, w_out, norm_mlp, w_up, w_down, norm_final):
    b, t, d = x.shape
    assert w_in.shape[0] == 1, "one layer: the final norm is fused into the MLP kernel"
    assert t % FOX_TQ == 0 and t // SEL_BLOCK == 32 and (t - CMP_BLOCK) // CMP_STRIDE + 1 == LANES - 1
    augk, augw, augc, ovt, trisel, triwin, trifox = _position_constants(t)
    h = x.reshape(b * t, d)
    proj, gf, mg, wn, wf, wo, wu, wd = _inproj(h, norm_mix[0][None, :], jnp.swapaxes(w_in, 1, 2), w_merge_gate,
                                               b_merge_gate[0][None, :],
                                               (w_branch_nsa, w_branch_fox, w_out, w_up, w_down), ROW_TILE)
    proj = proj.reshape(b, t, PROJ_W)
    gf = gf.reshape(b, t, LANES)

    kcmp, vcmp = _compress(proj, cmp_w1_k, jnp.swapaxes(cmp_w2_k, 1, 2), cmp_pos_k,
                           cmp_w1_v, jnp.swapaxes(cmp_w2_v, 1, 2), cmp_pos_v)
    o_n = _nsa(proj, gf, kcmp, vcmp, augk, augw, augc, ovt, trisel, triwin)
    cq, faug = _foxprep(gf, fox_f_bias)
    o_f = _fox(proj, cq, faug, trifox)
    out = _post(h, o_n.reshape(b * t, -1), o_f.reshape(b * t, -1), mg,
                wn, wf, wo, norm_mlp[0][None, :], wu, wd, norm_final[None, :], ROW_TILE)
    return out.reshape(b, t, d)
```
